```python
import jax, jax.numpy as jnp
from jax import lax
import numpy as np

D_MODEL = 1024
BATCH = 4
SEQ = 4096
DEPTH = 2
DEC_BATCH = 32
DEC_SEQ = 4
PAST_LEN = 16384
PAGE_SIZE = 128

N_EVEN = (DEPTH + 1) // 2
N_ODD = DEPTH // 2
D_SCONV = D_MODEL // 2
SCONV_W = 3
RET_HEADS = 4
RET_DK = (D_MODEL // 2) // RET_HEADS
RET_DV = RET_DK
RET_CHUNK = 128
ROPE_BASE = 10000.0
NSA_HEADS = 16
NSA_KV = 4
NSA_GROUP = NSA_HEADS // NSA_KV
NSA_DH = D_MODEL // NSA_HEADS
CMP_LEN = 32
CMP_STRIDE = 16
CMP_HIDDEN = 2 * NSA_DH
SLC_BLOCK = 64
SLC_TOPN = 16
WINDOW = 512
Q_BLOCK = 128
D_FF = 256 * ((8 * D_MODEL // 3 + 255) // 256)
FFN_W = 3
ALPHA = (2.0 * DEPTH) ** 0.25
BETA = (8.0 * DEPTH) ** -0.25
LN_EPS = 1e-5
NEG_INF = -1e30
FORCE_BONUS = 1e4

EVEN_SPLITS = [D_SCONV, D_SCONV, D_SCONV, RET_HEADS * RET_DK, RET_HEADS * RET_DK, RET_HEADS * RET_DV, RET_HEADS * RET_DV]
D_IN_EVEN = sum(EVEN_SPLITS)
D_MIX_EVEN = D_SCONV + RET_HEADS * RET_DV
ODD_SPLITS = [NSA_HEADS * NSA_DH] + [NSA_KV * NSA_DH] * 6 + [3 * NSA_HEADS]
D_IN_ODD = sum(ODD_SPLITS)
D_MIX_ODD = NSA_HEADS * NSA_DH

kernel_name = "hybrid_sconv_retention_nsa_convffn_step"


def _split(z, sizes):
    cuts = [int(c) for c in np.cumsum(sizes)[:-1]]
    return jnp.split(z, cuts, axis=-1)


def layer_norm(x, g, b):
    xf = x.astype(jnp.float32)
    mu = jnp.mean(xf, axis=-1, keepdims=True)
    var = jnp.mean(jnp.square(xf - mu), axis=-1, keepdims=True)
    return ((xf - mu) * lax.rsqrt(var + LN_EPS) * g + b).astype(x.dtype)


def causal_dwconv(u, hist, w, b):
    t = u.shape[1]
    full = jnp.concatenate([hist.astype(u.dtype), u], axis=1)
    y = b + full[:, 0:t] * w[0]
    for j in range(1, w.shape[0]):
        y = y + full[:, j:j + t] * w[j]
    return y, full[:, t:]


def rope(x, pos):
    half = x.shape[-1] // 2
    inv = ROPE_BASE ** (-jnp.arange(half, dtype=jnp.float32) / half)
    ang = pos.astype(jnp.float32)[:, None] * inv
    cos = jnp.cos(ang)[None, :, None, :]
    sin = jnp.sin(ang)[None, :, None, :]
    x1, x2 = x[..., :half], x[..., half:]
    return jnp.concatenate([x1 * cos - x2 * sin, x1 * sin + x2 * cos], axis=-1)


def retention_chunk(state, q, k, v, log_gamma):
    c = q.shape[1]
    n = jnp.arange(c, dtype=jnp.float32)
    diff = n[:, None] - n[None, :]
    decay = jnp.where(diff >= 0, jnp.exp(log_gamma[:, None, None] * jnp.maximum(diff, 0.0)), 0.0)
    scores = jnp.einsum("bnhd,bmhd->bhnm", q, k) * decay
    intra = jnp.einsum("bhnm,bmhe->bnhe", scores, v)
    q_dec = jnp.exp((n[:, None] + 1.0) * log_gamma[None, :])
    k_dec = jnp.exp((c - 1.0 - n[:, None]) * log_gamma[None, :])
    cross = jnp.einsum("bnhd,bhde->bnhe", q, state) * q_dec[None, :, :, None]
    new_state = state * jnp.exp(c * log_gamma)[None, :, None, None] + jnp.einsum(
        "bmhd,bmhe->bhde", k * k_dec[None, :, :, None], v)
    return intra + cross, new_state


def retention(q, k, v, state, log_gamma):
    b, t, h, _ = q.shape
    c = RET_CHUNK if t % RET_CHUNK == 0 else t
    nc = t // c

    def to_chunks(a):
        return a.reshape(b, nc, c, h, a.shape[-1]).swapaxes(0, 1)

    def step(s, qkv):
        o, s = retention_chunk(s, qkv[0], qkv[1], qkv[2], log_gamma)
        return s, o

    state, out = lax.scan(step, state, (to_chunks(q), to_chunks(k), to_chunks(v)))
    return out.swapaxes(0, 1).reshape(b, t, h, -1), state


def even_mixer(x, pos, conv_hist, ret_state, w_in, conv_w, conv_b, gn_g, w_out):
    b, t, _ = x.shape
    f32 = jnp.float32
    h, gate_b, gate_c, q, k, v, g = _split(x @ w_in, EVEN_SPLITS)
    u, conv_hist = causal_dwconv(gate_c * h, conv_hist, conv_w, conv_b)
    y_a = gate_b * u
    q = rope(q.reshape(b, t, RET_HEADS, RET_DK).astype(f32), pos) * (RET_DK ** -0.5)
    k = rope(k.reshape(b, t, RET_HEADS, RET_DK).astype(f32), pos)
    v = v.reshape(b, t, RET_HEADS, RET_DV).astype(f32)
    log_gamma = jnp.log1p(-jnp.exp2(-5.0 - jnp.arange(RET_HEADS, dtype=f32)))
    o, ret_state = retention(q, k, v, ret_state.astype(f32), log_gamma)
    mu = jnp.mean(o, axis=-1, keepdims=True)
    var = jnp.mean(jnp.square(o - mu), axis=-1, keepdims=True)
    o = ((o - mu) * lax.rsqrt(var + LN_EPS)).reshape(b, t, -1) * gn_g
    y_b = jax.nn.silu(g.astype(f32)) * o
    y = jnp.concatenate([y_a.astype(x.dtype), y_b.astype(x.dtype)], axis=-1) @ w_out
    return y, conv_hist, ret_state


def compress(rows, pe, w1, w2):
    b, l, g, dh = rows.shape
    r = CMP_LEN // CMP_STRIDE
    n_chunks = l // CMP_STRIDE
    n_cmp = n_chunks - r + 1
    chunks = rows[:, :n_chunks * CMP_STRIDE].reshape(b, n_chunks, CMP_STRIDE, g, dh)
    chunks = chunks.transpose(0, 1, 3, 2, 4).reshape(b, n_chunks, g, CMP_STRIDE * dh)
    w1_parts = w1.reshape(r, CMP_STRIDE * dh, CMP_HIDDEN)
    pre = pe.reshape(-1) @ w1
    for m in range(r):
        pre = pre + jnp.einsum("bngf,fh->bngh", chunks[:, m:m + n_cmp], w1_parts[m])
    return jax.nn.gelu(pre) @ w2


def cmp_attention(q, kc, vc, q_pos):
    b, t = q.shape[:2]
    n = kc.shape[1]
    qg = q.reshape(b, t, NSA_KV, NSA_GROUP, NSA_DH)
    s = jnp.einsum("btgjd,bngd->bgjtn", qg, kc).astype(jnp.float32) * (NSA_DH ** -0.5)
    blk_end = jnp.arange(n) * CMP_STRIDE + CMP_LEN - 1
    valid = blk_end[None, :] <= q_pos[:, None]
    p = jnp.where(valid, jax.nn.softmax(jnp.where(valid, s, NEG_INF), axis=-1), 0.0)
    o = jnp.einsum("bgjtn,bngd->btgjd", p.astype(vc.dtype), vc).reshape(b, t, NSA_HEADS, NSA_DH)
    return o, p


def cmp_to_slc(n_cmp, n_slc):
    i = np.arange(n_cmp)[:, None]
    j = np.arange(n_slc)[None, :]
    start = i * CMP_STRIDE
    return ((start < (j + 1) * SLC_BLOCK) & (start + CMP_LEN > j * SLC_BLOCK)).astype(np.float32)


def select_blocks(p, q_pos, n_slc):
    overlap = jnp.asarray(cmp_to_slc(p.shape[-1], n_slc))
    imp = jnp.einsum("bgjtn,nm->bgtm", p, overlap)
    blk = jnp.arange(n_slc)[None, :]
    cur = (q_pos // SLC_BLOCK)[:, None]
    causal = blk <= cur
    forced = (blk == 0) | (blk == cur) | (blk == cur - 1)
    score = jnp.where(causal, imp + jnp.where(forced, FORCE_BONUS, 0.0), NEG_INF)
    _, idx = lax.top_k(score, min(SLC_TOPN, n_slc))
    ok = jnp.take_along_axis(jnp.broadcast_to(causal, score.shape), idx, axis=-1)
    return idx, ok


def gather_blocks(blocks, idx):
    return jax.vmap(jax.vmap(lambda a, i: a[i]))(blocks, idx)


def gather_paged(pool, page_table, new, kpos, past):
    b_idx = jnp.arange(kpos.shape[0]).reshape(-1, 1, 1, 1, 1)
    g_idx = jnp.arange(NSA_KV).reshape(1, -1, 1, 1, 1)
    page = jnp.clip(kpos // PAGE_SIZE, 0, page_table.shape[1] - 1)
    row = page_table[b_idx, page] * PAGE_SIZE + kpos % PAGE_SIZE
    old = pool.reshape(-1, NSA_KV, NSA_DH)[row, g_idx]
    fresh = new[b_idx, jnp.clip(kpos - past, 0, new.shape[1] - 1), g_idx]
    return jnp.where((kpos < past)[..., None], old, fresh)


def selected_attention(q, ks, vs, q_pos, k_pos, blk_ok):
    b, t = q.shape[:2]
    kk, bl = ks.shape[3], ks.shape[4]
    qg = q.reshape(b, t, NSA_KV, NSA_GROUP, NSA_DH)
    s = jnp.einsum("btgjd,bgtkld->bgjtkl", qg, ks).astype(jnp.float32) * (NSA_DH ** -0.5)
    valid = (blk_ok[..., None] & (k_pos <= q_pos[None, None, :, None, None]))[:, :, None]
    s = jnp.where(valid, s, NEG_INF).reshape(b, NSA_KV, NSA_GROUP, t, kk * bl)
    p = jax.nn.softmax(s, axis=-1).reshape(b, NSA_KV, NSA_GROUP, t, kk, bl)
    o = jnp.einsum("bgjtkl,bgtkld->btgjd", p.astype(vs.dtype), vs)
    return o.reshape(b, t, NSA_HEADS, NSA_DH)


def window_attention(q, k, v, q_pos, k_pos):
    b, t = q.shape[:2]
    qg = q.reshape(b, t, NSA_KV, NSA_GROUP, NSA_DH)
    s = jnp.einsum("btgjd,bsgd->bgjts", qg, k).astype(jnp.float32) * (NSA_DH ** -0.5)
    dist = q_pos[:, None] - k_pos[None, :]
    valid = (dist >= 0) & (dist < WINDOW) & (k_pos[None, :] >= 0)
    p = jax.nn.softmax(jnp.where(valid, s, NEG_INF), axis=-1)
    o = jnp.einsum("bgjts,bsgd->btgjd", p.astype(v.dtype), v)
    return o.reshape(b, t, NSA_HEADS, NSA_DH)


def nsa_project(x, w_in):
    b, t, _ = x.shape
    q, kc, vc, ks, vs, kw, vw, gates = _split(x @ w_in, ODD_SPLITS)

    def kv(a):
        return a.reshape(b, t, NSA_KV, NSA_DH)

    g = jax.nn.sigmoid(gates.astype(jnp.float32)).reshape(b, t, 3, NSA_HEADS)
    return q.reshape(b, t, NSA_HEADS, NSA_DH), kv(kc), kv(vc), kv(ks), kv(vs), kv(kw), kv(vw), g


def nsa_merge(o_cmp, o_slc, o_win, g, w_out):
    o = g[:, :, 0, :, None] * o_cmp + g[:, :, 1, :, None] * o_slc + g[:, :, 2, :, None] * o_win
    b, t = o.shape[:2]
    return o.reshape(b, t, -1).astype(w_out.dtype) @ w_out


def nsa_prompt(q, kc, vc, ks, vs, kw, vw, pe, w1, w2):
    b, s = q.shape[:2]
    pos = jnp.arange(s)
    kcc = compress(kc, pe[0], w1[0], w2[0])
    vcc = compress(vc, pe[1], w1[1], w2[1])
    o_cmp, p = cmp_attention(q, kcc, vcc, pos)
    n_slc = s // SLC_BLOCK
    idx, ok = select_blocks(p, pos, n_slc)
    kb = ks.reshape(b, n_slc, SLC_BLOCK, NSA_KV, NSA_DH).transpose(0, 3, 1, 2, 4)
    vb = vs.reshape(b, n_slc, SLC_BLOCK, NSA_KV, NSA_DH).transpose(0, 3, 1, 2, 4)
    pad = ((0, 0), (WINDOW, 0), (0, 0), (0, 0))
    kwp = jnp.pad(kw, pad)
    vwp = jnp.pad(vw, pad)

    def block(i):
        t0 = i * Q_BLOCK
        qb = lax.dynamic_slice_in_dim(q, t0, Q_BLOCK, axis=1)
        qpos = t0 + jnp.arange(Q_BLOCK)
        ib = lax.dynamic_slice_in_dim(idx, t0, Q_BLOCK, axis=2)
        okb = lax.dynamic_slice_in_dim(ok, t0, Q_BLOCK, axis=2)
        kpos = ib[..., None] * SLC_BLOCK + jnp.arange(SLC_BLOCK)
        o_slc = selected_attention(qb, gather_blocks(kb, ib), gather_blocks(vb, ib), qpos, kpos, okb)
        kwb = lax.dynamic_slice_in_dim(kwp, t0, WINDOW + Q_BLOCK, axis=1)
        vwb = lax.dynamic_slice_in_dim(vwp, t0, WINDOW + Q_BLOCK, axis=1)
        o_win = window_attention(qb, kwb, vwb, qpos, t0 - WINDOW + jnp.arange(WINDOW + Q_BLOCK))
        return o_slc, o_win

    o_slc, o_win = lax.map(block, jnp.arange(s // Q_BLOCK))
    o_slc = o_slc.swapaxes(0, 1).reshape(b, s, NSA_HEADS, NSA_DH)
    o_win = o_win.swapaxes(0, 1).reshape(b, s, NSA_HEADS, NSA_DH)
    return o_cmp, o_slc, o_win


def nsa_sample(q, kc, vc, ks, vs, kw, vw, pool_ck, pool_cv, pool_sk, pool_sv, win_k, win_v, page_table, pe, w1, w2):
    b, t = q.shape[:2]
    past = page_table.shape[1] * PAGE_SIZE
    pos = past + jnp.arange(t)

    def history(pool, new):
        return jnp.concatenate([pool[page_table].reshape(b, past, NSA_KV, NSA_DH), new], axis=1)

    kcc = compress(history(pool_ck, kc), pe[0], w1[0], w2[0])
    vcc = compress(history(pool_cv, vc), pe[1], w1[1], w2[1])
    o_cmp, p = cmp_attention(q, kcc, vcc, pos)
    n_slc = -(-(past + t) // SLC_BLOCK)
    idx, ok = select_blocks(p, pos, n_slc)
    kpos = idx[..., None] * SLC_BLOCK + jnp.arange(SLC_BLOCK)
    k_sel = gather_paged(pool_sk, page_table, ks, kpos, past)
    v_sel = gather_paged(pool_sv, page_table, vs, kpos, past)
    o_slc = selected_attention(q, k_sel, v_sel, pos, kpos, ok)
    wb = win_k.shape[1]
    kw_all = jnp.concatenate([win_k, kw], axis=1)
    vw_all = jnp.concatenate([win_v, vw], axis=1)
    o_win = window_attention(q, kw_all, vw_all, pos, past - wb + jnp.arange(wb + t))
    keep = min(WINDOW, wb + t)
    return o_cmp, o_slc, o_win, kw_all[:, -keep:], vw_all[:, -keep:]


def conv_ffn(x, hist, w_up, conv_w, conv_b, w_down):
    a, gate = jnp.split(x @ w_up, 2, axis=-1)
    a_conv, hist = causal_dwconv(a, hist, conv_w, conv_b)
    return (jax.nn.gelu(a_conv) * gate) @ w_down, hist


def setup_inputs(seed: int = 0) -> dict:
    key = jax.random.key(seed)
    keys = iter(jax.random.split(key, 40))

    def nrm(shape, scale=1.0):
        return jax.random.normal(next(keys), shape, jnp.float32) * scale

    n_pages = PAST_LEN // PAGE_SIZE
    n_phys = (5 * DEC_BATCH * n_pages) // 4
    win_buf = min(WINDOW, PAST_LEN)
    pool = (N_ODD, n_phys, PAGE_SIZE, NSA_KV, NSA_DH)
    perm = jax.random.permutation(next(keys), n_phys)
    page_table = perm[:DEC_BATCH * n_pages].reshape(DEC_BATCH, n_pages).astype(jnp.int32)
    return {
        "x_prompt": nrm((BATCH, SEQ, D_MODEL)),
        "x_sample": nrm((DEC_BATCH, DEC_SEQ, D_MODEL)),
        "state_sconv": nrm((N_EVEN, DEC_BATCH, SCONV_W - 1, D_SCONV)),
        "state_ret": nrm((N_EVEN, DEC_BATCH, RET_HEADS, RET_DK, RET_DV)),
        "cache_cmp_k": nrm(pool),
        "cache_cmp_v": nrm(pool),
        "cache_slc_k": nrm(pool),
        "cache_slc_v": nrm(pool),
        "cache_win_k": nrm((N_ODD, DEC_BATCH, win_buf, NSA_KV, NSA_DH)),
        "cache_win_v": nrm((N_ODD, DEC_BATCH, win_buf, NSA_KV, NSA_DH)),
        "state_ffn_conv": nrm((DEPTH, DEC_BATCH, FFN_W - 1, D_FF)),
        "page_table": page_table,
        "w_in_even": nrm((N_EVEN, D_MODEL, D_IN_EVEN), D_MODEL ** -0.5),
        "sconv_w": nrm((N_EVEN, SCONV_W, D_SCONV), SCONV_W ** -0.5),
        "sconv_b": nrm((N_EVEN, D_SCONV), 0.01),
        "ret_gn_g": 1.0 + nrm((N_EVEN, RET_HEADS * RET_DV), 0.01),
        "w_out_even": nrm((N_EVEN, D_MIX_EVEN, D_MODEL), BETA * D_MIX_EVEN ** -0.5),
        "w_in_odd": nrm((N_ODD, D_MODEL, D_IN_ODD), D_MODEL ** -0.5),
        "cmp_pe": nrm((N_ODD, 2, CMP_LEN, NSA_DH), 0.1),
        "cmp_w1": nrm((N_ODD, 2, CMP_LEN * NSA_DH, CMP_HIDDEN), (CMP_LEN * NSA_DH) ** -0.5),
        "cmp_w2": nrm((N_ODD, 2, CMP_HIDDEN, NSA_DH), CMP_HIDDEN ** -0.5),
        "w_out_odd": nrm((N_ODD, D_MIX_ODD, D_MODEL), BETA * D_MIX_ODD ** -0.5),
        "ln_mix_g": 1.0 + nrm((DEPTH, D_MODEL), 0.01),
        "ln_mix_b": nrm((DEPTH, D_MODEL), 0.01),
        "ffn_w_up": nrm((DEPTH, D_MODEL, 2 * D_FF), D_MODEL ** -0.5),
        "ffn_conv_w": nrm((DEPTH, FFN_W, D_FF), FFN_W ** -0.5),
        "ffn_conv_b": nrm((DEPTH, D_FF), 0.01),
        "ffn_w_down": nrm((DEPTH, D_FF, D_MODEL), BETA * D_FF ** -0.5),
        "ln_ffn_g": 1.0 + nrm((DEPTH, D_MODEL), 0.01),
        "ln_ffn_b": nrm((DEPTH, D_MODEL), 0.01),
    }


def reference(x_prompt, x_sample, state_sconv, state_ret, cache_cmp_k, cache_cmp_v, cache_slc_k, cache_slc_v,
              cache_win_k, cache_win_v, state_ffn_conv, page_table,
              w_in_even, sconv_w, sconv_b, ret_gn_g, w_out_even,
              w_in_odd, cmp_pe, cmp_w1, cmp_w2, w_out_odd,
              ln_mix_g, ln_mix_b, ffn_w_up, ffn_conv_w, ffn_conv_b, ffn_w_down, ln_ffn_g, ln_ffn_b):
    xp, xs = x_prompt, x_sample
    b_p, s_p, _ = xp.shape
    b_s, t_s, _ = xs.shape
    past = page_table.shape[1] * PAGE_SIZE
    pos_p = jnp.arange(s_p)
    pos_s = past + jnp.arange(t_s)
    sconv_p, sconv_s, ret_p, ret_s = [], [], [], []
    cmp_k_p, cmp_v_p, slc_k_p, slc_v_p = [], [], [], []
    cmp_k_s, cmp_v_s, slc_k_s, slc_v_s = [], [], [], []
    win_k_p, win_v_p, win_k_s, win_v_s = [], [], [], []
    ffn_p, ffn_s = [], []
    for layer in range(DEPTH):
        if layer % 2 == 0:
            e = layer // 2
            w = (w_in_even[e], sconv_w[e], sconv_b[e], ret_gn_g[e], w_out_even[e])
            mix_p, hc, st = even_mixer(xp, pos_p, jnp.zeros((b_p, SCONV_W - 1, D_SCONV), xp.dtype),
                                       jnp.zeros((b_p, RET_HEADS, RET_DK, RET_DV), jnp.float32), *w)
            sconv_p.append(hc)
            ret_p.append(st)
            mix_s, hc, st = even_mixer(xs, pos_s, state_sconv[e], state_ret[e], *w)
            sconv_s.append(hc)
            ret_s.append(st)
        else:
            o = layer // 2
            q, kc, vc, ks, vs, kw, vw, g = nsa_project(xp, w_in_odd[o])
            oc, osl, ow = nsa_prompt(q, kc, vc, ks, vs, kw, vw, cmp_pe[o], cmp_w1[o], cmp_w2[o])
            mix_p = nsa_merge(oc, osl, ow, g, w_out_odd[o])
            keep = min(WINDOW, s_p)
            cmp_k_p.append(kc)
            cmp_v_p.append(vc)
            slc_k_p.append(ks)
            slc_v_p.append(vs)
            win_k_p.append(kw[:, -keep:])
            win_v_p.append(vw[:, -keep:])
            q, kc, vc, ks, vs, kw, vw, g = nsa_project(xs, w_in_odd[o])
            oc, osl, ow, wk, wv = nsa_sample(q, kc, vc, ks, vs, kw, vw, cache_cmp_k[o], cache_cmp_v[o],
                                             cache_slc_k[o], cache_slc_v[o], cache_win_k[o], cache_win_v[o],
                                             page_table, cmp_pe[o], cmp_w1[o], cmp_w2[o])
            mix_s = nsa_merge(oc, osl, ow, g, w_out_odd[o])
            cmp_k_s.append(kc)
            cmp_v_s.append(vc)
            slc_k_s.append(ks)
            slc_v_s.append(vs)
            win_k_s.append(wk)
            win_v_s.append(wv)
        xp = layer_norm(ALPHA * xp + mix_p, ln_mix_g[layer], ln_mix_b[layer])
        xs = layer_norm(ALPHA * xs + mix_s, ln_mix_g[layer], ln_mix_b[layer])
        wf = (ffn_w_up[layer], ffn_conv_w[layer], ffn_conv_b[layer], ffn_w_down[layer])
        f_p, h_p = conv_ffn(xp, jnp.zeros((b_p, FFN_W - 1, D_FF), xp.dtype), *wf)
        f_s, h_s = conv_ffn(xs, state_ffn_conv[layer], *wf)
        ffn_p.append(h_p)
        ffn_s.append(h_s)
        xp = layer_norm(ALPHA * xp + f_p, ln_ffn_g[layer], ln_ffn_b[layer])
        xs = layer_norm(ALPHA * xs + f_s, ln_ffn_g[layer], ln_ffn_b[layer])
    st = jnp.stack
    return (xp, xs, st(sconv_p), st(sconv_s), st(ret_p), st(ret_s),
            st(cmp_k_p), st(cmp_v_p), st(slc_k_p), st(slc_v_p),
            st(cmp_k_s), st(cmp_v_s), st(slc_k_s), st(slc_v_s),
            st(win_k_p), st(win_v_p), st(win_k_s), st(win_v_s),
            st(ffn_p), st(ffn_s))
```

```python
import functools

import numpy as np
import jax
import jax.numpy as jnp
from jax import lax
from jax.experimental import pallas as pl
from jax.experimental.pallas import tpu as pltpu

F32 = jnp.float32
BF16 = jnp.bfloat16

SUBLANES = 8
LANES = 128
VMEM_LIMIT_BYTES = 56 * 1024 * 1024

DEPTH = 2
SCONV_W = 3
RET_HEADS = 4
RET_CHUNK = 128
ROPE_BASE = 10000.0
NSA_HEADS = 16
NSA_KV = 4
NSA_GROUP = NSA_HEADS // NSA_KV
NSA_DH = 64
CMP_LEN = 32
CMP_STRIDE = 16
SLC_BLOCK = 64
SLC_TOPN = 16
WINDOW = 512
PAGE_SIZE = 128
FFN_W = 3
ALPHA = (2.0 * DEPTH) ** 0.25
LN_EPS = 1e-5
NEG_INF = -1e30
REMOVED = -3e38
FORCE_BONUS = 1e4
SLOT = 2 * NSA_DH


def _cparams(n_grid):
    return pltpu.CompilerParams(dimension_semantics=("arbitrary",) * n_grid,
                                vmem_limit_bytes=VMEM_LIMIT_BYTES)


def _row_tile(m, want):
    t = min(m, want)
    assert m % t == 0, (m, t)
    return t


def _nt_dot(a, b):
    return lax.dot_general(a, b, (((1,), (1,)), ((), ())), preferred_element_type=F32)


def _tn_dot(a, b):
    return lax.dot_general(a, b, (((0,), (0,)), ((), ())), preferred_element_type=F32)


def _dot(a, b):
    return jnp.dot(a, b, preferred_element_type=F32)


def _gelu(x):
    return 0.5 * x * (1.0 + jnp.tanh(np.float32(np.sqrt(2.0 / np.pi)) * (x + 0.044715 * (x * x * x))))


def _layer_norm_rows(r, g, b):
    mu = jnp.mean(r, axis=-1, keepdims=True)
    d = r - mu
    var = jnp.mean(d * d, axis=-1, keepdims=True)
    return d * lax.rsqrt(var + LN_EPS) * g + b


def _mm_split_kernel(x_ref, w_ref, *o_refs, cuts):
    acc = _dot(x_ref[...].astype(BF16), w_ref[...])
    for o_ref, (lo, hi) in zip(o_refs, cuts):
        o_ref[...] = acc[:, lo:hi].astype(o_ref.dtype)


def matmul_split(x, w_bf16, widths, dtypes, tm=256):
    m, k = x.shape
    n = w_bf16.shape[1]
    assert sum(widths) == n and all(wd % LANES == 0 for wd in widths)
    tm = _row_tile(m, tm)
    cuts, lo = [], 0
    for wd in widths:
        cuts.append((lo, lo + wd))
        lo += wd
    return pl.pallas_call(
        functools.partial(_mm_split_kernel, cuts=tuple(cuts)),
        grid=(m // tm,),
        in_specs=[pl.BlockSpec((tm, k), lambda i: (i, 0)),
                  pl.BlockSpec((k, n), lambda i: (0, 0))],
        out_specs=[pl.BlockSpec((tm, wd), lambda i: (i, 0)) for wd in widths],
        out_shape=[jax.ShapeDtypeStruct((m, wd), dt) for wd, dt in zip(widths, dtypes)],
        compiler_params=_cparams(1),
        name="matmul_split",
    )(x, w_bf16)


def _mm_res_ln_kernel(a_ref, w_ref, x_ref, g_ref, b_ref, o_ref):
    y = _dot(a_ref[...].astype(BF16), w_ref[...])
    o_ref[...] = _layer_norm_rows(ALPHA * x_ref[...] + y, g_ref[...], b_ref[...])


def matmul_residual_ln(a, w_bf16, x, g, b, tm=256):
    m, k = a.shape
    d = w_bf16.shape[1]
    tm = _row_tile(m, tm)
    return pl.pallas_call(
        _mm_res_ln_kernel,
        grid=(m // tm,),
        in_specs=[pl.BlockSpec((tm, k), lambda i: (i, 0)),
                  pl.BlockSpec((k, d), lambda i: (0, 0)),
                  pl.BlockSpec((tm, d), lambda i: (i, 0)),
                  pl.BlockSpec((1, d), lambda i: (0, 0)),
                  pl.BlockSpec((1, d), lambda i: (0, 0))],
        out_specs=pl.BlockSpec((tm, d), lambda i: (i, 0)),
        out_shape=jax.ShapeDtypeStruct((m, d), F32),
        compiler_params=_cparams(1),
        name="matmul_residual_ln",
    )(a, w_bf16, x, g.reshape(1, d), b.reshape(1, d))


def _expand_gates(gates_raw, e_ref):
    sig = jax.nn.sigmoid(gates_raw)
    hi = sig.astype(BF16)
    lo = (sig - hi.astype(F32)).astype(BF16)
    e = e_ref[...]
    return _dot(hi, e) + _dot(lo, e)


def _nsa_merge_ln_kernel(oc_ref, os_ref, ow_ref, gt_ref, e_ref, w_ref, x_ref, g_ref, b_ref, o_ref, *, k):
    gx = _expand_gates(gt_ref[...], e_ref)
    o = gx[:, 0:k] * oc_ref[...] + gx[:, k:2 * k] * os_ref[...] + gx[:, 2 * k:3 * k] * ow_ref[...]
    y = _dot(o.astype(BF16), w_ref[...])
    o_ref[...] = _layer_norm_rows(ALPHA * x_ref[...] + y, g_ref[...], b_ref[...])


def nsa_merge_residual_ln(oc, osl, ow, gates, e_bf16, w_bf16, x, g, b, tm=256):
    m, k = oc.shape
    d = w_bf16.shape[1]
    tm = _row_tile(m, tm)
    row = lambda i: (i, 0)
    fixed = lambda i: (0, 0)
    return pl.pallas_call(
        functools.partial(_nsa_merge_ln_kernel, k=k),
        grid=(m // tm,),
        in_specs=[pl.BlockSpec((tm, k), row), pl.BlockSpec((tm, k), row), pl.BlockSpec((tm, k), row),
                  pl.BlockSpec((tm, LANES), row),
                  pl.BlockSpec((LANES, 3 * k), fixed),
                  pl.BlockSpec((k, d), fixed),
                  pl.BlockSpec((tm, d), row),
                  pl.BlockSpec((1, d), fixed), pl.BlockSpec((1, d), fixed)],
        out_specs=pl.BlockSpec((tm, d), row),
        out_shape=jax.ShapeDtypeStruct((m, d), F32),
        compiler_params=_cparams(1),
        name="nsa_merge_residual_ln",
    )(oc, osl, ow, gates, e_bf16, w_bf16, x, g.reshape(1, d), b.reshape(1, d))


def _even_mixer_kernel(z_ref, hist_ref, st_ref, cos_ref, sin_ref, decay_ref, qdec_ref, kdec_ref, sdec_ref,
                       cw_ref, cb_ref, gn_ref, y_ref, hist_out_ref, st_out_ref, carry, state,
                       *, rows, valid, dconv, dk):
    c = pl.program_id(1)
    r0 = valid - 2 - (rows - SUBLANES)

    @pl.when(c == 0)
    def _():
        carry[r0:r0 + 2, :] = hist_ref[0]
        state[...] = st_ref[0]

    d = dconv
    h = z_ref[:, 0:d]
    gate_b = z_ref[:, d:2 * d]
    gate_c = z_ref[:, 2 * d:3 * d]
    ch = gate_c * h
    row = lax.broadcasted_iota(jnp.int32, (rows, d), 0)
    h0 = carry[r0:r0 + 1, :]
    h1 = carry[r0 + 1:r0 + 2, :]
    m1 = jnp.where(row == 0, h1, pltpu.roll(ch, 1, 0))
    m2 = jnp.where(row == 0, h0, jnp.where(row == 1, h1, pltpu.roll(ch, 2, 0)))
    u = ((cb_ref[...] + m2 * cw_ref[0:1, :]) + m1 * cw_ref[1:2, :]) + ch * cw_ref[2:3, :]
    y_ref[:, 0:d] = gate_b * u
    carry[...] = ch[rows - SUBLANES:rows, :]
    hist_out_ref[0] = carry[r0:r0 + 2, :]

    cosf = cos_ref[...]
    sinf = sin_ref[...]
    scale = np.float32(dk ** -0.5)
    for hh in range(RET_HEADS):
        q = z_ref[:, 3 * d + hh * dk:3 * d + (hh + 1) * dk]
        k = z_ref[:, 4 * d + hh * dk:4 * d + (hh + 1) * dk]
        v = z_ref[:, 5 * d + hh * dk:5 * d + (hh + 1) * dk]
        gsw = z_ref[:, 6 * d + hh * dk:6 * d + (hh + 1) * dk]
        q = (q * cosf + pltpu.roll(q, dk // 2, 1) * sinf) * scale
        k = k * cosf + pltpu.roll(k, dk // 2, 1) * sinf
        qb = q.astype(BF16)
        vb = v.astype(BF16)
        s_old = state[hh]
        scores = _nt_dot(qb, k.astype(BF16)) * decay_ref[hh]
        intra = _dot(scores.astype(BF16), vb)
        cross = _dot(qb, s_old.astype(BF16)) * qdec_ref[hh]
        kd = (k * kdec_ref[hh]).astype(BF16)
        state[hh] = s_old * sdec_ref[hh] + _tn_dot(kd, vb)
        o = intra + cross
        mu = jnp.mean(o, axis=-1, keepdims=True)
        dv = o - mu
        var = jnp.mean(dv * dv, axis=-1, keepdims=True)
        on = dv * lax.rsqrt(var + LN_EPS) * gn_ref[:, hh * dk:(hh + 1) * dk]
        y_ref[:, d + hh * dk:d + (hh + 1) * dk] = (gsw * jax.nn.sigmoid(gsw)) * on
    st_out_ref[0] = state[...]


def _retention_tables(rows, valid, dk):
    log_gamma = jnp.log1p(-jnp.exp2(-5.0 - jnp.arange(RET_HEADS, dtype=F32)))
    n = jnp.arange(rows, dtype=F32)
    diff = n[:, None] - n[None, :]
    lg = log_gamma[:, None, None]
    decay = jnp.where(diff >= 0, jnp.exp(lg * jnp.maximum(diff, 0.0)), 0.0)
    q_dec = jnp.exp((n[None, :] + 1.0) * log_gamma[:, None])
    k_dec = jnp.where(n[None, :] < valid, jnp.exp((valid - 1.0 - n[None, :]) * log_gamma[:, None]), 0.0)
    s_dec = jnp.exp(valid * log_gamma)
    bc = lambda a: jnp.broadcast_to(a[:, :, None], (RET_HEADS, rows, dk))
    return decay, bc(q_dec), bc(k_dec), jnp.broadcast_to(s_dec[:, None, None], (RET_HEADS, 1, dk))


def _rope_tables(pos, dk):
    half = dk // 2
    inv = ROPE_BASE ** (-jnp.arange(half, dtype=F32) / half)
    ang = pos.astype(F32)[:, None] * inv
    cos, sin = jnp.cos(ang), jnp.sin(ang)
    return jnp.concatenate([cos, cos], axis=-1), jnp.concatenate([-sin, sin], axis=-1)


def even_mixer(z, hist, st, pos, rows, valid, conv_w, conv_b, gn_g):
    n_seq, _, dconv = hist.shape
    dk = st.shape[-1]
    n_chunks = z.shape[0] // (n_seq * rows)
    cosf, sinf = _rope_tables(pos, dk)
    decay, q_dec, k_dec, s_dec = _retention_tables(rows, valid, dk)
    fixed3 = lambda s, c: (0, 0, 0)
    fixed2 = lambda s, c: (0, 0)
    return pl.pallas_call(
        functools.partial(_even_mixer_kernel, rows=rows, valid=valid, dconv=dconv, dk=dk),
        grid=(n_seq, n_chunks),
        in_specs=[pl.BlockSpec((rows, 7 * dconv), lambda s, c: (s * n_chunks + c, 0)),
                  pl.BlockSpec((1, 2, dconv), lambda s, c: (s, 0, 0)),
                  pl.BlockSpec((1, RET_HEADS, dk, dk), lambda s, c: (s, 0, 0, 0)),
                  pl.BlockSpec((rows, dk), lambda s, c: (c, 0)),
                  pl.BlockSpec((rows, dk), lambda s, c: (c, 0)),
                  pl.BlockSpec((RET_HEADS, rows, rows), fixed3),
                  pl.BlockSpec((RET_HEADS, rows, dk), fixed3),
                  pl.BlockSpec((RET_HEADS, rows, dk), fixed3),
                  pl.BlockSpec((RET_HEADS, 1, dk), fixed3),
                  pl.BlockSpec((SCONV_W, dconv), fixed2),
                  pl.BlockSpec((1, dconv), fixed2),
                  pl.BlockSpec((1, RET_HEADS * dk), fixed2)],
        out_specs=[pl.BlockSpec((rows, 2 * dconv), lambda s, c: (s * n_chunks + c, 0)),
                   pl.BlockSpec((1, 2, dconv), lambda s, c: (s, 0, 0)),
                   pl.BlockSpec((1, RET_HEADS, dk, dk), lambda s, c: (s, 0, 0, 0))],
        out_shape=[jax.ShapeDtypeStruct((z.shape[0], 2 * dconv), F32),
                   jax.ShapeDtypeStruct((n_seq, 2, dconv), F32),
                   jax.ShapeDtypeStruct((n_seq, RET_HEADS, dk, dk), F32)],
        scratch_shapes=[pltpu.VMEM((SUBLANES, dconv), F32), pltpu.VMEM((RET_HEADS, dk, dk), F32)],
        compiler_params=_cparams(2),
        name="even_mixer",
    )(z, hist, st, cosf, sinf, decay, q_dec, k_dec, s_dec, conv_w, conv_b.reshape(1, dconv),
      gn_g.reshape(1, RET_HEADS * dk))


def _conv_gate(a, gate, m1, m2, cw_ref, cb_ref):
    conv = ((cb_ref[...] + m2 * cw_ref[0:1, :]) + m1 * cw_ref[1:2, :]) + a * cw_ref[2:3, :]
    return _gelu(conv) * gate


def _ffn_up_seq_kernel(x_ref, wa_ref, wg_ref, h_ref, cw_ref, cb_ref, o_ref, hist_out_ref, carry, *, tm):
    @pl.when(pl.program_id(2) == 0)
    def _():
        carry[SUBLANES - 2:SUBLANES, :] = h_ref[0]

    xb = x_ref[...].astype(BF16)
    a = _dot(xb, wa_ref[...])
    gate = _dot(xb, wg_ref[...])
    row = lax.broadcasted_iota(jnp.int32, a.shape, 0)
    h0 = carry[SUBLANES - 2:SUBLANES - 1, :]
    h1 = carry[SUBLANES - 1:SUBLANES, :]
    m1 = jnp.where(row == 0, h1, pltpu.roll(a, 1, 0))
    m2 = jnp.where(row == 0, h0, jnp.where(row == 1, h1, pltpu.roll(a, 2, 0)))
    o_ref[...] = _conv_gate(a, gate, m1, m2, cw_ref, cb_ref).astype(o_ref.dtype)
    carry[...] = a[tm - SUBLANES:tm, :]
    hist_out_ref[0] = carry[SUBLANES - 2:SUBLANES, :]


def ffn_up_sequences(x, n_seq, hist, w_up_bf16, conv_w, conv_b, tm=256, n_col=2):
    m, k = x.shape
    dff = conv_w.shape[1]
    seq = m // n_seq
    tm = _row_tile(seq, tm)
    tps = seq // tm
    tn = dff // n_col
    assert tn % LANES == 0
    return pl.pallas_call(
        functools.partial(_ffn_up_seq_kernel, tm=tm),
        grid=(n_col, n_seq, tps),
        in_specs=[pl.BlockSpec((tm, k), lambda j, s, i: (s * tps + i, 0)),
                  pl.BlockSpec((k, tn), lambda j, s, i: (0, j)),
                  pl.BlockSpec((k, tn), lambda j, s, i: (0, j + n_col)),
                  pl.BlockSpec((1, 2, tn), lambda j, s, i: (s, 0, j)),
                  pl.BlockSpec((FFN_W, tn), lambda j, s, i: (0, j)),
                  pl.BlockSpec((1, tn), lambda j, s, i: (0, j))],
        out_specs=[pl.BlockSpec((tm, tn), lambda j, s, i: (s * tps + i, j)),
                   pl.BlockSpec((1, 2, tn), lambda j, s, i: (s, 0, j))],
        out_shape=[jax.ShapeDtypeStruct((m, dff), BF16),
                   jax.ShapeDtypeStruct((n_seq, 2, dff), F32)],
        scratch_shapes=[pltpu.VMEM((SUBLANES, tn), F32)],
        compiler_params=_cparams(3),
        name="ffn_up_sequences",
    )(x, w_up_bf16, w_up_bf16, hist, conv_w, conv_b.reshape(1, dff))


def _ffn_up_short_kernel(x_ref, wa_ref, wg_ref, h1_ref, h2_ref, cw_ref, cb_ref, o_ref, a_ref):
    xb = x_ref[...].astype(BF16)
    a = _dot(xb, wa_ref[...])
    gate = _dot(xb, wg_ref[...])
    t = lax.broadcasted_iota(jnp.int32, a.shape, 0) % SUBLANES
    m1 = jnp.where(t == 0, h1_ref[...], pltpu.roll(a, 1, 0))
    m2 = jnp.where(t < 2, h2_ref[...], pltpu.roll(a, 2, 0))
    o_ref[...] = _conv_gate(a, gate, m1, m2, cw_ref, cb_ref).astype(o_ref.dtype)
    a_ref[...] = a


def ffn_up_short(x, hist, w_up_bf16, conv_w, conv_b, n_col=2):
    m, k = x.shape
    dff = conv_w.shape[1]
    n_seq = m // SUBLANES
    tn = dff // n_col
    zeros = jnp.zeros((n_seq, SUBLANES, dff), F32)
    h1 = zeros.at[:, 0].set(hist[:, 1]).reshape(m, dff)
    h2 = zeros.at[:, 0].set(hist[:, 0]).at[:, 1].set(hist[:, 1]).reshape(m, dff)
    col = lambda j: (0, j)
    return pl.pallas_call(
        _ffn_up_short_kernel,
        grid=(n_col,),
        in_specs=[pl.BlockSpec((m, k), lambda j: (0, 0)),
                  pl.BlockSpec((k, tn), col),
                  pl.BlockSpec((k, tn), lambda j: (0, j + n_col)),
                  pl.BlockSpec((m, tn), col), pl.BlockSpec((m, tn), col),
                  pl.BlockSpec((FFN_W, tn), col), pl.BlockSpec((1, tn), col)],
        out_specs=[pl.BlockSpec((m, tn), col), pl.BlockSpec((m, tn), col)],
        out_shape=[jax.ShapeDtypeStruct((m, dff), F32), jax.ShapeDtypeStruct((m, dff), F32)],
        compiler_params=_cparams(1),
        name="ffn_up_short",
    )(x, w_up_bf16, w_up_bf16, h1, h2, conv_w, conv_b.reshape(1, dff))


def _compress_kernel(pt_ref, *refs, pages):
    page_refs = refs[:pages + 1]
    w16_ref, pecol_ref, w1_ref, w2_ref, o_ref = refs[pages + 1:]
    cpp = PAGE_SIZE // CMP_STRIDE
    n = (pages + 1) * cpp
    hidden = w1_ref.shape[1]
    gpr = LANES // NSA_DH
    pieces = NSA_KV // gpr
    lane_group = lax.broadcasted_iota(jnp.int32, (n, LANES), 1) // NSA_DH
    acc = [jnp.zeros((n, 2 * hidden), F32) for _ in range(NSA_KV)]
    for tok in range(CMP_STRIDE):
        w = w16_ref[tok]
        for pc in range(pieces):
            rows = jnp.concatenate(
                [r[0, pl.ds(tok * pieces + pc, cpp, stride=CMP_STRIDE * pieces), :] for r in page_refs], axis=0)
            for gl in range(gpr):
                g = pc * gpr + gl
                acc[g] = acc[g] + _dot(jnp.where(lane_group == gl, rows, 0.0).astype(BF16), w)
    pe_term = jnp.sum(pecol_ref[...] * w1_ref[...], axis=0, keepdims=True)
    for g in range(NSA_KV):
        a = acc[g]
        nxt = pltpu.roll(a, n - 1, 0)
        pre = pe_term + a[:, 0:hidden]
        pre = pre + nxt[:, hidden:2 * hidden]
        o_ref[0, g] = _dot(_gelu(pre[0:pages * cpp]).astype(BF16), w2_ref[...])


def compress(rows_paged, page_table, pe, w1, w2, pages=16):
    n_seq, n_pages = page_table.shape
    pages = min(pages, n_pages)
    assert n_pages % pages == 0
    hidden = w1.shape[1]
    gd = NSA_KV * NSA_DH
    cpp = PAGE_SIZE // CMP_STRIDE
    r = CMP_LEN // CMP_STRIDE
    assert r == 2
    w1p = w1.reshape(r, CMP_STRIDE, NSA_DH, hidden)
    w16 = jnp.concatenate([w1p[0], w1p[1]], axis=-1)
    w16 = jnp.tile(w16, (1, LANES // NSA_DH, 1)).astype(BF16)
    pieces = gd // LANES
    rows_paged = rows_paged.reshape(rows_paged.shape[0], PAGE_SIZE * pieces, LANES)
    w2p = jnp.pad(w2, ((0, 0), (0, SLOT - NSA_DH))).astype(BF16)
    pecol = pe.reshape(CMP_LEN * NSA_DH, 1)

    def page_map(i):
        return lambda s, j, pt: (pt[s, jnp.minimum(j * pages + i, n_pages - 1)], 0, 0)

    fixed2 = lambda s, j, pt: (0, 0)
    grid_spec = pltpu.PrefetchScalarGridSpec(
        num_scalar_prefetch=1,
        grid=(n_seq, n_pages // pages),
        in_specs=[pl.BlockSpec((1, PAGE_SIZE * pieces, LANES), page_map(i)) for i in range(pages + 1)] + [
            pl.BlockSpec((CMP_STRIDE, LANES, 2 * hidden), lambda s, j, pt: (0, 0, 0)),
            pl.BlockSpec((CMP_LEN * NSA_DH, 1), fixed2),
            pl.BlockSpec((CMP_LEN * NSA_DH, hidden), fixed2),
            pl.BlockSpec((hidden, SLOT), fixed2)],
        out_specs=pl.BlockSpec((1, NSA_KV, pages * cpp, SLOT), lambda s, j, pt: (s, 0, j, 0)),
    )
    return pl.pallas_call(
        functools.partial(_compress_kernel, pages=pages),
        grid_spec=grid_spec,
        out_shape=jax.ShapeDtypeStruct((n_seq, NSA_KV, n_pages * cpp, SLOT), F32),
        compiler_params=_cparams(2),
        name="compress",
    )(page_table, *([rows_paged] * (pages + 1)), w16, pecol, w1, w2p)


def _cmp_select_kernel(q_ref, kc_ref, vc_ref, ov_ref, o_ref, sel_ref, idx_ref, *, tq, n_cmp, n_slc, lane_off, pos0):
    t0 = pos0 + pl.program_id(2) * tq
    ncp = kc_ref.shape[2]
    nsp = ov_ref.shape[1]
    kc = kc_ref[0, 0].astype(BF16)
    vc = vc_ref[0, 0].astype(BF16)
    q_pos = t0 + lax.broadcasted_iota(jnp.int32, (tq, ncp), 0)
    blk_i = lax.broadcasted_iota(jnp.int32, (tq, ncp), 1)
    valid = (blk_i * CMP_STRIDE + (CMP_LEN - 1) <= q_pos) & (blk_i < n_cmp)
    p_sum = jnp.zeros((tq, ncp), F32)
    for j in range(NSA_GROUP):
        qj = q_ref[:, j * SLOT:(j + 1) * SLOT].astype(BF16)
        s = jnp.where(valid, _nt_dot(qj, kc), NEG_INF)
        m = jnp.max(s, axis=-1, keepdims=True)
        e = jnp.where(valid, jnp.exp(s - m), 0.0)
        den = jnp.sum(e, axis=-1, keepdims=True)
        p = e / jnp.where(den > 0.0, den, 1.0)
        o_ref[:, j * SLOT:(j + 1) * SLOT] = _dot(p.astype(BF16), vc)
        p_sum = p_sum + p
    hi = p_sum.astype(BF16)
    lo = (p_sum - hi.astype(F32)).astype(BF16)
    ov = ov_ref[...]
    imp = _dot(hi, ov) + _dot(lo, ov)
    lane = lax.broadcasted_iota(jnp.int32, (tq, nsp), 1)
    blk = lane - lane_off
    cur = (t0 + lax.broadcasted_iota(jnp.int32, (tq, nsp), 0)) // SLC_BLOCK
    real = (blk >= 0) & (blk < n_slc)
    causal = real & (blk <= cur)
    forced = (blk == 0) | (blk == cur) | (blk == cur - 1)
    score = jnp.where(causal, imp + jnp.where(forced, FORCE_BONUS, 0.0), NEG_INF)
    score = jnp.where(real, score, REMOVED)
    picked = jnp.zeros((tq, nsp), jnp.bool_)
    idx = jnp.zeros((tq, LANES), jnp.int32)
    idx_lane = lax.broadcasted_iota(jnp.int32, (tq, LANES), 1)
    for it in range(min(SLC_TOPN, n_slc)):
        m = jnp.max(score, axis=-1, keepdims=True)
        first = jnp.min(jnp.where(score == m, lane, nsp), axis=-1, keepdims=True)
        hit = lane == first
        picked = picked | hit
        score = jnp.where(hit, REMOVED, score)
        idx = jnp.where(idx_lane == it, first - lane_off, idx)
    sel_ref[0, 0] = jnp.where(real & ~(picked & causal), NEG_INF, 0.0)
    idx_ref[0, 0] = idx


def cmp_block_overlap(n_cmp_pad, n_cmp, n_slc, n_slc_pad, lane_off):
    i = np.arange(n_cmp_pad)[:, None]
    j = np.arange(n_slc_pad)[None, :] - lane_off
    start = i * CMP_STRIDE
    hit = (start < (j + 1) * SLC_BLOCK) & (start + CMP_LEN > j * SLC_BLOCK) & (i < n_cmp) & (j >= 0) & (j < n_slc)
    return jnp.asarray(hit.astype(np.float32), dtype=BF16)


def cmp_attention_select(q_slots, kcc, vcc, n_seq, n_cmp, n_slc, lane_off, pos0, tq):
    tokens = q_slots.shape[0]
    t = tokens // n_seq
    tq = _row_tile(t, tq)
    nt = t // tq
    ncp = kcc.shape[2]
    nsp = -(-(lane_off + n_slc) // LANES) * LANES
    ov = cmp_block_overlap(ncp, n_cmp, n_slc, nsp, lane_off)
    gw = NSA_GROUP * SLOT
    return pl.pallas_call(
        functools.partial(_cmp_select_kernel, tq=tq, n_cmp=n_cmp, n_slc=n_slc, lane_off=lane_off, pos0=pos0),
        grid=(n_seq, NSA_KV, nt),
        in_specs=[pl.BlockSpec((tq, gw), lambda s, g, i: (s * nt + i, g)),
                  pl.BlockSpec((1, 1, ncp, SLOT), lambda s, g, i: (s, g, 0, 0)),
                  pl.BlockSpec((1, 1, ncp, SLOT), lambda s, g, i: (s, g, 0, 0)),
                  pl.BlockSpec((ncp, nsp), lambda s, g, i: (0, 0))],
        out_specs=[pl.BlockSpec((tq, gw), lambda s, g, i: (s * nt + i, g)),
                   pl.BlockSpec((1, 1, tq, nsp), lambda s, g, i: (s, g, i, 0)),
                   pl.BlockSpec((1, 1, tq, LANES), lambda s, g, i: (s, g, i, 0))],
        out_shape=[jax.ShapeDtypeStruct((tokens, NSA_HEADS * SLOT), F32),
                   jax.ShapeDtypeStruct((n_seq, NSA_KV, t, nsp), F32),
                   jax.ShapeDtypeStruct((n_seq, NSA_KV, t, LANES), jnp.int32)],
        compiler_params=_cparams(3),
        name="cmp_attention_select",
    )(q_slots, kcc, vcc, ov)


def _prompt_slc_win_kernel(q_ref, sel_ref, ks_ref, vs_ref, kw_ref, vw_ref, os_ref, ow_ref, *, tq, seq):
    qi = pl.program_id(2)
    t0 = qi * tq
    rows = NSA_GROUP * tq
    sel = sel_ref[0, 0]
    q_plain = jnp.concatenate([q_ref[:, j * SLOT:(j + 1) * SLOT] for j in range(NSA_GROUP)], axis=0)
    q_aug = jnp.concatenate([(q_ref[:, j * SLOT:(j + 1) * SLOT].astype(F32) + sel).astype(BF16)
                             for j in range(NSA_GROUP)], axis=0)
    q_pos = t0 + (lax.broadcasted_iota(jnp.int32, (rows, tq), 0) & (tq - 1))

    def tile(kt, carry, diagonal):
        m, l, acc = carry
        start = pl.multiple_of(kt * tq, tq)
        s = _nt_dot(q_aug, ks_ref[pl.ds(start, tq), :])
        if diagonal:
            k_pos = start + lax.broadcasted_iota(jnp.int32, (rows, tq), 1)
            s = jnp.where(k_pos <= q_pos, s, NEG_INF)
        m_new = jnp.maximum(m, jnp.max(s, axis=-1, keepdims=True))
        alpha = jnp.exp(m - m_new)
        p = jnp.exp(s - m_new)
        l = alpha * l + jnp.sum(p, axis=-1, keepdims=True)
        acc = alpha * acc + _dot(p.astype(BF16), vs_ref[pl.ds(start, tq), :])
        return m_new, l, acc

    init = (jnp.full((rows, 1), NEG_INF, F32), jnp.zeros((rows, 1), F32), jnp.zeros((rows, SLOT), F32))
    carry = lax.fori_loop(0, qi, lambda kt, c: tile(kt, c, False), init)
    _, l, acc = tile(qi, carry, True)
    o = acc / l
    for j in range(NSA_GROUP):
        os_ref[:, j * SLOT:(j + 1) * SLOT] = o[j * tq:(j + 1) * tq]

    wk = WINDOW + tq
    k0 = pl.multiple_of(jnp.clip(t0 - WINDOW, 0, seq - wk), tq)
    s = _nt_dot(q_plain, kw_ref[pl.ds(k0, wk), :])
    dist = (t0 + (lax.broadcasted_iota(jnp.int32, (rows, wk), 0) & (tq - 1))) - (
        k0 + lax.broadcasted_iota(jnp.int32, (rows, wk), 1))
    s = jnp.where((dist >= 0) & (dist < WINDOW), s, NEG_INF)
    m = jnp.max(s, axis=-1, keepdims=True)
    p = jnp.exp(s - m)
    l = jnp.sum(p, axis=-1, keepdims=True)
    o = _dot(p.astype(BF16), vw_ref[pl.ds(k0, wk), :]) / l
    for j in range(NSA_GROUP):
        ow_ref[:, j * SLOT:(j + 1) * SLOT] = o[j * tq:(j + 1) * tq]


def prompt_slc_win_attention(q_slots, sel, ks_aug, vs_slots, kw_slots, vw_slots, n_seq, tq):
    tokens = q_slots.shape[0]
    seq = tokens // n_seq
    tq = _row_tile(seq, tq)
    assert tq & (tq - 1) == 0 and WINDOW % tq == 0 and seq >= WINDOW + tq
    nt = seq // tq
    gw = NSA_GROUP * SLOT
    kv_spec = pl.BlockSpec((seq, SLOT), lambda s, g, i: (s, g))
    return pl.pallas_call(
        functools.partial(_prompt_slc_win_kernel, tq=tq, seq=seq),
        grid=(n_seq, NSA_KV, nt),
        in_specs=[pl.BlockSpec((tq, gw), lambda s, g, i: (s * nt + i, g)),
                  pl.BlockSpec((1, 1, tq, SLOT), lambda s, g, i: (s, g, i, 0)),
                  kv_spec, kv_spec, kv_spec, kv_spec],
        out_specs=[pl.BlockSpec((tq, gw), lambda s, g, i: (s * nt + i, g)),
                   pl.BlockSpec((tq, gw), lambda s, g, i: (s * nt + i, g))],
        out_shape=[jax.ShapeDtypeStruct((tokens, NSA_HEADS * SLOT), F32),
                   jax.ShapeDtypeStruct((tokens, NSA_HEADS * SLOT), F32)],
        compiler_params=_cparams(3),
        name="prompt_slc_win_attention",
    )(q_slots, sel, ks_aug, vs_slots, kw_slots, vw_slots)


def _sample_slc_kernel(idx_ref, pt_ref, q_ref, kn_ref, vn_ref, *refs, topn, past, t_pad):
    k_refs = refs[:topn]
    v_refs = refs[topn:2 * topn]
    o_ref = refs[2 * topn]
    s_id, g_id, t_id = pl.program_id(0), pl.program_id(1), pl.program_id(2)
    base = ((s_id * NSA_KV + g_id) * t_pad + t_id) * topn
    q_pos = past + t_id
    cur = q_pos // SLC_BLOCK
    first_new = past // SLC_BLOCK
    q = q_ref[0, 0, 0].astype(BF16)
    kb = jnp.concatenate([r[0] for r in k_refs], axis=0).astype(BF16)
    vb = jnp.concatenate([r[0] for r in v_refs], axis=0).astype(BF16)
    n_keys = topn * SLC_BLOCK
    lane = lax.broadcasted_iota(jnp.int32, (1, n_keys), 1)
    slot = lane // SLC_BLOCK
    k_pos = lane % SLC_BLOCK
    limit = jnp.zeros((1, n_keys), jnp.int32)
    n_new = jnp.int32(0)
    for kk in range(topn):
        b = idx_ref[base + kk]
        k_pos = jnp.where(slot == kk, k_pos + b * SLC_BLOCK, k_pos)
        limit = jnp.where(slot == kk, jnp.where(b <= cur, jnp.minimum(q_pos, past - 1), -1), limit)
        n_new = n_new + jnp.where(b == first_new, 1, 0)
    valid = k_pos <= limit
    s_old = jnp.where(valid, _nt_dot(q, kb), NEG_INF)
    kn = kn_ref[0].astype(BF16)
    new_lane = lax.broadcasted_iota(jnp.int32, (1, SUBLANES), 1)
    valid_new = past + new_lane <= jnp.where(n_new > 0, q_pos, past - 1)
    s_new = jnp.where(valid_new, _nt_dot(q, kn), NEG_INF)
    m = jnp.maximum(jnp.max(s_old, axis=-1, keepdims=True), jnp.max(s_new, axis=-1, keepdims=True))
    p_old = jnp.exp(s_old - m)
    p_new = jnp.exp(s_new - m)
    l = jnp.sum(p_old, axis=-1, keepdims=True) + jnp.sum(p_new, axis=-1, keepdims=True)
    o = _dot(p_old.astype(BF16), vb) + _dot(p_new.astype(BF16), vn_ref[0].astype(BF16))
    o_ref[0, 0, 0] = o / l


def sample_slc_attention(q_rows, idx, page_table, pool_k, pool_v, k_new, v_new, t_real, past):
    n_seq, _, t_pad, _, gd = q_rows.shape
    topn = idx.shape[-1]
    n_pages = page_table.shape[1]
    assert past % SLC_BLOCK == 0 and t_real <= SUBLANES and PAGE_SIZE == 2 * SLC_BLOCK
    last_old = past // SLC_BLOCK - 1

    def blk_map(kk):
        def f(s, g, t, idx_ref, pt_ref):
            b = jnp.clip(idx_ref[((s * NSA_KV + g) * t_pad + t) * topn + kk], 0, last_old)
            return (pt_ref[s * n_pages + b // 2] * 2 + b % 2, 0, 0)
        return f

    blk_specs = [pl.BlockSpec((1, SLC_BLOCK, gd), blk_map(kk)) for kk in range(topn)]
    new_spec = pl.BlockSpec((1, SUBLANES, gd), lambda s, g, t, i_r, p_r: (s, 0, 0))
    grid_spec = pltpu.PrefetchScalarGridSpec(
        num_scalar_prefetch=2,
        grid=(n_seq, NSA_KV, t_real),
        in_specs=[pl.BlockSpec((1, 1, 1, SUBLANES, gd), lambda s, g, t, i_r, p_r: (s, g, t, 0, 0)),
                  new_spec, new_spec] + blk_specs + blk_specs,
        out_specs=pl.BlockSpec((1, 1, 1, SUBLANES, gd), lambda s, g, t, i_r, p_r: (s, g, t, 0, 0)),
    )
    return pl.pallas_call(
        functools.partial(_sample_slc_kernel, topn=topn, past=past, t_pad=t_pad),
        grid_spec=grid_spec,
        out_shape=jax.ShapeDtypeStruct((n_seq, NSA_KV, t_real, SUBLANES, gd), F32),
        compiler_params=_cparams(3),
        name="sample_slc_attention",
    )(idx.reshape(-1), page_table.reshape(-1), q_rows, k_new, v_new, *([pool_k] * topn), *([pool_v] * topn))


def _sample_win_kernel(q_ref, wk_ref, wv_ref, kn_ref, vn_ref, o_ref, *, past, t_pad):
    rows = t_pad * SUBLANES
    wb = wk_ref.shape[1]
    q = q_ref[0, 0].reshape(rows, q_ref.shape[-1]).astype(BF16)
    q_pos = past + lax.broadcasted_iota(jnp.int32, (rows, 1), 0) // SUBLANES
    k_pos = past - wb + lax.broadcasted_iota(jnp.int32, (1, wb), 1)
    dist = q_pos - k_pos
    valid = (dist >= 0) & (dist < WINDOW) & (k_pos >= 0)
    s_old = jnp.where(valid, _nt_dot(q, wk_ref[0].astype(BF16)), NEG_INF)
    n_pos = past + lax.broadcasted_iota(jnp.int32, (1, SUBLANES), 1)
    dist_n = q_pos - n_pos
    valid_n = (dist_n >= 0) & (dist_n < WINDOW)
    s_new = jnp.where(valid_n, _nt_dot(q, kn_ref[0].astype(BF16)), NEG_INF)
    m = jnp.maximum(jnp.max(s_old, axis=-1, keepdims=True), jnp.max(s_new, axis=-1, keepdims=True))
    p_old = jnp.exp(s_old - m)
    p_new = jnp.exp(s_new - m)
    l = jnp.sum(p_old, axis=-1, keepdims=True) + jnp.sum(p_new, axis=-1, keepdims=True)
    o = _dot(p_old.astype(BF16), wv_ref[0].astype(BF16)) + _dot(p_new.astype(BF16), vn_ref[0].astype(BF16))
    o_ref[0, 0] = (o / l).reshape(t_pad, SUBLANES, o.shape[-1])


def sample_win_attention(q_rows, win_k, win_v, k_new, v_new, past):
    n_seq, _, t_pad, _, gd = q_rows.shape
    wb = win_k.shape[1]
    q_spec = pl.BlockSpec((1, 1, t_pad, SUBLANES, gd), lambda s, g: (s, g, 0, 0, 0))
    win_spec = pl.BlockSpec((1, wb, gd), lambda s, g: (s, 0, 0))
    new_spec = pl.BlockSpec((1, SUBLANES, gd), lambda s, g: (s, 0, 0))
    return pl.pallas_call(
        functools.partial(_sample_win_kernel, past=past, t_pad=t_pad),
        grid=(n_seq, NSA_KV),
        in_specs=[q_spec, win_spec, win_spec, new_spec, new_spec],
        out_specs=q_spec,
        out_shape=jax.ShapeDtypeStruct(q_rows.shape, F32),
        compiler_params=_cparams(2),
        name="sample_win_attention",
    )(q_rows, win_k, win_v, k_new, v_new)


def _to_slots(a):
    lead = a.shape[:-1]
    n = a.shape[-1] // NSA_DH
    a = a.reshape(*lead, n, NSA_DH)
    a = jnp.pad(a, [(0, 0)] * (a.ndim - 1) + [(0, SLOT - NSA_DH)])
    return a.reshape(*lead, n * SLOT)


def _odd_weights(w_in, w_out):
    d = w_in.shape[0]
    hq = NSA_HEADS * NSA_DH
    kvw = NSA_KV * NSA_DH
    wq = _to_slots(w_in[:, :hq] * np.float32(NSA_DH ** -0.5))
    wkv = w_in[:, hq:hq + 6 * kvw]
    wg = jnp.pad(w_in[:, hq + 6 * kvw:], ((0, 0), (0, LANES - 3 * NSA_HEADS)))
    w_big = jnp.concatenate([wq, wkv, wg], axis=1).astype(BF16)
    wo = jnp.pad(w_out.reshape(NSA_HEADS, NSA_DH, d), ((0, 0), (0, SLOT - NSA_DH), (0, 0)))
    wo = wo.reshape(NSA_HEADS * SLOT, d).astype(BF16)
    k = NSA_HEADS * SLOT
    e = np.zeros((LANES, 3 * k), np.float32)
    for c in range(3):
        for h in range(NSA_HEADS):
            e[c * NSA_HEADS + h, c * k + h * SLOT:c * k + (h + 1) * SLOT] = 1.0
    return w_big, wo, jnp.asarray(e, dtype=BF16)


def _group_rows(q_slots, n_seq, t_pad):
    q = q_slots.reshape(n_seq, t_pad, NSA_KV, NSA_GROUP, SLOT)[..., :NSA_DH]
    q = q.transpose(0, 2, 1, 3, 4)
    eye = jnp.eye(NSA_KV, dtype=q.dtype)
    q = q[:, :, :, :, None, :] * eye[None, :, None, None, :, None]
    q = q.reshape(n_seq, NSA_KV, t_pad, NSA_GROUP, NSA_KV * NSA_DH)
    return jnp.pad(q, ((0, 0), (0, 0), (0, 0), (0, SUBLANES - NSA_GROUP), (0, 0)))


def _ungroup_rows(o, n_seq, t_pad):
    t = o.shape[2]
    o = o[:, :, :, :NSA_GROUP].reshape(n_seq, NSA_KV, t, NSA_GROUP, NSA_KV, NSA_DH)
    o = jnp.stack([o[:, g, :, :, g] for g in range(NSA_KV)], axis=2)
    o = jnp.pad(o, ((0, 0), (0, t_pad - t), (0, 0), (0, 0), (0, SLOT - NSA_DH)))
    return o.reshape(n_seq * t_pad, NSA_HEADS * SLOT)


def _pad_rows(a, t_pad):
    return jnp.pad(a, ((0, 0), (0, t_pad - a.shape[1])) + ((0, 0),) * (a.ndim - 2))


def kernel(x_prompt, x_sample, state_sconv, state_ret, cache_cmp_k, cache_cmp_v, cache_slc_k, cache_slc_v,
           cache_win_k, cache_win_v, state_ffn_conv, page_table,
           w_in_even, sconv_w, sconv_b, ret_gn_g, w_out_even,
           w_in_odd, cmp_pe, cmp_w1, cmp_w2, w_out_odd,
           ln_mix_g, ln_mix_b, ffn_w_up, ffn_conv_w, ffn_conv_b, ffn_w_down, ln_ffn_g, ln_ffn_b):
    b_p, s_p, d_model = x_prompt.shape
    b_s, t_s, _ = x_sample.shape
    n_pages = page_table.shape[1]
    past = n_pages * PAGE_SIZE
    t_pad = SUBLANES
    assert t_s <= t_pad and t_s >= SCONV_W - 1 and t_s < CMP_STRIDE and past % PAGE_SIZE == 0
    assert s_p % RET_CHUNK == 0 and s_p % PAGE_SIZE == 0
    d_sconv = sconv_w.shape[-1]
    d_ff = ffn_conv_w.shape[-1]
    gd = NSA_KV * NSA_DH
    depth = ln_mix_g.shape[0]

    xp = x_prompt.reshape(b_p * s_p, d_model)
    xs = _pad_rows(x_sample, t_pad).reshape(b_s * t_pad, d_model)
    outs = {k: [] for k in ("sconv_p", "sconv_s", "ret_p", "ret_s", "cmp_k_p", "cmp_v_p", "slc_k_p", "slc_v_p",
                            "cmp_k_s", "cmp_v_s", "slc_k_s", "slc_v_s", "win_k_p", "win_v_p", "win_k_s",
                            "win_v_s", "ffn_p", "ffn_s")}

    for layer in range(depth):
        if layer % 2 == 0:
            e = layer // 2
            w_in = w_in_even[e].astype(BF16)
            w_out = w_out_even[e].astype(BF16)
            n_in = w_in.shape[1]
            (zp,) = matmul_split(xp, w_in, [n_in], [F32])
            yp, hc, st = even_mixer(zp, jnp.zeros((b_p, SCONV_W - 1, d_sconv), F32),
                                    jnp.zeros((b_p,) + state_ret.shape[2:], F32), jnp.arange(s_p),
                                    RET_CHUNK, RET_CHUNK, sconv_w[e], sconv_b[e], ret_gn_g[e])
            outs["sconv_p"].append(hc)
            outs["ret_p"].append(st)
            xp = matmul_residual_ln(yp, w_out, xp, ln_mix_g[layer], ln_mix_b[layer])
            (zs,) = matmul_split(xs, w_in, [n_in], [F32])
            ys, hc, st = even_mixer(zs, state_sconv[e], state_ret[e], past + jnp.arange(t_pad),
                                    t_pad, t_s, sconv_w[e], sconv_b[e], ret_gn_g[e])
            outs["sconv_s"].append(hc)
            outs["ret_s"].append(st)
            xs = matmul_residual_ln(ys, w_out, xs, ln_mix_g[layer], ln_mix_b[layer])
        else:
            o = layer // 2
            w_big, w_out, e_gate = _odd_weights(w_in_odd[o], w_out_odd[o])
            widths = [NSA_HEADS * SLOT, 6 * gd, LANES]
            pe, w1, w2 = cmp_pe[o], cmp_w1[o], cmp_w2[o]
            qp, kvp, gp = matmul_split(xp, w_big, widths, [BF16, F32, F32])
            kc, vc, ks, vs, kw, vw = [kvp[:, i * gd:(i + 1) * gd] for i in range(6)]
            as_cache = lambda a: a.reshape(b_p, s_p, NSA_KV, NSA_DH)
            keep = min(WINDOW, s_p)
            outs["cmp_k_p"].append(as_cache(kc))
            outs["cmp_v_p"].append(as_cache(vc))
            outs["slc_k_p"].append(as_cache(ks))
            outs["slc_v_p"].append(as_cache(vs))
            outs["win_k_p"].append(as_cache(kw)[:, -keep:])
            outs["win_v_p"].append(as_cache(vw)[:, -keep:])
            pages_p = s_p // PAGE_SIZE
            ident = jnp.arange(b_p * pages_p, dtype=jnp.int32).reshape(b_p, pages_p)
            kcc = compress(kc.reshape(b_p * pages_p, PAGE_SIZE, gd), ident, pe[0], w1[0], w2[0])
            vcc = compress(vc.reshape(b_p * pages_p, PAGE_SIZE, gd), ident, pe[1], w1[1], w2[1])
            n_cmp = s_p // CMP_STRIDE - CMP_LEN // CMP_STRIDE + 1
            n_slc = s_p // SLC_BLOCK
            assert n_slc <= SLOT - NSA_DH
            oc, sel, _ = cmp_attention_select(qp, kcc, vcc, b_p, n_cmp, n_slc, NSA_DH, 0, 256)
            blk_onehot = jax.nn.one_hot(jnp.arange(s_p) // SLC_BLOCK, SLOT - NSA_DH, dtype=F32)
            ks_aug = jnp.concatenate(
                [ks.reshape(b_p, s_p, NSA_KV, NSA_DH),
                 jnp.broadcast_to(blk_onehot[None, :, None, :], (b_p, s_p, NSA_KV, SLOT - NSA_DH))],
                axis=-1).reshape(b_p * s_p, NSA_KV * SLOT).astype(BF16)
            osl, ow = prompt_slc_win_attention(qp, sel, ks_aug, _to_slots(vs).astype(BF16),
                                               _to_slots(kw).astype(BF16), _to_slots(vw).astype(BF16), b_p, 256)
            xp = nsa_merge_residual_ln(oc, osl, ow, gp, e_gate, w_out, xp, ln_mix_g[layer], ln_mix_b[layer])
            qs, kvs, gs = matmul_split(xs, w_big, widths, [F32, F32, F32])
            kc, vc, ks, vs, kw, vw = [kvs[:, i * gd:(i + 1) * gd].reshape(b_s, t_pad, gd) for i in range(6)]
            new_rows = lambda a: a[:, :t_s].reshape(b_s, t_s, NSA_KV, NSA_DH)
            outs["cmp_k_s"].append(new_rows(kc))
            outs["cmp_v_s"].append(new_rows(vc))
            outs["slc_k_s"].append(new_rows(ks))
            outs["slc_v_s"].append(new_rows(vs))
            win_k = cache_win_k[o]
            win_v = cache_win_v[o]
            wb = win_k.shape[1]
            keep = min(WINDOW, wb + t_s)
            outs["win_k_s"].append(jnp.concatenate([win_k, new_rows(kw)], axis=1)[:, -keep:])
            outs["win_v_s"].append(jnp.concatenate([win_v, new_rows(vw)], axis=1)[:, -keep:])
            n_phys = cache_cmp_k.shape[1]
            paged = lambda c: c[o].reshape(n_phys, PAGE_SIZE, gd)
            kcc = compress(paged(cache_cmp_k), page_table, pe[0], w1[0], w2[0])
            vcc = compress(paged(cache_cmp_v), page_table, pe[1], w1[1], w2[1])
            n_cmp = (past + t_s) // CMP_STRIDE - CMP_LEN // CMP_STRIDE + 1
            n_slc = -(-(past + t_s) // SLC_BLOCK)
            oc, _, idx = cmp_attention_select(qs, kcc, vcc, b_s, n_cmp, n_slc, 0, past, t_pad)
            topn = min(SLC_TOPN, n_slc)
            q_rows = _group_rows(qs, b_s, t_pad)
            halves = lambda c: c[o].reshape(n_phys * 2, SLC_BLOCK, gd)
            osl = sample_slc_attention(q_rows, idx[..., :topn], page_table, halves(cache_slc_k),
                                       halves(cache_slc_v), ks, vs, t_s, past)
            ow = sample_win_attention(q_rows, win_k.reshape(b_s, wb, gd), win_v.reshape(b_s, wb, gd), kw, vw, past)
            xs = nsa_merge_residual_ln(oc, _ungroup_rows(osl, b_s, t_pad), _ungroup_rows(ow, b_s, t_pad), gs,
                                       e_gate, w_out, xs, ln_mix_g[layer], ln_mix_b[layer])
        w_up = ffn_w_up[layer].astype(BF16)
        w_down = ffn_w_down[layer].astype(BF16)
        hp, hist_p = ffn_up_sequences(xp, b_p, jnp.zeros((b_p, FFN_W - 1, d_ff), F32), w_up,
                                      ffn_conv_w[layer], ffn_conv_b[layer])
        outs["ffn_p"].append(hist_p)
        xp = matmul_residual_ln(hp, w_down, xp, ln_ffn_g[layer], ln_ffn_b[layer])
        hs, a_s = ffn_up_short(xs, state_ffn_conv[layer], w_up, ffn_conv_w[layer], ffn_conv_b[layer])
        outs["ffn_s"].append(a_s.reshape(b_s, t_pad, d_ff)[:, t_s - (FFN_W - 1):t_s])
        xs = matmul_residual_ln(hs, w_down, xs, ln_ffn_g[layer], ln_ffn_b[layer])

    st = jnp.stack
    y_p = xp.reshape(b_p, s_p, d_model)
    y_s = xs.reshape(b_s, t_pad, d_model)[:, :t_s]
    order = ("sconv_p", "sconv_s", "ret_p", "ret_s", "cmp_k_p", "cmp_v_p", "slc_k_p", "slc_v_p",
             "cmp_k_s", "cmp_v_s", "slc_k_s", "slc_v_s", "win_k_p", "win_v_p", "win_k_s", "win_v_s",
             "ffn_p", "ffn_s")
    return (y_p, y_s) + tuple(st(outs[k]) for k in order)
```

```python
import functools

import numpy as np
import jax
import jax.numpy as jnp
from jax import lax
from jax.experimental import pallas as pl
from jax.experimental.pallas import tpu as pltpu

F32 = jnp.float32
BF16 = jnp.bfloat16

SUBLANES = 8
LANES = 128
VMEM_LIMIT_BYTES = 56 * 1024 * 1024

DEPTH = 2
SCONV_W = 3
RET_HEADS = 4
RET_CHUNK = 128
ROPE_BASE = 10000.0
NSA_HEADS = 16
NSA_KV = 4
NSA_GROUP = NSA_HEADS // NSA_KV
NSA_DH = 64
CMP_LEN = 32
CMP_STRIDE = 16
SLC_BLOCK = 64
SLC_TOPN = 16
WINDOW = 512
PAGE_SIZE = 128
FFN_W = 3
ALPHA = (2.0 * DEPTH) ** 0.25
LN_EPS = 1e-5
NEG_INF = -1e30
REMOVED = -3e38
FORCE_BONUS = 1e4
SLOT = 2 * NSA_DH


def _cparams(n_grid):
    return pltpu.CompilerParams(dimension_semantics=("arbitrary",) * n_grid,
                                vmem_limit_bytes=VMEM_LIMIT_BYTES)


def _row_tile(m, want):
    t = min(m, want)
    assert m % t == 0, (m, t)
    return t


def _nt_dot(a, b):
    return lax.dot_general(a, b, (((1,), (1,)), ((), ())), preferred_element_type=F32)


def _tn_dot(a, b):
    return lax.dot_general(a, b, (((0,), (0,)), ((), ())), preferred_element_type=F32)


def _dot(a, b):
    return jnp.dot(a, b, preferred_element_type=F32)


def _gelu(x):
    return 0.5 * x * (1.0 + jnp.tanh(np.float32(np.sqrt(2.0 / np.pi)) * (x + 0.044715 * (x * x * x))))


def _layer_norm_rows(r, g, b):
    mu = jnp.mean(r, axis=-1, keepdims=True)
    d = r - mu
    var = jnp.mean(d * d, axis=-1, keepdims=True)
    return d * lax.rsqrt(var + LN_EPS) * g + b


def _mm_split_kernel(x_ref, w_ref, *o_refs, cuts):
    acc = _dot(x_ref[...].astype(BF16), w_ref[...])
    for o_ref, (lo, hi) in zip(o_refs, cuts):
        o_ref[...] = acc[:, lo:hi].astype(o_ref.dtype)


def matmul_split(x, w_bf16, widths, dtypes, tm=256):
    m, k = x.shape
    n = w_bf16.shape[1]
    assert sum(widths) == n and all(wd % LANES == 0 for wd in widths)
    tm = _row_tile(m, tm)
    cuts, lo = [], 0
    for wd in widths:
        cuts.append((lo, lo + wd))
        lo += wd
    return pl.pallas_call(
        functools.partial(_mm_split_kernel, cuts=tuple(cuts)),
        grid=(m // tm,),
        in_specs=[pl.BlockSpec((tm, k), lambda i: (i, 0)),
                  pl.BlockSpec((k, n), lambda i: (0, 0))],
        out_specs=[pl.BlockSpec((tm, wd), lambda i: (i, 0)) for wd in widths],
        out_shape=[jax.ShapeDtypeStruct((m, wd), dt) for wd, dt in zip(widths, dtypes)],
        compiler_params=_cparams(1),
        name="matmul_split",
    )(x, w_bf16)


def _nsa_proj_kernel(x_ref, wq_ref, wkvt_ref, q_ref, g_ref, *kv_refs, nq):
    xb = x_ref[...].astype(BF16)
    acc = _dot(xb, wq_ref[...])
    q_ref[...] = acc[:, :nq].astype(q_ref.dtype)
    g_ref[...] = acc[:, nq:]
    acc_t = _nt_dot(wkvt_ref[...], xb)
    gd = acc_t.shape[0] // len(kv_refs)
    for i, r in enumerate(kv_refs):
        r[0] = acc_t[i * gd:(i + 1) * gd, :]


def nsa_projection(x, n_seq, wq_bf16, wkvt_bf16, q_dtype, n_kv=6, tm=256):
    m, d = x.shape
    seq = m // n_seq
    tm = _row_tile(seq, tm)
    nt = seq // tm
    nq = wq_bf16.shape[1] - LANES
    gd = wkvt_bf16.shape[0] // n_kv
    return pl.pallas_call(
        functools.partial(_nsa_proj_kernel, nq=nq),
        grid=(n_seq, nt),
        in_specs=[pl.BlockSpec((tm, d), lambda s, i: (s * nt + i, 0)),
                  pl.BlockSpec((d, nq + LANES), lambda s, i: (0, 0)),
                  pl.BlockSpec((n_kv * gd, d), lambda s, i: (0, 0))],
        out_specs=[pl.BlockSpec((tm, nq), lambda s, i: (s * nt + i, 0)),
                   pl.BlockSpec((tm, LANES), lambda s, i: (s * nt + i, 0))] + [
                      pl.BlockSpec((1, gd, tm), lambda s, i: (s, 0, i)) for _ in range(n_kv)],
        out_shape=[jax.ShapeDtypeStruct((m, nq), q_dtype), jax.ShapeDtypeStruct((m, LANES), F32)] + [
            jax.ShapeDtypeStruct((n_seq, gd, seq), F32) for _ in range(n_kv)],
        compiler_params=_cparams(2),
        name="nsa_projection",
    )(x, wq_bf16, wkvt_bf16)


def _mm_res_ln_kernel(a_ref, w_ref, x_ref, g_ref, b_ref, o_ref):
    y = _dot(a_ref[...].astype(BF16), w_ref[...])
    o_ref[...] = _layer_norm_rows(ALPHA * x_ref[...] + y, g_ref[...], b_ref[...])


def matmul_residual_ln(a, w_bf16, x, g, b, tm=256):
    m, k = a.shape
    d = w_bf16.shape[1]
    tm = _row_tile(m, tm)
    return pl.pallas_call(
        _mm_res_ln_kernel,
        grid=(m // tm,),
        in_specs=[pl.BlockSpec((tm, k), lambda i: (i, 0)),
                  pl.BlockSpec((k, d), lambda i: (0, 0)),
                  pl.BlockSpec((tm, d), lambda i: (i, 0)),
                  pl.BlockSpec((1, d), lambda i: (0, 0)),
                  pl.BlockSpec((1, d), lambda i: (0, 0))],
        out_specs=pl.BlockSpec((tm, d), lambda i: (i, 0)),
        out_shape=jax.ShapeDtypeStruct((m, d), F32),
        compiler_params=_cparams(1),
        name="matmul_residual_ln",
    )(a, w_bf16, x, g.reshape(1, d), b.reshape(1, d))


def _expand_gates(gates_raw, e_ref):
    sig = jax.nn.sigmoid(gates_raw)
    hi = sig.astype(BF16)
    lo = (sig - hi.astype(F32)).astype(BF16)
    e = e_ref[...]
    return _dot(hi, e) + _dot(lo, e)


def _nsa_merge_ln_kernel(oc_ref, os_ref, ow_ref, gt_ref, e_ref, w_ref, x_ref, g_ref, b_ref, o_ref, *, k):
    gx = _expand_gates(gt_ref[...], e_ref)
    o = gx[:, 0:k] * oc_ref[...] + gx[:, k:2 * k] * os_ref[...] + gx[:, 2 * k:3 * k] * ow_ref[...]
    y = _dot(o.astype(BF16), w_ref[...])
    o_ref[...] = _layer_norm_rows(ALPHA * x_ref[...] + y, g_ref[...], b_ref[...])


def nsa_merge_residual_ln(oc, osl, ow, gates, e_bf16, w_bf16, x, g, b, tm=256):
    m, k = oc.shape
    d = w_bf16.shape[1]
    tm = _row_tile(m, tm)
    row = lambda i: (i, 0)
    fixed = lambda i: (0, 0)
    return pl.pallas_call(
        functools.partial(_nsa_merge_ln_kernel, k=k),
        grid=(m // tm,),
        in_specs=[pl.BlockSpec((tm, k), row), pl.BlockSpec((tm, k), row), pl.BlockSpec((tm, k), row),
                  pl.BlockSpec((tm, LANES), row),
                  pl.BlockSpec((LANES, 3 * k), fixed),
                  pl.BlockSpec((k, d), fixed),
                  pl.BlockSpec((tm, d), row),
                  pl.BlockSpec((1, d), fixed), pl.BlockSpec((1, d), fixed)],
        out_specs=pl.BlockSpec((tm, d), row),
        out_shape=jax.ShapeDtypeStruct((m, d), F32),
        compiler_params=_cparams(1),
        name="nsa_merge_residual_ln",
    )(oc, osl, ow, gates, e_bf16, w_bf16, x, g.reshape(1, d), b.reshape(1, d))


def _even_mixer_kernel(z_ref, hist_ref, st_ref, cos_ref, sin_ref, decay_ref, qdec_ref, kdec_ref, sdec_ref,
                       cw_ref, cb_ref, gn_ref, y_ref, hist_out_ref, st_out_ref, carry, state,
                       *, rows, valid, dconv, dk):
    c = pl.program_id(1)
    r0 = valid - 2 - (rows - SUBLANES)

    @pl.when(c == 0)
    def _():
        carry[r0:r0 + 2, :] = hist_ref[0]
        state[...] = st_ref[0]

    d = dconv
    h = z_ref[:, 0:d]
    gate_b = z_ref[:, d:2 * d]
    gate_c = z_ref[:, 2 * d:3 * d]
    ch = gate_c * h
    row = lax.broadcasted_iota(jnp.int32, (rows, d), 0)
    h0 = carry[r0:r0 + 1, :]
    h1 = carry[r0 + 1:r0 + 2, :]
    m1 = jnp.where(row == 0, h1, pltpu.roll(ch, 1, 0))
    m2 = jnp.where(row == 0, h0, jnp.where(row == 1, h1, pltpu.roll(ch, 2, 0)))
    u = ((cb_ref[...] + m2 * cw_ref[0:1, :]) + m1 * cw_ref[1:2, :]) + ch * cw_ref[2:3, :]
    y_ref[:, 0:d] = gate_b * u
    carry[...] = ch[rows - SUBLANES:rows, :]
    hist_out_ref[0] = carry[r0:r0 + 2, :]

    cosf = cos_ref[...]
    sinf = sin_ref[...]
    scale = np.float32(dk ** -0.5)
    for hh in range(RET_HEADS):
        q = z_ref[:, 3 * d + hh * dk:3 * d + (hh + 1) * dk]
        k = z_ref[:, 4 * d + hh * dk:4 * d + (hh + 1) * dk]
        v = z_ref[:, 5 * d + hh * dk:5 * d + (hh + 1) * dk]
        gsw = z_ref[:, 6 * d + hh * dk:6 * d + (hh + 1) * dk]
        q = (q * cosf + pltpu.roll(q, dk // 2, 1) * sinf) * scale
        k = k * cosf + pltpu.roll(k, dk // 2, 1) * sinf
        qb = q.astype(BF16)
        vb = v.astype(BF16)
        s_old = state[hh]
        scores = _nt_dot(qb, k.astype(BF16)) * decay_ref[hh]
        intra = _dot(scores.astype(BF16), vb)
        cross = _dot(qb, s_old.astype(BF16)) * qdec_ref[hh]
        kd = (k * kdec_ref[hh]).astype(BF16)
        state[hh] = s_old * sdec_ref[hh] + _tn_dot(kd, vb)
        o = intra + cross
        mu = jnp.mean(o, axis=-1, keepdims=True)
        dv = o - mu
        var = jnp.mean(dv * dv, axis=-1, keepdims=True)
        on = dv * lax.rsqrt(var + LN_EPS) * gn_ref[:, hh * dk:(hh + 1) * dk]
        y_ref[:, d + hh * dk:d + (hh + 1) * dk] = (gsw * jax.nn.sigmoid(gsw)) * on
    st_out_ref[0] = state[...]


def _retention_tables(rows, valid, dk):
    log_gamma = jnp.log1p(-jnp.exp2(-5.0 - jnp.arange(RET_HEADS, dtype=F32)))
    n = jnp.arange(rows, dtype=F32)
    diff = n[:, None] - n[None, :]
    lg = log_gamma[:, None, None]
    decay = jnp.where(diff >= 0, jnp.exp(lg * jnp.maximum(diff, 0.0)), 0.0)
    q_dec = jnp.exp((n[None, :] + 1.0) * log_gamma[:, None])
    k_dec = jnp.where(n[None, :] < valid, jnp.exp((valid - 1.0 - n[None, :]) * log_gamma[:, None]), 0.0)
    s_dec = jnp.exp(valid * log_gamma)
    bc = lambda a: jnp.broadcast_to(a[:, :, None], (RET_HEADS, rows, dk))
    return decay, bc(q_dec), bc(k_dec), jnp.broadcast_to(s_dec[:, None, None], (RET_HEADS, 1, dk))


def _rope_tables(pos, dk):
    half = dk // 2
    inv = ROPE_BASE ** (-jnp.arange(half, dtype=F32) / half)
    ang = pos.astype(F32)[:, None] * inv
    cos, sin = jnp.cos(ang), jnp.sin(ang)
    return jnp.concatenate([cos, cos], axis=-1), jnp.concatenate([-sin, sin], axis=-1)


def even_mixer(z, hist, st, pos, rows, valid, conv_w, conv_b, gn_g):
    n_seq, _, dconv = hist.shape
    dk = st.shape[-1]
    n_chunks = z.shape[0] // (n_seq * rows)
    cosf, sinf = _rope_tables(pos, dk)
    decay, q_dec, k_dec, s_dec = _retention_tables(rows, valid, dk)
    fixed3 = lambda s, c: (0, 0, 0)
    fixed2 = lambda s, c: (0, 0)
    return pl.pallas_call(
        functools.partial(_even_mixer_kernel, rows=rows, valid=valid, dconv=dconv, dk=dk),
        grid=(n_seq, n_chunks),
        in_specs=[pl.BlockSpec((rows, 7 * dconv), lambda s, c: (s * n_chunks + c, 0)),
                  pl.BlockSpec((1, 2, dconv), lambda s, c: (s, 0, 0)),
                  pl.BlockSpec((1, RET_HEADS, dk, dk), lambda s, c: (s, 0, 0, 0)),
                  pl.BlockSpec((rows, dk), lambda s, c: (c, 0)),
                  pl.BlockSpec((rows, dk), lambda s, c: (c, 0)),
                  pl.BlockSpec((RET_HEADS, rows, rows), fixed3),
                  pl.BlockSpec((RET_HEADS, rows, dk), fixed3),
                  pl.BlockSpec((RET_HEADS, rows, dk), fixed3),
                  pl.BlockSpec((RET_HEADS, 1, dk), fixed3),
                  pl.BlockSpec((SCONV_W, dconv), fixed2),
                  pl.BlockSpec((1, dconv), fixed2),
                  pl.BlockSpec((1, RET_HEADS * dk), fixed2)],
        out_specs=[pl.BlockSpec((rows, 2 * dconv), lambda s, c: (s * n_chunks + c, 0)),
                   pl.BlockSpec((1, 2, dconv), lambda s, c: (s, 0, 0)),
                   pl.BlockSpec((1, RET_HEADS, dk, dk), lambda s, c: (s, 0, 0, 0))],
        out_shape=[jax.ShapeDtypeStruct((z.shape[0], 2 * dconv), F32),
                   jax.ShapeDtypeStruct((n_seq, 2, dconv), F32),
                   jax.ShapeDtypeStruct((n_seq, RET_HEADS, dk, dk), F32)],
        scratch_shapes=[pltpu.VMEM((SUBLANES, dconv), F32), pltpu.VMEM((RET_HEADS, dk, dk), F32)],
        compiler_params=_cparams(2),
        name="even_mixer",
    )(z, hist, st, cosf, sinf, decay, q_dec, k_dec, s_dec, conv_w, conv_b.reshape(1, dconv),
      gn_g.reshape(1, RET_HEADS * dk))


def _conv_gate(a, gate, m1, m2, cw_ref, cb_ref):
    conv = ((cb_ref[...] + m2 * cw_ref[0:1, :]) + m1 * cw_ref[1:2, :]) + a * cw_ref[2:3, :]
    return _gelu(conv) * gate


def _ffn_up_seq_kernel(x_ref, wa_ref, wg_ref, h_ref, cw_ref, cb_ref, o_ref, hist_out_ref, carry, *, tm):
    @pl.when(pl.program_id(2) == 0)
    def _():
        carry[SUBLANES - 2:SUBLANES, :] = h_ref[0]

    xb = x_ref[...].astype(BF16)
    a = _dot(xb, wa_ref[...])
    gate = _dot(xb, wg_ref[...])
    row = lax.broadcasted_iota(jnp.int32, a.shape, 0)
    h0 = carry[SUBLANES - 2:SUBLANES - 1, :]
    h1 = carry[SUBLANES - 1:SUBLANES, :]
    m1 = jnp.where(row == 0, h1, pltpu.roll(a, 1, 0))
    m2 = jnp.where(row == 0, h0, jnp.where(row == 1, h1, pltpu.roll(a, 2, 0)))
    o_ref[...] = _conv_gate(a, gate, m1, m2, cw_ref, cb_ref).astype(o_ref.dtype)
    carry[...] = a[tm - SUBLANES:tm, :]
    hist_out_ref[0] = carry[SUBLANES - 2:SUBLANES, :]


def ffn_up_sequences(x, n_seq, hist, w_up_bf16, conv_w, conv_b, tm=256, n_col=2):
    m, k = x.shape
    dff = conv_w.shape[1]
    seq = m // n_seq
    tm = _row_tile(seq, tm)
    tps = seq // tm
    tn = dff // n_col
    assert tn % LANES == 0
    return pl.pallas_call(
        functools.partial(_ffn_up_seq_kernel, tm=tm),
        grid=(n_col, n_seq, tps),
        in_specs=[pl.BlockSpec((tm, k), lambda j, s, i: (s * tps + i, 0)),
                  pl.BlockSpec((k, tn), lambda j, s, i: (0, j)),
                  pl.BlockSpec((k, tn), lambda j, s, i: (0, j + n_col)),
                  pl.BlockSpec((1, 2, tn), lambda j, s, i: (s, 0, j)),
                  pl.BlockSpec((FFN_W, tn), lambda j, s, i: (0, j)),
                  pl.BlockSpec((1, tn), lambda j, s, i: (0, j))],
        out_specs=[pl.BlockSpec((tm, tn), lambda j, s, i: (s * tps + i, j)),
                   pl.BlockSpec((1, 2, tn), lambda j, s, i: (s, 0, j))],
        out_shape=[jax.ShapeDtypeStruct((m, dff), BF16),
                   jax.ShapeDtypeStruct((n_seq, 2, dff), F32)],
        scratch_shapes=[pltpu.VMEM((SUBLANES, tn), F32)],
        compiler_params=_cparams(3),
        name="ffn_up_sequences",
    )(x, w_up_bf16, w_up_bf16, hist, conv_w, conv_b.reshape(1, dff))


def _ffn_up_short_kernel(x_ref, wa_ref, wg_ref, h1_ref, h2_ref, cw_ref, cb_ref, o_ref, a_ref):
    xb = x_ref[...].astype(BF16)
    a = _dot(xb, wa_ref[...])
    gate = _dot(xb, wg_ref[...])
    t = lax.broadcasted_iota(jnp.int32, a.shape, 0) % SUBLANES
    m1 = jnp.where(t == 0, h1_ref[...], pltpu.roll(a, 1, 0))
    m2 = jnp.where(t < 2, h2_ref[...], pltpu.roll(a, 2, 0))
    o_ref[...] = _conv_gate(a, gate, m1, m2, cw_ref, cb_ref).astype(o_ref.dtype)
    a_ref[...] = a


def ffn_up_short(x, hist, w_up_bf16, conv_w, conv_b, n_col=2):
    m, k = x.shape
    dff = conv_w.shape[1]
    n_seq = m // SUBLANES
    tn = dff // n_col
    zeros = jnp.zeros((n_seq, SUBLANES, dff), F32)
    h1 = zeros.at[:, 0].set(hist[:, 1]).reshape(m, dff)
    h2 = zeros.at[:, 0].set(hist[:, 0]).at[:, 1].set(hist[:, 1]).reshape(m, dff)
    col = lambda j: (0, j)
    return pl.pallas_call(
        _ffn_up_short_kernel,
        grid=(n_col,),
        in_specs=[pl.BlockSpec((m, k), lambda j: (0, 0)),
                  pl.BlockSpec((k, tn), col),
                  pl.BlockSpec((k, tn), lambda j: (0, j + n_col)),
                  pl.BlockSpec((m, tn), col), pl.BlockSpec((m, tn), col),
                  pl.BlockSpec((FFN_W, tn), col), pl.BlockSpec((1, tn), col)],
        out_specs=[pl.BlockSpec((m, tn), col), pl.BlockSpec((m, tn), col)],
        out_shape=[jax.ShapeDtypeStruct((m, dff), F32), jax.ShapeDtypeStruct((m, dff), F32)],
        compiler_params=_cparams(1),
        name="ffn_up_short",
    )(x, w_up_bf16, w_up_bf16, h1, h2, conv_w, conv_b.reshape(1, dff))


def _compress_kernel(pt_ref, *refs, pages):
    page_refs = refs[:pages + 1]
    w2t_ref, pecol_ref, w1_ref, w2_ref, o_ref, rows_ref = refs[pages + 1:]
    cpp = PAGE_SIZE // CMP_STRIDE
    n = (pages + 1) * cpp
    hidden = w1_ref.shape[1]
    gpr = LANES // NSA_DH
    pieces = NSA_KV // gpr
    for i, r in enumerate(page_refs):
        for pc in range(pieces):
            tile = r[0, pc * gpr:(pc + 1) * gpr].reshape(LANES, PAGE_SIZE)
            rows_ref[pc, i * PAGE_SIZE:(i + 1) * PAGE_SIZE, :] = tile.T
    pe_term = jnp.sum(pecol_ref[...] * w1_ref[...], axis=0, keepdims=True)
    for pc in range(pieces):
        acc = jnp.zeros((n, gpr * 2 * hidden), F32)
        for tp in range(CMP_STRIDE // 2):
            lhs = jnp.concatenate([rows_ref[pc, pl.ds(2 * tp, n, stride=CMP_STRIDE), :],
                                   rows_ref[pc, pl.ds(2 * tp + 1, n, stride=CMP_STRIDE), :]], axis=1)
            acc = acc + _dot(lhs.astype(BF16), w2t_ref[tp])
        for gl in range(gpr):
            a = acc[:, gl * 2 * hidden:(gl + 1) * 2 * hidden]
            nxt = pltpu.roll(a, n - 1, 0)
            pre = pe_term + a[:, 0:hidden]
            pre = pre + nxt[:, hidden:2 * hidden]
            o_ref[0, pc * gpr + gl] = _dot(_gelu(pre[0:pages * cpp]).astype(BF16), w2_ref[...])


def compress(rows_t, page_table, pe, w1, w2, pages=32):
    pooled = page_table is not None
    if pooled:
        n_seq, n_pages = page_table.shape
    else:
        n_seq, n_pages = rows_t.shape[0], rows_t.shape[3] // PAGE_SIZE
        page_table = jnp.zeros((1, 1), jnp.int32)
    pages = min(pages, n_pages)
    assert n_pages % pages == 0
    hidden = w1.shape[1]
    cpp = PAGE_SIZE // CMP_STRIDE
    r = CMP_LEN // CMP_STRIDE
    gpr = LANES // NSA_DH
    assert r == 2 and gpr == 2
    w1p = w1.reshape(r, CMP_STRIDE, NSA_DH, hidden)
    w16 = jnp.concatenate([w1p[0], w1p[1]], axis=-1)
    zero = jnp.zeros_like(w16)
    per_tok = jnp.concatenate([jnp.concatenate([w16, zero], axis=-1),
                               jnp.concatenate([zero, w16], axis=-1)], axis=1)
    w2t = per_tok.reshape(CMP_STRIDE // 2, 2 * LANES, gpr * 2 * hidden).astype(BF16)
    w2p = jnp.pad(w2, ((0, 0), (0, SLOT - NSA_DH))).astype(BF16)
    pecol = pe.reshape(CMP_LEN * NSA_DH, 1)

    def page_map(i):
        if pooled:
            return lambda s, j, pt: (pt[s, jnp.minimum(j * pages + i, n_pages - 1)], 0, 0, 0)
        return lambda s, j, pt: (s, 0, 0, jnp.minimum(j * pages + i, n_pages - 1))

    fixed2 = lambda s, j, pt: (0, 0)
    grid_spec = pltpu.PrefetchScalarGridSpec(
        num_scalar_prefetch=1,
        grid=(n_seq, n_pages // pages),
        in_specs=[pl.BlockSpec((1, NSA_KV, NSA_DH, PAGE_SIZE), page_map(i)) for i in range(pages + 1)] + [
            pl.BlockSpec((CMP_STRIDE // 2, 2 * LANES, gpr * 2 * hidden), lambda s, j, pt: (0, 0, 0)),
            pl.BlockSpec((CMP_LEN * NSA_DH, 1), fixed2),
            pl.BlockSpec((CMP_LEN * NSA_DH, hidden), fixed2),
            pl.BlockSpec((hidden, SLOT), fixed2)],
        out_specs=pl.BlockSpec((1, NSA_KV, pages * cpp, SLOT), lambda s, j, pt: (s, 0, j, 0)),
        scratch_shapes=[pltpu.VMEM((NSA_KV // gpr, (pages + 1) * PAGE_SIZE, LANES), F32)],
    )
    return pl.pallas_call(
        functools.partial(_compress_kernel, pages=pages),
        grid_spec=grid_spec,
        out_shape=jax.ShapeDtypeStruct((n_seq, NSA_KV, n_pages * cpp, SLOT), F32),
        compiler_params=_cparams(2),
        name="compress",
    )(page_table, *([rows_t] * (pages + 1)), w2t, pecol, w1, w2p)


def _cmp_select_kernel(q_ref, kc_ref, vc_ref, ov_ref, o_ref, sel_ref, idx_ref, *, tq, n_cmp, n_slc, lane_off, pos0):
    t0 = pos0 + pl.program_id(2) * tq
    ncp = kc_ref.shape[2]
    nsp = ov_ref.shape[1]
    kc = kc_ref[0, 0].astype(BF16)
    vc = vc_ref[0, 0].astype(BF16)
    q_pos = t0 + lax.broadcasted_iota(jnp.int32, (tq, ncp), 0)
    blk_i = lax.broadcasted_iota(jnp.int32, (tq, ncp), 1)
    valid = (blk_i * CMP_STRIDE + (CMP_LEN - 1) <= q_pos) & (blk_i < n_cmp)
    p_sum = jnp.zeros((tq, ncp), F32)
    for j in range(NSA_GROUP):
        qj = q_ref[:, j * SLOT:(j + 1) * SLOT].astype(BF16)
        s = jnp.where(valid, _nt_dot(qj, kc), NEG_INF)
        m = jnp.max(s, axis=-1, keepdims=True)
        e = jnp.where(valid, jnp.exp(s - m), 0.0)
        den = jnp.sum(e, axis=-1, keepdims=True)
        p = e / jnp.where(den > 0.0, den, 1.0)
        o_ref[:, j * SLOT:(j + 1) * SLOT] = _dot(p.astype(BF16), vc)
        p_sum = p_sum + p
    hi = p_sum.astype(BF16)
    lo = (p_sum - hi.astype(F32)).astype(BF16)
    ov = ov_ref[...]
    imp = _dot(hi, ov) + _dot(lo, ov)
    lane = lax.broadcasted_iota(jnp.int32, (tq, nsp), 1)
    blk = lane - lane_off
    cur = (t0 + lax.broadcasted_iota(jnp.int32, (tq, nsp), 0)) // SLC_BLOCK
    real = (blk >= 0) & (blk < n_slc)
    causal = real & (blk <= cur)
    forced = (blk == 0) | (blk == cur) | (blk == cur - 1)
    score = jnp.where(causal, imp + jnp.where(forced, FORCE_BONUS, 0.0), NEG_INF)
    score = jnp.where(real, score, REMOVED)
    picked = jnp.zeros((tq, nsp), jnp.bool_)
    idx = jnp.zeros((tq, LANES), jnp.int32)
    idx_lane = lax.broadcasted_iota(jnp.int32, (tq, LANES), 1)
    for it in range(min(SLC_TOPN, n_slc)):
        m = jnp.max(score, axis=-1, keepdims=True)
        first = jnp.min(jnp.where(score == m, lane, nsp), axis=-1, keepdims=True)
        hit = lane == first
        picked = picked | hit
        score = jnp.where(hit, REMOVED, score)
        idx = jnp.where(idx_lane == it, first - lane_off, idx)
    sel_ref[0, 0] = jnp.where(real & ~(picked & causal), NEG_INF, 0.0)
    idx_ref[0, 0] = idx


def cmp_block_overlap(n_cmp_pad, n_cmp, n_slc, n_slc_pad, lane_off):
    i = np.arange(n_cmp_pad)[:, None]
    j = np.arange(n_slc_pad)[None, :] - lane_off
    start = i * CMP_STRIDE
    hit = (start < (j + 1) * SLC_BLOCK) & (start + CMP_LEN > j * SLC_BLOCK) & (i < n_cmp) & (j >= 0) & (j < n_slc)
    return jnp.asarray(hit.astype(np.float32), dtype=BF16)


def cmp_attention_select(q_slots, kcc, vcc, n_seq, n_cmp, n_slc, lane_off, pos0, tq):
    tokens = q_slots.shape[0]
    t = tokens // n_seq
    tq = _row_tile(t, tq)
    nt = t // tq
    ncp = kcc.shape[2]
    nsp = -(-(lane_off + n_slc) // LANES) * LANES
    ov = cmp_block_overlap(ncp, n_cmp, n_slc, nsp, lane_off)
    gw = NSA_GROUP * SLOT
    return pl.pallas_call(
        functools.partial(_cmp_select_kernel, tq=tq, n_cmp=n_cmp, n_slc=n_slc, lane_off=lane_off, pos0=pos0),
        grid=(n_seq, NSA_KV, nt),
        in_specs=[pl.BlockSpec((tq, gw), lambda s, g, i: (s * nt + i, g)),
                  pl.BlockSpec((1, 1, ncp, SLOT), lambda s, g, i: (s, g, 0, 0)),
                  pl.BlockSpec((1, 1, ncp, SLOT), lambda s, g, i: (s, g, 0, 0)),
                  pl.BlockSpec((ncp, nsp), lambda s, g, i: (0, 0))],
        out_specs=[pl.BlockSpec((tq, gw), lambda s, g, i: (s * nt + i, g)),
                   pl.BlockSpec((1, 1, tq, nsp), lambda s, g, i: (s, g, i, 0)),
                   pl.BlockSpec((1, 1, tq, LANES), lambda s, g, i: (s, g, i, 0))],
        out_shape=[jax.ShapeDtypeStruct((tokens, NSA_HEADS * SLOT), F32),
                   jax.ShapeDtypeStruct((n_seq, NSA_KV, t, nsp), F32),
                   jax.ShapeDtypeStruct((n_seq, NSA_KV, t, LANES), jnp.int32)],
        compiler_params=_cparams(3),
        name="cmp_attention_select",
    )(q_slots, kcc, vcc, ov)


def _cmp_select_prompt_kernel(q_ref, kc_ref, vc_ref, ovt_ref, o_ref, sel_ref, *, tq, n_cmp, n_slc):
    t0 = pl.program_id(2) * tq
    ncp = kc_ref.shape[2]
    nsr = ovt_ref.shape[0]
    kc = kc_ref[0, 0].astype(BF16)
    vc = vc_ref[0, 0].astype(BF16)
    q_pos = t0 + lax.broadcasted_iota(jnp.int32, (ncp, tq), 1)
    blk_i = lax.broadcasted_iota(jnp.int32, (ncp, tq), 0)
    valid = (blk_i * CMP_STRIDE + (CMP_LEN - 1) <= q_pos) & (blk_i < n_cmp)
    p_sum = jnp.zeros((ncp, tq), F32)
    for j in range(NSA_GROUP):
        s = jnp.where(valid, _nt_dot(kc, q_ref[:, j * SLOT:(j + 1) * SLOT]), NEG_INF)
        m = jnp.max(s, axis=0, keepdims=True)
        e = jnp.where(valid, jnp.exp(s - m), 0.0)
        den = jnp.sum(e, axis=0, keepdims=True)
        p = e / jnp.where(den > 0.0, den, 1.0)
        o_ref[:, j * SLOT:(j + 1) * SLOT] = _tn_dot(p.astype(BF16), vc)
        p_sum = p_sum + p
    hi = p_sum.astype(BF16)
    lo = (p_sum - hi.astype(F32)).astype(BF16)
    ovt = ovt_ref[...]
    imp = _dot(ovt, hi) + _dot(ovt, lo)
    blk = lax.broadcasted_iota(jnp.int32, (nsr, tq), 0)
    cur = (t0 + lax.broadcasted_iota(jnp.int32, (nsr, tq), 1)) // SLC_BLOCK
    real = blk < n_slc
    causal = real & (blk <= cur)
    forced = (blk == 0) | (blk == cur) | (blk == cur - 1)
    score = jnp.where(causal, imp + jnp.where(forced, FORCE_BONUS, 0.0), NEG_INF)
    score = jnp.where(real, score, REMOVED)
    picked = jnp.zeros((nsr, tq), jnp.bool_)
    for _ in range(min(SLC_TOPN, n_slc)):
        m = jnp.max(score, axis=0, keepdims=True)
        first = jnp.min(jnp.where(score == m, blk, nsr), axis=0, keepdims=True)
        hit = blk == first
        picked = picked | hit
        score = jnp.where(hit, REMOVED, score)
    bias_t = jnp.where(real & ~(picked & causal), NEG_INF, 0.0)
    slot_t = jnp.concatenate([jnp.zeros((NSA_DH, tq), F32), bias_t], axis=0)
    sel_ref[0, 0] = slot_t.T


def cmp_attention_select_prompt(q_slots, kcc, vcc, n_seq, n_cmp, n_slc, tq):
    tokens = q_slots.shape[0]
    t = tokens // n_seq
    tq = _row_tile(t, tq)
    nt = t // tq
    ncp = kcc.shape[2]
    nsr = SLOT - NSA_DH
    assert n_slc <= nsr
    ovt = cmp_block_overlap(ncp, n_cmp, n_slc, nsr, 0).T
    gw = NSA_GROUP * SLOT
    return pl.pallas_call(
        functools.partial(_cmp_select_prompt_kernel, tq=tq, n_cmp=n_cmp, n_slc=n_slc),
        grid=(n_seq, NSA_KV, nt),
        in_specs=[pl.BlockSpec((tq, gw), lambda s, g, i: (s * nt + i, g)),
                  pl.BlockSpec((1, 1, ncp, SLOT), lambda s, g, i: (s, g, 0, 0)),
                  pl.BlockSpec((1, 1, ncp, SLOT), lambda s, g, i: (s, g, 0, 0)),
                  pl.BlockSpec((nsr, ncp), lambda s, g, i: (0, 0))],
        out_specs=[pl.BlockSpec((tq, gw), lambda s, g, i: (s * nt + i, g)),
                   pl.BlockSpec((1, 1, tq, SLOT), lambda s, g, i: (s, g, i, 0))],
        out_shape=[jax.ShapeDtypeStruct((tokens, NSA_HEADS * SLOT), F32),
                   jax.ShapeDtypeStruct((n_seq, NSA_KV, t, SLOT), F32)],
        compiler_params=_cparams(3),
        name="cmp_attention_select_prompt",
    )(q_slots, kcc, vcc, ovt)


def _prompt_slc_win_kernel(q_ref, sel_ref, ks_ref, vs_ref, kw_ref, vw_ref, oh_ref, os_ref, ow_ref, *, tq, seq):
    qi = pl.program_id(2)
    t0 = qi * tq
    rows = NSA_GROUP * tq
    sel = sel_ref[0, 0]
    q_plain = jnp.concatenate([q_ref[:, j * SLOT:j * SLOT + NSA_DH] for j in range(NSA_GROUP)], axis=0)
    q_aug = jnp.concatenate([(q_ref[:, j * SLOT:(j + 1) * SLOT].astype(F32) + sel).astype(BF16)
                             for j in range(NSA_GROUP)], axis=0)
    q_pos = t0 + (lax.broadcasted_iota(jnp.int32, (rows, tq), 0) & (tq - 1))
    v_pad = jnp.zeros((SLOT - NSA_DH, tq), BF16)

    def tile(kt, carry, diagonal):
        m, l, acc = carry
        start = pl.multiple_of(kt * tq, tq)
        k_aug = jnp.concatenate([ks_ref[0, :, pl.ds(start, tq)].astype(BF16), oh_ref[:, pl.ds(start, tq)]], axis=0)
        s = _dot(q_aug, k_aug)
        if diagonal:
            k_pos = start + lax.broadcasted_iota(jnp.int32, (rows, tq), 1)
            s = jnp.where(k_pos <= q_pos, s, NEG_INF)
        m_new = jnp.maximum(m, jnp.max(s, axis=-1, keepdims=True))
        alpha = jnp.exp(m - m_new)
        p = jnp.exp(s - m_new)
        l = alpha * l + jnp.sum(p, axis=-1, keepdims=True)
        v_t = jnp.concatenate([vs_ref[0, :, pl.ds(start, tq)].astype(BF16), v_pad], axis=0)
        acc = alpha * acc + _nt_dot(p.astype(BF16), v_t)
        return m_new, l, acc

    init = (jnp.full((rows, 1), NEG_INF, F32), jnp.zeros((rows, 1), F32), jnp.zeros((rows, SLOT), F32))
    carry = lax.fori_loop(0, qi, lambda kt, c: tile(kt, c, False), init)
    _, l, acc = tile(qi, carry, True)
    o = acc / l
    for j in range(NSA_GROUP):
        os_ref[:, j * SLOT:(j + 1) * SLOT] = o[j * tq:(j + 1) * tq]

    wk = WINDOW + tq
    k0 = pl.multiple_of(jnp.clip(t0 - WINDOW, 0, seq - wk), tq)
    s = _dot(q_plain, kw_ref[0, :, pl.ds(k0, wk)].astype(BF16))
    dist = (t0 + (lax.broadcasted_iota(jnp.int32, (rows, wk), 0) & (tq - 1))) - (
        k0 + lax.broadcasted_iota(jnp.int32, (rows, wk), 1))
    s = jnp.where((dist >= 0) & (dist < WINDOW), s, NEG_INF)
    m = jnp.max(s, axis=-1, keepdims=True)
    p = jnp.exp(s - m)
    l = jnp.sum(p, axis=-1, keepdims=True)
    vw_t = jnp.concatenate([vw_ref[0, :, pl.ds(k0, wk)].astype(BF16), jnp.zeros((SLOT - NSA_DH, wk), BF16)], axis=0)
    o = _nt_dot(p.astype(BF16), vw_t) / l
    for j in range(NSA_GROUP):
        ow_ref[:, j * SLOT:(j + 1) * SLOT] = o[j * tq:(j + 1) * tq]


def prompt_slc_win_attention(q_slots, sel, ks_t, vs_t, kw_t, vw_t, n_seq, tq):
    tokens = q_slots.shape[0]
    seq = tokens // n_seq
    tq = _row_tile(seq, tq)
    assert tq & (tq - 1) == 0 and WINDOW % tq == 0 and seq >= WINDOW + tq
    nt = seq // tq
    gw = NSA_GROUP * SLOT
    onehot_t = jax.nn.one_hot(jnp.arange(seq) // SLC_BLOCK, SLOT - NSA_DH, dtype=BF16).T
    kv_spec = pl.BlockSpec((1, NSA_DH, seq), lambda s, g, i: (s, g, 0))
    return pl.pallas_call(
        functools.partial(_prompt_slc_win_kernel, tq=tq, seq=seq),
        grid=(n_seq, NSA_KV, nt),
        in_specs=[pl.BlockSpec((tq, gw), lambda s, g, i: (s * nt + i, g)),
                  pl.BlockSpec((1, 1, tq, SLOT), lambda s, g, i: (s, g, i, 0)),
                  kv_spec, kv_spec, kv_spec, kv_spec,
                  pl.BlockSpec((SLOT - NSA_DH, seq), lambda s, g, i: (0, 0))],
        out_specs=[pl.BlockSpec((tq, gw), lambda s, g, i: (s * nt + i, g)),
                   pl.BlockSpec((tq, gw), lambda s, g, i: (s * nt + i, g))],
        out_shape=[jax.ShapeDtypeStruct((tokens, NSA_HEADS * SLOT), F32),
                   jax.ShapeDtypeStruct((tokens, NSA_HEADS * SLOT), F32)],
        compiler_params=_cparams(3),
        name="prompt_slc_win_attention",
    )(q_slots, sel, ks_t, vs_t, kw_t, vw_t, onehot_t)


def _sample_slc_kernel(idx_ref, pt_ref, q_ref, kn_ref, vn_ref, *refs, topn, past, t_pad):
    k_refs = refs[:topn]
    v_refs = refs[topn:2 * topn]
    o_ref = refs[2 * topn]
    s_id, g_id, t_id = pl.program_id(0), pl.program_id(1), pl.program_id(2)
    base = ((s_id * NSA_KV + g_id) * t_pad + t_id) * topn
    q_pos = past + t_id
    cur = q_pos // SLC_BLOCK
    first_new = past // SLC_BLOCK
    q = q_ref[0, 0, 0].astype(BF16)
    kb = jnp.concatenate([r[0, 0] for r in k_refs], axis=1).astype(BF16)
    vb = jnp.concatenate([r[0, 0] for r in v_refs], axis=1).astype(BF16)
    n_keys = topn * PAGE_SIZE
    lane = lax.broadcasted_iota(jnp.int32, (1, n_keys), 1)
    slot = lane // PAGE_SIZE
    in_page = lane % PAGE_SIZE
    k_pos = in_page
    limit = jnp.zeros((1, n_keys), jnp.int32)
    n_new = jnp.int32(0)
    for kk in range(topn):
        b = idx_ref[base + kk]
        here = slot == kk
        k_pos = jnp.where(here, (b // 2) * PAGE_SIZE + in_page, k_pos)
        last = jnp.where(b <= cur, jnp.minimum(q_pos, past - 1), -1)
        limit = jnp.where(here, jnp.where(in_page // SLC_BLOCK == b % 2, last, -1), limit)
        n_new = n_new + jnp.where(b == first_new, 1, 0)
    valid = k_pos <= limit
    s_old = jnp.where(valid, _dot(q, kb), NEG_INF)
    new_lane = lax.broadcasted_iota(jnp.int32, (1, SUBLANES), 1)
    valid_new = past + new_lane <= jnp.where(n_new > 0, q_pos, past - 1)
    s_new = jnp.where(valid_new, _dot(q, kn_ref[0, 0].astype(BF16)), NEG_INF)
    m = jnp.maximum(jnp.max(s_old, axis=-1, keepdims=True), jnp.max(s_new, axis=-1, keepdims=True))
    p_old = jnp.exp(s_old - m)
    p_new = jnp.exp(s_new - m)
    l = jnp.sum(p_old, axis=-1, keepdims=True) + jnp.sum(p_new, axis=-1, keepdims=True)
    o = _nt_dot(p_old.astype(BF16), vb) + _nt_dot(p_new.astype(BF16), vn_ref[0, 0].astype(BF16))
    o_ref[0, 0, 0] = o / l


def sample_slc_attention(q_rows, idx, page_table, pool_k, pool_v, k_new_t, v_new_t, t_real, past):
    n_seq, _, t_pad, _, dh = q_rows.shape
    topn = idx.shape[-1]
    n_pages = page_table.shape[1]
    assert past % SLC_BLOCK == 0 and t_real <= SUBLANES and PAGE_SIZE == 2 * SLC_BLOCK
    last_old = past // SLC_BLOCK - 1

    def blk_map(kk):
        def f(s, g, t, idx_ref, pt_ref):
            b = jnp.clip(idx_ref[((s * NSA_KV + g) * t_pad + t) * topn + kk], 0, last_old)
            return (pt_ref[s * n_pages + b // 2], g, 0, 0)
        return f

    blk_specs = [pl.BlockSpec((1, 1, dh, PAGE_SIZE), blk_map(kk)) for kk in range(topn)]
    new_spec = pl.BlockSpec((1, 1, dh, SUBLANES), lambda s, g, t, i_r, p_r: (s, g, 0, 0))
    q_spec = pl.BlockSpec((1, 1, 1, SUBLANES, dh), lambda s, g, t, i_r, p_r: (s, g, t, 0, 0))
    grid_spec = pltpu.PrefetchScalarGridSpec(
        num_scalar_prefetch=2,
        grid=(n_seq, NSA_KV, t_real),
        in_specs=[q_spec, new_spec, new_spec] + blk_specs + blk_specs,
        out_specs=q_spec,
    )
    return pl.pallas_call(
        functools.partial(_sample_slc_kernel, topn=topn, past=past, t_pad=t_pad),
        grid_spec=grid_spec,
        out_shape=jax.ShapeDtypeStruct((n_seq, NSA_KV, t_real, SUBLANES, dh), F32),
        compiler_params=_cparams(3),
        name="sample_slc_attention",
    )(idx.reshape(-1), page_table.reshape(-1), q_rows, k_new_t, v_new_t, *([pool_k] * topn), *([pool_v] * topn))


def _sample_win_kernel(q_ref, wk_ref, wv_ref, kn_ref, vn_ref, o_ref, *, past, t_pad):
    rows = t_pad * SUBLANES
    wb = wk_ref.shape[-1]
    q = q_ref[0, 0].reshape(rows, q_ref.shape[-1]).astype(BF16)
    q_pos = past + lax.broadcasted_iota(jnp.int32, (rows, 1), 0) // SUBLANES
    k_pos = past - wb + lax.broadcasted_iota(jnp.int32, (1, wb), 1)
    dist = q_pos - k_pos
    valid = (dist >= 0) & (dist < WINDOW) & (k_pos >= 0)
    s_old = jnp.where(valid, _dot(q, wk_ref[0, 0].astype(BF16)), NEG_INF)
    n_pos = past + lax.broadcasted_iota(jnp.int32, (1, SUBLANES), 1)
    dist_n = q_pos - n_pos
    valid_n = (dist_n >= 0) & (dist_n < WINDOW)
    s_new = jnp.where(valid_n, _dot(q, kn_ref[0, 0].astype(BF16)), NEG_INF)
    m = jnp.maximum(jnp.max(s_old, axis=-1, keepdims=True), jnp.max(s_new, axis=-1, keepdims=True))
    p_old = jnp.exp(s_old - m)
    p_new = jnp.exp(s_new - m)
    l = jnp.sum(p_old, axis=-1, keepdims=True) + jnp.sum(p_new, axis=-1, keepdims=True)
    o = _nt_dot(p_old.astype(BF16), wv_ref[0, 0].astype(BF16)) + _nt_dot(p_new.astype(BF16),
                                                                      vn_ref[0, 0].astype(BF16))
    o_ref[0, 0] = (o / l).reshape(t_pad, SUBLANES, o.shape[-1])


def sample_win_attention(q_rows, win_k_t, win_v_t, k_new_t, v_new_t, past):
    n_seq, _, t_pad, _, dh = q_rows.shape
    wb = win_k_t.shape[-1]
    q_spec = pl.BlockSpec((1, 1, t_pad, SUBLANES, dh), lambda s, g: (s, g, 0, 0, 0))
    win_spec = pl.BlockSpec((1, 1, dh, wb), lambda s, g: (s, g, 0, 0))
    new_spec = pl.BlockSpec((1, 1, dh, SUBLANES), lambda s, g: (s, g, 0, 0))
    return pl.pallas_call(
        functools.partial(_sample_win_kernel, past=past, t_pad=t_pad),
        grid=(n_seq, NSA_KV),
        in_specs=[q_spec, win_spec, win_spec, new_spec, new_spec],
        out_specs=q_spec,
        out_shape=jax.ShapeDtypeStruct(q_rows.shape, F32),
        compiler_params=_cparams(2),
        name="sample_win_attention",
    )(q_rows, win_k_t, win_v_t, k_new_t, v_new_t)


def _to_slots(a):
    lead = a.shape[:-1]
    n = a.shape[-1] // NSA_DH
    a = a.reshape(*lead, n, NSA_DH)
    a = jnp.pad(a, [(0, 0)] * (a.ndim - 1) + [(0, SLOT - NSA_DH)])
    return a.reshape(*lead, n * SLOT)


def _odd_weights(w_in, w_out):
    d = w_in.shape[0]
    hq = NSA_HEADS * NSA_DH
    kvw = NSA_KV * NSA_DH
    wq = _to_slots(w_in[:, :hq] * np.float32(NSA_DH ** -0.5))
    wg = jnp.pad(w_in[:, hq + 6 * kvw:], ((0, 0), (0, LANES - 3 * NSA_HEADS)))
    w_q = jnp.concatenate([wq, wg], axis=1).astype(BF16)
    w_kvt = w_in[:, hq:hq + 6 * kvw].T.astype(BF16)
    wo = jnp.pad(w_out.reshape(NSA_HEADS, NSA_DH, d), ((0, 0), (0, SLOT - NSA_DH), (0, 0)))
    wo = wo.reshape(NSA_HEADS * SLOT, d).astype(BF16)
    k = NSA_HEADS * SLOT
    e = np.zeros((LANES, 3 * k), np.float32)
    for c in range(3):
        for h in range(NSA_HEADS):
            e[c * NSA_HEADS + h, c * k + h * SLOT:c * k + (h + 1) * SLOT] = 1.0
    return w_q, w_kvt, wo, jnp.asarray(e, dtype=BF16)


def _group_rows(q_slots, n_seq, t_pad):
    q = q_slots.reshape(n_seq, t_pad, NSA_KV, NSA_GROUP, SLOT)[..., :NSA_DH]
    q = q.transpose(0, 2, 1, 3, 4)
    return jnp.pad(q, ((0, 0), (0, 0), (0, 0), (0, SUBLANES - NSA_GROUP), (0, 0)))


def _ungroup_rows(o, n_seq, t_pad):
    t = o.shape[2]
    o = o[:, :, :, :NSA_GROUP].transpose(0, 2, 1, 3, 4)
    o = jnp.pad(o, ((0, 0), (0, t_pad - t), (0, 0), (0, 0), (0, SLOT - NSA_DH)))
    return o.reshape(n_seq * t_pad, NSA_HEADS * SLOT)


def _feature_major(cache):
    return cache.transpose(0, 2, 3, 1)


def _token_major(a_t):
    return a_t.transpose(0, 3, 1, 2)


def _pad_rows(a, t_pad):
    return jnp.pad(a, ((0, 0), (0, t_pad - a.shape[1])) + ((0, 0),) * (a.ndim - 2))


def kernel(x_prompt, x_sample, state_sconv, state_ret, cache_cmp_k, cache_cmp_v, cache_slc_k, cache_slc_v,
           cache_win_k, cache_win_v, state_ffn_conv, page_table,
           w_in_even, sconv_w, sconv_b, ret_gn_g, w_out_even,
           w_in_odd, cmp_pe, cmp_w1, cmp_w2, w_out_odd,
           ln_mix_g, ln_mix_b, ffn_w_up, ffn_conv_w, ffn_conv_b, ffn_w_down, ln_ffn_g, ln_ffn_b):
    b_p, s_p, d_model = x_prompt.shape
    b_s, t_s, _ = x_sample.shape
    n_pages = page_table.shape[1]
    past = n_pages * PAGE_SIZE
    t_pad = SUBLANES
    assert t_s <= t_pad and t_s >= SCONV_W - 1 and t_s < CMP_STRIDE and past % PAGE_SIZE == 0
    assert s_p % RET_CHUNK == 0 and s_p % PAGE_SIZE == 0
    d_sconv = sconv_w.shape[-1]
    d_ff = ffn_conv_w.shape[-1]
    gd = NSA_KV * NSA_DH
    depth = ln_mix_g.shape[0]

    xp = x_prompt.reshape(b_p * s_p, d_model)
    xs = _pad_rows(x_sample, t_pad).reshape(b_s * t_pad, d_model)
    outs = {k: [] for k in ("sconv_p", "sconv_s", "ret_p", "ret_s", "cmp_k_p", "cmp_v_p", "slc_k_p", "slc_v_p",
                            "cmp_k_s", "cmp_v_s", "slc_k_s", "slc_v_s", "win_k_p", "win_v_p", "win_k_s",
                            "win_v_s", "ffn_p", "ffn_s")}

    for layer in range(depth):
        if layer % 2 == 0:
            e = layer // 2
            w_in = w_in_even[e].astype(BF16)
            w_out = w_out_even[e].astype(BF16)
            n_in = w_in.shape[1]
            (zp,) = matmul_split(xp, w_in, [n_in], [F32])
            yp, hc, st = even_mixer(zp, jnp.zeros((b_p, SCONV_W - 1, d_sconv), F32),
                                    jnp.zeros((b_p,) + state_ret.shape[2:], F32), jnp.arange(s_p),
                                    RET_CHUNK, RET_CHUNK, sconv_w[e], sconv_b[e], ret_gn_g[e])
            outs["sconv_p"].append(hc)
            outs["ret_p"].append(st)
            xp = matmul_residual_ln(yp, w_out, xp, ln_mix_g[layer], ln_mix_b[layer])
            (zs,) = matmul_split(xs, w_in, [n_in], [F32])
            ys, hc, st = even_mixer(zs, state_sconv[e], state_ret[e], past + jnp.arange(t_pad),
                                    t_pad, t_s, sconv_w[e], sconv_b[e], ret_gn_g[e])
            outs["sconv_s"].append(hc)
            outs["ret_s"].append(st)
            xs = matmul_residual_ln(ys, w_out, xs, ln_mix_g[layer], ln_mix_b[layer])
        else:
            o = layer // 2
            w_q, w_kvt, w_out, e_gate = _odd_weights(w_in_odd[o], w_out_odd[o])
            pe, w1, w2 = cmp_pe[o], cmp_w1[o], cmp_w2[o]
            qp, gp, kc, vc, ks, vs, kw, vw = nsa_projection(xp, b_p, w_q, w_kvt, BF16)
            as_cache = lambda a_t: _token_major(a_t.reshape(b_p, NSA_KV, NSA_DH, -1))
            keep = min(WINDOW, s_p)
            outs["cmp_k_p"].append(as_cache(kc))
            outs["cmp_v_p"].append(as_cache(vc))
            outs["slc_k_p"].append(as_cache(ks))
            outs["slc_v_p"].append(as_cache(vs))
            outs["win_k_p"].append(as_cache(kw[:, :, s_p - keep:]))
            outs["win_v_p"].append(as_cache(vw[:, :, s_p - keep:]))
            kcc = compress(kc.reshape(b_p, NSA_KV, NSA_DH, s_p), None, pe[0], w1[0], w2[0])
            vcc = compress(vc.reshape(b_p, NSA_KV, NSA_DH, s_p), None, pe[1], w1[1], w2[1])
            n_cmp = s_p // CMP_STRIDE - CMP_LEN // CMP_STRIDE + 1
            n_slc = s_p // SLC_BLOCK
            oc, sel = cmp_attention_select_prompt(qp, kcc, vcc, b_p, n_cmp, n_slc, 256)
            osl, ow = prompt_slc_win_attention(qp, sel, ks, vs, kw, vw, b_p, 256)
            xp = nsa_merge_residual_ln(oc, osl, ow, gp, e_gate, w_out, xp, ln_mix_g[layer], ln_mix_b[layer])
            qs, gs, *kv_s = nsa_projection(xs, 1, w_q, w_kvt, F32)
            kc, vc, ks, vs, kw, vw = [a.reshape(NSA_KV, NSA_DH, b_s, t_pad).transpose(2, 0, 1, 3) for a in kv_s]
            new_rows = lambda a_t: _token_major(a_t[..., :t_s])
            outs["cmp_k_s"].append(new_rows(kc))
            outs["cmp_v_s"].append(new_rows(vc))
            outs["slc_k_s"].append(new_rows(ks))
            outs["slc_v_s"].append(new_rows(vs))
            win_k = _feature_major(cache_win_k[o])
            win_v = _feature_major(cache_win_v[o])
            wb = win_k.shape[-1]
            keep = min(WINDOW, wb + t_s)
            outs["win_k_s"].append(_token_major(jnp.concatenate([win_k, kw[..., :t_s]], axis=-1)[..., -keep:]))
            outs["win_v_s"].append(_token_major(jnp.concatenate([win_v, vw[..., :t_s]], axis=-1)[..., -keep:]))
            kcc = compress(_feature_major(cache_cmp_k[o]), page_table, pe[0], w1[0], w2[0])
            vcc = compress(_feature_major(cache_cmp_v[o]), page_table, pe[1], w1[1], w2[1])
            n_cmp = (past + t_s) // CMP_STRIDE - CMP_LEN // CMP_STRIDE + 1
            n_slc = -(-(past + t_s) // SLC_BLOCK)
            oc, _, idx = cmp_attention_select(qs, kcc, vcc, b_s, n_cmp, n_slc, 0, past, t_pad)
            topn = min(SLC_TOPN, n_slc)
            q_rows = _group_rows(qs, b_s, t_pad)
            osl = sample_slc_attention(q_rows, idx[..., :topn], page_table, _feature_major(cache_slc_k[o]),
                                       _feature_major(cache_slc_v[o]), ks, vs, t_s, past)
            ow = sample_win_attention(q_rows, win_k, win_v, kw, vw, past)
            xs = nsa_merge_residual_ln(oc, _ungroup_rows(osl, b_s, t_pad), _ungroup_rows(ow, b_s, t_pad), gs,
                                       e_gate, w_out, xs, ln_mix_g[layer], ln_mix_b[layer])
        w_up = ffn_w_up[layer].astype(BF16)
        w_down = ffn_w_down[layer].astype(BF16)
        hp, hist_p = ffn_up_sequences(xp, b_p, jnp.zeros((b_p, FFN_W - 1, d_ff), F32), w_up,
                                      ffn_conv_w[layer], ffn_conv_b[layer])
        outs["ffn_p"].append(hist_p)
        xp = matmul_residual_ln(hp, w_down, xp, ln_ffn_g[layer], ln_ffn_b[layer])
        hs, a_s = ffn_up_short(xs, state_ffn_conv[layer], w_up, ffn_conv_w[layer], ffn_conv_b[layer])
        outs["ffn_s"].append(a_s.reshape(b_s, t_pad, d_ff)[:, t_s - (FFN_W - 1):t_s])
        xs = matmul_residual_ln(hs, w_down, xs, ln_ffn_g[layer], ln_ffn_b[layer])

    st = jnp.stack
    y_p = xp.reshape(b_p, s_p, d_model)
    y_s = xs.reshape(b_s, t_pad, d_model)[:, :t_s]
    order = ("sconv_p", "sconv_s", "ret_p", "ret_s", "cmp_k_p", "cmp_v_p", "slc_k_p", "slc_v_p",
             "cmp_k_s", "cmp_v_s", "slc_k_s", "slc_v_s", "win_k_p", "win_v_p", "win_k_s", "win_v_s",
             "ffn_p", "ffn_s")
    return (y_p, y_s) + tuple(st(outs[k]) for k in order)
```

```python
import functools

import numpy as np
import jax
import jax.numpy as jnp
from jax import lax
from jax.experimental import pallas as pl
from jax.experimental.pallas import tpu as pltpu

F32 = jnp.float32
BF16 = jnp.bfloat16

SUBLANES = 8
LANES = 128
VMEM_LIMIT_BYTES = 56 * 1024 * 1024

DEPTH = 2
SCONV_W = 3
RET_HEADS = 4
RET_CHUNK = 128
ROPE_BASE = 10000.0
NSA_HEADS = 16
NSA_KV = 4
NSA_GROUP = NSA_HEADS // NSA_KV
NSA_DH = 64
CMP_LEN = 32
CMP_STRIDE = 16
SLC_BLOCK = 64
SLC_TOPN = 16
WINDOW = 512
PAGE_SIZE = 128
FFN_W = 3
ALPHA = (2.0 * DEPTH) ** 0.25
LN_EPS = 1e-5
NEG_INF = -1e30
REMOVED = -3e38
FORCE_BONUS = 1e4
SLOT = 2 * NSA_DH


def _cparams(n_grid):
    return pltpu.CompilerParams(dimension_semantics=("arbitrary",) * n_grid,
                                vmem_limit_bytes=VMEM_LIMIT_BYTES)


def _row_tile(m, want):
    t = min(m, want)
    assert m % t == 0, (m, t)
    return t


def _nt_dot(a, b):
    return lax.dot_general(a, b, (((1,), (1,)), ((), ())), preferred_element_type=F32)


def _tn_dot(a, b):
    return lax.dot_general(a, b, (((0,), (0,)), ((), ())), preferred_element_type=F32)


def _dot(a, b):
    return jnp.dot(a, b, preferred_element_type=F32)


def _gelu(x):
    return 0.5 * x * (1.0 + jnp.tanh(np.float32(np.sqrt(2.0 / np.pi)) * (x + 0.044715 * (x * x * x))))


def _layer_norm_rows(r, g, b):
    mu = jnp.mean(r, axis=-1, keepdims=True)
    d = r - mu
    var = jnp.mean(d * d, axis=-1, keepdims=True)
    return d * lax.rsqrt(var + LN_EPS) * g + b


def _mm_split_kernel(x_ref, w_ref, *o_refs, cuts):
    acc = _dot(x_ref[...].astype(BF16), w_ref[...])
    for o_ref, (lo, hi) in zip(o_refs, cuts):
        o_ref[...] = acc[:, lo:hi].astype(o_ref.dtype)


def matmul_split(x, w_bf16, widths, dtypes, tm=256):
    m, k = x.shape
    n = w_bf16.shape[1]
    assert sum(widths) == n and all(wd % LANES == 0 for wd in widths)
    tm = _row_tile(m, tm)
    cuts, lo = [], 0
    for wd in widths:
        cuts.append((lo, lo + wd))
        lo += wd
    return pl.pallas_call(
        functools.partial(_mm_split_kernel, cuts=tuple(cuts)),
        grid=(m // tm,),
        in_specs=[pl.BlockSpec((tm, k), lambda i: (i, 0)),
                  pl.BlockSpec((k, n), lambda i: (0, 0))],
        out_specs=[pl.BlockSpec((tm, wd), lambda i: (i, 0)) for wd in widths],
        out_shape=[jax.ShapeDtypeStruct((m, wd), dt) for wd, dt in zip(widths, dtypes)],
        compiler_params=_cparams(1),
        name="matmul_split",
    )(x, w_bf16)


def _nsa_proj_kernel(x_ref, wq_ref, wkvt_ref, q_ref, g_ref, *kv_refs, nq):
    xb = x_ref[...].astype(BF16)
    acc = _dot(xb, wq_ref[...])
    q_ref[...] = acc[:, :nq].astype(q_ref.dtype)
    g_ref[...] = acc[:, nq:]
    acc_t = _nt_dot(wkvt_ref[...], xb)
    gd = acc_t.shape[0] // len(kv_refs)
    for i, r in enumerate(kv_refs):
        r[0] = acc_t[i * gd:(i + 1) * gd, :]


def nsa_projection(x, n_seq, wq_bf16, wkvt_bf16, q_dtype, n_kv=6, tm=256):
    m, d = x.shape
    seq = m // n_seq
    tm = _row_tile(seq, tm)
    nt = seq // tm
    nq = wq_bf16.shape[1] - LANES
    gd = wkvt_bf16.shape[0] // n_kv
    return pl.pallas_call(
        functools.partial(_nsa_proj_kernel, nq=nq),
        grid=(n_seq, nt),
        in_specs=[pl.BlockSpec((tm, d), lambda s, i: (s * nt + i, 0)),
                  pl.BlockSpec((d, nq + LANES), lambda s, i: (0, 0)),
                  pl.BlockSpec((n_kv * gd, d), lambda s, i: (0, 0))],
        out_specs=[pl.BlockSpec((tm, nq), lambda s, i: (s * nt + i, 0)),
                   pl.BlockSpec((tm, LANES), lambda s, i: (s * nt + i, 0))] + [
                      pl.BlockSpec((1, gd, tm), lambda s, i: (s, 0, i)) for _ in range(n_kv)],
        out_shape=[jax.ShapeDtypeStruct((m, nq), q_dtype), jax.ShapeDtypeStruct((m, LANES), F32)] + [
            jax.ShapeDtypeStruct((n_seq, gd, seq), F32) for _ in range(n_kv)],
        compiler_params=_cparams(2),
        name="nsa_projection",
    )(x, wq_bf16, wkvt_bf16)


def _mm_res_ln_kernel(a_ref, w_ref, x_ref, g_ref, b_ref, o_ref):
    y = _dot(a_ref[...].astype(BF16), w_ref[...])
    o_ref[...] = _layer_norm_rows(ALPHA * x_ref[...] + y, g_ref[...], b_ref[...])


def matmul_residual_ln(a, w_bf16, x, g, b, tm=256):
    m, k = a.shape
    d = w_bf16.shape[1]
    tm = _row_tile(m, tm)
    return pl.pallas_call(
        _mm_res_ln_kernel,
        grid=(m // tm,),
        in_specs=[pl.BlockSpec((tm, k), lambda i: (i, 0)),
                  pl.BlockSpec((k, d), lambda i: (0, 0)),
                  pl.BlockSpec((tm, d), lambda i: (i, 0)),
                  pl.BlockSpec((1, d), lambda i: (0, 0)),
                  pl.BlockSpec((1, d), lambda i: (0, 0))],
        out_specs=pl.BlockSpec((tm, d), lambda i: (i, 0)),
        out_shape=jax.ShapeDtypeStruct((m, d), F32),
        compiler_params=_cparams(1),
        name="matmul_residual_ln",
    )(a, w_bf16, x, g.reshape(1, d), b.reshape(1, d))


def _expand_gates(gates_raw, e_ref):
    sig = jax.nn.sigmoid(gates_raw)
    hi = sig.astype(BF16)
    lo = (sig - hi.astype(F32)).astype(BF16)
    e = e_ref[...]
    return _dot(hi, e) + _dot(lo, e)


def _nsa_merge_ln_kernel(oc_ref, os_ref, ow_ref, gt_ref, e_ref, w_ref, x_ref, g_ref, b_ref, o_ref, *, k):
    gx = _expand_gates(gt_ref[...], e_ref)
    o = gx[:, 0:k] * oc_ref[...] + gx[:, k:2 * k] * os_ref[...] + gx[:, 2 * k:3 * k] * ow_ref[...]
    y = _dot(o.astype(BF16), w_ref[...])
    o_ref[...] = _layer_norm_rows(ALPHA * x_ref[...] + y, g_ref[...], b_ref[...])


def nsa_merge_residual_ln(oc, osl, ow, gates, e_bf16, w_bf16, x, g, b, tm=256):
    m, k = oc.shape
    d = w_bf16.shape[1]
    tm = _row_tile(m, tm)
    row = lambda i: (i, 0)
    fixed = lambda i: (0, 0)
    return pl.pallas_call(
        functools.partial(_nsa_merge_ln_kernel, k=k),
        grid=(m // tm,),
        in_specs=[pl.BlockSpec((tm, k), row), pl.BlockSpec((tm, k), row), pl.BlockSpec((tm, k), row),
                  pl.BlockSpec((tm, LANES), row),
                  pl.BlockSpec((LANES, 3 * k), fixed),
                  pl.BlockSpec((k, d), fixed),
                  pl.BlockSpec((tm, d), row),
                  pl.BlockSpec((1, d), fixed), pl.BlockSpec((1, d), fixed)],
        out_specs=pl.BlockSpec((tm, d), row),
        out_shape=jax.ShapeDtypeStruct((m, d), F32),
        compiler_params=_cparams(1),
        name="nsa_merge_residual_ln",
    )(oc, osl, ow, gates, e_bf16, w_bf16, x, g.reshape(1, d), b.reshape(1, d))


def _even_mixer_kernel(z_ref, hist_ref, st_ref, cos_ref, sin_ref, decay_ref, qdec_ref, kdec_ref, sdec_ref,
                       cw_ref, cb_ref, gn_ref, y_ref, hist_out_ref, st_out_ref, carry, state,
                       *, rows, valid, dconv, dk):
    c = pl.program_id(1)
    r0 = valid - 2 - (rows - SUBLANES)

    @pl.when(c == 0)
    def _():
        carry[r0:r0 + 2, :] = hist_ref[0]
        state[...] = st_ref[0]

    d = dconv
    h = z_ref[:, 0:d]
    gate_b = z_ref[:, d:2 * d]
    gate_c = z_ref[:, 2 * d:3 * d]
    ch = gate_c * h
    row = lax.broadcasted_iota(jnp.int32, (rows, d), 0)
    h0 = carry[r0:r0 + 1, :]
    h1 = carry[r0 + 1:r0 + 2, :]
    m1 = jnp.where(row == 0, h1, pltpu.roll(ch, 1, 0))
    m2 = jnp.where(row == 0, h0, jnp.where(row == 1, h1, pltpu.roll(ch, 2, 0)))
    u = ((cb_ref[...] + m2 * cw_ref[0:1, :]) + m1 * cw_ref[1:2, :]) + ch * cw_ref[2:3, :]
    y_ref[:, 0:d] = gate_b * u
    carry[...] = ch[rows - SUBLANES:rows, :]
    hist_out_ref[0] = carry[r0:r0 + 2, :]

    cosf = cos_ref[...]
    sinf = sin_ref[...]
    scale = np.float32(dk ** -0.5)
    for hh in range(RET_HEADS):
        q = z_ref[:, 3 * d + hh * dk:3 * d + (hh + 1) * dk]
        k = z_ref[:, 4 * d + hh * dk:4 * d + (hh + 1) * dk]
        v = z_ref[:, 5 * d + hh * dk:5 * d + (hh + 1) * dk]
        gsw = z_ref[:, 6 * d + hh * dk:6 * d + (hh + 1) * dk]
        q = (q * cosf + pltpu.roll(q, dk // 2, 1) * sinf) * scale
        k = k * cosf + pltpu.roll(k, dk // 2, 1) * sinf
        qb = q.astype(BF16)
        vb = v.astype(BF16)
        s_old = state[hh]
        scores = _nt_dot(qb, k.astype(BF16)) * decay_ref[hh]
        intra = _dot(scores.astype(BF16), vb)
        cross = _dot(qb, s_old.astype(BF16)) * qdec_ref[hh]
        kd = (k * kdec_ref[hh]).astype(BF16)
        state[hh] = s_old * sdec_ref[hh] + _tn_dot(kd, vb)
        o = intra + cross
        mu = jnp.mean(o, axis=-1, keepdims=True)
        dv = o - mu
        var = jnp.mean(dv * dv, axis=-1, keepdims=True)
        on = dv * lax.rsqrt(var + LN_EPS) * gn_ref[:, hh * dk:(hh + 1) * dk]
        y_ref[:, d + hh * dk:d + (hh + 1) * dk] = (gsw * jax.nn.sigmoid(gsw)) * on
    st_out_ref[0] = state[...]


def _retention_tables(rows, valid, dk):
    log_gamma = jnp.log1p(-jnp.exp2(-5.0 - jnp.arange(RET_HEADS, dtype=F32)))
    n = jnp.arange(rows, dtype=F32)
    diff = n[:, None] - n[None, :]
    lg = log_gamma[:, None, None]
    decay = jnp.where(diff >= 0, jnp.exp(lg * jnp.maximum(diff, 0.0)), 0.0)
    q_dec = jnp.exp((n[None, :] + 1.0) * log_gamma[:, None])
    k_dec = jnp.where(n[None, :] < valid, jnp.exp((valid - 1.0 - n[None, :]) * log_gamma[:, None]), 0.0)
    s_dec = jnp.exp(valid * log_gamma)
    bc = lambda a: jnp.broadcast_to(a[:, :, None], (RET_HEADS, rows, dk))
    return decay, bc(q_dec), bc(k_dec), jnp.broadcast_to(s_dec[:, None, None], (RET_HEADS, 1, dk))


def _rope_tables(pos, dk):
    half = dk // 2
    inv = ROPE_BASE ** (-jnp.arange(half, dtype=F32) / half)
    ang = pos.astype(F32)[:, None] * inv
    cos, sin = jnp.cos(ang), jnp.sin(ang)
    return jnp.concatenate([cos, cos], axis=-1), jnp.concatenate([-sin, sin], axis=-1)


def even_mixer(z, hist, st, pos, rows, valid, conv_w, conv_b, gn_g):
    n_seq, _, dconv = hist.shape
    dk = st.shape[-1]
    n_chunks = z.shape[0] // (n_seq * rows)
    cosf, sinf = _rope_tables(pos, dk)
    decay, q_dec, k_dec, s_dec = _retention_tables(rows, valid, dk)
    fixed3 = lambda s, c: (0, 0, 0)
    fixed2 = lambda s, c: (0, 0)
    return pl.pallas_call(
        functools.partial(_even_mixer_kernel, rows=rows, valid=valid, dconv=dconv, dk=dk),
        grid=(n_seq, n_chunks),
        in_specs=[pl.BlockSpec((rows, 7 * dconv), lambda s, c: (s * n_chunks + c, 0)),
                  pl.BlockSpec((1, 2, dconv), lambda s, c: (s, 0, 0)),
                  pl.BlockSpec((1, RET_HEADS, dk, dk), lambda s, c: (s, 0, 0, 0)),
                  pl.BlockSpec((rows, dk), lambda s, c: (c, 0)),
                  pl.BlockSpec((rows, dk), lambda s, c: (c, 0)),
                  pl.BlockSpec((RET_HEADS, rows, rows), fixed3),
                  pl.BlockSpec((RET_HEADS, rows, dk), fixed3),
                  pl.BlockSpec((RET_HEADS, rows, dk), fixed3),
                  pl.BlockSpec((RET_HEADS, 1, dk), fixed3),
                  pl.BlockSpec((SCONV_W, dconv), fixed2),
                  pl.BlockSpec((1, dconv), fixed2),
                  pl.BlockSpec((1, RET_HEADS * dk), fixed2)],
        out_specs=[pl.BlockSpec((rows, 2 * dconv), lambda s, c: (s * n_chunks + c, 0)),
                   pl.BlockSpec((1, 2, dconv), lambda s, c: (s, 0, 0)),
                   pl.BlockSpec((1, RET_HEADS, dk, dk), lambda s, c: (s, 0, 0, 0))],
        out_shape=[jax.ShapeDtypeStruct((z.shape[0], 2 * dconv), F32),
                   jax.ShapeDtypeStruct((n_seq, 2, dconv), F32),
                   jax.ShapeDtypeStruct((n_seq, RET_HEADS, dk, dk), F32)],
        scratch_shapes=[pltpu.VMEM((SUBLANES, dconv), F32), pltpu.VMEM((RET_HEADS, dk, dk), F32)],
        compiler_params=_cparams(2),
        name="even_mixer",
    )(z, hist, st, cosf, sinf, decay, q_dec, k_dec, s_dec, conv_w, conv_b.reshape(1, dconv),
      gn_g.reshape(1, RET_HEADS * dk))


def _conv_gate(a, gate, m1, m2, cw_ref, cb_ref):
    conv = ((cb_ref[...] + m2 * cw_ref[0:1, :]) + m1 * cw_ref[1:2, :]) + a * cw_ref[2:3, :]
    return _gelu(conv) * gate


def _ffn_up_seq_kernel(x_ref, wa_ref, wg_ref, h_ref, cw_ref, cb_ref, o_ref, hist_out_ref, carry, *, tm):
    @pl.when(pl.program_id(2) == 0)
    def _():
        carry[SUBLANES - 2:SUBLANES, :] = h_ref[0]

    xb = x_ref[...].astype(BF16)
    a = _dot(xb, wa_ref[...])
    gate = _dot(xb, wg_ref[...])
    row = lax.broadcasted_iota(jnp.int32, a.shape, 0)
    h0 = carry[SUBLANES - 2:SUBLANES - 1, :]
    h1 = carry[SUBLANES - 1:SUBLANES, :]
    m1 = jnp.where(row == 0, h1, pltpu.roll(a, 1, 0))
    m2 = jnp.where(row == 0, h0, jnp.where(row == 1, h1, pltpu.roll(a, 2, 0)))
    o_ref[...] = _conv_gate(a, gate, m1, m2, cw_ref, cb_ref).astype(o_ref.dtype)
    carry[...] = a[tm - SUBLANES:tm, :]
    hist_out_ref[0] = carry[SUBLANES - 2:SUBLANES, :]


def ffn_up_sequences(x, n_seq, hist, w_up_bf16, conv_w, conv_b, tm=256, n_col=2):
    m, k = x.shape
    dff = conv_w.shape[1]
    seq = m // n_seq
    tm = _row_tile(seq, tm)
    tps = seq // tm
    tn = dff // n_col
    assert tn % LANES == 0
    return pl.pallas_call(
        functools.partial(_ffn_up_seq_kernel, tm=tm),
        grid=(n_col, n_seq, tps),
        in_specs=[pl.BlockSpec((tm, k), lambda j, s, i: (s * tps + i, 0)),
                  pl.BlockSpec((k, tn), lambda j, s, i: (0, j)),
                  pl.BlockSpec((k, tn), lambda j, s, i: (0, j + n_col)),
                  pl.BlockSpec((1, 2, tn), lambda j, s, i: (s, 0, j)),
                  pl.BlockSpec((FFN_W, tn), lambda j, s, i: (0, j)),
                  pl.BlockSpec((1, tn), lambda j, s, i: (0, j))],
        out_specs=[pl.BlockSpec((tm, tn), lambda j, s, i: (s * tps + i, j)),
                   pl.BlockSpec((1, 2, tn), lambda j, s, i: (s, 0, j))],
        out_shape=[jax.ShapeDtypeStruct((m, dff), BF16),
                   jax.ShapeDtypeStruct((n_seq, 2, dff), F32)],
        scratch_shapes=[pltpu.VMEM((SUBLANES, tn), F32)],
        compiler_params=_cparams(3),
        name="ffn_up_sequences",
    )(x, w_up_bf16, w_up_bf16, hist, conv_w, conv_b.reshape(1, dff))


def _ffn_up_short_kernel(x_ref, wa_ref, wg_ref, h1_ref, h2_ref, cw_ref, cb_ref, o_ref, a_ref):
    xb = x_ref[...].astype(BF16)
    a = _dot(xb, wa_ref[...])
    gate = _dot(xb, wg_ref[...])
    t = lax.broadcasted_iota(jnp.int32, a.shape, 0) % SUBLANES
    m1 = jnp.where(t == 0, h1_ref[...], pltpu.roll(a, 1, 0))
    m2 = jnp.where(t < 2, h2_ref[...], pltpu.roll(a, 2, 0))
    o_ref[...] = _conv_gate(a, gate, m1, m2, cw_ref, cb_ref).astype(o_ref.dtype)
    a_ref[...] = a


def ffn_up_short(x, hist, w_up_bf16, conv_w, conv_b, n_col=2):
    m, k = x.shape
    dff = conv_w.shape[1]
    n_seq = m // SUBLANES
    tn = dff // n_col
    zeros = jnp.zeros((n_seq, SUBLANES, dff), F32)
    h1 = zeros.at[:, 0].set(hist[:, 1]).reshape(m, dff)
    h2 = zeros.at[:, 0].set(hist[:, 0]).at[:, 1].set(hist[:, 1]).reshape(m, dff)
    col = lambda j: (0, j)
    return pl.pallas_call(
        _ffn_up_short_kernel,
        grid=(n_col,),
        in_specs=[pl.BlockSpec((m, k), lambda j: (0, 0)),
                  pl.BlockSpec((k, tn), col),
                  pl.BlockSpec((k, tn), lambda j: (0, j + n_col)),
                  pl.BlockSpec((m, tn), col), pl.BlockSpec((m, tn), col),
                  pl.BlockSpec((FFN_W, tn), col), pl.BlockSpec((1, tn), col)],
        out_specs=[pl.BlockSpec((m, tn), col), pl.BlockSpec((m, tn), col)],
        out_shape=[jax.ShapeDtypeStruct((m, dff), F32), jax.ShapeDtypeStruct((m, dff), F32)],
        compiler_params=_cparams(1),
        name="ffn_up_short",
    )(x, w_up_bf16, w_up_bf16, h1, h2, conv_w, conv_b.reshape(1, dff))


def _compress_kernel(pt_ref, *refs, pages):
    page_refs = refs[:pages + 1]
    w2t_ref, pecol_ref, w1_ref, w2_ref, o_ref, rows_ref = refs[pages + 1:]
    cpp = PAGE_SIZE // CMP_STRIDE
    n = (pages + 1) * cpp
    hidden = w1_ref.shape[1]
    gpr = LANES // NSA_DH
    pieces = NSA_KV // gpr
    for i, r in enumerate(page_refs):
        for pc in range(pieces):
            tile = r[0, pc * gpr:(pc + 1) * gpr].reshape(LANES, PAGE_SIZE)
            rows_ref[pc, i * PAGE_SIZE:(i + 1) * PAGE_SIZE, :] = tile.T
    pe_term = jnp.sum(pecol_ref[...] * w1_ref[...], axis=0, keepdims=True)
    acc = jnp.zeros((pieces * n, gpr * 2 * hidden), F32)
    for tp in range(CMP_STRIDE // 2):
        lhs = jnp.concatenate(
            [jnp.concatenate([rows_ref[pc, pl.ds(2 * tp, n, stride=CMP_STRIDE), :],
                              rows_ref[pc, pl.ds(2 * tp + 1, n, stride=CMP_STRIDE), :]], axis=1)
             for pc in range(pieces)], axis=0)
        acc = acc + _dot(lhs.astype(BF16), w2t_ref[tp])
    for pc in range(pieces):
        for gl in range(gpr):
            a = acc[pc * n:(pc + 1) * n, gl * 2 * hidden:(gl + 1) * 2 * hidden]
            nxt = pltpu.roll(a, n - 1, 0)
            pre = pe_term + a[:, 0:hidden]
            pre = pre + nxt[:, hidden:2 * hidden]
            o_ref[0, pc * gpr + gl] = _dot(_gelu(pre[0:pages * cpp]).astype(BF16), w2_ref[...])


def compress(rows_t, page_table, pe, w1, w2, pages=32):
    pooled = page_table is not None
    if pooled:
        n_seq, n_pages = page_table.shape
    else:
        n_seq, n_pages = rows_t.shape[0], rows_t.shape[3] // PAGE_SIZE
        page_table = jnp.zeros((1, 1), jnp.int32)
    pages = min(pages, n_pages)
    assert n_pages % pages == 0
    hidden = w1.shape[1]
    cpp = PAGE_SIZE // CMP_STRIDE
    r = CMP_LEN // CMP_STRIDE
    gpr = LANES // NSA_DH
    assert r == 2 and gpr == 2
    w1p = w1.reshape(r, CMP_STRIDE, NSA_DH, hidden)
    w16 = jnp.concatenate([w1p[0], w1p[1]], axis=-1)
    zero = jnp.zeros_like(w16)
    per_tok = jnp.concatenate([jnp.concatenate([w16, zero], axis=-1),
                               jnp.concatenate([zero, w16], axis=-1)], axis=1)
    w2t = per_tok.reshape(CMP_STRIDE // 2, 2 * LANES, gpr * 2 * hidden).astype(BF16)
    w2p = jnp.pad(w2, ((0, 0), (0, SLOT - NSA_DH))).astype(BF16)
    pecol = pe.reshape(CMP_LEN * NSA_DH, 1)

    def page_map(i):
        if pooled:
            return lambda s, j, pt: (pt[s, jnp.minimum(j * pages + i, n_pages - 1)], 0, 0, 0)
        return lambda s, j, pt: (s, 0, 0, jnp.minimum(j * pages + i, n_pages - 1))

    fixed2 = lambda s, j, pt: (0, 0)
    grid_spec = pltpu.PrefetchScalarGridSpec(
        num_scalar_prefetch=1,
        grid=(n_seq, n_pages // pages),
        in_specs=[pl.BlockSpec((1, NSA_KV, NSA_DH, PAGE_SIZE), page_map(i)) for i in range(pages + 1)] + [
            pl.BlockSpec((CMP_STRIDE // 2, 2 * LANES, gpr * 2 * hidden), lambda s, j, pt: (0, 0, 0)),
            pl.BlockSpec((CMP_LEN * NSA_DH, 1), fixed2),
            pl.BlockSpec((CMP_LEN * NSA_DH, hidden), fixed2),
            pl.BlockSpec((hidden, SLOT), fixed2)],
        out_specs=pl.BlockSpec((1, NSA_KV, pages * cpp, SLOT), lambda s, j, pt: (s, 0, j, 0)),
        scratch_shapes=[pltpu.VMEM((NSA_KV // gpr, (pages + 1) * PAGE_SIZE, LANES), F32)],
    )
    return pl.pallas_call(
        functools.partial(_compress_kernel, pages=pages),
        grid_spec=grid_spec,
        out_shape=jax.ShapeDtypeStruct((n_seq, NSA_KV, n_pages * cpp, SLOT), F32),
        compiler_params=_cparams(2),
        name="compress",
    )(page_table, *([rows_t] * (pages + 1)), w2t, pecol, w1, w2p)


def _cmp_select_kernel(q_ref, kc_ref, vc_ref, ov_ref, o_ref, idx_ref, *, tq, n_cmp, n_slc, pos0):
    ncp = kc_ref.shape[2]
    nsp = ov_ref.shape[1]
    hrows = NSA_GROUP * tq
    q_pos = pos0 + (lax.broadcasted_iota(jnp.int32, (hrows, ncp), 0) & (tq - 1))
    blk_i = lax.broadcasted_iota(jnp.int32, (hrows, ncp), 1)
    valid = (blk_i * CMP_STRIDE + (CMP_LEN - 1) <= q_pos) & (blk_i < n_cmp)
    ov = ov_ref[...]
    imps = []
    for g in range(NSA_KV):
        kc = kc_ref[0, g].astype(BF16)
        vc = vc_ref[0, g].astype(BF16)
        heads = range(g * NSA_GROUP, (g + 1) * NSA_GROUP)
        qg = jnp.concatenate([q_ref[:, h * SLOT:(h + 1) * SLOT] for h in heads], axis=0).astype(BF16)
        s = jnp.where(valid, _nt_dot(qg, kc), NEG_INF)
        m = jnp.max(s, axis=-1, keepdims=True)
        e = jnp.where(valid, jnp.exp(s - m), 0.0)
        den = jnp.sum(e, axis=-1, keepdims=True)
        p = e / jnp.where(den > 0.0, den, 1.0)
        o = _dot(p.astype(BF16), vc)
        p_sum = jnp.zeros((tq, ncp), F32)
        for j, h in enumerate(heads):
            o_ref[:, h * SLOT:(h + 1) * SLOT] = o[j * tq:(j + 1) * tq]
            p_sum = p_sum + p[j * tq:(j + 1) * tq]
        hi = p_sum.astype(BF16)
        lo = (p_sum - hi.astype(F32)).astype(BF16)
        imps.append(_dot(hi, ov) + _dot(lo, ov))
    rows = NSA_KV * tq
    imp = jnp.concatenate(imps, axis=0)
    blk = lax.broadcasted_iota(jnp.int32, (rows, nsp), 1)
    cur = (pos0 + (lax.broadcasted_iota(jnp.int32, (rows, nsp), 0) & (tq - 1))) // SLC_BLOCK
    real = blk < n_slc
    causal = real & (blk <= cur)
    forced = (blk == 0) | (blk == cur) | (blk == cur - 1)
    score = jnp.where(causal, imp + jnp.where(forced, FORCE_BONUS, 0.0), NEG_INF)
    score = jnp.where(real, score, REMOVED)
    idx = jnp.zeros((rows, LANES), jnp.int32)
    idx_lane = lax.broadcasted_iota(jnp.int32, (rows, LANES), 1)
    for it in range(min(SLC_TOPN, n_slc)):
        m = jnp.max(score, axis=-1, keepdims=True)
        first = jnp.min(jnp.where(score == m, blk, nsp), axis=-1, keepdims=True)
        score = jnp.where(blk == first, REMOVED, score)
        idx = jnp.where(idx_lane == it, first, idx)
    idx_ref[0] = idx.reshape(NSA_KV, tq, LANES)


def cmp_block_overlap(n_cmp_pad, n_cmp, n_slc, n_slc_pad, lane_off):
    i = np.arange(n_cmp_pad)[:, None]
    j = np.arange(n_slc_pad)[None, :] - lane_off
    start = i * CMP_STRIDE
    hit = (start < (j + 1) * SLC_BLOCK) & (start + CMP_LEN > j * SLC_BLOCK) & (i < n_cmp) & (j >= 0) & (j < n_slc)
    return jnp.asarray(hit.astype(np.float32), dtype=BF16)


def cmp_attention_select(q_slots, kcc, vcc, n_seq, n_cmp, n_slc, pos0):
    tokens = q_slots.shape[0]
    tq = tokens // n_seq
    assert tq & (tq - 1) == 0
    ncp = kcc.shape[2]
    nsp = -(-n_slc // LANES) * LANES
    ov = cmp_block_overlap(ncp, n_cmp, n_slc, nsp, 0)
    hw = NSA_HEADS * SLOT
    return pl.pallas_call(
        functools.partial(_cmp_select_kernel, tq=tq, n_cmp=n_cmp, n_slc=n_slc, pos0=pos0),
        grid=(n_seq,),
        in_specs=[pl.BlockSpec((tq, hw), lambda s: (s, 0)),
                  pl.BlockSpec((1, NSA_KV, ncp, SLOT), lambda s: (s, 0, 0, 0)),
                  pl.BlockSpec((1, NSA_KV, ncp, SLOT), lambda s: (s, 0, 0, 0)),
                  pl.BlockSpec((ncp, nsp), lambda s: (0, 0))],
        out_specs=[pl.BlockSpec((tq, hw), lambda s: (s, 0)),
                   pl.BlockSpec((1, NSA_KV, tq, LANES), lambda s: (s, 0, 0, 0))],
        out_shape=[jax.ShapeDtypeStruct((tokens, hw), F32),
                   jax.ShapeDtypeStruct((n_seq, NSA_KV, tq, LANES), jnp.int32)],
        compiler_params=_cparams(1),
        name="cmp_attention_select",
    )(q_slots, kcc, vcc, ov)


def _cmp_select_prompt_kernel(q_ref, kc_ref, vc_ref, ovt_ref, o_ref, sel_ref, *, tq, n_cmp, n_slc):
    t0 = pl.program_id(2) * tq
    ncp = kc_ref.shape[2]
    nsr = ovt_ref.shape[0]
    kc = kc_ref[0, 0].astype(BF16)
    vc = vc_ref[0, 0].astype(BF16)
    q_pos = t0 + lax.broadcasted_iota(jnp.int32, (ncp, tq), 1)
    blk_i = lax.broadcasted_iota(jnp.int32, (ncp, tq), 0)
    valid = (blk_i * CMP_STRIDE + (CMP_LEN - 1) <= q_pos) & (blk_i < n_cmp)
    p_sum = jnp.zeros((ncp, tq), F32)
    for j in range(NSA_GROUP):
        s = jnp.where(valid, _nt_dot(kc, q_ref[:, j * SLOT:(j + 1) * SLOT]), NEG_INF)
        m = jnp.max(s, axis=0, keepdims=True)
        e = jnp.where(valid, jnp.exp(s - m), 0.0)
        den = jnp.sum(e, axis=0, keepdims=True)
        p = e / jnp.where(den > 0.0, den, 1.0)
        o_ref[:, j * SLOT:(j + 1) * SLOT] = _tn_dot(p.astype(BF16), vc)
        p_sum = p_sum + p
    hi = p_sum.astype(BF16)
    lo = (p_sum - hi.astype(F32)).astype(BF16)
    ovt = ovt_ref[...]
    imp = _dot(ovt, hi) + _dot(ovt, lo)
    blk = lax.broadcasted_iota(jnp.int32, (nsr, tq), 0)
    cur = (t0 + lax.broadcasted_iota(jnp.int32, (nsr, tq), 1)) // SLC_BLOCK
    real = blk < n_slc
    causal = real & (blk <= cur)
    forced = (blk == 0) | (blk == cur) | (blk == cur - 1)
    score = jnp.where(causal, imp + jnp.where(forced, FORCE_BONUS, 0.0), NEG_INF)
    score = jnp.where(real, score, REMOVED)
    picked = jnp.zeros((nsr, tq), jnp.bool_)
    for _ in range(min(SLC_TOPN, n_slc)):
        m = jnp.max(score, axis=0, keepdims=True)
        first = jnp.min(jnp.where(score == m, blk, nsr), axis=0, keepdims=True)
        hit = blk == first
        picked = picked | hit
        score = jnp.where(hit, REMOVED, score)
    bias_t = jnp.where(real & ~(picked & causal), NEG_INF, 0.0)
    slot_t = jnp.concatenate([jnp.zeros((NSA_DH, tq), F32), bias_t], axis=0)
    sel_ref[0, 0] = slot_t.T


def cmp_attention_select_prompt(q_slots, kcc, vcc, n_seq, n_cmp, n_slc, tq):
    tokens = q_slots.shape[0]
    t = tokens // n_seq
    tq = _row_tile(t, tq)
    nt = t // tq
    ncp = kcc.shape[2]
    nsr = SLOT - NSA_DH
    assert n_slc <= nsr
    ovt = cmp_block_overlap(ncp, n_cmp, n_slc, nsr, 0).T
    gw = NSA_GROUP * SLOT
    return pl.pallas_call(
        functools.partial(_cmp_select_prompt_kernel, tq=tq, n_cmp=n_cmp, n_slc=n_slc),
        grid=(n_seq, NSA_KV, nt),
        in_specs=[pl.BlockSpec((tq, gw), lambda s, g, i: (s * nt + i, g)),
                  pl.BlockSpec((1, 1, ncp, SLOT), lambda s, g, i: (s, g, 0, 0)),
                  pl.BlockSpec((1, 1, ncp, SLOT), lambda s, g, i: (s, g, 0, 0)),
                  pl.BlockSpec((nsr, ncp), lambda s, g, i: (0, 0))],
        out_specs=[pl.BlockSpec((tq, gw), lambda s, g, i: (s * nt + i, g)),
                   pl.BlockSpec((1, 1, tq, SLOT), lambda s, g, i: (s, g, i, 0))],
        out_shape=[jax.ShapeDtypeStruct((tokens, NSA_HEADS * SLOT), F32),
                   jax.ShapeDtypeStruct((n_seq, NSA_KV, t, SLOT), F32)],
        compiler_params=_cparams(3),
        name="cmp_attention_select_prompt",
    )(q_slots, kcc, vcc, ovt)


def _prompt_slc_win_kernel(q_ref, sel_ref, ks_ref, vs_ref, kw_ref, vw_ref, oh_ref, os_ref, ow_ref, *, tq, seq):
    acc_rows = NSA_DH + 16
    qi = pl.program_id(2)
    t0 = qi * tq
    rows = NSA_GROUP * tq
    sel = sel_ref[0, 0]
    q_plain = jnp.concatenate([q_ref[:, j * SLOT:(j + 1) * SLOT] for j in range(NSA_GROUP)], axis=0)
    q_aug = jnp.concatenate([(q_ref[:, j * SLOT:(j + 1) * SLOT].astype(F32) + sel).astype(BF16)
                             for j in range(NSA_GROUP)], axis=0)
    zeros_k = jnp.zeros((SLOT - NSA_DH, tq), F32)
    ones_row = (lax.broadcasted_iota(jnp.int32, (acc_rows - NSA_DH, tq), 0) == 0).astype(BF16)
    k_off = lax.broadcasted_iota(jnp.int32, (tq, LANES), 0)
    q_off = lax.broadcasted_iota(jnp.int32, (tq, LANES), 1)

    def scores(q_rows, k_top, k_bottom):
        k_rows = jnp.concatenate([k_top, k_bottom], axis=0).T.astype(BF16)
        return jnp.concatenate([_nt_dot(k_rows, q_rows[c:c + 2 * LANES]) for c in range(0, rows, 2 * LANES)],
                               axis=1)

    def update(state, s_t, v_top, start, mask):
        m, acc = state
        v_t = jnp.concatenate([v_top.astype(BF16), ones_row], axis=0)
        m_parts, acc_parts = [], []
        for cg in range(rows // (2 * LANES)):
            p_parts, a_parts = [], []
            for h in range(2):
                c0 = (2 * cg + h) * LANES
                x = s_t[:, c0:c0 + LANES]
                if mask is not None:
                    dist = (t0 + ((c0 + q_off) & (tq - 1))) - (start + k_off)
                    x = jnp.where(mask(dist), x, NEG_INF)
                m_old = m[:, c0:c0 + LANES]
                m_new = jnp.maximum(m_old, jnp.max(x, axis=0, keepdims=True))
                m_parts.append(m_new)
                a_parts.append(jnp.exp(m_old - m_new))
                p_parts.append(jnp.exp(x - m_new).astype(BF16))
            c0 = 2 * cg * LANES
            pv = _dot(v_t, jnp.concatenate(p_parts, axis=1))
            acc_parts.append(jnp.concatenate(a_parts, axis=1) * acc[:, c0:c0 + 2 * LANES] + pv)
        return jnp.concatenate(m_parts, axis=1), jnp.concatenate(acc_parts, axis=1)

    init = (jnp.full((1, rows), NEG_INF, F32), jnp.zeros((acc_rows, rows), F32))
    slot_pad = jnp.zeros((SLOT - NSA_DH, tq), F32)

    def finish(state, o_ref):
        acc = state[1]
        for j in range(NSA_GROUP):
            a = acc[:, j * tq:(j + 1) * tq]
            o_t = jnp.concatenate([a[0:NSA_DH] / a[NSA_DH:NSA_DH + 1, :], slot_pad], axis=0)
            o_ref[:, j * SLOT:(j + 1) * SLOT] = o_t.T

    def slc_scores(start):
        return scores(q_aug, ks_ref[0, :, pl.ds(start, tq)], oh_ref[:, pl.ds(start, tq)])

    k0 = jnp.clip(t0 - WINDOW, 0, seq - WINDOW - tq)
    n_win = WINDOW // tq + 1
    win_start = [pl.multiple_of(k0 + i * tq, tq) for i in range(n_win)]
    win_mask = lambda dist: (dist >= 0) & (dist < WINDOW)

    def win_scores(i):
        return scores(q_plain, kw_ref[0, :, pl.ds(win_start[i], tq)], zeros_k)

    def body(kt, carry):
        s_t, state = carry
        start = pl.multiple_of(kt * tq, tq)
        s_next = slc_scores(pl.multiple_of(start + tq, tq))
        return s_next, update(state, s_t, vs_ref[0, :, pl.ds(start, tq)], start, None)

    s_t, state = lax.fori_loop(0, qi, body, (slc_scores(0), init))
    s_next = win_scores(0)
    diag = pl.multiple_of(t0, tq)
    finish(update(state, s_t, vs_ref[0, :, pl.ds(diag, tq)], diag, lambda dist: dist >= 0), os_ref)
    state = init
    for i in range(n_win):
        s_t = s_next
        if i + 1 < n_win:
            s_next = win_scores(i + 1)
        state = update(state, s_t, vw_ref[0, :, pl.ds(win_start[i], tq)], win_start[i], win_mask)
    finish(state, ow_ref)


def prompt_slc_win_attention(q_slots, sel, ks_t, vs_t, kw_t, vw_t, n_seq, tq):
    tokens = q_slots.shape[0]
    seq = tokens // n_seq
    tq = _row_tile(seq, tq)
    assert tq & (tq - 1) == 0 and WINDOW % tq == 0 and seq >= WINDOW + tq
    nt = seq // tq
    gw = NSA_GROUP * SLOT
    assert tq % (2 * LANES) == 0
    onehot_t = jax.nn.one_hot(jnp.arange(seq) // SLC_BLOCK, SLOT - NSA_DH, dtype=F32).T
    kv_spec = pl.BlockSpec((1, NSA_DH, seq), lambda s, g, i: (s, g, 0))
    return pl.pallas_call(
        functools.partial(_prompt_slc_win_kernel, tq=tq, seq=seq),
        grid=(n_seq, NSA_KV, nt),
        in_specs=[pl.BlockSpec((tq, gw), lambda s, g, i: (s * nt + i, g)),
                  pl.BlockSpec((1, 1, tq, SLOT), lambda s, g, i: (s, g, i, 0)),
                  kv_spec, kv_spec, kv_spec, kv_spec,
                  pl.BlockSpec((SLOT - NSA_DH, seq), lambda s, g, i: (0, 0))],
        out_specs=[pl.BlockSpec((tq, gw), lambda s, g, i: (s * nt + i, g)),
                   pl.BlockSpec((tq, gw), lambda s, g, i: (s * nt + i, g))],
        out_shape=[jax.ShapeDtypeStruct((tokens, NSA_HEADS * SLOT), F32),
                   jax.ShapeDtypeStruct((tokens, NSA_HEADS * SLOT), F32)],
        compiler_params=_cparams(3),
        name="prompt_slc_win_attention",
    )(q_slots, sel, ks_t, vs_t, kw_t, vw_t, onehot_t)


def _sample_slc_kernel(idx_ref, pt_ref, q_ref, kn_ref, vn_ref, *refs, topn, past, t_pad):
    k_refs = refs[:topn]
    v_refs = refs[topn:2 * topn]
    o_ref = refs[2 * topn]
    s_id, g_id, t_id = pl.program_id(0), pl.program_id(1), pl.program_id(2)
    base = ((s_id * NSA_KV + g_id) * t_pad + t_id) * topn
    q_pos = past + t_id
    cur = q_pos // SLC_BLOCK
    first_new = past // SLC_BLOCK
    q = q_ref[0, 0, 0].astype(BF16)
    kb = jnp.concatenate([r[0, 0] for r in k_refs], axis=1).astype(BF16)
    vb = jnp.concatenate([r[0, 0] for r in v_refs], axis=1).astype(BF16)
    n_keys = topn * PAGE_SIZE
    lane = lax.broadcasted_iota(jnp.int32, (1, n_keys), 1)
    slot = lane // PAGE_SIZE
    in_page = lane % PAGE_SIZE
    k_pos = in_page
    limit = jnp.zeros((1, n_keys), jnp.int32)
    n_new = jnp.int32(0)
    for kk in range(topn):
        b = idx_ref[base + kk]
        here = slot == kk
        k_pos = jnp.where(here, (b // 2) * PAGE_SIZE + in_page, k_pos)
        last = jnp.where(b <= cur, jnp.minimum(q_pos, past - 1), -1)
        limit = jnp.where(here, jnp.where(in_page // SLC_BLOCK == b % 2, last, -1), limit)
        n_new = n_new + jnp.where(b == first_new, 1, 0)
    valid = k_pos <= limit
    s_old = jnp.where(valid, _dot(q, kb), NEG_INF)
    new_lane = lax.broadcasted_iota(jnp.int32, (1, SUBLANES), 1)
    valid_new = past + new_lane <= jnp.where(n_new > 0, q_pos, past - 1)
    s_new = jnp.where(valid_new, _dot(q, kn_ref[0, 0].astype(BF16)), NEG_INF)
    m = jnp.maximum(jnp.max(s_old, axis=-1, keepdims=True), jnp.max(s_new, axis=-1, keepdims=True))
    p_old = jnp.exp(s_old - m)
    p_new = jnp.exp(s_new - m)
    l = jnp.sum(p_old, axis=-1, keepdims=True) + jnp.sum(p_new, axis=-1, keepdims=True)
    o = _nt_dot(p_old.astype(BF16), vb) + _nt_dot(p_new.astype(BF16), vn_ref[0, 0].astype(BF16))
    o_ref[0, 0, 0] = o / l


def sample_slc_attention(q_rows, idx, page_table, pool_k, pool_v, k_new_t, v_new_t, t_real, past):
    n_seq, _, t_pad, _, dh = q_rows.shape
    topn = idx.shape[-1]
    assert past % SLC_BLOCK == 0 and t_real <= SUBLANES and PAGE_SIZE == 2 * SLC_BLOCK
    last_old = past // SLC_BLOCK - 1
    logical = jnp.clip(idx, 0, last_old) // 2
    phys = jnp.take_along_axis(page_table[:, None, None, :], logical, axis=-1)

    def blk_map(kk):
        def f(s, g, t, idx_ref, pg_ref):
            return (pg_ref[((s * NSA_KV + g) * t_pad + t) * topn + kk], g, 0, 0)
        return f

    blk_specs = [pl.BlockSpec((1, 1, dh, PAGE_SIZE), blk_map(kk)) for kk in range(topn)]
    new_spec = pl.BlockSpec((1, 1, dh, SUBLANES), lambda s, g, t, i_r, p_r: (s, g, 0, 0))
    q_spec = pl.BlockSpec((1, 1, 1, SUBLANES, dh), lambda s, g, t, i_r, p_r: (s, g, t, 0, 0))
    grid_spec = pltpu.PrefetchScalarGridSpec(
        num_scalar_prefetch=2,
        grid=(n_seq, NSA_KV, t_real),
        in_specs=[q_spec, new_spec, new_spec] + blk_specs + blk_specs,
        out_specs=q_spec,
    )
    return pl.pallas_call(
        functools.partial(_sample_slc_kernel, topn=topn, past=past, t_pad=t_pad),
        grid_spec=grid_spec,
        out_shape=jax.ShapeDtypeStruct((n_seq, NSA_KV, t_real, SUBLANES, dh), F32),
        compiler_params=_cparams(3),
        name="sample_slc_attention",
    )(idx.reshape(-1), phys.reshape(-1), q_rows, k_new_t, v_new_t, *([pool_k] * topn), *([pool_v] * topn))


def _sample_win_kernel(q_ref, wk_ref, wv_ref, kn_ref, vn_ref, o_ref, *, past, t_pad):
    rows = t_pad * SUBLANES
    wb = wk_ref.shape[-1]
    q = q_ref[0, 0].reshape(rows, q_ref.shape[-1]).astype(BF16)
    q_pos = past + lax.broadcasted_iota(jnp.int32, (rows, 1), 0) // SUBLANES
    k_pos = past - wb + lax.broadcasted_iota(jnp.int32, (1, wb), 1)
    dist = q_pos - k_pos
    valid = (dist >= 0) & (dist < WINDOW) & (k_pos >= 0)
    s_old = jnp.where(valid, _dot(q, wk_ref[0, 0].astype(BF16)), NEG_INF)
    n_pos = past + lax.broadcasted_iota(jnp.int32, (1, SUBLANES), 1)
    dist_n = q_pos - n_pos
    valid_n = (dist_n >= 0) & (dist_n < WINDOW)
    s_new = jnp.where(valid_n, _dot(q, kn_ref[0, 0].astype(BF16)), NEG_INF)
    m = jnp.maximum(jnp.max(s_old, axis=-1, keepdims=True), jnp.max(s_new, axis=-1, keepdims=True))
    p_old = jnp.exp(s_old - m)
    p_new = jnp.exp(s_new - m)
    l = jnp.sum(p_old, axis=-1, keepdims=True) + jnp.sum(p_new, axis=-1, keepdims=True)
    o = _nt_dot(p_old.astype(BF16), wv_ref[0, 0].astype(BF16)) + _nt_dot(p_new.astype(BF16),
                                                                      vn_ref[0, 0].astype(BF16))
    o_ref[0, 0] = (o / l).reshape(t_pad, SUBLANES, o.shape[-1])


def sample_win_attention(q_rows, win_k_t, win_v_t, k_new_t, v_new_t, past):
    n_seq, _, t_pad, _, dh = q_rows.shape
    wb = win_k_t.shape[-1]
    q_spec = pl.BlockSpec((1, 1, t_pad, SUBLANES, dh), lambda s, g: (s, g, 0, 0, 0))
    win_spec = pl.BlockSpec((1, 1, dh, wb), lambda s, g: (s, g, 0, 0))
    new_spec = pl.BlockSpec((1, 1, dh, SUBLANES), lambda s, g: (s, g, 0, 0))
    return pl.pallas_call(
        functools.partial(_sample_win_kernel, past=past, t_pad=t_pad),
        grid=(n_seq, NSA_KV),
        in_specs=[q_spec, win_spec, win_spec, new_spec, new_spec],
        out_specs=q_spec,
        out_shape=jax.ShapeDtypeStruct(q_rows.shape, F32),
        compiler_params=_cparams(2),
        name="sample_win_attention",
    )(q_rows, win_k_t, win_v_t, k_new_t, v_new_t)


def _to_slots(a):
    lead = a.shape[:-1]
    n = a.shape[-1] // NSA_DH
    a = a.reshape(*lead, n, NSA_DH)
    a = jnp.pad(a, [(0, 0)] * (a.ndim - 1) + [(0, SLOT - NSA_DH)])
    return a.reshape(*lead, n * SLOT)


def _odd_weights(w_in, w_out):
    d = w_in.shape[0]
    hq = NSA_HEADS * NSA_DH
    kvw = NSA_KV * NSA_DH
    wq = _to_slots(w_in[:, :hq] * np.float32(NSA_DH ** -0.5))
    wg = jnp.pad(w_in[:, hq + 6 * kvw:], ((0, 0), (0, LANES - 3 * NSA_HEADS)))
    w_q = jnp.concatenate([wq, wg], axis=1).astype(BF16)
    w_kvt = w_in[:, hq:hq + 6 * kvw].T.astype(BF16)
    wo = jnp.pad(w_out.reshape(NSA_HEADS, NSA_DH, d), ((0, 0), (0, SLOT - NSA_DH), (0, 0)))
    wo = wo.reshape(NSA_HEADS * SLOT, d).astype(BF16)
    k = NSA_HEADS * SLOT
    e = np.zeros((LANES, 3 * k), np.float32)
    for c in range(3):
        for h in range(NSA_HEADS):
            e[c * NSA_HEADS + h, c * k + h * SLOT:c * k + (h + 1) * SLOT] = 1.0
    return w_q, w_kvt, wo, jnp.asarray(e, dtype=BF16)


def _group_rows(q_slots, n_seq, t_pad):
    q = q_slots.reshape(n_seq, t_pad, NSA_KV, NSA_GROUP, SLOT)[..., :NSA_DH]
    q = q.transpose(0, 2, 1, 3, 4)
    return jnp.pad(q, ((0, 0), (0, 0), (0, 0), (0, SUBLANES - NSA_GROUP), (0, 0)))


def _ungroup_rows(o, n_seq, t_pad):
    t = o.shape[2]
    o = o[:, :, :, :NSA_GROUP].transpose(0, 2, 1, 3, 4)
    o = jnp.pad(o, ((0, 0), (0, t_pad - t), (0, 0), (0, 0), (0, SLOT - NSA_DH)))
    return o.reshape(n_seq * t_pad, NSA_HEADS * SLOT)


def _feature_major(cache):
    return cache.transpose(0, 2, 3, 1)


def _token_major(a_t):
    return a_t.transpose(0, 3, 1, 2)


def _pad_rows(a, t_pad):
    return jnp.pad(a, ((0, 0), (0, t_pad - a.shape[1])) + ((0, 0),) * (a.ndim - 2))


def kernel(x_prompt, x_sample, state_sconv, state_ret, cache_cmp_k, cache_cmp_v, cache_slc_k, cache_slc_v,
           cache_win_k, cache_win_v, state_ffn_conv, page_table,
           w_in_even, sconv_w, sconv_b, ret_gn_g, w_out_even,
           w_in_odd, cmp_pe, cmp_w1, cmp_w2, w_out_odd,
           ln_mix_g, ln_mix_b, ffn_w_up, ffn_conv_w, ffn_conv_b, ffn_w_down, ln_ffn_g, ln_ffn_b):
    b_p, s_p, d_model = x_prompt.shape
    b_s, t_s, _ = x_sample.shape
    n_pages = page_table.shape[1]
    past = n_pages * PAGE_SIZE
    t_pad = SUBLANES
    assert t_s <= t_pad and t_s >= SCONV_W - 1 and t_s < CMP_STRIDE and past % PAGE_SIZE == 0
    assert s_p % RET_CHUNK == 0 and s_p % PAGE_SIZE == 0
    d_sconv = sconv_w.shape[-1]
    d_ff = ffn_conv_w.shape[-1]
    gd = NSA_KV * NSA_DH
    depth = ln_mix_g.shape[0]

    xp = x_prompt.reshape(b_p * s_p, d_model)
    xs = _pad_rows(x_sample, t_pad).reshape(b_s * t_pad, d_model)
    outs = {k: [] for k in ("sconv_p", "sconv_s", "ret_p", "ret_s", "cmp_k_p", "cmp_v_p", "slc_k_p", "slc_v_p",
                            "cmp_k_s", "cmp_v_s", "slc_k_s", "slc_v_s", "win_k_p", "win_v_p", "win_k_s",
                            "win_v_s", "ffn_p", "ffn_s")}

    for layer in range(depth):
        if layer % 2 == 0:
            e = layer // 2
            w_in = w_in_even[e].astype(BF16)
            w_out = w_out_even[e].astype(BF16)
            n_in = w_in.shape[1]
            (zp,) = matmul_split(xp, w_in, [n_in], [F32])
            yp, hc, st = even_mixer(zp, jnp.zeros((b_p, SCONV_W - 1, d_sconv), F32),
                                    jnp.zeros((b_p,) + state_ret.shape[2:], F32), jnp.arange(s_p),
                                    RET_CHUNK, RET_CHUNK, sconv_w[e], sconv_b[e], ret_gn_g[e])
            outs["sconv_p"].append(hc)
            outs["ret_p"].append(st)
            xp = matmul_residual_ln(yp, w_out, xp, ln_mix_g[layer], ln_mix_b[layer])
            (zs,) = matmul_split(xs, w_in, [n_in], [F32])
            ys, hc, st = even_mixer(zs, state_sconv[e], state_ret[e], past + jnp.arange(t_pad),
                                    t_pad, t_s, sconv_w[e], sconv_b[e], ret_gn_g[e])
            outs["sconv_s"].append(hc)
            outs["ret_s"].append(st)
            xs = matmul_residual_ln(ys, w_out, xs, ln_mix_g[layer], ln_mix_b[layer])
        else:
            o = layer // 2
            w_q, w_kvt, w_out, e_gate = _odd_weights(w_in_odd[o], w_out_odd[o])
            pe, w1, w2 = cmp_pe[o], cmp_w1[o], cmp_w2[o]
            qp, gp, kc, vc, ks, vs, kw, vw = nsa_projection(xp, b_p, w_q, w_kvt, BF16)
            as_cache = lambda a_t: _token_major(a_t.reshape(b_p, NSA_KV, NSA_DH, -1))
            keep = min(WINDOW, s_p)
            outs["cmp_k_p"].append(as_cache(kc))
            outs["cmp_v_p"].append(as_cache(vc))
            outs["slc_k_p"].append(as_cache(ks))
            outs["slc_v_p"].append(as_cache(vs))
            outs["win_k_p"].append(as_cache(kw[:, :, s_p - keep:]))
            outs["win_v_p"].append(as_cache(vw[:, :, s_p - keep:]))
            kcc = compress(kc.reshape(b_p, NSA_KV, NSA_DH, s_p), None, pe[0], w1[0], w2[0])
            vcc = compress(vc.reshape(b_p, NSA_KV, NSA_DH, s_p), None, pe[1], w1[1], w2[1])
            n_cmp = s_p // CMP_STRIDE - CMP_LEN // CMP_STRIDE + 1
            n_slc = s_p // SLC_BLOCK
            oc, sel = cmp_attention_select_prompt(qp, kcc, vcc, b_p, n_cmp, n_slc, 256)
            osl, ow = prompt_slc_win_attention(qp, sel, ks, vs, kw, vw, b_p, 256)
            xp = nsa_merge_residual_ln(oc, osl, ow, gp, e_gate, w_out, xp, ln_mix_g[layer], ln_mix_b[layer])
            qs, gs, *kv_s = nsa_projection(xs, 1, w_q, w_kvt, F32)
            kc, vc, ks, vs, kw, vw = [a.reshape(NSA_KV, NSA_DH, b_s, t_pad).transpose(2, 0, 1, 3) for a in kv_s]
            new_rows = lambda a_t: _token_major(a_t[..., :t_s])
            outs["cmp_k_s"].append(new_rows(kc))
            outs["cmp_v_s"].append(new_rows(vc))
            outs["slc_k_s"].append(new_rows(ks))
            outs["slc_v_s"].append(new_rows(vs))
            win_k = _feature_major(cache_win_k[o])
            win_v = _feature_major(cache_win_v[o])
            wb = win_k.shape[-1]
            keep = min(WINDOW, wb + t_s)
            outs["win_k_s"].append(_token_major(jnp.concatenate([win_k, kw[..., :t_s]], axis=-1)[..., -keep:]))
            outs["win_v_s"].append(_token_major(jnp.concatenate([win_v, vw[..., :t_s]], axis=-1)[..., -keep:]))
            kcc = compress(_feature_major(cache_cmp_k[o]), page_table, pe[0], w1[0], w2[0])
            vcc = compress(_feature_major(cache_cmp_v[o]), page_table, pe[1], w1[1], w2[1])
            n_cmp = (past + t_s) // CMP_STRIDE - CMP_LEN // CMP_STRIDE + 1
            n_slc = -(-(past + t_s) // SLC_BLOCK)
            oc, idx = cmp_attention_select(qs, kcc, vcc, b_s, n_cmp, n_slc, past)
            topn = min(SLC_TOPN, n_slc)
            q_rows = _group_rows(qs, b_s, t_pad)
            osl = sample_slc_attention(q_rows, idx[..., :topn], page_table, _feature_major(cache_slc_k[o]),
                                       _feature_major(cache_slc_v[o]), ks, vs, t_s, past)
            ow = sample_win_attention(q_rows, win_k, win_v, kw, vw, past)
            xs = nsa_merge_residual_ln(oc, _ungroup_rows(osl, b_s, t_pad), _ungroup_rows(ow, b_s, t_pad), gs,
                                       e_gate, w_out, xs, ln_mix_g[layer], ln_mix_b[layer])
        w_up = ffn_w_up[layer].astype(BF16)
        w_down = ffn_w_down[layer].astype(BF16)
        hp, hist_p = ffn_up_sequences(xp, b_p, jnp.zeros((b_p, FFN_W - 1, d_ff), F32), w_up,
                                      ffn_conv_w[layer], ffn_conv_b[layer])
        outs["ffn_p"].append(hist_p)
        xp = matmul_residual_ln(hp, w_down, xp, ln_ffn_g[layer], ln_ffn_b[layer])
        hs, a_s = ffn_up_short(xs, state_ffn_conv[layer], w_up, ffn_conv_w[layer], ffn_conv_b[layer])
        outs["ffn_s"].append(a_s.reshape(b_s, t_pad, d_ff)[:, t_s - (FFN_W - 1):t_s])
        xs = matmul_residual_ln(hs, w_down, xs, ln_ffn_g[layer], ln_ffn_b[layer])

    st = jnp.stack
    y_p = xp.reshape(b_p, s_p, d_model)
    y_s = xs.reshape(b_s, t_pad, d_model)[:, :t_s]
    order = ("sconv_p", "sconv_s", "ret_p", "ret_s", "cmp_k_p", "cmp_v_p", "slc_k_p", "slc_v_p",
             "cmp_k_s", "cmp_v_s", "slc_k_s", "slc_v_s", "win_k_p", "win_v_p", "win_k_s", "win_v_s",
             "ffn_p", "ffn_s")
    return (y_p, y_s) + tuple(st(outs[k]) for k in order)
```

```python
import functools

import numpy as np
import jax
import jax.numpy as jnp
from jax import lax
from jax.experimental import pallas as pl
from jax.experimental.pallas import tpu as pltpu

F32 = jnp.float32
BF16 = jnp.bfloat16

SUBLANES = 8
LANES = 128
VMEM_LIMIT_BYTES = 56 * 1024 * 1024

DEPTH = 2
SCONV_W = 3
RET_HEADS = 4
RET_CHUNK = 128
ROPE_BASE = 10000.0
NSA_HEADS = 16
NSA_KV = 4
NSA_GROUP = NSA_HEADS // NSA_KV
NSA_DH = 64
CMP_LEN = 32
CMP_STRIDE = 16
SLC_BLOCK = 64
SLC_TOPN = 16
WINDOW = 512
PAGE_SIZE = 128
FFN_W = 3
ALPHA = (2.0 * DEPTH) ** 0.25
LN_EPS = 1e-5
NEG_INF = -1e30
REMOVED = -3e38
FORCE_BONUS = 1e4
SLOT = 2 * NSA_DH


def _cparams(n_grid):
    return pltpu.CompilerParams(dimension_semantics=("arbitrary",) * n_grid,
                                vmem_limit_bytes=VMEM_LIMIT_BYTES)


def _row_tile(m, want):
    t = min(m, want)
    assert m % t == 0, (m, t)
    return t


def _nt_dot(a, b):
    return lax.dot_general(a, b, (((1,), (1,)), ((), ())), preferred_element_type=F32)


def _tn_dot(a, b):
    return lax.dot_general(a, b, (((0,), (0,)), ((), ())), preferred_element_type=F32)


def _dot(a, b):
    return jnp.dot(a, b, preferred_element_type=F32)


def _gelu(x):
    return 0.5 * x * (1.0 + jnp.tanh(np.float32(np.sqrt(2.0 / np.pi)) * (x + 0.044715 * (x * x * x))))


def _layer_norm_rows(r, g, b):
    mu = jnp.mean(r, axis=-1, keepdims=True)
    d = r - mu
    var = jnp.mean(d * d, axis=-1, keepdims=True)
    return d * lax.rsqrt(var + LN_EPS) * g + b


def _mm_split_kernel(x_ref, w_ref, *o_refs, cuts):
    acc = _dot(x_ref[...].astype(BF16), w_ref[...])
    for o_ref, (lo, hi) in zip(o_refs, cuts):
        o_ref[...] = acc[:, lo:hi].astype(o_ref.dtype)


def matmul_split(x, w_bf16, widths, dtypes, tm=256):
    m, k = x.shape
    n = w_bf16.shape[1]
    assert sum(widths) == n and all(wd % LANES == 0 for wd in widths)
    tm = _row_tile(m, tm)
    cuts, lo = [], 0
    for wd in widths:
        cuts.append((lo, lo + wd))
        lo += wd
    return pl.pallas_call(
        functools.partial(_mm_split_kernel, cuts=tuple(cuts)),
        grid=(m // tm,),
        in_specs=[pl.BlockSpec((tm, k), lambda i: (i, 0)),
                  pl.BlockSpec((k, n), lambda i: (0, 0))],
        out_specs=[pl.BlockSpec((tm, wd), lambda i: (i, 0)) for wd in widths],
        out_shape=[jax.ShapeDtypeStruct((m, wd), dt) for wd, dt in zip(widths, dtypes)],
        compiler_params=_cparams(1),
        name="matmul_split",
    )(x, w_bf16)


def _nsa_proj_kernel(x_ref, wq_ref, wkvt_ref, q_ref, g_ref, *kv_refs, nq):
    xb = x_ref[...].astype(BF16)
    acc = _dot(xb, wq_ref[...])
    q_ref[...] = acc[:, :nq].astype(q_ref.dtype)
    g_ref[...] = acc[:, nq:]
    acc_t = _nt_dot(wkvt_ref[...], xb)
    gd = acc_t.shape[0] // len(kv_refs)
    for i, r in enumerate(kv_refs):
        r[0] = acc_t[i * gd:(i + 1) * gd, :]


def nsa_projection(x, n_seq, wq_bf16, wkvt_bf16, q_dtype, n_kv=6, tm=256):
    m, d = x.shape
    seq = m // n_seq
    tm = _row_tile(seq, tm)
    nt = seq // tm
    nq = wq_bf16.shape[1] - LANES
    gd = wkvt_bf16.shape[0] // n_kv
    return pl.pallas_call(
        functools.partial(_nsa_proj_kernel, nq=nq),
        grid=(n_seq, nt),
        in_specs=[pl.BlockSpec((tm, d), lambda s, i: (s * nt + i, 0)),
                  pl.BlockSpec((d, nq + LANES), lambda s, i: (0, 0)),
                  pl.BlockSpec((n_kv * gd, d), lambda s, i: (0, 0))],
        out_specs=[pl.BlockSpec((tm, nq), lambda s, i: (s * nt + i, 0)),
                   pl.BlockSpec((tm, LANES), lambda s, i: (s * nt + i, 0))] + [
                      pl.BlockSpec((1, gd, tm), lambda s, i: (s, 0, i)) for _ in range(n_kv)],
        out_shape=[jax.ShapeDtypeStruct((m, nq), q_dtype), jax.ShapeDtypeStruct((m, LANES), F32)] + [
            jax.ShapeDtypeStruct((n_seq, gd, seq), F32) for _ in range(n_kv)],
        compiler_params=_cparams(2),
        name="nsa_projection",
    )(x, wq_bf16, wkvt_bf16)


def _mm_res_ln_kernel(a_ref, w_ref, x_ref, g_ref, b_ref, o_ref):
    y = _dot(a_ref[...].astype(BF16), w_ref[...])
    o_ref[...] = _layer_norm_rows(ALPHA * x_ref[...] + y, g_ref[...], b_ref[...])


def matmul_residual_ln(a, w_bf16, x, g, b, tm=256):
    m, k = a.shape
    d = w_bf16.shape[1]
    tm = _row_tile(m, tm)
    return pl.pallas_call(
        _mm_res_ln_kernel,
        grid=(m // tm,),
        in_specs=[pl.BlockSpec((tm, k), lambda i: (i, 0)),
                  pl.BlockSpec((k, d), lambda i: (0, 0)),
                  pl.BlockSpec((tm, d), lambda i: (i, 0)),
                  pl.BlockSpec((1, d), lambda i: (0, 0)),
                  pl.BlockSpec((1, d), lambda i: (0, 0))],
        out_specs=pl.BlockSpec((tm, d), lambda i: (i, 0)),
        out_shape=jax.ShapeDtypeStruct((m, d), F32),
        compiler_params=_cparams(1),
        name="matmul_residual_ln",
    )(a, w_bf16, x, g.reshape(1, d), b.reshape(1, d))


def _expand_gates(gates_raw, e_ref):
    sig = jax.nn.sigmoid(gates_raw)
    hi = sig.astype(BF16)
    lo = (sig - hi.astype(F32)).astype(BF16)
    e = e_ref[...]
    return _dot(hi, e) + _dot(lo, e)


def _nsa_merge_ln_kernel(oc_ref, os_ref, ow_ref, gt_ref, e_ref, w_ref, x_ref, g_ref, b_ref, o_ref, *, k):
    gx = _expand_gates(gt_ref[...], e_ref)
    o = gx[:, 0:k] * oc_ref[...] + gx[:, k:2 * k] * os_ref[...] + gx[:, 2 * k:3 * k] * ow_ref[...]
    y = _dot(o.astype(BF16), w_ref[...])
    o_ref[...] = _layer_norm_rows(ALPHA * x_ref[...] + y, g_ref[...], b_ref[...])


def nsa_merge_residual_ln(oc, osl, ow, gates, e_bf16, w_bf16, x, g, b, tm=256):
    m, k = oc.shape
    d = w_bf16.shape[1]
    tm = _row_tile(m, tm)
    row = lambda i: (i, 0)
    fixed = lambda i: (0, 0)
    return pl.pallas_call(
        functools.partial(_nsa_merge_ln_kernel, k=k),
        grid=(m // tm,),
        in_specs=[pl.BlockSpec((tm, k), row), pl.BlockSpec((tm, k), row), pl.BlockSpec((tm, k), row),
                  pl.BlockSpec((tm, LANES), row),
                  pl.BlockSpec((LANES, 3 * k), fixed),
                  pl.BlockSpec((k, d), fixed),
                  pl.BlockSpec((tm, d), row),
                  pl.BlockSpec((1, d), fixed), pl.BlockSpec((1, d), fixed)],
        out_specs=pl.BlockSpec((tm, d), row),
        out_shape=jax.ShapeDtypeStruct((m, d), F32),
        compiler_params=_cparams(1),
        name="nsa_merge_residual_ln",
    )(oc, osl, ow, gates, e_bf16, w_bf16, x, g.reshape(1, d), b.reshape(1, d))


def _even_mixer_kernel(z_ref, hist_ref, st_ref, cos_ref, sin_ref, decay_ref, qdec_ref, kdec_ref, sdec_ref,
                       cw_ref, cb_ref, gn_ref, y_ref, hist_out_ref, st_out_ref, carry, state,
                       *, rows, valid, dconv, dk):
    c = pl.program_id(1)
    r0 = valid - 2 - (rows - SUBLANES)

    @pl.when(c == 0)
    def _():
        carry[r0:r0 + 2, :] = hist_ref[0]
        state[...] = st_ref[0]

    d = dconv
    h = z_ref[:, 0:d]
    gate_b = z_ref[:, d:2 * d]
    gate_c = z_ref[:, 2 * d:3 * d]
    ch = gate_c * h
    row = lax.broadcasted_iota(jnp.int32, (rows, d), 0)
    h0 = carry[r0:r0 + 1, :]
    h1 = carry[r0 + 1:r0 + 2, :]
    m1 = jnp.where(row == 0, h1, pltpu.roll(ch, 1, 0))
    m2 = jnp.where(row == 0, h0, jnp.where(row == 1, h1, pltpu.roll(ch, 2, 0)))
    u = ((cb_ref[...] + m2 * cw_ref[0:1, :]) + m1 * cw_ref[1:2, :]) + ch * cw_ref[2:3, :]
    y_ref[:, 0:d] = gate_b * u
    carry[...] = ch[rows - SUBLANES:rows, :]
    hist_out_ref[0] = carry[r0:r0 + 2, :]

    cosf = cos_ref[...]
    sinf = sin_ref[...]
    scale = np.float32(dk ** -0.5)
    for hh in range(RET_HEADS):
        q = z_ref[:, 3 * d + hh * dk:3 * d + (hh + 1) * dk]
        k = z_ref[:, 4 * d + hh * dk:4 * d + (hh + 1) * dk]
        v = z_ref[:, 5 * d + hh * dk:5 * d + (hh + 1) * dk]
        gsw = z_ref[:, 6 * d + hh * dk:6 * d + (hh + 1) * dk]
        q = (q * cosf + pltpu.roll(q, dk // 2, 1) * sinf) * scale
        k = k * cosf + pltpu.roll(k, dk // 2, 1) * sinf
        qb = q.astype(BF16)
        vb = v.astype(BF16)
        s_old = state[hh]
        scores = _nt_dot(qb, k.astype(BF16)) * decay_ref[hh]
        intra = _dot(scores.astype(BF16), vb)
        cross = _dot(qb, s_old.astype(BF16)) * qdec_ref[hh]
        kd = (k * kdec_ref[hh]).astype(BF16)
        state[hh] = s_old * sdec_ref[hh] + _tn_dot(kd, vb)
        o = intra + cross
        mu = jnp.mean(o, axis=-1, keepdims=True)
        dv = o - mu
        var = jnp.mean(dv * dv, axis=-1, keepdims=True)
        on = dv * lax.rsqrt(var + LN_EPS) * gn_ref[:, hh * dk:(hh + 1) * dk]
        y_ref[:, d + hh * dk:d + (hh + 1) * dk] = (gsw * jax.nn.sigmoid(gsw)) * on
    st_out_ref[0] = state[...]


def _retention_tables(rows, valid, dk):
    log_gamma = jnp.log1p(-jnp.exp2(-5.0 - jnp.arange(RET_HEADS, dtype=F32)))
    n = jnp.arange(rows, dtype=F32)
    diff = n[:, None] - n[None, :]
    lg = log_gamma[:, None, None]
    decay = jnp.where(diff >= 0, jnp.exp(lg * jnp.maximum(diff, 0.0)), 0.0)
    q_dec = jnp.exp((n[None, :] + 1.0) * log_gamma[:, None])
    k_dec = jnp.where(n[None, :] < valid, jnp.exp((valid - 1.0 - n[None, :]) * log_gamma[:, None]), 0.0)
    s_dec = jnp.exp(valid * log_gamma)
    bc = lambda a: jnp.broadcast_to(a[:, :, None], (RET_HEADS, rows, dk))
    return decay, bc(q_dec), bc(k_dec), jnp.broadcast_to(s_dec[:, None, None], (RET_HEADS, 1, dk))


def _rope_tables(pos, dk):
    half = dk // 2
    inv = ROPE_BASE ** (-jnp.arange(half, dtype=F32) / half)
    ang = pos.astype(F32)[:, None] * inv
    cos, sin = jnp.cos(ang), jnp.sin(ang)
    return jnp.concatenate([cos, cos], axis=-1), jnp.concatenate([-sin, sin], axis=-1)


def even_mixer(z, hist, st, pos, rows, valid, conv_w, conv_b, gn_g):
    n_seq, _, dconv = hist.shape
    dk = st.shape[-1]
    n_chunks = z.shape[0] // (n_seq * rows)
    cosf, sinf = _rope_tables(pos, dk)
    decay, q_dec, k_dec, s_dec = _retention_tables(rows, valid, dk)
    fixed3 = lambda s, c: (0, 0, 0)
    fixed2 = lambda s, c: (0, 0)
    return pl.pallas_call(
        functools.partial(_even_mixer_kernel, rows=rows, valid=valid, dconv=dconv, dk=dk),
        grid=(n_seq, n_chunks),
        in_specs=[pl.BlockSpec((rows, 7 * dconv), lambda s, c: (s * n_chunks + c, 0)),
                  pl.BlockSpec((1, 2, dconv), lambda s, c: (s, 0, 0)),
                  pl.BlockSpec((1, RET_HEADS, dk, dk), lambda s, c: (s, 0, 0, 0)),
                  pl.BlockSpec((rows, dk), lambda s, c: (c, 0)),
                  pl.BlockSpec((rows, dk), lambda s, c: (c, 0)),
                  pl.BlockSpec((RET_HEADS, rows, rows), fixed3),
                  pl.BlockSpec((RET_HEADS, rows, dk), fixed3),
                  pl.BlockSpec((RET_HEADS, rows, dk), fixed3),
                  pl.BlockSpec((RET_HEADS, 1, dk), fixed3),
                  pl.BlockSpec((SCONV_W, dconv), fixed2),
                  pl.BlockSpec((1, dconv), fixed2),
                  pl.BlockSpec((1, RET_HEADS * dk), fixed2)],
        out_specs=[pl.BlockSpec((rows, 2 * dconv), lambda s, c: (s * n_chunks + c, 0)),
                   pl.BlockSpec((1, 2, dconv), lambda s, c: (s, 0, 0)),
                   pl.BlockSpec((1, RET_HEADS, dk, dk), lambda s, c: (s, 0, 0, 0))],
        out_shape=[jax.ShapeDtypeStruct((z.shape[0], 2 * dconv), F32),
                   jax.ShapeDtypeStruct((n_seq, 2, dconv), F32),
                   jax.ShapeDtypeStruct((n_seq, RET_HEADS, dk, dk), F32)],
        scratch_shapes=[pltpu.VMEM((SUBLANES, dconv), F32), pltpu.VMEM((RET_HEADS, dk, dk), F32)],
        compiler_params=_cparams(2),
        name="even_mixer",
    )(z, hist, st, cosf, sinf, decay, q_dec, k_dec, s_dec, conv_w, conv_b.reshape(1, dconv),
      gn_g.reshape(1, RET_HEADS * dk))


def _conv_gate(a, gate, m1, m2, cw_ref, cb_ref):
    conv = ((cb_ref[...] + m2 * cw_ref[0:1, :]) + m1 * cw_ref[1:2, :]) + a * cw_ref[2:3, :]
    return _gelu(conv) * gate


def _ffn_up_seq_kernel(x_ref, wa_ref, wg_ref, h_ref, cw_ref, cb_ref, o_ref, hist_out_ref, carry, *, tm):
    @pl.when(pl.program_id(2) == 0)
    def _():
        carry[SUBLANES - 2:SUBLANES, :] = h_ref[0]

    xb = x_ref[...].astype(BF16)
    a = _dot(xb, wa_ref[...])
    gate = _dot(xb, wg_ref[...])
    row = lax.broadcasted_iota(jnp.int32, a.shape, 0)
    h0 = carry[SUBLANES - 2:SUBLANES - 1, :]
    h1 = carry[SUBLANES - 1:SUBLANES, :]
    m1 = jnp.where(row == 0, h1, pltpu.roll(a, 1, 0))
    m2 = jnp.where(row == 0, h0, jnp.where(row == 1, h1, pltpu.roll(a, 2, 0)))
    o_ref[...] = _conv_gate(a, gate, m1, m2, cw_ref, cb_ref).astype(o_ref.dtype)
    carry[...] = a[tm - SUBLANES:tm, :]
    hist_out_ref[0] = carry[SUBLANES - 2:SUBLANES, :]


def ffn_up_sequences(x, n_seq, hist, w_up_bf16, conv_w, conv_b, tm=256, n_col=2):
    m, k = x.shape
    dff = conv_w.shape[1]
    seq = m // n_seq
    tm = _row_tile(seq, tm)
    tps = seq // tm
    tn = dff // n_col
    assert tn % LANES == 0
    return pl.pallas_call(
        functools.partial(_ffn_up_seq_kernel, tm=tm),
        grid=(n_col, n_seq, tps),
        in_specs=[pl.BlockSpec((tm, k), lambda j, s, i: (s * tps + i, 0)),
                  pl.BlockSpec((k, tn), lambda j, s, i: (0, j)),
                  pl.BlockSpec((k, tn), lambda j, s, i: (0, j + n_col)),
                  pl.BlockSpec((1, 2, tn), lambda j, s, i: (s, 0, j)),
                  pl.BlockSpec((FFN_W, tn), lambda j, s, i: (0, j)),
                  pl.BlockSpec((1, tn), lambda j, s, i: (0, j))],
        out_specs=[pl.BlockSpec((tm, tn), lambda j, s, i: (s * tps + i, j)),
                   pl.BlockSpec((1, 2, tn), lambda j, s, i: (s, 0, j))],
        out_shape=[jax.ShapeDtypeStruct((m, dff), BF16),
                   jax.ShapeDtypeStruct((n_seq, 2, dff), F32)],
        scratch_shapes=[pltpu.VMEM((SUBLANES, tn), F32)],
        compiler_params=_cparams(3),
        name="ffn_up_sequences",
    )(x, w_up_bf16, w_up_bf16, hist, conv_w, conv_b.reshape(1, dff))


def _ffn_up_short_kernel(x_ref, wa_ref, wg_ref, h1_ref, h2_ref, cw_ref, cb_ref, o_ref, a_ref):
    xb = x_ref[...].astype(BF16)
    a = _dot(xb, wa_ref[...])
    gate = _dot(xb, wg_ref[...])
    t = lax.broadcasted_iota(jnp.int32, a.shape, 0) % SUBLANES
    m1 = jnp.where(t == 0, h1_ref[...], pltpu.roll(a, 1, 0))
    m2 = jnp.where(t < 2, h2_ref[...], pltpu.roll(a, 2, 0))
    o_ref[...] = _conv_gate(a, gate, m1, m2, cw_ref, cb_ref).astype(o_ref.dtype)
    a_ref[...] = a


def ffn_up_short(x, hist, w_up_bf16, conv_w, conv_b, n_col=2):
    m, k = x.shape
    dff = conv_w.shape[1]
    n_seq = m // SUBLANES
    tn = dff // n_col
    zeros = jnp.zeros((n_seq, SUBLANES, dff), F32)
    h1 = zeros.at[:, 0].set(hist[:, 1]).reshape(m, dff)
    h2 = zeros.at[:, 0].set(hist[:, 0]).at[:, 1].set(hist[:, 1]).reshape(m, dff)
    col = lambda j: (0, j)
    return pl.pallas_call(
        _ffn_up_short_kernel,
        grid=(n_col,),
        in_specs=[pl.BlockSpec((m, k), lambda j: (0, 0)),
                  pl.BlockSpec((k, tn), col),
                  pl.BlockSpec((k, tn), lambda j: (0, j + n_col)),
                  pl.BlockSpec((m, tn), col), pl.BlockSpec((m, tn), col),
                  pl.BlockSpec((FFN_W, tn), col), pl.BlockSpec((1, tn), col)],
        out_specs=[pl.BlockSpec((m, tn), col), pl.BlockSpec((m, tn), col)],
        out_shape=[jax.ShapeDtypeStruct((m, dff), F32), jax.ShapeDtypeStruct((m, dff), F32)],
        compiler_params=_cparams(1),
        name="ffn_up_short",
    )(x, w_up_bf16, w_up_bf16, h1, h2, conv_w, conv_b.reshape(1, dff))


def _compress_kernel(pt_ref, *refs, pages):
    page_refs = refs[:pages + 1]
    w2t_ref, pecol_ref, w1_ref, w2_ref, o_ref, rows_ref = refs[pages + 1:]
    cpp = PAGE_SIZE // CMP_STRIDE
    n = (pages + 1) * cpp
    hidden = w1_ref.shape[1]
    gpr = LANES // NSA_DH
    pieces = NSA_KV // gpr
    for i, r in enumerate(page_refs):
        for pc in range(pieces):
            tile = r[0, pc * gpr:(pc + 1) * gpr].reshape(LANES, PAGE_SIZE)
            rows_ref[pc, i * PAGE_SIZE:(i + 1) * PAGE_SIZE, :] = tile.T
    pe_term = jnp.sum(pecol_ref[...] * w1_ref[...], axis=0, keepdims=True)
    acc = jnp.zeros((pieces * n, gpr * 2 * hidden), F32)
    for tp in range(CMP_STRIDE // 2):
        lhs = jnp.concatenate(
            [jnp.concatenate([rows_ref[pc, pl.ds(2 * tp, n, stride=CMP_STRIDE), :],
                              rows_ref[pc, pl.ds(2 * tp + 1, n, stride=CMP_STRIDE), :]], axis=1)
             for pc in range(pieces)], axis=0)
        acc = acc + _dot(lhs.astype(BF16), w2t_ref[tp])
    for pc in range(pieces):
        for gl in range(gpr):
            a = acc[pc * n:(pc + 1) * n, gl * 2 * hidden:(gl + 1) * 2 * hidden]
            nxt = pltpu.roll(a, n - 1, 0)
            pre = pe_term + a[:, 0:hidden]
            pre = pre + nxt[:, hidden:2 * hidden]
            o_ref[0, pc * gpr + gl] = _dot(_gelu(pre[0:pages * cpp]).astype(BF16), w2_ref[...])


def compress(rows_t, page_table, pe, w1, w2, pages=32):
    pooled = page_table is not None
    if pooled:
        n_seq, n_pages = page_table.shape
    else:
        n_seq, n_pages = rows_t.shape[0], rows_t.shape[3] // PAGE_SIZE
        page_table = jnp.zeros((1, 1), jnp.int32)
    pages = min(pages, n_pages)
    assert n_pages % pages == 0
    hidden = w1.shape[1]
    cpp = PAGE_SIZE // CMP_STRIDE
    r = CMP_LEN // CMP_STRIDE
    gpr = LANES // NSA_DH
    assert r == 2 and gpr == 2
    w1p = w1.reshape(r, CMP_STRIDE, NSA_DH, hidden)
    w16 = jnp.concatenate([w1p[0], w1p[1]], axis=-1)
    zero = jnp.zeros_like(w16)
    per_tok = jnp.concatenate([jnp.concatenate([w16, zero], axis=-1),
                               jnp.concatenate([zero, w16], axis=-1)], axis=1)
    w2t = per_tok.reshape(CMP_STRIDE // 2, 2 * LANES, gpr * 2 * hidden).astype(BF16)
    w2p = jnp.pad(w2, ((0, 0), (0, SLOT - NSA_DH))).astype(BF16)
    pecol = pe.reshape(CMP_LEN * NSA_DH, 1)

    def page_map(i):
        if pooled:
            return lambda s, j, pt: (pt[s, jnp.minimum(j * pages + i, n_pages - 1)], 0, 0, 0)
        return lambda s, j, pt: (s, 0, 0, jnp.minimum(j * pages + i, n_pages - 1))

    fixed2 = lambda s, j, pt: (0, 0)
    grid_spec = pltpu.PrefetchScalarGridSpec(
        num_scalar_prefetch=1,
        grid=(n_seq, n_pages // pages),
        in_specs=[pl.BlockSpec((1, NSA_KV, NSA_DH, PAGE_SIZE), page_map(i)) for i in range(pages + 1)] + [
            pl.BlockSpec((CMP_STRIDE // 2, 2 * LANES, gpr * 2 * hidden), lambda s, j, pt: (0, 0, 0)),
            pl.BlockSpec((CMP_LEN * NSA_DH, 1), fixed2),
            pl.BlockSpec((CMP_LEN * NSA_DH, hidden), fixed2),
            pl.BlockSpec((hidden, SLOT), fixed2)],
        out_specs=pl.BlockSpec((1, NSA_KV, pages * cpp, SLOT), lambda s, j, pt: (s, 0, j, 0)),
        scratch_shapes=[pltpu.VMEM((NSA_KV // gpr, (pages + 1) * PAGE_SIZE, LANES), F32)],
    )
    return pl.pallas_call(
        functools.partial(_compress_kernel, pages=pages),
        grid_spec=grid_spec,
        out_shape=jax.ShapeDtypeStruct((n_seq, NSA_KV, n_pages * cpp, SLOT), F32),
        compiler_params=_cparams(2),
        name="compress",
    )(page_table, *([rows_t] * (pages + 1)), w2t, pecol, w1, w2p)


def _cmp_select_kernel(q_ref, kc_ref, vc_ref, ov_ref, o_ref, idx_ref, *, tq, n_cmp, n_slc, pos0):
    ncp = kc_ref.shape[2]
    nsp = ov_ref.shape[1]
    hrows = NSA_GROUP * tq
    q_pos = pos0 + (lax.broadcasted_iota(jnp.int32, (hrows, ncp), 0) & (tq - 1))
    blk_i = lax.broadcasted_iota(jnp.int32, (hrows, ncp), 1)
    valid = (blk_i * CMP_STRIDE + (CMP_LEN - 1) <= q_pos) & (blk_i < n_cmp)
    ov = ov_ref[...]
    imps = []
    for g in range(NSA_KV):
        kc = kc_ref[0, g].astype(BF16)
        vc = vc_ref[0, g].astype(BF16)
        heads = range(g * NSA_GROUP, (g + 1) * NSA_GROUP)
        qg = jnp.concatenate([q_ref[:, h * SLOT:(h + 1) * SLOT] for h in heads], axis=0).astype(BF16)
        s = jnp.where(valid, _nt_dot(qg, kc), NEG_INF)
        m = jnp.max(s, axis=-1, keepdims=True)
        e = jnp.where(valid, jnp.exp(s - m), 0.0)
        den = jnp.sum(e, axis=-1, keepdims=True)
        p = e * (1.0 / jnp.where(den > 0.0, den, 1.0))
        o = _dot(p.astype(BF16), vc)
        p_sum = jnp.zeros((tq, ncp), F32)
        for j, h in enumerate(heads):
            o_ref[:, h * SLOT:(h + 1) * SLOT] = o[j * tq:(j + 1) * tq]
            p_sum = p_sum + p[j * tq:(j + 1) * tq]
        hi = p_sum.astype(BF16)
        lo = (p_sum - hi.astype(F32)).astype(BF16)
        imps.append(_dot(hi, ov) + _dot(lo, ov))
    rows = NSA_KV * tq
    imp = jnp.concatenate(imps, axis=0)
    blk = lax.broadcasted_iota(jnp.int32, (rows, nsp), 1)
    cur = (pos0 + (lax.broadcasted_iota(jnp.int32, (rows, nsp), 0) & (tq - 1))) // SLC_BLOCK
    real = blk < n_slc
    causal = real & (blk <= cur)
    forced = (blk == 0) | (blk == cur) | (blk == cur - 1)
    score = jnp.where(causal, imp + jnp.where(forced, FORCE_BONUS, 0.0), NEG_INF)
    score = jnp.where(real, score, REMOVED)
    idx = jnp.zeros((rows, LANES), jnp.int32)
    idx_lane = lax.broadcasted_iota(jnp.int32, (rows, LANES), 1)
    for it in range(min(SLC_TOPN, n_slc)):
        m = jnp.max(score, axis=-1, keepdims=True)
        first = jnp.min(jnp.where(score == m, blk, nsp), axis=-1, keepdims=True)
        score = jnp.where(blk == first, REMOVED, score)
        idx = jnp.where(idx_lane == it, first, idx)
    idx_ref[0] = idx.reshape(NSA_KV, tq, LANES)


def cmp_block_overlap(n_cmp_pad, n_cmp, n_slc, n_slc_pad, lane_off):
    i = np.arange(n_cmp_pad)[:, None]
    j = np.arange(n_slc_pad)[None, :] - lane_off
    start = i * CMP_STRIDE
    hit = (start < (j + 1) * SLC_BLOCK) & (start + CMP_LEN > j * SLC_BLOCK) & (i < n_cmp) & (j >= 0) & (j < n_slc)
    return jnp.asarray(hit.astype(np.float32), dtype=BF16)


def cmp_attention_select(q_slots, kcc, vcc, n_seq, n_cmp, n_slc, pos0):
    tokens = q_slots.shape[0]
    tq = tokens // n_seq
    assert tq & (tq - 1) == 0
    ncp = kcc.shape[2]
    nsp = -(-n_slc // LANES) * LANES
    ov = cmp_block_overlap(ncp, n_cmp, n_slc, nsp, 0)
    hw = NSA_HEADS * SLOT
    return pl.pallas_call(
        functools.partial(_cmp_select_kernel, tq=tq, n_cmp=n_cmp, n_slc=n_slc, pos0=pos0),
        grid=(n_seq,),
        in_specs=[pl.BlockSpec((tq, hw), lambda s: (s, 0)),
                  pl.BlockSpec((1, NSA_KV, ncp, SLOT), lambda s: (s, 0, 0, 0)),
                  pl.BlockSpec((1, NSA_KV, ncp, SLOT), lambda s: (s, 0, 0, 0)),
                  pl.BlockSpec((ncp, nsp), lambda s: (0, 0))],
        out_specs=[pl.BlockSpec((tq, hw), lambda s: (s, 0)),
                   pl.BlockSpec((1, NSA_KV, tq, LANES), lambda s: (s, 0, 0, 0))],
        out_shape=[jax.ShapeDtypeStruct((tokens, hw), F32),
                   jax.ShapeDtypeStruct((n_seq, NSA_KV, tq, LANES), jnp.int32)],
        compiler_params=_cparams(1),
        name="cmp_attention_select",
    )(q_slots, kcc, vcc, ov)


def _cmp_select_prompt_kernel(q_ref, kc_ref, vc_ref, ovt_ref, o_ref, sel_ref, *, tq, n_cmp, n_slc):
    t0 = pl.program_id(2) * tq
    ncp = kc_ref.shape[2]
    nsr = ovt_ref.shape[0]
    kc = kc_ref[0, 0].astype(BF16)
    vc = vc_ref[0, 0].astype(BF16)
    q_pos = t0 + lax.broadcasted_iota(jnp.int32, (ncp, tq), 1)
    blk_i = lax.broadcasted_iota(jnp.int32, (ncp, tq), 0)
    valid = (blk_i * CMP_STRIDE + (CMP_LEN - 1) <= q_pos) & (blk_i < n_cmp)
    p_sum = jnp.zeros((ncp, tq), F32)
    for j in range(NSA_GROUP):
        s = jnp.where(valid, _nt_dot(kc, q_ref[:, j * SLOT:(j + 1) * SLOT]), NEG_INF)
        m = jnp.max(s, axis=0, keepdims=True)
        e = jnp.where(valid, jnp.exp(s - m), 0.0)
        den = jnp.sum(e, axis=0, keepdims=True)
        p = e * (1.0 / jnp.where(den > 0.0, den, 1.0))
        o_ref[:, j * SLOT:(j + 1) * SLOT] = _tn_dot(p.astype(BF16), vc)
        p_sum = p_sum + p
    hi = p_sum.astype(BF16)
    lo = (p_sum - hi.astype(F32)).astype(BF16)
    ovt = ovt_ref[...]
    imp = _dot(ovt, hi) + _dot(ovt, lo)
    blk = lax.broadcasted_iota(jnp.int32, (nsr, tq), 0)
    cur = (t0 + lax.broadcasted_iota(jnp.int32, (nsr, tq), 1)) // SLC_BLOCK
    real = blk < n_slc
    causal = real & (blk <= cur)
    forced = (blk == 0) | (blk == cur) | (blk == cur - 1)
    score = jnp.where(causal, imp + jnp.where(forced, FORCE_BONUS, 0.0), NEG_INF)
    score = jnp.where(real, score, REMOVED)
    picked = jnp.zeros((nsr, tq), jnp.bool_)
    for _ in range(min(SLC_TOPN, n_slc)):
        m = jnp.max(score, axis=0, keepdims=True)
        first = jnp.min(jnp.where(score == m, blk, nsr), axis=0, keepdims=True)
        hit = blk == first
        picked = picked | hit
        score = jnp.where(hit, REMOVED, score)
    bias_t = jnp.where(real & ~(picked & causal), NEG_INF, 0.0)
    slot_t = jnp.concatenate([jnp.zeros((NSA_DH, tq), F32), bias_t], axis=0)
    sel_ref[0, 0] = slot_t.T


def cmp_attention_select_prompt(q_slots, kcc, vcc, n_seq, n_cmp, n_slc, tq):
    tokens = q_slots.shape[0]
    t = tokens // n_seq
    tq = _row_tile(t, tq)
    nt = t // tq
    ncp = kcc.shape[2]
    nsr = SLOT - NSA_DH
    assert n_slc <= nsr
    ovt = cmp_block_overlap(ncp, n_cmp, n_slc, nsr, 0).T
    gw = NSA_GROUP * SLOT
    return pl.pallas_call(
        functools.partial(_cmp_select_prompt_kernel, tq=tq, n_cmp=n_cmp, n_slc=n_slc),
        grid=(n_seq, NSA_KV, nt),
        in_specs=[pl.BlockSpec((tq, gw), lambda s, g, i: (s * nt + i, g)),
                  pl.BlockSpec((1, 1, ncp, SLOT), lambda s, g, i: (s, g, 0, 0)),
                  pl.BlockSpec((1, 1, ncp, SLOT), lambda s, g, i: (s, g, 0, 0)),
                  pl.BlockSpec((nsr, ncp), lambda s, g, i: (0, 0))],
        out_specs=[pl.BlockSpec((tq, gw), lambda s, g, i: (s * nt + i, g)),
                   pl.BlockSpec((1, 1, tq, SLOT), lambda s, g, i: (s, g, i, 0))],
        out_shape=[jax.ShapeDtypeStruct((tokens, NSA_HEADS * SLOT), F32),
                   jax.ShapeDtypeStruct((n_seq, NSA_KV, t, SLOT), F32)],
        compiler_params=_cparams(3),
        name="cmp_attention_select_prompt",
    )(q_slots, kcc, vcc, ovt)


def _prompt_slc_win_kernel(q_ref, sel_ref, ks_ref, vs_ref, kw_ref, vw_ref, oh_ref, os_ref, ow_ref,
                           m_ref, acc_ref, *, tq, seq):
    acc_rows = NSA_DH + 16
    qi = pl.program_id(2)
    t0 = qi * tq
    rows = NSA_GROUP * tq
    sel = sel_ref[0, 0]
    q_plain = jnp.concatenate([q_ref[:, j * SLOT:(j + 1) * SLOT] for j in range(NSA_GROUP)], axis=0)
    q_aug = jnp.concatenate([(q_ref[:, j * SLOT:(j + 1) * SLOT].astype(F32) + sel).astype(BF16)
                             for j in range(NSA_GROUP)], axis=0)
    zeros_k = jnp.zeros((SLOT - NSA_DH, tq), F32)
    ones_row = (lax.broadcasted_iota(jnp.int32, (acc_rows - NSA_DH, tq), 0) == 0).astype(BF16)
    k_off = lax.broadcasted_iota(jnp.int32, (tq, LANES), 0)
    q_off = lax.broadcasted_iota(jnp.int32, (tq, LANES), 1)

    def scores(q_rows, k_top, k_bottom):
        k_rows = jnp.concatenate([k_top, k_bottom], axis=0).T.astype(BF16)
        return [_nt_dot(k_rows, q_rows[c:c + 2 * LANES]) for c in range(0, rows, 2 * LANES)]

    def update(s_t, v_top, start, mask):
        v_t = jnp.concatenate([v_top.astype(BF16), ones_row], axis=0)
        for cg in range(rows // (2 * LANES)):
            p_parts, a_parts = [], []
            for h in range(2):
                c0 = (2 * cg + h) * LANES
                x = s_t[cg][:, h * LANES:(h + 1) * LANES]
                if mask is not None:
                    dist = (t0 + ((c0 + q_off) & (tq - 1))) - (start + k_off)
                    x = jnp.where(mask(dist), x, NEG_INF)
                m_old = m_ref[:, c0:c0 + LANES]
                m_new = jnp.maximum(m_old, jnp.max(x, axis=0, keepdims=True))
                m_ref[:, c0:c0 + LANES] = m_new
                a_parts.append(jnp.exp(m_old - m_new))
                p_parts.append(jnp.exp(x - m_new).astype(BF16))
            c0 = 2 * cg * LANES
            pv = _dot(v_t, jnp.concatenate(p_parts, axis=1))
            acc_ref[:, c0:c0 + 2 * LANES] = jnp.concatenate(a_parts, axis=1) * acc_ref[:, c0:c0 + 2 * LANES] + pv

    def reset():
        m_ref[...] = jnp.full(m_ref.shape, NEG_INF, F32)
        acc_ref[...] = jnp.zeros(acc_ref.shape, F32)

    slot_pad = jnp.zeros((SLOT - NSA_DH, tq), F32)

    def finish(o_ref):
        for j in range(NSA_GROUP):
            a = acc_ref[:, j * tq:(j + 1) * tq]
            o_t = jnp.concatenate([a[0:NSA_DH] * (1.0 / a[NSA_DH:NSA_DH + 1, :]), slot_pad], axis=0)
            o_ref[:, j * SLOT:(j + 1) * SLOT] = o_t.T

    def slc_scores(start):
        return scores(q_aug, ks_ref[0, :, pl.ds(start, tq)], oh_ref[:, pl.ds(start, tq)])

    def slc_update(s_t, start, mask):
        update(s_t, vs_ref[0, :, pl.ds(start, tq)], start, mask)

    k0 = jnp.clip(t0 - WINDOW, 0, seq - WINDOW - tq)
    n_win = WINDOW // tq + 1
    win_start = [pl.multiple_of(k0 + i * tq, tq) for i in range(n_win)]
    win_mask = lambda dist: (dist >= 0) & (dist < WINDOW)

    def win_scores(i):
        return scores(q_plain, kw_ref[0, :, pl.ds(win_start[i], tq)], zeros_k)

    reset()

    def pair(k2, carry):
        start_a = pl.multiple_of(2 * k2 * tq, tq)
        start_b = pl.multiple_of(start_a + tq, tq)
        s_a = slc_scores(start_a)
        s_b = slc_scores(start_b)
        slc_update(s_a, start_a, None)
        slc_update(s_b, start_b, None)
        return carry

    lax.fori_loop(0, qi // 2, pair, 0)

    @pl.when(qi % 2 == 1)
    def _():
        start = pl.multiple_of((qi - 1) * tq, tq)
        slc_update(slc_scores(start), start, None)

    diag = pl.multiple_of(t0, tq)
    s_cur = slc_scores(diag)
    s_next = win_scores(0)
    slc_update(s_cur, diag, lambda dist: dist >= 0)
    finish(os_ref)
    reset()
    for i in range(n_win):
        s_cur = s_next
        if i + 1 < n_win:
            s_next = win_scores(i + 1)
        update(s_cur, vw_ref[0, :, pl.ds(win_start[i], tq)], win_start[i], win_mask)
    finish(ow_ref)


def prompt_slc_win_attention(q_slots, sel, ks_t, vs_t, kw_t, vw_t, n_seq, tq):
    tokens = q_slots.shape[0]
    seq = tokens // n_seq
    tq = _row_tile(seq, tq)
    assert tq & (tq - 1) == 0 and WINDOW % tq == 0 and seq >= WINDOW + tq
    nt = seq // tq
    gw = NSA_GROUP * SLOT
    assert tq % (2 * LANES) == 0
    onehot_t = jax.nn.one_hot(jnp.arange(seq) // SLC_BLOCK, SLOT - NSA_DH, dtype=F32).T
    kv_spec = pl.BlockSpec((1, NSA_DH, seq), lambda s, g, i: (s, g, 0))
    return pl.pallas_call(
        functools.partial(_prompt_slc_win_kernel, tq=tq, seq=seq),
        grid=(n_seq, NSA_KV, nt),
        in_specs=[pl.BlockSpec((tq, gw), lambda s, g, i: (s * nt + i, g)),
                  pl.BlockSpec((1, 1, tq, SLOT), lambda s, g, i: (s, g, i, 0)),
                  kv_spec, kv_spec, kv_spec, kv_spec,
                  pl.BlockSpec((SLOT - NSA_DH, seq), lambda s, g, i: (0, 0))],
        out_specs=[pl.BlockSpec((tq, gw), lambda s, g, i: (s * nt + i, g)),
                   pl.BlockSpec((tq, gw), lambda s, g, i: (s * nt + i, g))],
        out_shape=[jax.ShapeDtypeStruct((tokens, NSA_HEADS * SLOT), F32),
                   jax.ShapeDtypeStruct((tokens, NSA_HEADS * SLOT), F32)],
        scratch_shapes=[pltpu.VMEM((1, NSA_GROUP * tq), F32),
                        pltpu.VMEM((NSA_DH + 16, NSA_GROUP * tq), F32)],
        compiler_params=_cparams(3),
        name="prompt_slc_win_attention",
    )(q_slots, sel, ks_t, vs_t, kw_t, vw_t, onehot_t)


def _sample_slc_kernel(idx_ref, pt_ref, q_ref, kn_ref, vn_ref, *refs, topn, past, t_pad):
    k_refs = refs[:topn]
    v_refs = refs[topn:2 * topn]
    o_ref = refs[2 * topn]
    s_id, g_id, t_id = pl.program_id(0), pl.program_id(1), pl.program_id(2)
    base = ((s_id * NSA_KV + g_id) * t_pad + t_id) * topn
    q_pos = past + t_id
    cur = q_pos // SLC_BLOCK
    first_new = past // SLC_BLOCK
    q = q_ref[0, 0, 0].astype(BF16)
    kb = jnp.concatenate([r[0, 0] for r in k_refs], axis=1).astype(BF16)
    vb = jnp.concatenate([r[0, 0] for r in v_refs], axis=1).astype(BF16)
    n_keys = topn * PAGE_SIZE
    lane = lax.broadcasted_iota(jnp.int32, (1, n_keys), 1)
    slot = lane // PAGE_SIZE
    in_page = lane % PAGE_SIZE
    k_pos = in_page
    limit = jnp.zeros((1, n_keys), jnp.int32)
    n_new = jnp.int32(0)
    for kk in range(topn):
        b = idx_ref[base + kk]
        here = slot == kk
        k_pos = jnp.where(here, (b // 2) * PAGE_SIZE + in_page, k_pos)
        last = jnp.where(b <= cur, jnp.minimum(q_pos, past - 1), -1)
        limit = jnp.where(here, jnp.where(in_page // SLC_BLOCK == b % 2, last, -1), limit)
        n_new = n_new + jnp.where(b == first_new, 1, 0)
    valid = k_pos <= limit
    s_old = jnp.where(valid, _dot(q, kb), NEG_INF)
    new_lane = lax.broadcasted_iota(jnp.int32, (1, SUBLANES), 1)
    valid_new = past + new_lane <= jnp.where(n_new > 0, q_pos, past - 1)
    s_new = jnp.where(valid_new, _dot(q, kn_ref[0, 0].astype(BF16)), NEG_INF)
    m = jnp.maximum(jnp.max(s_old, axis=-1, keepdims=True), jnp.max(s_new, axis=-1, keepdims=True))
    p_old = jnp.exp(s_old - m)
    p_new = jnp.exp(s_new - m)
    l = jnp.sum(p_old, axis=-1, keepdims=True) + jnp.sum(p_new, axis=-1, keepdims=True)
    o = _nt_dot(p_old.astype(BF16), vb) + _nt_dot(p_new.astype(BF16), vn_ref[0, 0].astype(BF16))
    o_ref[0, 0, 0] = o / l


def sample_slc_attention(q_rows, idx, page_table, pool_k, pool_v, k_new_t, v_new_t, t_real, past):
    n_seq, _, t_pad, _, dh = q_rows.shape
    topn = idx.shape[-1]
    assert past % SLC_BLOCK == 0 and t_real <= SUBLANES and PAGE_SIZE == 2 * SLC_BLOCK
    last_old = past // SLC_BLOCK - 1
    logical = jnp.clip(idx, 0, last_old) // 2
    n_pages = page_table.shape[1]
    hit = logical[..., None] == jnp.arange(n_pages, dtype=jnp.int32)
    phys = jnp.sum(jnp.where(hit, page_table[:, None, None, None, :], 0), axis=-1)

    def blk_map(kk):
        def f(s, g, t, idx_ref, pg_ref):
            return (pg_ref[((s * NSA_KV + g) * t_pad + t) * topn + kk], g, 0, 0)
        return f

    blk_specs = [pl.BlockSpec((1, 1, dh, PAGE_SIZE), blk_map(kk)) for kk in range(topn)]
    new_spec = pl.BlockSpec((1, 1, dh, SUBLANES), lambda s, g, t, i_r, p_r: (s, g, 0, 0))
    q_spec = pl.BlockSpec((1, 1, 1, SUBLANES, dh), lambda s, g, t, i_r, p_r: (s, g, t, 0, 0))
    grid_spec = pltpu.PrefetchScalarGridSpec(
        num_scalar_prefetch=2,
        grid=(n_seq, NSA_KV, t_real),
        in_specs=[q_spec, new_spec, new_spec] + blk_specs + blk_specs,
        out_specs=q_spec,
    )
    return pl.pallas_call(
        functools.partial(_sample_slc_kernel, topn=topn, past=past, t_pad=t_pad),
        grid_spec=grid_spec,
        out_shape=jax.ShapeDtypeStruct((n_seq, NSA_KV, t_real, SUBLANES, dh), F32),
        compiler_params=_cparams(3),
        name="sample_slc_attention",
    )(idx.reshape(-1), phys.reshape(-1), q_rows, k_new_t, v_new_t, *([pool_k] * topn), *([pool_v] * topn))


def _sample_win_kernel(q_ref, wk_ref, wv_ref, kn_ref, vn_ref, o_ref, *, past, t_pad):
    rows = t_pad * SUBLANES
    wb = wk_ref.shape[-1]
    q = q_ref[0, 0].reshape(rows, q_ref.shape[-1]).astype(BF16)
    q_pos = past + lax.broadcasted_iota(jnp.int32, (rows, 1), 0) // SUBLANES
    k_pos = past - wb + lax.broadcasted_iota(jnp.int32, (1, wb), 1)
    dist = q_pos - k_pos
    valid = (dist >= 0) & (dist < WINDOW) & (k_pos >= 0)
    s_old = jnp.where(valid, _dot(q, wk_ref[0, 0].astype(BF16)), NEG_INF)
    n_pos = past + lax.broadcasted_iota(jnp.int32, (1, SUBLANES), 1)
    dist_n = q_pos - n_pos
    valid_n = (dist_n >= 0) & (dist_n < WINDOW)
    s_new = jnp.where(valid_n, _dot(q, kn_ref[0, 0].astype(BF16)), NEG_INF)
    m = jnp.maximum(jnp.max(s_old, axis=-1, keepdims=True), jnp.max(s_new, axis=-1, keepdims=True))
    p_old = jnp.exp(s_old - m)
    p_new = jnp.exp(s_new - m)
    l = jnp.sum(p_old, axis=-1, keepdims=True) + jnp.sum(p_new, axis=-1, keepdims=True)
    o = _nt_dot(p_old.astype(BF16), wv_ref[0, 0].astype(BF16)) + _nt_dot(p_new.astype(BF16),
                                                                      vn_ref[0, 0].astype(BF16))
    o_ref[0, 0] = (o / l).reshape(t_pad, SUBLANES, o.shape[-1])


def sample_win_attention(q_rows, win_k_t, win_v_t, k_new_t, v_new_t, past):
    n_seq, _, t_pad, _, dh = q_rows.shape
    wb = win_k_t.shape[-1]
    q_spec = pl.BlockSpec((1, 1, t_pad, SUBLANES, dh), lambda s, g: (s, g, 0, 0, 0))
    win_spec = pl.BlockSpec((1, 1, dh, wb), lambda s, g: (s, g, 0, 0))
    new_spec = pl.BlockSpec((1, 1, dh, SUBLANES), lambda s, g: (s, g, 0, 0))
    return pl.pallas_call(
        functools.partial(_sample_win_kernel, past=past, t_pad=t_pad),
        grid=(n_seq, NSA_KV),
        in_specs=[q_spec, win_spec, win_spec, new_spec, new_spec],
        out_specs=q_spec,
        out_shape=jax.ShapeDtypeStruct(q_rows.shape, F32),
        compiler_params=_cparams(2),
        name="sample_win_attention",
    )(q_rows, win_k_t, win_v_t, k_new_t, v_new_t)


def _to_slots(a):
    lead = a.shape[:-1]
    n = a.shape[-1] // NSA_DH
    a = a.reshape(*lead, n, NSA_DH)
    a = jnp.pad(a, [(0, 0)] * (a.ndim - 1) + [(0, SLOT - NSA_DH)])
    return a.reshape(*lead, n * SLOT)


def _odd_weights(w_in, w_out):
    d = w_in.shape[0]
    hq = NSA_HEADS * NSA_DH
    kvw = NSA_KV * NSA_DH
    wq = _to_slots(w_in[:, :hq] * np.float32(NSA_DH ** -0.5))
    wg = jnp.pad(w_in[:, hq + 6 * kvw:], ((0, 0), (0, LANES - 3 * NSA_HEADS)))
    w_q = jnp.concatenate([wq, wg], axis=1).astype(BF16)
    w_kvt = w_in[:, hq:hq + 6 * kvw].T.astype(BF16)
    wo = jnp.pad(w_out.reshape(NSA_HEADS, NSA_DH, d), ((0, 0), (0, SLOT - NSA_DH), (0, 0)))
    wo = wo.reshape(NSA_HEADS * SLOT, d).astype(BF16)
    k = NSA_HEADS * SLOT
    e = np.zeros((LANES, 3 * k), np.float32)
    for c in range(3):
        for h in range(NSA_HEADS):
            e[c * NSA_HEADS + h, c * k + h * SLOT:c * k + (h + 1) * SLOT] = 1.0
    return w_q, w_kvt, wo, jnp.asarray(e, dtype=BF16)


def _group_rows(q_slots, n_seq, t_pad):
    q = q_slots.reshape(n_seq, t_pad, NSA_KV, NSA_GROUP, SLOT)[..., :NSA_DH]
    q = q.transpose(0, 2, 1, 3, 4)
    return jnp.pad(q, ((0, 0), (0, 0), (0, 0), (0, SUBLANES - NSA_GROUP), (0, 0)))


def _ungroup_rows(o, n_seq, t_pad):
    t = o.shape[2]
    o = o[:, :, :, :NSA_GROUP].transpose(0, 2, 1, 3, 4)
    o = jnp.pad(o, ((0, 0), (0, t_pad - t), (0, 0), (0, 0), (0, SLOT - NSA_DH)))
    return o.reshape(n_seq * t_pad, NSA_HEADS * SLOT)


def _feature_major(cache):
    return cache.transpose(0, 2, 3, 1)


def _token_major(a_t):
    return a_t.transpose(0, 3, 1, 2)


def _pad_rows(a, t_pad):
    return jnp.pad(a, ((0, 0), (0, t_pad - a.shape[1])) + ((0, 0),) * (a.ndim - 2))


def kernel(x_prompt, x_sample, state_sconv, state_ret, cache_cmp_k, cache_cmp_v, cache_slc_k, cache_slc_v,
           cache_win_k, cache_win_v, state_ffn_conv, page_table,
           w_in_even, sconv_w, sconv_b, ret_gn_g, w_out_even,
           w_in_odd, cmp_pe, cmp_w1, cmp_w2, w_out_odd,
           ln_mix_g, ln_mix_b, ffn_w_up, ffn_conv_w, ffn_conv_b, ffn_w_down, ln_ffn_g, ln_ffn_b):
    b_p, s_p, d_model = x_prompt.shape
    b_s, t_s, _ = x_sample.shape
    n_pages = page_table.shape[1]
    past = n_pages * PAGE_SIZE
    t_pad = SUBLANES
    assert t_s <= t_pad and t_s >= SCONV_W - 1 and t_s < CMP_STRIDE and past % PAGE_SIZE == 0
    assert s_p % RET_CHUNK == 0 and s_p % PAGE_SIZE == 0
    d_sconv = sconv_w.shape[-1]
    d_ff = ffn_conv_w.shape[-1]
    gd = NSA_KV * NSA_DH
    depth = ln_mix_g.shape[0]

    xp = x_prompt.reshape(b_p * s_p, d_model)
    xs = _pad_rows(x_sample, t_pad).reshape(b_s * t_pad, d_model)
    outs = {k: [] for k in ("sconv_p", "sconv_s", "ret_p", "ret_s", "cmp_k_p", "cmp_v_p", "slc_k_p", "slc_v_p",
                            "cmp_k_s", "cmp_v_s", "slc_k_s", "slc_v_s", "win_k_p", "win_v_p", "win_k_s",
                            "win_v_s", "ffn_p", "ffn_s")}

    for layer in range(depth):
        if layer % 2 == 0:
            e = layer // 2
            w_in = w_in_even[e].astype(BF16)
            w_out = w_out_even[e].astype(BF16)
            n_in = w_in.shape[1]
            (zp,) = matmul_split(xp, w_in, [n_in], [F32])
            yp, hc, st = even_mixer(zp, jnp.zeros((b_p, SCONV_W - 1, d_sconv), F32),
                                    jnp.zeros((b_p,) + state_ret.shape[2:], F32), jnp.arange(s_p),
                                    RET_CHUNK, RET_CHUNK, sconv_w[e], sconv_b[e], ret_gn_g[e])
            outs["sconv_p"].append(hc)
            outs["ret_p"].append(st)
            xp = matmul_residual_ln(yp, w_out, xp, ln_mix_g[layer], ln_mix_b[layer])
            (zs,) = matmul_split(xs, w_in, [n_in], [F32])
            ys, hc, st = even_mixer(zs, state_sconv[e], state_ret[e], past + jnp.arange(t_pad),
                                    t_pad, t_s, sconv_w[e], sconv_b[e], ret_gn_g[e])
            outs["sconv_s"].append(hc)
            outs["ret_s"].append(st)
            xs = matmul_residual_ln(ys, w_out, xs, ln_mix_g[layer], ln_mix_b[layer])
        else:
            o = layer // 2
            w_q, w_kvt, w_out, e_gate = _odd_weights(w_in_odd[o], w_out_odd[o])
            pe, w1, w2 = cmp_pe[o], cmp_w1[o], cmp_w2[o]
            qp, gp, kc, vc, ks, vs, kw, vw = nsa_projection(xp, b_p, w_q, w_kvt, BF16)
            as_cache = lambda a_t: _token_major(a_t.reshape(b_p, NSA_KV, NSA_DH, -1))
            keep = min(WINDOW, s_p)
            outs["cmp_k_p"].append(as_cache(kc))
            outs["cmp_v_p"].append(as_cache(vc))
            outs["slc_k_p"].append(as_cache(ks))
            outs["slc_v_p"].append(as_cache(vs))
            outs["win_k_p"].append(as_cache(kw[:, :, s_p - keep:]))
            outs["win_v_p"].append(as_cache(vw[:, :, s_p - keep:]))
            kcc = compress(kc.reshape(b_p, NSA_KV, NSA_DH, s_p), None, pe[0], w1[0], w2[0])
            vcc = compress(vc.reshape(b_p, NSA_KV, NSA_DH, s_p), None, pe[1], w1[1], w2[1])
            n_cmp = s_p // CMP_STRIDE - CMP_LEN // CMP_STRIDE + 1
            n_slc = s_p // SLC_BLOCK
            oc, sel = cmp_attention_select_prompt(qp, kcc, vcc, b_p, n_cmp, n_slc, 256)
            osl, ow = prompt_slc_win_attention(qp, sel, ks, vs, kw, vw, b_p, 256)
            xp = nsa_merge_residual_ln(oc, osl, ow, gp, e_gate, w_out, xp, ln_mix_g[layer], ln_mix_b[layer])
            qs, gs, *kv_s = nsa_projection(xs, 1, w_q, w_kvt, F32)
            kc, vc, ks, vs, kw, vw = [a.reshape(NSA_KV, NSA_DH, b_s, t_pad).transpose(2, 0, 1, 3) for a in kv_s]
            new_rows = lambda a_t: _token_major(a_t[..., :t_s])
            outs["cmp_k_s"].append(new_rows(kc))
            outs["cmp_v_s"].append(new_rows(vc))
            outs["slc_k_s"].append(new_rows(ks))
            outs["slc_v_s"].append(new_rows(vs))
            win_k = _feature_major(cache_win_k[o])
            win_v = _feature_major(cache_win_v[o])
            wb = win_k.shape[-1]
            keep = min(WINDOW, wb + t_s)
            outs["win_k_s"].append(_token_major(jnp.concatenate([win_k, kw[..., :t_s]], axis=-1)[..., -keep:]))
            outs["win_v_s"].append(_token_major(jnp.concatenate([win_v, vw[..., :t_s]], axis=-1)[..., -keep:]))
            kcc = compress(_feature_major(cache_cmp_k[o]), page_table, pe[0], w1[0], w2[0])
            vcc = compress(_feature_major(cache_cmp_v[o]), page_table, pe[1], w1[1], w2[1])
            n_cmp = (past + t_s) // CMP_STRIDE - CMP_LEN // CMP_STRIDE + 1
            n_slc = -(-(past + t_s) // SLC_BLOCK)
            oc, idx = cmp_attention_select(qs, kcc, vcc, b_s, n_cmp, n_slc, past)
            topn = min(SLC_TOPN, n_slc)
            q_rows = _group_rows(qs, b_s, t_pad)
            osl = sample_slc_attention(q_rows, idx[..., :topn], page_table, _feature_major(cache_slc_k[o]),
                                       _feature_major(cache_slc_v[o]), ks, vs, t_s, past)
            ow = sample_win_attention(q_rows, win_k, win_v, kw, vw, past)
            xs = nsa_merge_residual_ln(oc, _ungroup_rows(osl, b_s, t_pad), _ungroup_rows(ow, b_s, t_pad), gs,
                                       e_gate, w_out, xs, ln_mix_g[layer], ln_mix_b[layer])
        w_up = ffn_w_up[layer].astype(BF16)
        w_down = ffn_w_down[layer].astype(BF16)
        hp, hist_p = ffn_up_sequences(xp, b_p, jnp.zeros((b_p, FFN_W - 1, d_ff), F32), w_up,
                                      ffn_conv_w[layer], ffn_conv_b[layer])
        outs["ffn_p"].append(hist_p)
        xp = matmul_residual_ln(hp, w_down, xp, ln_ffn_g[layer], ln_ffn_b[layer])
        hs, a_s = ffn_up_short(xs, state_ffn_conv[layer], w_up, ffn_conv_w[layer], ffn_conv_b[layer])
        outs["ffn_s"].append(a_s.reshape(b_s, t_pad, d_ff)[:, t_s - (FFN_W - 1):t_s])
        xs = matmul_residual_ln(hs, w_down, xs, ln_ffn_g[layer], ln_ffn_b[layer])

    st = jnp.stack
    y_p = xp.reshape(b_p, s_p, d_model)
    y_s = xs.reshape(b_s, t_pad, d_model)[:, :t_s]
    order = ("sconv_p", "sconv_s", "ret_p", "ret_s", "cmp_k_p", "cmp_v_p", "slc_k_p", "slc_v_p",
             "cmp_k_s", "cmp_v_s", "slc_k_s", "slc_v_s", "win_k_p", "win_v_p", "win_k_s", "win_v_s",
             "ffn_p", "ffn_s")
    return (y_p, y_s) + tuple(st(outs[k]) for k in order)
```

```python
import functools

import numpy as np
import jax
import jax.numpy as jnp
from jax import lax
from jax.experimental import pallas as pl
from jax.experimental.pallas import tpu as pltpu

F32 = jnp.float32
BF16 = jnp.bfloat16

SUBLANES = 8
LANES = 128
VMEM_LIMIT_BYTES = 56 * 1024 * 1024
MATMUL_ROWS = 512
MERGE_ROWS = 256
ATTN_ROWS = 256
SELECT_ROWS = 1024
COMPRESS_PAGES = 32

DEPTH = 2
SCONV_W = 3
RET_HEADS = 4
RET_CHUNK = 128
ROPE_BASE = 10000.0
NSA_HEADS = 16
NSA_KV = 4
NSA_GROUP = NSA_HEADS // NSA_KV
NSA_DH = 64
CMP_LEN = 32
CMP_STRIDE = 16
SLC_BLOCK = 64
SLC_TOPN = 16
WINDOW = 512
PAGE_SIZE = 128
FFN_W = 3
ALPHA = (2.0 * DEPTH) ** 0.25
LN_EPS = 1e-5
NEG_INF = -1e30
REMOVED = -3e38
FORCE_BONUS = 1e4
SLOT = 2 * NSA_DH


def _cparams(n_grid):
    return pltpu.CompilerParams(dimension_semantics=("arbitrary",) * n_grid,
                                vmem_limit_bytes=VMEM_LIMIT_BYTES)


def _row_tile(m, want):
    t = min(m, want)
    assert m % t == 0, (m, t)
    return t


def _nt_dot(a, b):
    return lax.dot_general(a, b, (((1,), (1,)), ((), ())), preferred_element_type=F32)


def _tn_dot(a, b):
    return lax.dot_general(a, b, (((0,), (0,)), ((), ())), preferred_element_type=F32)


def _dot(a, b):
    return jnp.dot(a, b, preferred_element_type=F32)


def _gelu(x):
    return 0.5 * x * (1.0 + jnp.tanh(np.float32(np.sqrt(2.0 / np.pi)) * (x + 0.044715 * (x * x * x))))


def _layer_norm_rows(r, g, b):
    mu = jnp.mean(r, axis=-1, keepdims=True)
    d = r - mu
    var = jnp.mean(d * d, axis=-1, keepdims=True)
    return d * lax.rsqrt(var + LN_EPS) * g + b


def _mm_split_kernel(x_ref, w_ref, *o_refs, cuts):
    acc = _dot(x_ref[...].astype(BF16), w_ref[...])
    for o_ref, (lo, hi) in zip(o_refs, cuts):
        o_ref[...] = acc[:, lo:hi].astype(o_ref.dtype)


def matmul_split(x, w_bf16, widths, dtypes, tm=MATMUL_ROWS):
    m, k = x.shape
    n = w_bf16.shape[1]
    assert sum(widths) == n and all(wd % LANES == 0 for wd in widths)
    tm = _row_tile(m, tm)
    cuts, lo = [], 0
    for wd in widths:
        cuts.append((lo, lo + wd))
        lo += wd
    return pl.pallas_call(
        functools.partial(_mm_split_kernel, cuts=tuple(cuts)),
        grid=(m // tm,),
        in_specs=[pl.BlockSpec((tm, k), lambda i: (i, 0)),
                  pl.BlockSpec((k, n), lambda i: (0, 0))],
        out_specs=[pl.BlockSpec((tm, wd), lambda i: (i, 0)) for wd in widths],
        out_shape=[jax.ShapeDtypeStruct((m, wd), dt) for wd, dt in zip(widths, dtypes)],
        compiler_params=_cparams(1),
        name="matmul_split",
    )(x, w_bf16)


def _nsa_proj_kernel(x_ref, wq_ref, wkvt_ref, q_ref, g_ref, *kv_refs, nq):
    xb = x_ref[...].astype(BF16)
    acc = _dot(xb, wq_ref[...])
    q_ref[...] = acc[:, :nq].astype(q_ref.dtype)
    g_ref[...] = acc[:, nq:]
    acc_t = _nt_dot(wkvt_ref[...], xb)
    gd = acc_t.shape[0] // len(kv_refs)
    for i, r in enumerate(kv_refs):
        r[0] = acc_t[i * gd:(i + 1) * gd, :]


def nsa_projection(x, n_seq, wq_bf16, wkvt_bf16, q_dtype, n_kv=6, tm=MATMUL_ROWS):
    m, d = x.shape
    seq = m // n_seq
    tm = _row_tile(seq, tm)
    nt = seq // tm
    nq = wq_bf16.shape[1] - LANES
    gd = wkvt_bf16.shape[0] // n_kv
    return pl.pallas_call(
        functools.partial(_nsa_proj_kernel, nq=nq),
        grid=(n_seq, nt),
        in_specs=[pl.BlockSpec((tm, d), lambda s, i: (s * nt + i, 0)),
                  pl.BlockSpec((d, nq + LANES), lambda s, i: (0, 0)),
                  pl.BlockSpec((n_kv * gd, d), lambda s, i: (0, 0))],
        out_specs=[pl.BlockSpec((tm, nq), lambda s, i: (s * nt + i, 0)),
                   pl.BlockSpec((tm, LANES), lambda s, i: (s * nt + i, 0))] + [
                      pl.BlockSpec((1, gd, tm), lambda s, i: (s, 0, i)) for _ in range(n_kv)],
        out_shape=[jax.ShapeDtypeStruct((m, nq), q_dtype), jax.ShapeDtypeStruct((m, LANES), F32)] + [
            jax.ShapeDtypeStruct((n_seq, gd, seq), F32) for _ in range(n_kv)],
        compiler_params=_cparams(2),
        name="nsa_projection",
    )(x, wq_bf16, wkvt_bf16)


def _mm_res_ln_kernel(a_ref, w_ref, x_ref, g_ref, b_ref, o_ref):
    y = _dot(a_ref[...].astype(BF16), w_ref[...])
    o_ref[...] = _layer_norm_rows(ALPHA * x_ref[...] + y, g_ref[...], b_ref[...])


def matmul_residual_ln(a, w_bf16, x, g, b, tm=MATMUL_ROWS):
    m, k = a.shape
    d = w_bf16.shape[1]
    tm = _row_tile(m, tm)
    return pl.pallas_call(
        _mm_res_ln_kernel,
        grid=(m // tm,),
        in_specs=[pl.BlockSpec((tm, k), lambda i: (i, 0)),
                  pl.BlockSpec((k, d), lambda i: (0, 0)),
                  pl.BlockSpec((tm, d), lambda i: (i, 0)),
                  pl.BlockSpec((1, d), lambda i: (0, 0)),
                  pl.BlockSpec((1, d), lambda i: (0, 0))],
        out_specs=pl.BlockSpec((tm, d), lambda i: (i, 0)),
        out_shape=jax.ShapeDtypeStruct((m, d), F32),
        compiler_params=_cparams(1),
        name="matmul_residual_ln",
    )(a, w_bf16, x, g.reshape(1, d), b.reshape(1, d))


def _expand_gates(gates_raw, e_ref):
    sig = jax.nn.sigmoid(gates_raw)
    hi = sig.astype(BF16)
    lo = (sig - hi.astype(F32)).astype(BF16)
    e = e_ref[...]
    return _dot(hi, e) + _dot(lo, e)


def _nsa_merge_ln_kernel(oc_ref, os_ref, ow_ref, gt_ref, e_ref, w_ref, x_ref, g_ref, b_ref, o_ref, *, k):
    gx = _expand_gates(gt_ref[...], e_ref)
    o = gx[:, 0:k] * oc_ref[...] + gx[:, k:2 * k] * os_ref[...] + gx[:, 2 * k:3 * k] * ow_ref[...]
    y = _dot(o.astype(BF16), w_ref[...])
    o_ref[...] = _layer_norm_rows(ALPHA * x_ref[...] + y, g_ref[...], b_ref[...])


def nsa_merge_residual_ln(oc, osl, ow, gates, e_bf16, w_bf16, x, g, b, tm=MERGE_ROWS):
    m, k = oc.shape
    d = w_bf16.shape[1]
    tm = _row_tile(m, tm)
    row = lambda i: (i, 0)
    fixed = lambda i: (0, 0)
    return pl.pallas_call(
        functools.partial(_nsa_merge_ln_kernel, k=k),
        grid=(m // tm,),
        in_specs=[pl.BlockSpec((tm, k), row), pl.BlockSpec((tm, k), row), pl.BlockSpec((tm, k), row),
                  pl.BlockSpec((tm, LANES), row),
                  pl.BlockSpec((LANES, 3 * k), fixed),
                  pl.BlockSpec((k, d), fixed),
                  pl.BlockSpec((tm, d), row),
                  pl.BlockSpec((1, d), fixed), pl.BlockSpec((1, d), fixed)],
        out_specs=pl.BlockSpec((tm, d), row),
        out_shape=jax.ShapeDtypeStruct((m, d), F32),
        compiler_params=_cparams(1),
        name="nsa_merge_residual_ln",
    )(oc, osl, ow, gates, e_bf16, w_bf16, x, g.reshape(1, d), b.reshape(1, d))


def _even_mixer_kernel(z_ref, hist_ref, st_ref, cos_ref, sin_ref, decay_ref, qdec_ref, kdec_ref, sdec_ref,
                       cw_ref, cb_ref, gn_ref, y_ref, hist_out_ref, st_out_ref, carry, state,
                       *, rows, valid, dconv, dk):
    c = pl.program_id(1)
    r0 = valid - 2 - (rows - SUBLANES)

    @pl.when(c == 0)
    def _():
        carry[r0:r0 + 2, :] = hist_ref[0]
        state[...] = st_ref[0]

    d = dconv
    h = z_ref[:, 0:d]
    gate_b = z_ref[:, d:2 * d]
    gate_c = z_ref[:, 2 * d:3 * d]
    ch = gate_c * h
    row = lax.broadcasted_iota(jnp.int32, (rows, d), 0)
    h0 = carry[r0:r0 + 1, :]
    h1 = carry[r0 + 1:r0 + 2, :]
    m1 = jnp.where(row == 0, h1, pltpu.roll(ch, 1, 0))
    m2 = jnp.where(row == 0, h0, jnp.where(row == 1, h1, pltpu.roll(ch, 2, 0)))
    u = ((cb_ref[...] + m2 * cw_ref[0:1, :]) + m1 * cw_ref[1:2, :]) + ch * cw_ref[2:3, :]
    y_ref[:, 0:d] = gate_b * u
    carry[...] = ch[rows - SUBLANES:rows, :]
    hist_out_ref[0] = carry[r0:r0 + 2, :]

    cosf = cos_ref[...]
    sinf = sin_ref[...]
    scale = np.float32(dk ** -0.5)
    for hh in range(RET_HEADS):
        q = z_ref[:, 3 * d + hh * dk:3 * d + (hh + 1) * dk]
        k = z_ref[:, 4 * d + hh * dk:4 * d + (hh + 1) * dk]
        v = z_ref[:, 5 * d + hh * dk:5 * d + (hh + 1) * dk]
        gsw = z_ref[:, 6 * d + hh * dk:6 * d + (hh + 1) * dk]
        q = (q * cosf + pltpu.roll(q, dk // 2, 1) * sinf) * scale
        k = k * cosf + pltpu.roll(k, dk // 2, 1) * sinf
        qb = q.astype(BF16)
        vb = v.astype(BF16)
        s_old = state[hh]
        scores = _nt_dot(qb, k.astype(BF16)) * decay_ref[hh]
        intra = _dot(scores.astype(BF16), vb)
        cross = _dot(qb, s_old.astype(BF16)) * qdec_ref[hh]
        kd = (k * kdec_ref[hh]).astype(BF16)
        state[hh] = s_old * sdec_ref[hh] + _tn_dot(kd, vb)
        o = intra + cross
        mu = jnp.mean(o, axis=-1, keepdims=True)
        dv = o - mu
        var = jnp.mean(dv * dv, axis=-1, keepdims=True)
        on = dv * lax.rsqrt(var + LN_EPS) * gn_ref[:, hh * dk:(hh + 1) * dk]
        y_ref[:, d + hh * dk:d + (hh + 1) * dk] = (gsw * jax.nn.sigmoid(gsw)) * on
    st_out_ref[0] = state[...]


def _retention_tables(rows, valid, dk):
    log_gamma = jnp.log1p(-jnp.exp2(-5.0 - jnp.arange(RET_HEADS, dtype=F32)))
    n = jnp.arange(rows, dtype=F32)
    diff = n[:, None] - n[None, :]
    lg = log_gamma[:, None, None]
    decay = jnp.where(diff >= 0, jnp.exp(lg * jnp.maximum(diff, 0.0)), 0.0)
    q_dec = jnp.exp((n[None, :] + 1.0) * log_gamma[:, None])
    k_dec = jnp.where(n[None, :] < valid, jnp.exp((valid - 1.0 - n[None, :]) * log_gamma[:, None]), 0.0)
    s_dec = jnp.exp(valid * log_gamma)
    bc = lambda a: jnp.broadcast_to(a[:, :, None], (RET_HEADS, rows, dk))
    return decay, bc(q_dec), bc(k_dec), jnp.broadcast_to(s_dec[:, None, None], (RET_HEADS, 1, dk))


def _rope_tables(pos, dk):
    half = dk // 2
    inv = ROPE_BASE ** (-jnp.arange(half, dtype=F32) / half)
    ang = pos.astype(F32)[:, None] * inv
    cos, sin = jnp.cos(ang), jnp.sin(ang)
    return jnp.concatenate([cos, cos], axis=-1), jnp.concatenate([-sin, sin], axis=-1)


def even_mixer(z, hist, st, pos, rows, valid, conv_w, conv_b, gn_g):
    n_seq, _, dconv = hist.shape
    dk = st.shape[-1]
    n_chunks = z.shape[0] // (n_seq * rows)
    cosf, sinf = _rope_tables(pos, dk)
    decay, q_dec, k_dec, s_dec = _retention_tables(rows, valid, dk)
    fixed3 = lambda s, c: (0, 0, 0)
    fixed2 = lambda s, c: (0, 0)
    return pl.pallas_call(
        functools.partial(_even_mixer_kernel, rows=rows, valid=valid, dconv=dconv, dk=dk),
        grid=(n_seq, n_chunks),
        in_specs=[pl.BlockSpec((rows, 7 * dconv), lambda s, c: (s * n_chunks + c, 0)),
                  pl.BlockSpec((1, 2, dconv), lambda s, c: (s, 0, 0)),
                  pl.BlockSpec((1, RET_HEADS, dk, dk), lambda s, c: (s, 0, 0, 0)),
                  pl.BlockSpec((rows, dk), lambda s, c: (c, 0)),
                  pl.BlockSpec((rows, dk), lambda s, c: (c, 0)),
                  pl.BlockSpec((RET_HEADS, rows, rows), fixed3),
                  pl.BlockSpec((RET_HEADS, rows, dk), fixed3),
                  pl.BlockSpec((RET_HEADS, rows, dk), fixed3),
                  pl.BlockSpec((RET_HEADS, 1, dk), fixed3),
                  pl.BlockSpec((SCONV_W, dconv), fixed2),
                  pl.BlockSpec((1, dconv), fixed2),
                  pl.BlockSpec((1, RET_HEADS * dk), fixed2)],
        out_specs=[pl.BlockSpec((rows, 2 * dconv), lambda s, c: (s * n_chunks + c, 0)),
                   pl.BlockSpec((1, 2, dconv), lambda s, c: (s, 0, 0)),
                   pl.BlockSpec((1, RET_HEADS, dk, dk), lambda s, c: (s, 0, 0, 0))],
        out_shape=[jax.ShapeDtypeStruct((z.shape[0], 2 * dconv), F32),
                   jax.ShapeDtypeStruct((n_seq, 2, dconv), F32),
                   jax.ShapeDtypeStruct((n_seq, RET_HEADS, dk, dk), F32)],
        scratch_shapes=[pltpu.VMEM((SUBLANES, dconv), F32), pltpu.VMEM((RET_HEADS, dk, dk), F32)],
        compiler_params=_cparams(2),
        name="even_mixer",
    )(z, hist, st, cosf, sinf, decay, q_dec, k_dec, s_dec, conv_w, conv_b.reshape(1, dconv),
      gn_g.reshape(1, RET_HEADS * dk))


def _conv_gate(a, gate, m1, m2, cw_ref, cb_ref):
    conv = ((cb_ref[...] + m2 * cw_ref[0:1, :]) + m1 * cw_ref[1:2, :]) + a * cw_ref[2:3, :]
    return _gelu(conv) * gate


def _ffn_up_seq_kernel(x_ref, wa_ref, wg_ref, h_ref, cw_ref, cb_ref, o_ref, hist_out_ref, carry, *, tm):
    @pl.when(pl.program_id(2) == 0)
    def _():
        carry[SUBLANES - 2:SUBLANES, :] = h_ref[0]

    xb = x_ref[...].astype(BF16)
    a = _dot(xb, wa_ref[...])
    gate = _dot(xb, wg_ref[...])
    row = lax.broadcasted_iota(jnp.int32, a.shape, 0)
    h0 = carry[SUBLANES - 2:SUBLANES - 1, :]
    h1 = carry[SUBLANES - 1:SUBLANES, :]
    m1 = jnp.where(row == 0, h1, pltpu.roll(a, 1, 0))
    m2 = jnp.where(row == 0, h0, jnp.where(row == 1, h1, pltpu.roll(a, 2, 0)))
    o_ref[...] = _conv_gate(a, gate, m1, m2, cw_ref, cb_ref).astype(o_ref.dtype)
    carry[...] = a[tm - SUBLANES:tm, :]
    hist_out_ref[0] = carry[SUBLANES - 2:SUBLANES, :]


def ffn_up_sequences(x, n_seq, hist, w_up_bf16, conv_w, conv_b, tm=MATMUL_ROWS, n_col=2):
    m, k = x.shape
    dff = conv_w.shape[1]
    seq = m // n_seq
    tm = _row_tile(seq, tm)
    tps = seq // tm
    tn = dff // n_col
    assert tn % LANES == 0
    return pl.pallas_call(
        functools.partial(_ffn_up_seq_kernel, tm=tm),
        grid=(n_col, n_seq, tps),
        in_specs=[pl.BlockSpec((tm, k), lambda j, s, i: (s * tps + i, 0)),
                  pl.BlockSpec((k, tn), lambda j, s, i: (0, j)),
                  pl.BlockSpec((k, tn), lambda j, s, i: (0, j + n_col)),
                  pl.BlockSpec((1, 2, tn), lambda j, s, i: (s, 0, j)),
                  pl.BlockSpec((FFN_W, tn), lambda j, s, i: (0, j)),
                  pl.BlockSpec((1, tn), lambda j, s, i: (0, j))],
        out_specs=[pl.BlockSpec((tm, tn), lambda j, s, i: (s * tps + i, j)),
                   pl.BlockSpec((1, 2, tn), lambda j, s, i: (s, 0, j))],
        out_shape=[jax.ShapeDtypeStruct((m, dff), BF16),
                   jax.ShapeDtypeStruct((n_seq, 2, dff), F32)],
        scratch_shapes=[pltpu.VMEM((SUBLANES, tn), F32)],
        compiler_params=_cparams(3),
        name="ffn_up_sequences",
    )(x, w_up_bf16, w_up_bf16, hist, conv_w, conv_b.reshape(1, dff))


def _ffn_up_short_kernel(x_ref, wa_ref, wg_ref, h1_ref, h2_ref, cw_ref, cb_ref, o_ref, a_ref):
    xb = x_ref[...].astype(BF16)
    a = _dot(xb, wa_ref[...])
    gate = _dot(xb, wg_ref[...])
    t = lax.broadcasted_iota(jnp.int32, a.shape, 0) % SUBLANES
    m1 = jnp.where(t == 0, h1_ref[...], pltpu.roll(a, 1, 0))
    m2 = jnp.where(t < 2, h2_ref[...], pltpu.roll(a, 2, 0))
    o_ref[...] = _conv_gate(a, gate, m1, m2, cw_ref, cb_ref).astype(o_ref.dtype)
    a_ref[...] = a


def ffn_up_short(x, hist, w_up_bf16, conv_w, conv_b, n_col=2):
    m, k = x.shape
    dff = conv_w.shape[1]
    n_seq = m // SUBLANES
    tn = dff // n_col
    zeros = jnp.zeros((n_seq, SUBLANES, dff), F32)
    h1 = zeros.at[:, 0].set(hist[:, 1]).reshape(m, dff)
    h2 = zeros.at[:, 0].set(hist[:, 0]).at[:, 1].set(hist[:, 1]).reshape(m, dff)
    col = lambda j: (0, j)
    return pl.pallas_call(
        _ffn_up_short_kernel,
        grid=(n_col,),
        in_specs=[pl.BlockSpec((m, k), lambda j: (0, 0)),
                  pl.BlockSpec((k, tn), col),
                  pl.BlockSpec((k, tn), lambda j: (0, j + n_col)),
                  pl.BlockSpec((m, tn), col), pl.BlockSpec((m, tn), col),
                  pl.BlockSpec((FFN_W, tn), col), pl.BlockSpec((1, tn), col)],
        out_specs=[pl.BlockSpec((m, tn), col), pl.BlockSpec((m, tn), col)],
        out_shape=[jax.ShapeDtypeStruct((m, dff), F32), jax.ShapeDtypeStruct((m, dff), F32)],
        compiler_params=_cparams(1),
        name="ffn_up_short",
    )(x, w_up_bf16, w_up_bf16, h1, h2, conv_w, conv_b.reshape(1, dff))


def _compress_kernel(pt_ref, *refs, pages):
    page_refs = refs[:pages + 1]
    w2t_ref, pecol_ref, w1_ref, w2_ref, o_ref = refs[pages + 1:pages + 6]
    rows_refs = refs[pages + 6:]
    parts = len(rows_refs)
    ppp = (pages + 1) // parts
    cpp = PAGE_SIZE // CMP_STRIDE
    n = (pages + 1) * cpp
    n_p = ppp * cpp
    hidden = w1_ref.shape[1]
    gpr = LANES // NSA_DH
    pieces = NSA_KV // gpr
    for i, r in enumerate(page_refs):
        for pc in range(pieces):
            tile = r[0, pc * gpr:(pc + 1) * gpr].reshape(LANES, PAGE_SIZE)
            rows_refs[i // ppp][pc, (i % ppp) * PAGE_SIZE:(i % ppp + 1) * PAGE_SIZE, :] = tile.T
    pe_term = jnp.sum(pecol_ref[...] * w1_ref[...], axis=0, keepdims=True)
    accs = []
    for rows_ref in rows_refs:
        acc = jnp.zeros((pieces * n_p, gpr * 2 * hidden), F32)
        for tp in range(CMP_STRIDE // 2):
            lhs = jnp.concatenate(
                [jnp.concatenate([rows_ref[pc, pl.ds(2 * tp, n_p, stride=CMP_STRIDE), :],
                                  rows_ref[pc, pl.ds(2 * tp + 1, n_p, stride=CMP_STRIDE), :]], axis=1)
                 for pc in range(pieces)], axis=0)
            acc = acc + _dot(lhs.astype(BF16), w2t_ref[tp])
        accs.append(acc)
    for pc in range(pieces):
        for gl in range(gpr):
            a = jnp.concatenate([acc[pc * n_p:(pc + 1) * n_p, gl * 2 * hidden:(gl + 1) * 2 * hidden]
                                 for acc in accs], axis=0)
            nxt = pltpu.roll(a, n - 1, 0)
            pre = pe_term + a[:, 0:hidden]
            pre = pre + nxt[:, hidden:2 * hidden]
            o_ref[0, pc * gpr + gl] = _dot(_gelu(pre[0:pages * cpp]).astype(BF16), w2_ref[...])


def compress(rows_t, page_table, pe, w1, w2, pages=COMPRESS_PAGES):
    pooled = page_table is not None
    if pooled:
        n_seq, n_pages = page_table.shape
    else:
        n_seq, n_pages = rows_t.shape[0], rows_t.shape[3] // PAGE_SIZE
        page_table = jnp.zeros((1, 1), jnp.int32)
    pages = min(pages, n_pages)
    assert n_pages % pages == 0
    parts = 3 if (pages + 1) % 3 == 0 else 1
    hidden = w1.shape[1]
    cpp = PAGE_SIZE // CMP_STRIDE
    r = CMP_LEN // CMP_STRIDE
    gpr = LANES // NSA_DH
    assert r == 2 and gpr == 2
    w1p = w1.reshape(r, CMP_STRIDE, NSA_DH, hidden)
    w16 = jnp.concatenate([w1p[0], w1p[1]], axis=-1)
    zero = jnp.zeros_like(w16)
    per_tok = jnp.concatenate([jnp.concatenate([w16, zero], axis=-1),
                               jnp.concatenate([zero, w16], axis=-1)], axis=1)
    w2t = per_tok.reshape(CMP_STRIDE // 2, 2 * LANES, gpr * 2 * hidden).astype(BF16)
    w2p = jnp.pad(w2, ((0, 0), (0, SLOT - NSA_DH))).astype(BF16)
    pecol = pe.reshape(CMP_LEN * NSA_DH, 1)

    def page_map(i):
        if pooled:
            return lambda s, j, pt: (pt[s, jnp.minimum(j * pages + i, n_pages - 1)], 0, 0, 0)
        return lambda s, j, pt: (s, 0, 0, jnp.minimum(j * pages + i, n_pages - 1))

    fixed2 = lambda s, j, pt: (0, 0)
    grid_spec = pltpu.PrefetchScalarGridSpec(
        num_scalar_prefetch=1,
        grid=(n_seq, n_pages // pages),
        in_specs=[pl.BlockSpec((1, NSA_KV, NSA_DH, PAGE_SIZE), page_map(i)) for i in range(pages + 1)] + [
            pl.BlockSpec((CMP_STRIDE // 2, 2 * LANES, gpr * 2 * hidden), lambda s, j, pt: (0, 0, 0)),
            pl.BlockSpec((CMP_LEN * NSA_DH, 1), fixed2),
            pl.BlockSpec((CMP_LEN * NSA_DH, hidden), fixed2),
            pl.BlockSpec((hidden, SLOT), fixed2)],
        out_specs=pl.BlockSpec((1, NSA_KV, pages * cpp, SLOT), lambda s, j, pt: (s, 0, j, 0)),
        scratch_shapes=[pltpu.VMEM((NSA_KV // gpr, (pages + 1) // parts * PAGE_SIZE, LANES), F32)
                        for _ in range(parts)],
    )
    return pl.pallas_call(
        functools.partial(_compress_kernel, pages=pages),
        grid_spec=grid_spec,
        out_shape=jax.ShapeDtypeStruct((n_seq, NSA_KV, n_pages * cpp, SLOT), F32),
        compiler_params=_cparams(2),
        name="compress",
    )(page_table, *([rows_t] * (pages + 1)), w2t, pecol, w1, w2p)


def _cmp_select_kernel(q_ref, kc_ref, vc_ref, ov_ref, o_ref, idx_ref, *, tq, n_cmp, n_slc, pos0):
    ncp = kc_ref.shape[2]
    nsp = ov_ref.shape[1]
    hrows = NSA_GROUP * tq
    q_pos = pos0 + (lax.broadcasted_iota(jnp.int32, (hrows, ncp), 0) & (tq - 1))
    blk_i = lax.broadcasted_iota(jnp.int32, (hrows, ncp), 1)
    valid = (blk_i * CMP_STRIDE + (CMP_LEN - 1) <= q_pos) & (blk_i < n_cmp)
    ov = ov_ref[...]
    imps = []
    for g in range(NSA_KV):
        kc = kc_ref[0, g].astype(BF16)
        vc = vc_ref[0, g].astype(BF16)
        heads = range(g * NSA_GROUP, (g + 1) * NSA_GROUP)
        qg = jnp.concatenate([q_ref[:, h * SLOT:(h + 1) * SLOT] for h in heads], axis=0).astype(BF16)
        s = jnp.where(valid, _nt_dot(qg, kc), NEG_INF)
        m = jnp.max(s, axis=-1, keepdims=True)
        e = jnp.where(valid, jnp.exp(s - m), 0.0)
        den = jnp.sum(e, axis=-1, keepdims=True)
        p = e * (1.0 / jnp.where(den > 0.0, den, 1.0))
        o = _dot(p.astype(BF16), vc)
        p_sum = jnp.zeros((tq, ncp), F32)
        for j, h in enumerate(heads):
            o_ref[:, h * SLOT:(h + 1) * SLOT] = o[j * tq:(j + 1) * tq]
            p_sum = p_sum + p[j * tq:(j + 1) * tq]
        hi = p_sum.astype(BF16)
        lo = (p_sum - hi.astype(F32)).astype(BF16)
        imps.append(_dot(hi, ov) + _dot(lo, ov))
    rows = NSA_KV * tq
    imp = jnp.concatenate(imps, axis=0)
    blk = lax.broadcasted_iota(jnp.int32, (rows, nsp), 1)
    cur = (pos0 + (lax.broadcasted_iota(jnp.int32, (rows, nsp), 0) & (tq - 1))) // SLC_BLOCK
    real = blk < n_slc
    causal = real & (blk <= cur)
    forced = (blk == 0) | (blk == cur) | (blk == cur - 1)
    score = jnp.where(causal, imp + jnp.where(forced, FORCE_BONUS, 0.0), NEG_INF)
    score = jnp.where(real, score, REMOVED)
    idx = jnp.zeros((rows, LANES), jnp.int32)
    idx_lane = lax.broadcasted_iota(jnp.int32, (rows, LANES), 1)
    for it in range(min(SLC_TOPN, n_slc)):
        m = jnp.max(score, axis=-1, keepdims=True)
        first = jnp.min(jnp.where(score == m, blk, nsp), axis=-1, keepdims=True)
        score = jnp.where(blk == first, REMOVED, score)
        idx = jnp.where(idx_lane == it, first, idx)
    idx_ref[0] = idx.reshape(NSA_KV, tq, LANES)


def cmp_block_overlap(n_cmp_pad, n_cmp, n_slc, n_slc_pad, lane_off):
    i = np.arange(n_cmp_pad)[:, None]
    j = np.arange(n_slc_pad)[None, :] - lane_off
    start = i * CMP_STRIDE
    hit = (start < (j + 1) * SLC_BLOCK) & (start + CMP_LEN > j * SLC_BLOCK) & (i < n_cmp) & (j >= 0) & (j < n_slc)
    return jnp.asarray(hit.astype(np.float32), dtype=BF16)


def cmp_attention_select(q_slots, kcc, vcc, n_seq, n_cmp, n_slc, pos0):
    tokens = q_slots.shape[0]
    tq = tokens // n_seq
    assert tq & (tq - 1) == 0
    ncp = kcc.shape[2]
    nsp = -(-n_slc // LANES) * LANES
    ov = cmp_block_overlap(ncp, n_cmp, n_slc, nsp, 0)
    hw = NSA_HEADS * SLOT
    return pl.pallas_call(
        functools.partial(_cmp_select_kernel, tq=tq, n_cmp=n_cmp, n_slc=n_slc, pos0=pos0),
        grid=(n_seq,),
        in_specs=[pl.BlockSpec((tq, hw), lambda s: (s, 0)),
                  pl.BlockSpec((1, NSA_KV, ncp, SLOT), lambda s: (s, 0, 0, 0)),
                  pl.BlockSpec((1, NSA_KV, ncp, SLOT), lambda s: (s, 0, 0, 0)),
                  pl.BlockSpec((ncp, nsp), lambda s: (0, 0))],
        out_specs=[pl.BlockSpec((tq, hw), lambda s: (s, 0)),
                   pl.BlockSpec((1, NSA_KV, tq, LANES), lambda s: (s, 0, 0, 0))],
        out_shape=[jax.ShapeDtypeStruct((tokens, hw), F32),
                   jax.ShapeDtypeStruct((n_seq, NSA_KV, tq, LANES), jnp.int32)],
        compiler_params=_cparams(1),
        name="cmp_attention_select",
    )(q_slots, kcc, vcc, ov)


def _cmp_select_prompt_kernel(q_ref, kc_ref, vc_ref, ovt_ref, o_ref, sel_ref, *, tq, n_cmp, n_slc):
    t0 = pl.program_id(2) * tq
    ncp = kc_ref.shape[2]
    nsr = ovt_ref.shape[0]
    kc = kc_ref[0, 0].astype(BF16)
    vc = vc_ref[0, 0].astype(BF16)
    q_pos = t0 + lax.broadcasted_iota(jnp.int32, (ncp, tq), 1)
    blk_i = lax.broadcasted_iota(jnp.int32, (ncp, tq), 0)
    valid = (blk_i * CMP_STRIDE + (CMP_LEN - 1) <= q_pos) & (blk_i < n_cmp)
    p_sum = jnp.zeros((ncp, tq), F32)
    for j in range(NSA_GROUP):
        s = jnp.where(valid, _nt_dot(kc, q_ref[:, j * SLOT:(j + 1) * SLOT]), NEG_INF)
        m = jnp.max(s, axis=0, keepdims=True)
        e = jnp.where(valid, jnp.exp(s - m), 0.0)
        den = jnp.sum(e, axis=0, keepdims=True)
        p = e * (1.0 / jnp.where(den > 0.0, den, 1.0))
        o_ref[:, j * SLOT:(j + 1) * SLOT] = _tn_dot(p.astype(BF16), vc)
        p_sum = p_sum + p
    hi = p_sum.astype(BF16)
    lo = (p_sum - hi.astype(F32)).astype(BF16)
    ovt = ovt_ref[...]
    imp = _dot(ovt, hi) + _dot(ovt, lo)
    blk = lax.broadcasted_iota(jnp.int32, (nsr, tq), 0)
    cur = (t0 + lax.broadcasted_iota(jnp.int32, (nsr, tq), 1)) // SLC_BLOCK
    real = blk < n_slc
    causal = real & (blk <= cur)
    forced = (blk == 0) | (blk == cur) | (blk == cur - 1)
    score = jnp.where(causal, imp + jnp.where(forced, FORCE_BONUS, 0.0), NEG_INF)
    score = jnp.where(real, score, REMOVED)
    picked = jnp.zeros((nsr, tq), jnp.bool_)
    for _ in range(min(SLC_TOPN, n_slc)):
        m = jnp.max(score, axis=0, keepdims=True)
        first = jnp.min(jnp.where(score == m, blk, nsr), axis=0, keepdims=True)
        hit = blk == first
        picked = picked | hit
        score = jnp.where(hit, REMOVED, score)
    bias_t = jnp.where(real & ~(picked & causal), NEG_INF, 0.0)
    slot_t = jnp.concatenate([jnp.zeros((NSA_DH, tq), F32), bias_t], axis=0)
    sel_ref[0, 0] = slot_t.T


def cmp_attention_select_prompt(q_slots, kcc, vcc, n_seq, n_cmp, n_slc, tq):
    tokens = q_slots.shape[0]
    t = tokens // n_seq
    tq = _row_tile(t, tq)
    nt = t // tq
    ncp = kcc.shape[2]
    nsr = SLOT - NSA_DH
    assert n_slc <= nsr
    ovt = cmp_block_overlap(ncp, n_cmp, n_slc, nsr, 0).T
    gw = NSA_GROUP * SLOT
    return pl.pallas_call(
        functools.partial(_cmp_select_prompt_kernel, tq=tq, n_cmp=n_cmp, n_slc=n_slc),
        grid=(n_seq, NSA_KV, nt),
        in_specs=[pl.BlockSpec((tq, gw), lambda s, g, i: (s * nt + i, g)),
                  pl.BlockSpec((1, 1, ncp, SLOT), lambda s, g, i: (s, g, 0, 0)),
                  pl.BlockSpec((1, 1, ncp, SLOT), lambda s, g, i: (s, g, 0, 0)),
                  pl.BlockSpec((nsr, ncp), lambda s, g, i: (0, 0))],
        out_specs=[pl.BlockSpec((tq, gw), lambda s, g, i: (s * nt + i, g)),
                   pl.BlockSpec((1, 1, tq, SLOT), lambda s, g, i: (s, g, i, 0))],
        out_shape=[jax.ShapeDtypeStruct((tokens, NSA_HEADS * SLOT), F32),
                   jax.ShapeDtypeStruct((n_seq, NSA_KV, t, SLOT), F32)],
        compiler_params=_cparams(3),
        name="cmp_attention_select_prompt",
    )(q_slots, kcc, vcc, ovt)


def _prompt_slc_win_kernel(q_ref, sel_ref, ks_ref, vs_ref, kw_ref, vw_ref, oh_ref, os_ref, ow_ref,
                           m_ref, acc_ref, *, tq, seq):
    acc_rows = NSA_DH + 16
    qi = pl.program_id(2)
    t0 = qi * tq
    rows = NSA_GROUP * tq
    sel = sel_ref[0, 0]
    q_plain = jnp.concatenate([q_ref[:, j * SLOT:(j + 1) * SLOT] for j in range(NSA_GROUP)], axis=0)
    q_aug = jnp.concatenate([(q_ref[:, j * SLOT:(j + 1) * SLOT].astype(F32) + sel).astype(BF16)
                             for j in range(NSA_GROUP)], axis=0)
    zeros_k = jnp.zeros((SLOT - NSA_DH, tq), F32)
    ones_row = (lax.broadcasted_iota(jnp.int32, (acc_rows - NSA_DH, tq), 0) == 0).astype(BF16)
    rel = lax.broadcasted_iota(jnp.int32, (tq, LANES), 1) - lax.broadcasted_iota(jnp.int32, (tq, LANES), 0)

    def scores(q_rows, k_top, k_bottom):
        k_rows = jnp.concatenate([k_top, k_bottom], axis=0).T.astype(BF16)
        return [_nt_dot(k_rows, q_rows[c:c + 2 * LANES]) for c in range(0, rows, 2 * LANES)]

    def update(s_t, v_top, start, mask):
        v_t = jnp.concatenate([v_top.astype(BF16), ones_row], axis=0)
        for cg in range(rows // (2 * LANES)):
            p_parts, a_parts = [], []
            for h in range(2):
                c0 = (2 * cg + h) * LANES
                x = s_t[cg][:, h * LANES:(h + 1) * LANES]
                if mask is not None:
                    lo, hi = mask
                    off = t0 - start + (c0 & (tq - 1))
                    keep = rel >= lo - off
                    if hi is not None:
                        keep = keep & (rel < hi - off)
                    x = jnp.where(keep, x, NEG_INF)
                m_old = m_ref[:, c0:c0 + LANES]
                m_new = jnp.maximum(m_old, jnp.max(x, axis=0, keepdims=True))
                m_ref[:, c0:c0 + LANES] = m_new
                a_parts.append(jnp.exp(m_old - m_new))
                p_parts.append(jnp.exp(x - m_new).astype(BF16))
            c0 = 2 * cg * LANES
            pv = _dot(v_t, jnp.concatenate(p_parts, axis=1))
            acc_ref[:, c0:c0 + 2 * LANES] = jnp.concatenate(a_parts, axis=1) * acc_ref[:, c0:c0 + 2 * LANES] + pv

    def reset():
        m_ref[...] = jnp.full(m_ref.shape, NEG_INF, F32)
        acc_ref[...] = jnp.zeros(acc_ref.shape, F32)

    slot_pad = jnp.zeros((SLOT - NSA_DH, tq), F32)

    def finish(o_ref):
        for j in range(NSA_GROUP):
            a = acc_ref[:, j * tq:(j + 1) * tq]
            o_t = jnp.concatenate([a[0:NSA_DH] * (1.0 / a[NSA_DH:NSA_DH + 1, :]), slot_pad], axis=0)
            o_ref[:, j * SLOT:(j + 1) * SLOT] = o_t.T

    def slc_scores(start):
        return scores(q_aug, ks_ref[0, :, pl.ds(start, tq)], oh_ref[:, pl.ds(start, tq)])

    def slc_update(s_t, start, mask):
        update(s_t, vs_ref[0, :, pl.ds(start, tq)], start, mask)

    k0 = jnp.clip(t0 - WINDOW, 0, seq - WINDOW - tq)
    n_win = WINDOW // tq + 1
    win_start = [pl.multiple_of(k0 + i * tq, tq) for i in range(n_win)]
    win_mask = (0, WINDOW)

    def win_scores(i):
        return scores(q_plain, kw_ref[0, :, pl.ds(win_start[i], tq)], zeros_k)

    reset()

    def pair(k2, carry):
        start_a = pl.multiple_of(2 * k2 * tq, tq)
        start_b = pl.multiple_of(start_a + tq, tq)
        s_a = slc_scores(start_a)
        s_b = slc_scores(start_b)
        slc_update(s_a, start_a, None)
        slc_update(s_b, start_b, None)
        return carry

    lax.fori_loop(0, qi // 2, pair, 0)

    @pl.when(qi % 2 == 1)
    def _():
        start = pl.multiple_of((qi - 1) * tq, tq)
        slc_update(slc_scores(start), start, None)

    diag = pl.multiple_of(t0, tq)
    s_cur = slc_scores(diag)
    s_next = win_scores(0)
    slc_update(s_cur, diag, (0, None))
    finish(os_ref)
    reset()
    for i in range(n_win):
        s_cur = s_next
        if i + 1 < n_win:
            s_next = win_scores(i + 1)
        update(s_cur, vw_ref[0, :, pl.ds(win_start[i], tq)], win_start[i], win_mask)
    finish(ow_ref)


def prompt_slc_win_attention(q_slots, sel, ks_t, vs_t, kw_t, vw_t, n_seq, tq):
    tokens = q_slots.shape[0]
    seq = tokens // n_seq
    tq = _row_tile(seq, tq)
    assert tq & (tq - 1) == 0 and WINDOW % tq == 0 and seq >= WINDOW + tq
    nt = seq // tq
    gw = NSA_GROUP * SLOT
    assert tq % (2 * LANES) == 0
    onehot_t = jax.nn.one_hot(jnp.arange(seq) // SLC_BLOCK, SLOT - NSA_DH, dtype=F32).T
    kv_spec = pl.BlockSpec((1, NSA_DH, seq), lambda s, g, i: (s, g, 0))
    return pl.pallas_call(
        functools.partial(_prompt_slc_win_kernel, tq=tq, seq=seq),
        grid=(n_seq, NSA_KV, nt),
        in_specs=[pl.BlockSpec((tq, gw), lambda s, g, i: (s * nt + i, g)),
                  pl.BlockSpec((1, 1, tq, SLOT), lambda s, g, i: (s, g, i, 0)),
                  kv_spec, kv_spec, kv_spec, kv_spec,
                  pl.BlockSpec((SLOT - NSA_DH, seq), lambda s, g, i: (0, 0))],
        out_specs=[pl.BlockSpec((tq, gw), lambda s, g, i: (s * nt + i, g)),
                   pl.BlockSpec((tq, gw), lambda s, g, i: (s * nt + i, g))],
        out_shape=[jax.ShapeDtypeStruct((tokens, NSA_HEADS * SLOT), F32),
                   jax.ShapeDtypeStruct((tokens, NSA_HEADS * SLOT), F32)],
        scratch_shapes=[pltpu.VMEM((1, NSA_GROUP * tq), F32),
                        pltpu.VMEM((NSA_DH + 16, NSA_GROUP * tq), F32)],
        compiler_params=_cparams(3),
        name="prompt_slc_win_attention",
    )(q_slots, sel, ks_t, vs_t, kw_t, vw_t, onehot_t)


def _sample_slc_kernel(idx_ref, pt_ref, q_ref, kn_ref, vn_ref, *refs, topn, past, t_pad):
    k_refs = refs[:topn]
    v_refs = refs[topn:2 * topn]
    o_ref = refs[2 * topn]
    s_id, g_id, t_id = pl.program_id(0), pl.program_id(1), pl.program_id(2)
    base = ((s_id * NSA_KV + g_id) * t_pad + t_id) * topn
    q_pos = past + t_id
    cur = q_pos // SLC_BLOCK
    first_new = past // SLC_BLOCK
    q = q_ref[0, 0, 0].astype(BF16)
    kb = jnp.concatenate([r[0, 0] for r in k_refs], axis=1).astype(BF16)
    vb = jnp.concatenate([r[0, 0] for r in v_refs], axis=1).astype(BF16)
    n_keys = topn * PAGE_SIZE
    lane = lax.broadcasted_iota(jnp.int32, (1, n_keys), 1)
    slot = lane // PAGE_SIZE
    in_page = lane % PAGE_SIZE
    k_pos = in_page
    limit = jnp.zeros((1, n_keys), jnp.int32)
    n_new = jnp.int32(0)
    for kk in range(topn):
        b = idx_ref[base + kk]
        here = slot == kk
        k_pos = jnp.where(here, (b // 2) * PAGE_SIZE + in_page, k_pos)
        last = jnp.where(b <= cur, jnp.minimum(q_pos, past - 1), -1)
        limit = jnp.where(here, jnp.where(in_page // SLC_BLOCK == b % 2, last, -1), limit)
        n_new = n_new + jnp.where(b == first_new, 1, 0)
    valid = k_pos <= limit
    s_old = jnp.where(valid, _dot(q, kb), NEG_INF)
    new_lane = lax.broadcasted_iota(jnp.int32, (1, SUBLANES), 1)
    valid_new = past + new_lane <= jnp.where(n_new > 0, q_pos, past - 1)
    s_new = jnp.where(valid_new, _dot(q, kn_ref[0, 0].astype(BF16)), NEG_INF)
    m = jnp.maximum(jnp.max(s_old, axis=-1, keepdims=True), jnp.max(s_new, axis=-1, keepdims=True))
    p_old = jnp.exp(s_old - m)
    p_new = jnp.exp(s_new - m)
    l = jnp.sum(p_old, axis=-1, keepdims=True) + jnp.sum(p_new, axis=-1, keepdims=True)
    o = _nt_dot(p_old.astype(BF16), vb) + _nt_dot(p_new.astype(BF16), vn_ref[0, 0].astype(BF16))
    o_ref[0, 0, 0] = o / l


def sample_slc_attention(q_rows, idx, page_table, pool_k, pool_v, k_new_t, v_new_t, t_real, past):
    n_seq, _, t_pad, _, dh = q_rows.shape
    topn = idx.shape[-1]
    assert past % SLC_BLOCK == 0 and t_real <= SUBLANES and PAGE_SIZE == 2 * SLC_BLOCK
    last_old = past // SLC_BLOCK - 1
    logical = jnp.clip(idx, 0, last_old) // 2
    n_pages = page_table.shape[1]
    hit = logical[..., None] == jnp.arange(n_pages, dtype=jnp.int32)
    phys = jnp.sum(jnp.where(hit, page_table[:, None, None, None, :], 0), axis=-1)

    def blk_map(kk):
        def f(s, g, t, idx_ref, pg_ref):
            return (pg_ref[((s * NSA_KV + g) * t_pad + t) * topn + kk], g, 0, 0)
        return f

    blk_specs = [pl.BlockSpec((1, 1, dh, PAGE_SIZE), blk_map(kk)) for kk in range(topn)]
    new_spec = pl.BlockSpec((1, 1, dh, SUBLANES), lambda s, g, t, i_r, p_r: (s, g, 0, 0))
    q_spec = pl.BlockSpec((1, 1, 1, SUBLANES, dh), lambda s, g, t, i_r, p_r: (s, g, t, 0, 0))
    grid_spec = pltpu.PrefetchScalarGridSpec(
        num_scalar_prefetch=2,
        grid=(n_seq, NSA_KV, t_real),
        in_specs=[q_spec, new_spec, new_spec] + blk_specs + blk_specs,
        out_specs=q_spec,
    )
    return pl.pallas_call(
        functools.partial(_sample_slc_kernel, topn=topn, past=past, t_pad=t_pad),
        grid_spec=grid_spec,
        out_shape=jax.ShapeDtypeStruct((n_seq, NSA_KV, t_real, SUBLANES, dh), F32),
        compiler_params=_cparams(3),
        name="sample_slc_attention",
    )(idx.reshape(-1), phys.reshape(-1), q_rows, k_new_t, v_new_t, *([pool_k] * topn), *([pool_v] * topn))


def _sample_win_kernel(q_ref, wk_ref, wv_ref, kn_ref, vn_ref, o_ref, *, past, t_pad):
    rows = t_pad * SUBLANES
    wb = wk_ref.shape[-1]
    q = q_ref[0, 0].reshape(rows, q_ref.shape[-1]).astype(BF16)
    q_pos = past + lax.broadcasted_iota(jnp.int32, (rows, 1), 0) // SUBLANES
    k_pos = past - wb + lax.broadcasted_iota(jnp.int32, (1, wb), 1)
    dist = q_pos - k_pos
    valid = (dist >= 0) & (dist < WINDOW) & (k_pos >= 0)
    s_old = jnp.where(valid, _dot(q, wk_ref[0, 0].astype(BF16)), NEG_INF)
    n_pos = past + lax.broadcasted_iota(jnp.int32, (1, SUBLANES), 1)
    dist_n = q_pos - n_pos
    valid_n = (dist_n >= 0) & (dist_n < WINDOW)
    s_new = jnp.where(valid_n, _dot(q, kn_ref[0, 0].astype(BF16)), NEG_INF)
    m = jnp.maximum(jnp.max(s_old, axis=-1, keepdims=True), jnp.max(s_new, axis=-1, keepdims=True))
    p_old = jnp.exp(s_old - m)
    p_new = jnp.exp(s_new - m)
    l = jnp.sum(p_old, axis=-1, keepdims=True) + jnp.sum(p_new, axis=-1, keepdims=True)
    o = _nt_dot(p_old.astype(BF16), wv_ref[0, 0].astype(BF16)) + _nt_dot(p_new.astype(BF16),
                                                                      vn_ref[0, 0].astype(BF16))
    o_ref[0, 0] = (o / l).reshape(t_pad, SUBLANES, o.shape[-1])


def sample_win_attention(q_rows, win_k_t, win_v_t, k_new_t, v_new_t, past):
    n_seq, _, t_pad, _, dh = q_rows.shape
    wb = win_k_t.shape[-1]
    q_spec = pl.BlockSpec((1, 1, t_pad, SUBLANES, dh), lambda s, g: (s, g, 0, 0, 0))
    win_spec = pl.BlockSpec((1, 1, dh, wb), lambda s, g: (s, g, 0, 0))
    new_spec = pl.BlockSpec((1, 1, dh, SUBLANES), lambda s, g: (s, g, 0, 0))
    return pl.pallas_call(
        functools.partial(_sample_win_kernel, past=past, t_pad=t_pad),
        grid=(n_seq, NSA_KV),
        in_specs=[q_spec, win_spec, win_spec, new_spec, new_spec],
        out_specs=q_spec,
        out_shape=jax.ShapeDtypeStruct(q_rows.shape, F32),
        compiler_params=_cparams(2),
        name="sample_win_attention",
    )(q_rows, win_k_t, win_v_t, k_new_t, v_new_t)


def _to_slots(a):
    lead = a.shape[:-1]
    n = a.shape[-1] // NSA_DH
    a = a.reshape(*lead, n, NSA_DH)
    a = jnp.pad(a, [(0, 0)] * (a.ndim - 1) + [(0, SLOT - NSA_DH)])
    return a.reshape(*lead, n * SLOT)


def _odd_weights(w_in, w_out):
    d = w_in.shape[0]
    hq = NSA_HEADS * NSA_DH
    kvw = NSA_KV * NSA_DH
    wq = _to_slots(w_in[:, :hq] * np.float32(NSA_DH ** -0.5))
    wg = jnp.pad(w_in[:, hq + 6 * kvw:], ((0, 0), (0, LANES - 3 * NSA_HEADS)))
    w_q = jnp.concatenate([wq, wg], axis=1).astype(BF16)
    w_kvt = w_in[:, hq:hq + 6 * kvw].T.astype(BF16)
    wo = jnp.pad(w_out.reshape(NSA_HEADS, NSA_DH, d), ((0, 0), (0, SLOT - NSA_DH), (0, 0)))
    wo = wo.reshape(NSA_HEADS * SLOT, d).astype(BF16)
    k = NSA_HEADS * SLOT
    e = np.zeros((LANES, 3 * k), np.float32)
    for c in range(3):
        for h in range(NSA_HEADS):
            e[c * NSA_HEADS + h, c * k + h * SLOT:c * k + (h + 1) * SLOT] = 1.0
    return w_q, w_kvt, wo, jnp.asarray(e, dtype=BF16)


def _group_rows(q_slots, n_seq, t_pad):
    q = q_slots.reshape(n_seq, t_pad, NSA_KV, NSA_GROUP, SLOT)[..., :NSA_DH]
    q = q.transpose(0, 2, 1, 3, 4)
    return jnp.pad(q, ((0, 0), (0, 0), (0, 0), (0, SUBLANES - NSA_GROUP), (0, 0)))


def _ungroup_rows(o, n_seq, t_pad):
    t = o.shape[2]
    o = o[:, :, :, :NSA_GROUP].transpose(0, 2, 1, 3, 4)
    o = jnp.pad(o, ((0, 0), (0, t_pad - t), (0, 0), (0, 0), (0, SLOT - NSA_DH)))
    return o.reshape(n_seq * t_pad, NSA_HEADS * SLOT)


def _feature_major(cache):
    return cache.transpose(0, 2, 3, 1)


def _token_major(a_t):
    return a_t.transpose(0, 3, 1, 2)


def _pad_rows(a, t_pad):
    return jnp.pad(a, ((0, 0), (0, t_pad - a.shape[1])) + ((0, 0),) * (a.ndim - 2))


def kernel(x_prompt, x_sample, state_sconv, state_ret, cache_cmp_k, cache_cmp_v, cache_slc_k, cache_slc_v,
           cache_win_k, cache_win_v, state_ffn_conv, page_table,
           w_in_even, sconv_w, sconv_b, ret_gn_g, w_out_even,
           w_in_odd, cmp_pe, cmp_w1, cmp_w2, w_out_odd,
           ln_mix_g, ln_mix_b, ffn_w_up, ffn_conv_w, ffn_conv_b, ffn_w_down, ln_ffn_g, ln_ffn_b):
    b_p, s_p, d_model = x_prompt.shape
    b_s, t_s, _ = x_sample.shape
    n_pages = page_table.shape[1]
    past = n_pages * PAGE_SIZE
    t_pad = SUBLANES
    assert t_s <= t_pad and t_s >= SCONV_W - 1 and t_s < CMP_STRIDE and past % PAGE_SIZE == 0
    assert s_p % RET_CHUNK == 0 and s_p % PAGE_SIZE == 0
    d_sconv = sconv_w.shape[-1]
    d_ff = ffn_conv_w.shape[-1]
    gd = NSA_KV * NSA_DH
    depth = ln_mix_g.shape[0]

    xp = x_prompt.reshape(b_p * s_p, d_model)
    xs = _pad_rows(x_sample, t_pad).reshape(b_s * t_pad, d_model)
    outs = {k: [] for k in ("sconv_p", "sconv_s", "ret_p", "ret_s", "cmp_k_p", "cmp_v_p", "slc_k_p", "slc_v_p",
                            "cmp_k_s", "cmp_v_s", "slc_k_s", "slc_v_s", "win_k_p", "win_v_p", "win_k_s",
                            "win_v_s", "ffn_p", "ffn_s")}

    for layer in range(depth):
        if layer % 2 == 0:
            e = layer // 2
            w_in = w_in_even[e].astype(BF16)
            w_out = w_out_even[e].astype(BF16)
            n_in = w_in.shape[1]
            (zp,) = matmul_split(xp, w_in, [n_in], [F32])
            yp, hc, st = even_mixer(zp, jnp.zeros((b_p, SCONV_W - 1, d_sconv), F32),
                                    jnp.zeros((b_p,) + state_ret.shape[2:], F32), jnp.arange(s_p),
                                    RET_CHUNK, RET_CHUNK, sconv_w[e], sconv_b[e], ret_gn_g[e])
            outs["sconv_p"].append(hc)
            outs["ret_p"].append(st)
            xp = matmul_residual_ln(yp, w_out, xp, ln_mix_g[layer], ln_mix_b[layer])
            (zs,) = matmul_split(xs, w_in, [n_in], [F32])
            ys, hc, st = even_mixer(zs, state_sconv[e], state_ret[e], past + jnp.arange(t_pad),
                                    t_pad, t_s, sconv_w[e], sconv_b[e], ret_gn_g[e])
            outs["sconv_s"].append(hc)
            outs["ret_s"].append(st)
            xs = matmul_residual_ln(ys, w_out, xs, ln_mix_g[layer], ln_mix_b[layer])
        else:
            o = layer // 2
            w_q, w_kvt, w_out, e_gate = _odd_weights(w_in_odd[o], w_out_odd[o])
            pe, w1, w2 = cmp_pe[o], cmp_w1[o], cmp_w2[o]
            qp, gp, kc, vc, ks, vs, kw, vw = nsa_projection(xp, b_p, w_q, w_kvt, BF16)
            as_cache = lambda a_t: _token_major(a_t.reshape(b_p, NSA_KV, NSA_DH, -1))
            keep = min(WINDOW, s_p)
            outs["cmp_k_p"].append(as_cache(kc))
            outs["cmp_v_p"].append(as_cache(vc))
            outs["slc_k_p"].append(as_cache(ks))
            outs["slc_v_p"].append(as_cache(vs))
            outs["win_k_p"].append(as_cache(kw[:, :, s_p - keep:]))
            outs["win_v_p"].append(as_cache(vw[:, :, s_p - keep:]))
            kcc = compress(kc.reshape(b_p, NSA_KV, NSA_DH, s_p), None, pe[0], w1[0], w2[0])
            vcc = compress(vc.reshape(b_p, NSA_KV, NSA_DH, s_p), None, pe[1], w1[1], w2[1])
            n_cmp = s_p // CMP_STRIDE - CMP_LEN // CMP_STRIDE + 1
            n_slc = s_p // SLC_BLOCK
            oc, sel = cmp_attention_select_prompt(qp, kcc, vcc, b_p, n_cmp, n_slc, SELECT_ROWS)
            osl, ow = prompt_slc_win_attention(qp, sel, ks, vs, kw, vw, b_p, ATTN_ROWS)
            xp = nsa_merge_residual_ln(oc, osl, ow, gp, e_gate, w_out, xp, ln_mix_g[layer], ln_mix_b[layer])
            qs, gs, *kv_s = nsa_projection(xs, 1, w_q, w_kvt, F32)
            kc, vc, ks, vs, kw, vw = [a.reshape(NSA_KV, NSA_DH, b_s, t_pad).transpose(2, 0, 1, 3) for a in kv_s]
            new_rows = lambda a_t: _token_major(a_t[..., :t_s])
            outs["cmp_k_s"].append(new_rows(kc))
            outs["cmp_v_s"].append(new_rows(vc))
            outs["slc_k_s"].append(new_rows(ks))
            outs["slc_v_s"].append(new_rows(vs))
            win_k = _feature_major(cache_win_k[o])
            win_v = _feature_major(cache_win_v[o])
            wb = win_k.shape[-1]
            keep = min(WINDOW, wb + t_s)
            outs["win_k_s"].append(_token_major(jnp.concatenate([win_k, kw[..., :t_s]], axis=-1)[..., -keep:]))
            outs["win_v_s"].append(_token_major(jnp.concatenate([win_v, vw[..., :t_s]], axis=-1)[..., -keep:]))
            kcc = compress(_feature_major(cache_cmp_k[o]), page_table, pe[0], w1[0], w2[0])
            vcc = compress(_feature_major(cache_cmp_v[o]), page_table, pe[1], w1[1], w2[1])
            n_cmp = (past + t_s) // CMP_STRIDE - CMP_LEN // CMP_STRIDE + 1
            n_slc = -(-(past + t_s) // SLC_BLOCK)
            oc, idx = cmp_attention_select(qs, kcc, vcc, b_s, n_cmp, n_slc, past)
            topn = min(SLC_TOPN, n_slc)
            q_rows = _group_rows(qs, b_s, t_pad)
            osl = sample_slc_attention(q_rows, idx[..., :topn], page_table, _feature_major(cache_slc_k[o]),
                                       _feature_major(cache_slc_v[o]), ks, vs, t_s, past)
            ow = sample_win_attention(q_rows, win_k, win_v, kw, vw, past)
            xs = nsa_merge_residual_ln(oc, _ungroup_rows(osl, b_s, t_pad), _ungroup_rows(ow, b_s, t_pad), gs,
                                       e_gate, w_out, xs, ln_mix_g[layer], ln_mix_b[layer])
        w_up = ffn_w_up[layer].astype(BF16)
        w_down = ffn_w_down[layer].astype(BF16)
        hp, hist_p = ffn_up_sequences(xp, b_p, jnp.zeros((b_p, FFN_W - 1, d_ff), F32), w_up,
                                      ffn_conv_w[layer], ffn_conv_b[layer])
        outs["ffn_p"].append(hist_p)
        xp = matmul_residual_ln(hp, w_down, xp, ln_ffn_g[layer], ln_ffn_b[layer])
        hs, a_s = ffn_up_short(xs, state_ffn_conv[layer], w_up, ffn_conv_w[layer], ffn_conv_b[layer])
        outs["ffn_s"].append(a_s.reshape(b_s, t_pad, d_ff)[:, t_s - (FFN_W - 1):t_s])
        xs = matmul_residual_ln(hs, w_down, xs, ln_ffn_g[layer], ln_ffn_b[layer])

    st = jnp.stack
    y_p = xp.reshape(b_p, s_p, d_model)
    y_s = xs.reshape(b_s, t_pad, d_model)[:, :t_s]
    order = ("sconv_p", "sconv_s", "ret_p", "ret_s", "cmp_k_p", "cmp_v_p", "slc_k_p", "slc_v_p",
             "cmp_k_s", "cmp_v_s", "slc_k_s", "slc_v_s", "win_k_p", "win_v_p", "win_k_s", "win_v_s",
             "ffn_p", "ffn_s")
    return (y_p, y_s) + tuple(st(outs[k]) for k in order)
```

```python
import functools

import numpy as np
import jax
import jax.numpy as jnp
from jax import lax
from jax.experimental import pallas as pl
from jax.experimental.pallas import tpu as pltpu

F32 = jnp.float32
BF16 = jnp.bfloat16

SUBLANES = 8
LANES = 128
VMEM_LIMIT_BYTES = 56 * 1024 * 1024
MATMUL_ROWS = 512
MERGE_ROWS = 256
ATTN_ROWS = 256
SELECT_ROWS = 1024
COMPRESS_PAGES = 32

DEPTH = 2
SCONV_W = 3
RET_HEADS = 4
RET_CHUNK = 128
ROPE_BASE = 10000.0
NSA_HEADS = 16
NSA_KV = 4
NSA_GROUP = NSA_HEADS // NSA_KV
NSA_DH = 64
CMP_LEN = 32
CMP_STRIDE = 16
SLC_BLOCK = 64
SLC_TOPN = 16
WINDOW = 512
PAGE_SIZE = 128
FFN_W = 3
ALPHA = (2.0 * DEPTH) ** 0.25
LN_EPS = 1e-5
NEG_INF = -1e30
REMOVED = -3e38
FORCE_BONUS = 1e4
SLOT = 2 * NSA_DH


def _cparams(n_grid):
    return pltpu.CompilerParams(dimension_semantics=("arbitrary",) * n_grid,
                                vmem_limit_bytes=VMEM_LIMIT_BYTES)


def _row_tile(m, want):
    t = min(m, want)
    assert m % t == 0, (m, t)
    return t


def _nt_dot(a, b):
    return lax.dot_general(a, b, (((1,), (1,)), ((), ())), preferred_element_type=F32)


def _tn_dot(a, b):
    return lax.dot_general(a, b, (((0,), (0,)), ((), ())), preferred_element_type=F32)


def _dot(a, b):
    return jnp.dot(a, b, preferred_element_type=F32)


def _gelu(x):
    return 0.5 * x * (1.0 + jnp.tanh(np.float32(np.sqrt(2.0 / np.pi)) * (x + 0.044715 * (x * x * x))))


def _layer_norm_rows(r, g, b):
    mu = jnp.mean(r, axis=-1, keepdims=True)
    d = r - mu
    var = jnp.mean(d * d, axis=-1, keepdims=True)
    return d * lax.rsqrt(var + LN_EPS) * g + b


def _mm_split_kernel(x_ref, w_ref, *o_refs, cuts):
    acc = _dot(x_ref[...].astype(BF16), w_ref[...])
    for o_ref, (lo, hi) in zip(o_refs, cuts):
        o_ref[...] = acc[:, lo:hi].astype(o_ref.dtype)


def matmul_split(x, w_bf16, widths, dtypes, tm=MATMUL_ROWS):
    m, k = x.shape
    n = w_bf16.shape[1]
    assert sum(widths) == n and all(wd % LANES == 0 for wd in widths)
    tm = _row_tile(m, tm)
    cuts, lo = [], 0
    for wd in widths:
        cuts.append((lo, lo + wd))
        lo += wd
    return pl.pallas_call(
        functools.partial(_mm_split_kernel, cuts=tuple(cuts)),
        grid=(m // tm,),
        in_specs=[pl.BlockSpec((tm, k), lambda i: (i, 0)),
                  pl.BlockSpec((k, n), lambda i: (0, 0))],
        out_specs=[pl.BlockSpec((tm, wd), lambda i: (i, 0)) for wd in widths],
        out_shape=[jax.ShapeDtypeStruct((m, wd), dt) for wd, dt in zip(widths, dtypes)],
        compiler_params=_cparams(1),
        name="matmul_split",
    )(x, w_bf16)


def _nsa_proj_kernel(x_ref, wq_ref, wkvt_ref, q_ref, g_ref, *kv_refs, nq):
    xb = x_ref[...].astype(BF16)
    acc = _dot(xb, wq_ref[...])
    q_ref[...] = acc[:, :nq].astype(q_ref.dtype)
    g_ref[...] = acc[:, nq:]
    acc_t = _nt_dot(wkvt_ref[...], xb)
    gd = acc_t.shape[0] // len(kv_refs)
    for i, r in enumerate(kv_refs):
        r[0] = acc_t[i * gd:(i + 1) * gd, :]


def nsa_projection(x, n_seq, wq_bf16, wkvt_bf16, q_dtype, n_kv=6, tm=MATMUL_ROWS):
    m, d = x.shape
    seq = m // n_seq
    tm = _row_tile(seq, tm)
    nt = seq // tm
    nq = wq_bf16.shape[1] - LANES
    gd = wkvt_bf16.shape[0] // n_kv
    return pl.pallas_call(
        functools.partial(_nsa_proj_kernel, nq=nq),
        grid=(n_seq, nt),
        in_specs=[pl.BlockSpec((tm, d), lambda s, i: (s * nt + i, 0)),
                  pl.BlockSpec((d, nq + LANES), lambda s, i: (0, 0)),
                  pl.BlockSpec((n_kv * gd, d), lambda s, i: (0, 0))],
        out_specs=[pl.BlockSpec((tm, nq), lambda s, i: (s * nt + i, 0)),
                   pl.BlockSpec((tm, LANES), lambda s, i: (s * nt + i, 0))] + [
                      pl.BlockSpec((1, gd, tm), lambda s, i: (s, 0, i)) for _ in range(n_kv)],
        out_shape=[jax.ShapeDtypeStruct((m, nq), q_dtype), jax.ShapeDtypeStruct((m, LANES), F32)] + [
            jax.ShapeDtypeStruct((n_seq, gd, seq), F32) for _ in range(n_kv)],
        compiler_params=_cparams(2),
        name="nsa_projection",
    )(x, wq_bf16, wkvt_bf16)


def _mm_res_ln_kernel(a_ref, w_ref, x_ref, g_ref, b_ref, o_ref):
    y = _dot(a_ref[...].astype(BF16), w_ref[...])
    o_ref[...] = _layer_norm_rows(ALPHA * x_ref[...] + y, g_ref[...], b_ref[...])


def matmul_residual_ln(a, w_bf16, x, g, b, tm=MATMUL_ROWS):
    m, k = a.shape
    d = w_bf16.shape[1]
    tm = _row_tile(m, tm)
    return pl.pallas_call(
        _mm_res_ln_kernel,
        grid=(m // tm,),
        in_specs=[pl.BlockSpec((tm, k), lambda i: (i, 0)),
                  pl.BlockSpec((k, d), lambda i: (0, 0)),
                  pl.BlockSpec((tm, d), lambda i: (i, 0)),
                  pl.BlockSpec((1, d), lambda i: (0, 0)),
                  pl.BlockSpec((1, d), lambda i: (0, 0))],
        out_specs=pl.BlockSpec((tm, d), lambda i: (i, 0)),
        out_shape=jax.ShapeDtypeStruct((m, d), F32),
        compiler_params=_cparams(1),
        name="matmul_residual_ln",
    )(a, w_bf16, x, g.reshape(1, d), b.reshape(1, d))


def _expand_gates(gates_raw, e_ref):
    sig = jax.nn.sigmoid(gates_raw)
    hi = sig.astype(BF16)
    lo = (sig - hi.astype(F32)).astype(BF16)
    e = e_ref[...]
    return _dot(hi, e) + _dot(lo, e)


def _nsa_merge_ln_kernel(oc_ref, os_ref, ow_ref, gt_ref, e_ref, w_ref, x_ref, g_ref, b_ref, o_ref, *, k):
    gx = _expand_gates(gt_ref[...], e_ref)
    o = gx[:, 0:k] * oc_ref[...] + gx[:, k:2 * k] * os_ref[...] + gx[:, 2 * k:3 * k] * ow_ref[...]
    y = _dot(o.astype(BF16), w_ref[...])
    o_ref[...] = _layer_norm_rows(ALPHA * x_ref[...] + y, g_ref[...], b_ref[...])


def nsa_merge_residual_ln(oc, osl, ow, gates, e_bf16, w_bf16, x, g, b, tm=MERGE_ROWS):
    m, k = oc.shape
    d = w_bf16.shape[1]
    tm = _row_tile(m, tm)
    row = lambda i: (i, 0)
    fixed = lambda i: (0, 0)
    return pl.pallas_call(
        functools.partial(_nsa_merge_ln_kernel, k=k),
        grid=(m // tm,),
        in_specs=[pl.BlockSpec((tm, k), row), pl.BlockSpec((tm, k), row), pl.BlockSpec((tm, k), row),
                  pl.BlockSpec((tm, LANES), row),
                  pl.BlockSpec((LANES, 3 * k), fixed),
                  pl.BlockSpec((k, d), fixed),
                  pl.BlockSpec((tm, d), row),
                  pl.BlockSpec((1, d), fixed), pl.BlockSpec((1, d), fixed)],
        out_specs=pl.BlockSpec((tm, d), row),
        out_shape=jax.ShapeDtypeStruct((m, d), F32),
        compiler_params=_cparams(1),
        name="nsa_merge_residual_ln",
    )(oc, osl, ow, gates, e_bf16, w_bf16, x, g.reshape(1, d), b.reshape(1, d))


def _even_mixer_kernel(z_ref, hist_ref, st_ref, cos_ref, sin_ref, decay_ref, qdec_ref, kdec_ref, sdec_ref,
                       cw_ref, cb_ref, gn_ref, y_ref, hist_out_ref, st_out_ref, carry, state,
                       *, rows, valid, dconv, dk):
    c = pl.program_id(1)
    r0 = valid - 2 - (rows - SUBLANES)

    @pl.when(c == 0)
    def _():
        carry[r0:r0 + 2, :] = hist_ref[0]
        state[...] = st_ref[0]

    d = dconv
    h = z_ref[:, 0:d]
    gate_b = z_ref[:, d:2 * d]
    gate_c = z_ref[:, 2 * d:3 * d]
    ch = gate_c * h
    row = lax.broadcasted_iota(jnp.int32, (rows, d), 0)
    h0 = carry[r0:r0 + 1, :]
    h1 = carry[r0 + 1:r0 + 2, :]
    m1 = jnp.where(row == 0, h1, pltpu.roll(ch, 1, 0))
    m2 = jnp.where(row == 0, h0, jnp.where(row == 1, h1, pltpu.roll(ch, 2, 0)))
    u = ((cb_ref[...] + m2 * cw_ref[0:1, :]) + m1 * cw_ref[1:2, :]) + ch * cw_ref[2:3, :]
    y_ref[:, 0:d] = gate_b * u
    carry[...] = ch[rows - SUBLANES:rows, :]
    hist_out_ref[0] = carry[r0:r0 + 2, :]

    cosf = cos_ref[...]
    sinf = sin_ref[...]
    scale = np.float32(dk ** -0.5)
    for hh in range(RET_HEADS):
        q = z_ref[:, 3 * d + hh * dk:3 * d + (hh + 1) * dk]
        k = z_ref[:, 4 * d + hh * dk:4 * d + (hh + 1) * dk]
        v = z_ref[:, 5 * d + hh * dk:5 * d + (hh + 1) * dk]
        gsw = z_ref[:, 6 * d + hh * dk:6 * d + (hh + 1) * dk]
        q = (q * cosf + pltpu.roll(q, dk // 2, 1) * sinf) * scale
        k = k * cosf + pltpu.roll(k, dk // 2, 1) * sinf
        qb = q.astype(BF16)
        vb = v.astype(BF16)
        s_old = state[hh]
        scores = _nt_dot(qb, k.astype(BF16)) * decay_ref[hh]
        intra = _dot(scores.astype(BF16), vb)
        cross = _dot(qb, s_old.astype(BF16)) * qdec_ref[hh]
        kd = (k * kdec_ref[hh]).astype(BF16)
        state[hh] = s_old * sdec_ref[hh] + _tn_dot(kd, vb)
        o = intra + cross
        mu = jnp.mean(o, axis=-1, keepdims=True)
        dv = o - mu
        var = jnp.mean(dv * dv, axis=-1, keepdims=True)
        on = dv * lax.rsqrt(var + LN_EPS) * gn_ref[:, hh * dk:(hh + 1) * dk]
        y_ref[:, d + hh * dk:d + (hh + 1) * dk] = (gsw * jax.nn.sigmoid(gsw)) * on
    st_out_ref[0] = state[...]


def _retention_tables(rows, valid, dk):
    log_gamma = jnp.log1p(-jnp.exp2(-5.0 - jnp.arange(RET_HEADS, dtype=F32)))
    n = jnp.arange(rows, dtype=F32)
    diff = n[:, None] - n[None, :]
    lg = log_gamma[:, None, None]
    decay = jnp.where(diff >= 0, jnp.exp(lg * jnp.maximum(diff, 0.0)), 0.0)
    q_dec = jnp.exp((n[None, :] + 1.0) * log_gamma[:, None])
    k_dec = jnp.where(n[None, :] < valid, jnp.exp((valid - 1.0 - n[None, :]) * log_gamma[:, None]), 0.0)
    s_dec = jnp.exp(valid * log_gamma)
    bc = lambda a: jnp.broadcast_to(a[:, :, None], (RET_HEADS, rows, dk))
    return decay, bc(q_dec), bc(k_dec), jnp.broadcast_to(s_dec[:, None, None], (RET_HEADS, 1, dk))


def _rope_tables(pos, dk):
    half = dk // 2
    inv = ROPE_BASE ** (-jnp.arange(half, dtype=F32) / half)
    ang = pos.astype(F32)[:, None] * inv
    cos, sin = jnp.cos(ang), jnp.sin(ang)
    return jnp.concatenate([cos, cos], axis=-1), jnp.concatenate([-sin, sin], axis=-1)


def even_mixer(z, hist, st, pos, rows, valid, conv_w, conv_b, gn_g):
    n_seq, _, dconv = hist.shape
    dk = st.shape[-1]
    n_chunks = z.shape[0] // (n_seq * rows)
    cosf, sinf = _rope_tables(pos, dk)
    decay, q_dec, k_dec, s_dec = _retention_tables(rows, valid, dk)
    fixed3 = lambda s, c: (0, 0, 0)
    fixed2 = lambda s, c: (0, 0)
    return pl.pallas_call(
        functools.partial(_even_mixer_kernel, rows=rows, valid=valid, dconv=dconv, dk=dk),
        grid=(n_seq, n_chunks),
        in_specs=[pl.BlockSpec((rows, 7 * dconv), lambda s, c: (s * n_chunks + c, 0)),
                  pl.BlockSpec((1, 2, dconv), lambda s, c: (s, 0, 0)),
                  pl.BlockSpec((1, RET_HEADS, dk, dk), lambda s, c: (s, 0, 0, 0)),
                  pl.BlockSpec((rows, dk), lambda s, c: (c, 0)),
                  pl.BlockSpec((rows, dk), lambda s, c: (c, 0)),
                  pl.BlockSpec((RET_HEADS, rows, rows), fixed3),
                  pl.BlockSpec((RET_HEADS, rows, dk), fixed3),
                  pl.BlockSpec((RET_HEADS, rows, dk), fixed3),
                  pl.BlockSpec((RET_HEADS, 1, dk), fixed3),
                  pl.BlockSpec((SCONV_W, dconv), fixed2),
                  pl.BlockSpec((1, dconv), fixed2),
                  pl.BlockSpec((1, RET_HEADS * dk), fixed2)],
        out_specs=[pl.BlockSpec((rows, 2 * dconv), lambda s, c: (s * n_chunks + c, 0)),
                   pl.BlockSpec((1, 2, dconv), lambda s, c: (s, 0, 0)),
                   pl.BlockSpec((1, RET_HEADS, dk, dk), lambda s, c: (s, 0, 0, 0))],
        out_shape=[jax.ShapeDtypeStruct((z.shape[0], 2 * dconv), F32),
                   jax.ShapeDtypeStruct((n_seq, 2, dconv), F32),
                   jax.ShapeDtypeStruct((n_seq, RET_HEADS, dk, dk), F32)],
        scratch_shapes=[pltpu.VMEM((SUBLANES, dconv), F32), pltpu.VMEM((RET_HEADS, dk, dk), F32)],
        compiler_params=_cparams(2),
        name="even_mixer",
    )(z, hist, st, cosf, sinf, decay, q_dec, k_dec, s_dec, conv_w, conv_b.reshape(1, dconv),
      gn_g.reshape(1, RET_HEADS * dk))


def _conv_gate(a, gate, m1, m2, cw_ref, cb_ref):
    conv = ((cb_ref[...] + m2 * cw_ref[0:1, :]) + m1 * cw_ref[1:2, :]) + a * cw_ref[2:3, :]
    return _gelu(conv) * gate


def _ffn_up_seq_kernel(x_ref, wa_ref, wg_ref, h_ref, cw_ref, cb_ref, o_ref, hist_out_ref, carry, *, tm):
    @pl.when(pl.program_id(2) == 0)
    def _():
        carry[SUBLANES - 2:SUBLANES, :] = h_ref[0]

    xb = x_ref[...].astype(BF16)
    a = _dot(xb, wa_ref[...])
    gate = _dot(xb, wg_ref[...])
    row = lax.broadcasted_iota(jnp.int32, a.shape, 0)
    h0 = carry[SUBLANES - 2:SUBLANES - 1, :]
    h1 = carry[SUBLANES - 1:SUBLANES, :]
    m1 = jnp.where(row == 0, h1, pltpu.roll(a, 1, 0))
    m2 = jnp.where(row == 0, h0, jnp.where(row == 1, h1, pltpu.roll(a, 2, 0)))
    o_ref[...] = _conv_gate(a, gate, m1, m2, cw_ref, cb_ref).astype(o_ref.dtype)
    carry[...] = a[tm - SUBLANES:tm, :]
    hist_out_ref[0] = carry[SUBLANES - 2:SUBLANES, :]


def ffn_up_sequences(x, n_seq, hist, w_up_bf16, conv_w, conv_b, tm=MATMUL_ROWS, n_col=2):
    m, k = x.shape
    dff = conv_w.shape[1]
    seq = m // n_seq
    tm = _row_tile(seq, tm)
    tps = seq // tm
    tn = dff // n_col
    assert tn % LANES == 0
    return pl.pallas_call(
        functools.partial(_ffn_up_seq_kernel, tm=tm),
        grid=(n_col, n_seq, tps),
        in_specs=[pl.BlockSpec((tm, k), lambda j, s, i: (s * tps + i, 0)),
                  pl.BlockSpec((k, tn), lambda j, s, i: (0, j)),
                  pl.BlockSpec((k, tn), lambda j, s, i: (0, j + n_col)),
                  pl.BlockSpec((1, 2, tn), lambda j, s, i: (s, 0, j)),
                  pl.BlockSpec((FFN_W, tn), lambda j, s, i: (0, j)),
                  pl.BlockSpec((1, tn), lambda j, s, i: (0, j))],
        out_specs=[pl.BlockSpec((tm, tn), lambda j, s, i: (s * tps + i, j)),
                   pl.BlockSpec((1, 2, tn), lambda j, s, i: (s, 0, j))],
        out_shape=[jax.ShapeDtypeStruct((m, dff), BF16),
                   jax.ShapeDtypeStruct((n_seq, 2, dff), F32)],
        scratch_shapes=[pltpu.VMEM((SUBLANES, tn), F32)],
        compiler_params=_cparams(3),
        name="ffn_up_sequences",
    )(x, w_up_bf16, w_up_bf16, hist, conv_w, conv_b.reshape(1, dff))


def _ffn_up_short_kernel(x_ref, wa_ref, wg_ref, h1_ref, h2_ref, cw_ref, cb_ref, o_ref, a_ref):
    xb = x_ref[...].astype(BF16)
    a = _dot(xb, wa_ref[...])
    gate = _dot(xb, wg_ref[...])
    t = lax.broadcasted_iota(jnp.int32, a.shape, 0) % SUBLANES
    m1 = jnp.where(t == 0, h1_ref[...], pltpu.roll(a, 1, 0))
    m2 = jnp.where(t < 2, h2_ref[...], pltpu.roll(a, 2, 0))
    o_ref[...] = _conv_gate(a, gate, m1, m2, cw_ref, cb_ref).astype(o_ref.dtype)
    a_ref[...] = a


def ffn_up_short(x, hist, w_up_bf16, conv_w, conv_b, n_col=2):
    m, k = x.shape
    dff = conv_w.shape[1]
    n_seq = m // SUBLANES
    tn = dff // n_col
    zeros = jnp.zeros((n_seq, SUBLANES, dff), F32)
    h1 = zeros.at[:, 0].set(hist[:, 1]).reshape(m, dff)
    h2 = zeros.at[:, 0].set(hist[:, 0]).at[:, 1].set(hist[:, 1]).reshape(m, dff)
    col = lambda j: (0, j)
    return pl.pallas_call(
        _ffn_up_short_kernel,
        grid=(n_col,),
        in_specs=[pl.BlockSpec((m, k), lambda j: (0, 0)),
                  pl.BlockSpec((k, tn), col),
                  pl.BlockSpec((k, tn), lambda j: (0, j + n_col)),
                  pl.BlockSpec((m, tn), col), pl.BlockSpec((m, tn), col),
                  pl.BlockSpec((FFN_W, tn), col), pl.BlockSpec((1, tn), col)],
        out_specs=[pl.BlockSpec((m, tn), col), pl.BlockSpec((m, tn), col)],
        out_shape=[jax.ShapeDtypeStruct((m, dff), F32), jax.ShapeDtypeStruct((m, dff), F32)],
        compiler_params=_cparams(1),
        name="ffn_up_short",
    )(x, w_up_bf16, w_up_bf16, h1, h2, conv_w, conv_b.reshape(1, dff))


def _compress_kernel(pt_ref, *refs, pages):
    page_refs = refs[:pages + 1]
    w2t_ref, pecol_ref, w1_ref, w2_ref, o_ref = refs[pages + 1:pages + 6]
    rows_refs = refs[pages + 6:]
    parts = len(rows_refs)
    ppp = (pages + 1) // parts
    cpp = PAGE_SIZE // CMP_STRIDE
    n = (pages + 1) * cpp
    n_p = ppp * cpp
    hidden = w1_ref.shape[1]
    gpr = LANES // NSA_DH
    pieces = NSA_KV // gpr
    for i, r in enumerate(page_refs):
        for pc in range(pieces):
            tile = r[0, pc * gpr:(pc + 1) * gpr].reshape(LANES, PAGE_SIZE)
            rows_refs[i // ppp][pc, (i % ppp) * PAGE_SIZE:(i % ppp + 1) * PAGE_SIZE, :] = tile.T
    pe_term = jnp.sum(pecol_ref[...] * w1_ref[...], axis=0, keepdims=True)
    accs = []
    for rows_ref in rows_refs:
        acc = jnp.zeros((pieces * n_p, gpr * 2 * hidden), F32)
        for tp in range(CMP_STRIDE // 2):
            lhs = jnp.concatenate(
                [jnp.concatenate([rows_ref[pc, pl.ds(2 * tp, n_p, stride=CMP_STRIDE), :],
                                  rows_ref[pc, pl.ds(2 * tp + 1, n_p, stride=CMP_STRIDE), :]], axis=1)
                 for pc in range(pieces)], axis=0)
            acc = acc + _dot(lhs.astype(BF16), w2t_ref[tp])
        accs.append(acc)
    for pc in range(pieces):
        for gl in range(gpr):
            a = jnp.concatenate([acc[pc * n_p:(pc + 1) * n_p, gl * 2 * hidden:(gl + 1) * 2 * hidden]
                                 for acc in accs], axis=0)
            nxt = pltpu.roll(a, n - 1, 0)
            pre = pe_term + a[:, 0:hidden]
            pre = pre + nxt[:, hidden:2 * hidden]
            o_ref[0, pc * gpr + gl] = _dot(_gelu(pre[0:pages * cpp]).astype(BF16), w2_ref[...])


def compress(rows_t, page_table, pe, w1, w2, pages=COMPRESS_PAGES):
    pooled = page_table is not None
    if pooled:
        n_seq, n_pages = page_table.shape
    else:
        n_seq, n_pages = rows_t.shape[0], rows_t.shape[3] // PAGE_SIZE
        page_table = jnp.zeros((1, 1), jnp.int32)
    pages = min(pages, n_pages)
    assert n_pages % pages == 0
    parts = 3 if (pages + 1) % 3 == 0 else 1
    hidden = w1.shape[1]
    cpp = PAGE_SIZE // CMP_STRIDE
    r = CMP_LEN // CMP_STRIDE
    gpr = LANES // NSA_DH
    assert r == 2 and gpr == 2
    w1p = w1.reshape(r, CMP_STRIDE, NSA_DH, hidden)
    w16 = jnp.concatenate([w1p[0], w1p[1]], axis=-1)
    zero = jnp.zeros_like(w16)
    per_tok = jnp.concatenate([jnp.concatenate([w16, zero], axis=-1),
                               jnp.concatenate([zero, w16], axis=-1)], axis=1)
    w2t = per_tok.reshape(CMP_STRIDE // 2, 2 * LANES, gpr * 2 * hidden).astype(BF16)
    w2p = jnp.pad(w2, ((0, 0), (0, SLOT - NSA_DH))).astype(BF16)
    pecol = pe.reshape(CMP_LEN * NSA_DH, 1)

    def page_map(i):
        if pooled:
            return lambda s, j, pt: (pt[s, jnp.minimum(j * pages + i, n_pages - 1)], 0, 0, 0)
        return lambda s, j, pt: (s, 0, 0, jnp.minimum(j * pages + i, n_pages - 1))

    fixed2 = lambda s, j, pt: (0, 0)
    grid_spec = pltpu.PrefetchScalarGridSpec(
        num_scalar_prefetch=1,
        grid=(n_seq, n_pages // pages),
        in_specs=[pl.BlockSpec((1, NSA_KV, NSA_DH, PAGE_SIZE), page_map(i)) for i in range(pages + 1)] + [
            pl.BlockSpec((CMP_STRIDE // 2, 2 * LANES, gpr * 2 * hidden), lambda s, j, pt: (0, 0, 0)),
            pl.BlockSpec((CMP_LEN * NSA_DH, 1), fixed2),
            pl.BlockSpec((CMP_LEN * NSA_DH, hidden), fixed2),
            pl.BlockSpec((hidden, SLOT), fixed2)],
        out_specs=pl.BlockSpec((1, NSA_KV, pages * cpp, SLOT), lambda s, j, pt: (s, 0, j, 0)),
        scratch_shapes=[pltpu.VMEM((NSA_KV // gpr, (pages + 1) // parts * PAGE_SIZE, LANES), F32)
                        for _ in range(parts)],
    )
    return pl.pallas_call(
        functools.partial(_compress_kernel, pages=pages),
        grid_spec=grid_spec,
        out_shape=jax.ShapeDtypeStruct((n_seq, NSA_KV, n_pages * cpp, SLOT), F32),
        compiler_params=_cparams(2),
        name="compress",
    )(page_table, *([rows_t] * (pages + 1)), w2t, pecol, w1, w2p)


def _cmp_select_kernel(q_ref, kc_ref, vc_ref, ov_ref, o_ref, idx_ref, *, tq, n_cmp, n_slc, pos0):
    ncp = kc_ref.shape[2]
    nsp = ov_ref.shape[0]
    hrows = NSA_GROUP * tq
    q_pos = pos0 + (lax.broadcasted_iota(jnp.int32, (hrows, ncp), 0) & (tq - 1))
    blk_i = lax.broadcasted_iota(jnp.int32, (hrows, ncp), 1)
    valid = (blk_i * CMP_STRIDE + (CMP_LEN - 1) <= q_pos) & (blk_i < n_cmp)
    ov = ov_ref[...]
    groups = range(NSA_KV)
    heads = [range(g * NSA_GROUP, (g + 1) * NSA_GROUP) for g in groups]
    scores = [_nt_dot(jnp.concatenate([q_ref[:, h * SLOT:(h + 1) * SLOT] for h in heads[g]], axis=0).astype(BF16),
                      kc_ref[0, g].astype(BF16)) for g in groups]
    probs = []
    for g in groups:
        s = jnp.where(valid, scores[g], NEG_INF)
        m = jnp.max(s, axis=-1, keepdims=True)
        e = jnp.where(valid, jnp.exp(s - m), 0.0)
        den = jnp.sum(e, axis=-1, keepdims=True)
        probs.append(e * (1.0 / jnp.where(den > 0.0, den, 1.0)))
    outs = [_dot(probs[g].astype(BF16), vc_ref[0, g].astype(BF16)) for g in groups]
    imps = []
    for g in groups:
        p_sum = jnp.zeros((tq, ncp), F32)
        for j, h in enumerate(heads[g]):
            o_ref[:, h * SLOT:(h + 1) * SLOT] = outs[g][j * tq:(j + 1) * tq]
            p_sum = p_sum + probs[g][j * tq:(j + 1) * tq]
        hi = p_sum.astype(BF16)
        lo = (p_sum - hi.astype(F32)).astype(BF16)
        imps.append(_nt_dot(ov, hi) + _nt_dot(ov, lo))
    cols = NSA_KV * tq
    imp = jnp.concatenate(imps, axis=1)
    blk = lax.broadcasted_iota(jnp.int32, (nsp, cols), 0)
    cur = (pos0 + (lax.broadcasted_iota(jnp.int32, (nsp, cols), 1) & (tq - 1))) // SLC_BLOCK
    real = blk < n_slc
    causal = real & (blk <= cur)
    forced = (blk == 0) | (blk == cur) | (blk == cur - 1)
    score = jnp.where(causal, imp + jnp.where(forced, FORCE_BONUS, 0.0), NEG_INF)
    score = jnp.where(real, score, REMOVED)
    idx = jnp.zeros((SLC_TOPN, cols), jnp.int32)
    idx_row = lax.broadcasted_iota(jnp.int32, (SLC_TOPN, cols), 0)
    for it in range(min(SLC_TOPN, n_slc)):
        m = jnp.max(score, axis=0, keepdims=True)
        first = jnp.min(jnp.where(score == m, blk, nsp), axis=0, keepdims=True)
        score = jnp.where(blk == first, REMOVED, score)
        idx = jnp.where(idx_row == it, first, idx)
    idx_ref[0] = idx


def cmp_block_overlap(n_cmp_pad, n_cmp, n_slc, n_slc_pad, lane_off):
    i = np.arange(n_cmp_pad)[:, None]
    j = np.arange(n_slc_pad)[None, :] - lane_off
    start = i * CMP_STRIDE
    hit = (start < (j + 1) * SLC_BLOCK) & (start + CMP_LEN > j * SLC_BLOCK) & (i < n_cmp) & (j >= 0) & (j < n_slc)
    return jnp.asarray(hit.astype(np.float32), dtype=BF16)


def cmp_attention_select(q_slots, kcc, vcc, n_seq, n_cmp, n_slc, pos0):
    tokens = q_slots.shape[0]
    tq = tokens // n_seq
    assert tq & (tq - 1) == 0
    ncp = kcc.shape[2]
    nsp = -(-n_slc // SUBLANES) * SUBLANES
    ov_t = cmp_block_overlap(ncp, n_cmp, n_slc, nsp, 0).T
    hw = NSA_HEADS * SLOT
    o_cmp, idx_t = pl.pallas_call(
        functools.partial(_cmp_select_kernel, tq=tq, n_cmp=n_cmp, n_slc=n_slc, pos0=pos0),
        grid=(n_seq,),
        in_specs=[pl.BlockSpec((tq, hw), lambda s: (s, 0)),
                  pl.BlockSpec((1, NSA_KV, ncp, SLOT), lambda s: (s, 0, 0, 0)),
                  pl.BlockSpec((1, NSA_KV, ncp, SLOT), lambda s: (s, 0, 0, 0)),
                  pl.BlockSpec((nsp, ncp), lambda s: (0, 0))],
        out_specs=[pl.BlockSpec((tq, hw), lambda s: (s, 0)),
                   pl.BlockSpec((1, SLC_TOPN, NSA_KV * tq), lambda s: (s, 0, 0))],
        out_shape=[jax.ShapeDtypeStruct((tokens, hw), F32),
                   jax.ShapeDtypeStruct((n_seq, SLC_TOPN, NSA_KV * tq), jnp.int32)],
        compiler_params=_cparams(1),
        name="cmp_attention_select",
    )(q_slots, kcc, vcc, ov_t)
    topn = min(SLC_TOPN, n_slc)
    idx = idx_t[:, :topn].reshape(n_seq, topn, NSA_KV, tq).transpose(0, 2, 3, 1)
    return o_cmp, idx


def _cmp_select_prompt_kernel(q_ref, kc_ref, vc_ref, ovt_ref, o_ref, sel_ref, *, tq, n_cmp, n_slc):
    t0 = pl.program_id(2) * tq
    ncp = kc_ref.shape[2]
    nsr = ovt_ref.shape[0]
    kc = kc_ref[0, 0].astype(BF16)
    vc = vc_ref[0, 0].astype(BF16)
    q_pos = t0 + lax.broadcasted_iota(jnp.int32, (ncp, tq), 1)
    blk_i = lax.broadcasted_iota(jnp.int32, (ncp, tq), 0)
    valid = (blk_i * CMP_STRIDE + (CMP_LEN - 1) <= q_pos) & (blk_i < n_cmp)
    p_sum = jnp.zeros((ncp, tq), F32)
    for j in range(NSA_GROUP):
        s = jnp.where(valid, _nt_dot(kc, q_ref[:, j * SLOT:(j + 1) * SLOT]), NEG_INF)
        m = jnp.max(s, axis=0, keepdims=True)
        e = jnp.where(valid, jnp.exp(s - m), 0.0)
        den = jnp.sum(e, axis=0, keepdims=True)
        p = e * (1.0 / jnp.where(den > 0.0, den, 1.0))
        o_ref[:, j * SLOT:(j + 1) * SLOT] = _tn_dot(p.astype(BF16), vc)
        p_sum = p_sum + p
    hi = p_sum.astype(BF16)
    lo = (p_sum - hi.astype(F32)).astype(BF16)
    ovt = ovt_ref[...]
    imp = _dot(ovt, hi) + _dot(ovt, lo)
    blk = lax.broadcasted_iota(jnp.int32, (nsr, tq), 0)
    cur = (t0 + lax.broadcasted_iota(jnp.int32, (nsr, tq), 1)) // SLC_BLOCK
    real = blk < n_slc
    causal = real & (blk <= cur)
    forced = (blk == 0) | (blk == cur) | (blk == cur - 1)
    score = jnp.where(causal, imp + jnp.where(forced, FORCE_BONUS, 0.0), NEG_INF)
    score = jnp.where(real, score, REMOVED)
    picked = jnp.zeros((nsr, tq), jnp.bool_)
    for _ in range(min(SLC_TOPN, n_slc)):
        m = jnp.max(score, axis=0, keepdims=True)
        first = jnp.min(jnp.where(score == m, blk, nsr), axis=0, keepdims=True)
        hit = blk == first
        picked = picked | hit
        score = jnp.where(hit, REMOVED, score)
    bias_t = jnp.where(real & ~(picked & causal), NEG_INF, 0.0)
    slot_t = jnp.concatenate([jnp.zeros((NSA_DH, tq), F32), bias_t], axis=0)
    sel_ref[0, 0] = slot_t.T


def cmp_attention_select_prompt(q_slots, kcc, vcc, n_seq, n_cmp, n_slc, tq):
    tokens = q_slots.shape[0]
    t = tokens // n_seq
    tq = _row_tile(t, tq)
    nt = t // tq
    ncp = kcc.shape[2]
    nsr = SLOT - NSA_DH
    assert n_slc <= nsr
    ovt = cmp_block_overlap(ncp, n_cmp, n_slc, nsr, 0).T
    gw = NSA_GROUP * SLOT
    return pl.pallas_call(
        functools.partial(_cmp_select_prompt_kernel, tq=tq, n_cmp=n_cmp, n_slc=n_slc),
        grid=(n_seq, NSA_KV, nt),
        in_specs=[pl.BlockSpec((tq, gw), lambda s, g, i: (s * nt + i, g)),
                  pl.BlockSpec((1, 1, ncp, SLOT), lambda s, g, i: (s, g, 0, 0)),
                  pl.BlockSpec((1, 1, ncp, SLOT), lambda s, g, i: (s, g, 0, 0)),
                  pl.BlockSpec((nsr, ncp), lambda s, g, i: (0, 0))],
        out_specs=[pl.BlockSpec((tq, gw), lambda s, g, i: (s * nt + i, g)),
                   pl.BlockSpec((1, 1, tq, SLOT), lambda s, g, i: (s, g, i, 0))],
        out_shape=[jax.ShapeDtypeStruct((tokens, NSA_HEADS * SLOT), F32),
                   jax.ShapeDtypeStruct((n_seq, NSA_KV, t, SLOT), F32)],
        compiler_params=_cparams(3),
        name="cmp_attention_select_prompt",
    )(q_slots, kcc, vcc, ovt)


def _prompt_slc_win_kernel(q_ref, sel_ref, ks_ref, vs_ref, kw_ref, vw_ref, oh_ref, os_ref, ow_ref,
                           m_ref, acc_ref, *, tq, seq):
    acc_rows = NSA_DH + 16
    qi = pl.program_id(2)
    t0 = qi * tq
    rows = NSA_GROUP * tq
    sel = sel_ref[0, 0]
    q_plain = jnp.concatenate([q_ref[:, j * SLOT:(j + 1) * SLOT] for j in range(NSA_GROUP)], axis=0)
    q_aug = jnp.concatenate([(q_ref[:, j * SLOT:(j + 1) * SLOT].astype(F32) + sel).astype(BF16)
                             for j in range(NSA_GROUP)], axis=0)
    zeros_k = jnp.zeros((SLOT - NSA_DH, tq), F32)
    ones_row = (lax.broadcasted_iota(jnp.int32, (acc_rows - NSA_DH, tq), 0) == 0).astype(BF16)
    rel = lax.broadcasted_iota(jnp.int32, (tq, LANES), 1) - lax.broadcasted_iota(jnp.int32, (tq, LANES), 0)

    def scores(q_rows, k_top, k_bottom):
        k_rows = jnp.concatenate([k_top, k_bottom], axis=0).T.astype(BF16)
        return [_nt_dot(k_rows, q_rows[c:c + 2 * LANES]) for c in range(0, rows, 2 * LANES)]

    def update(s_t, v_top, start, mask):
        v_t = jnp.concatenate([v_top.astype(BF16), ones_row], axis=0)
        for cg in range(rows // (2 * LANES)):
            p_parts, a_parts = [], []
            for h in range(2):
                c0 = (2 * cg + h) * LANES
                x = s_t[cg][:, h * LANES:(h + 1) * LANES]
                if mask is not None:
                    lo, hi = mask
                    off = t0 - start + (c0 & (tq - 1))
                    keep = rel >= lo - off
                    if hi is not None:
                        keep = keep & (rel < hi - off)
                    x = jnp.where(keep, x, NEG_INF)
                m_old = m_ref[:, c0:c0 + LANES]
                m_new = jnp.maximum(m_old, jnp.max(x, axis=0, keepdims=True))
                m_ref[:, c0:c0 + LANES] = m_new
                a_parts.append(jnp.exp(m_old - m_new))
                p_parts.append(jnp.exp(x - m_new).astype(BF16))
            c0 = 2 * cg * LANES
            pv = _dot(v_t, jnp.concatenate(p_parts, axis=1))
            acc_ref[:, c0:c0 + 2 * LANES] = jnp.concatenate(a_parts, axis=1) * acc_ref[:, c0:c0 + 2 * LANES] + pv

    def reset():
        m_ref[...] = jnp.full(m_ref.shape, NEG_INF, F32)
        acc_ref[...] = jnp.zeros(acc_ref.shape, F32)

    slot_pad = jnp.zeros((SLOT - NSA_DH, tq), F32)

    def finish(o_ref):
        for j in range(NSA_GROUP):
            a = acc_ref[:, j * tq:(j + 1) * tq]
            o_t = jnp.concatenate([a[0:NSA_DH] * (1.0 / a[NSA_DH:NSA_DH + 1, :]), slot_pad], axis=0)
            o_ref[:, j * SLOT:(j + 1) * SLOT] = o_t.T

    def slc_scores(start):
        return scores(q_aug, ks_ref[0, :, pl.ds(start, tq)], oh_ref[:, pl.ds(start, tq)])

    def slc_update(s_t, start, mask):
        update(s_t, vs_ref[0, :, pl.ds(start, tq)], start, mask)

    k0 = jnp.clip(t0 - WINDOW, 0, seq - WINDOW - tq)
    n_win = WINDOW // tq + 1
    win_start = [pl.multiple_of(k0 + i * tq, tq) for i in range(n_win)]
    win_mask = (0, WINDOW)

    def win_scores(i):
        return scores(q_plain, kw_ref[0, :, pl.ds(win_start[i], tq)], zeros_k)

    reset()

    def slc_tiles(first, count):
        starts = [pl.multiple_of((first + i) * tq, tq) for i in range(count)]
        s_all = [slc_scores(st) for st in starts]
        for s_t, st in zip(s_all, starts):
            slc_update(s_t, st, None)

    def quad(k4, carry):
        slc_tiles(4 * k4, 4)
        return carry

    lax.fori_loop(0, qi // 4, quad, 0)

    @pl.when(qi % 4 >= 2)
    def _():
        slc_tiles(4 * (qi // 4), 2)

    @pl.when(qi % 2 == 1)
    def _():
        slc_tiles(qi - 1, 1)

    diag = pl.multiple_of(t0, tq)
    s_cur = slc_scores(diag)
    s_next = win_scores(0)
    slc_update(s_cur, diag, (0, None))
    finish(os_ref)
    reset()
    for i in range(n_win):
        s_cur = s_next
        if i + 1 < n_win:
            s_next = win_scores(i + 1)
        update(s_cur, vw_ref[0, :, pl.ds(win_start[i], tq)], win_start[i], win_mask)
    finish(ow_ref)


def prompt_slc_win_attention(q_slots, sel, ks_t, vs_t, kw_t, vw_t, n_seq, tq):
    tokens = q_slots.shape[0]
    seq = tokens // n_seq
    tq = _row_tile(seq, tq)
    assert tq & (tq - 1) == 0 and WINDOW % tq == 0 and seq >= WINDOW + tq
    nt = seq // tq
    gw = NSA_GROUP * SLOT
    assert tq % (2 * LANES) == 0
    onehot_t = jax.nn.one_hot(jnp.arange(seq) // SLC_BLOCK, SLOT - NSA_DH, dtype=F32).T
    kv_spec = pl.BlockSpec((1, NSA_DH, seq), lambda s, g, i: (s, g, 0))
    return pl.pallas_call(
        functools.partial(_prompt_slc_win_kernel, tq=tq, seq=seq),
        grid=(n_seq, NSA_KV, nt),
        in_specs=[pl.BlockSpec((tq, gw), lambda s, g, i: (s * nt + i, g)),
                  pl.BlockSpec((1, 1, tq, SLOT), lambda s, g, i: (s, g, i, 0)),
                  kv_spec, kv_spec, kv_spec, kv_spec,
                  pl.BlockSpec((SLOT - NSA_DH, seq), lambda s, g, i: (0, 0))],
        out_specs=[pl.BlockSpec((tq, gw), lambda s, g, i: (s * nt + i, g)),
                   pl.BlockSpec((tq, gw), lambda s, g, i: (s * nt + i, g))],
        out_shape=[jax.ShapeDtypeStruct((tokens, NSA_HEADS * SLOT), F32),
                   jax.ShapeDtypeStruct((tokens, NSA_HEADS * SLOT), F32)],
        scratch_shapes=[pltpu.VMEM((1, NSA_GROUP * tq), F32),
                        pltpu.VMEM((NSA_DH + 16, NSA_GROUP * tq), F32)],
        compiler_params=_cparams(3),
        name="prompt_slc_win_attention",
    )(q_slots, sel, ks_t, vs_t, kw_t, vw_t, onehot_t)


def _sample_slc_kernel(idx_ref, pt_ref, q_ref, kn_ref, vn_ref, *refs, topn, past, t_pad):
    k_refs = refs[:topn]
    v_refs = refs[topn:2 * topn]
    o_ref = refs[2 * topn]
    s_id, g_id, t_id = pl.program_id(0), pl.program_id(1), pl.program_id(2)
    base = ((s_id * NSA_KV + g_id) * t_pad + t_id) * topn
    q_pos = past + t_id
    cur = q_pos // SLC_BLOCK
    first_new = past // SLC_BLOCK
    q = q_ref[0, 0, 0].astype(BF16)
    kb = jnp.concatenate([r[0, 0] for r in k_refs], axis=1).astype(BF16)
    vb = jnp.concatenate([r[0, 0] for r in v_refs], axis=1).astype(BF16)
    n_keys = topn * PAGE_SIZE
    lane = lax.broadcasted_iota(jnp.int32, (1, n_keys), 1)
    slot = lane // PAGE_SIZE
    in_page = lane % PAGE_SIZE
    k_pos = in_page
    limit = jnp.zeros((1, n_keys), jnp.int32)
    n_new = jnp.int32(0)
    for kk in range(topn):
        b = idx_ref[base + kk]
        here = slot == kk
        k_pos = jnp.where(here, (b // 2) * PAGE_SIZE + in_page, k_pos)
        last = jnp.where(b <= cur, jnp.minimum(q_pos, past - 1), -1)
        limit = jnp.where(here, jnp.where(in_page // SLC_BLOCK == b % 2, last, -1), limit)
        n_new = n_new + jnp.where(b == first_new, 1, 0)
    valid = k_pos <= limit
    s_old = jnp.where(valid, _dot(q, kb), NEG_INF)
    new_lane = lax.broadcasted_iota(jnp.int32, (1, SUBLANES), 1)
    valid_new = past + new_lane <= jnp.where(n_new > 0, q_pos, past - 1)
    s_new = jnp.where(valid_new, _dot(q, kn_ref[0, 0].astype(BF16)), NEG_INF)
    m = jnp.maximum(jnp.max(s_old, axis=-1, keepdims=True), jnp.max(s_new, axis=-1, keepdims=True))
    p_old = jnp.exp(s_old - m)
    p_new = jnp.exp(s_new - m)
    l = jnp.sum(p_old, axis=-1, keepdims=True) + jnp.sum(p_new, axis=-1, keepdims=True)
    o = _nt_dot(p_old.astype(BF16), vb) + _nt_dot(p_new.astype(BF16), vn_ref[0, 0].astype(BF16))
    o_ref[0, 0, 0] = o / l


def sample_slc_attention(q_rows, idx, page_table, pool_k, pool_v, k_new_t, v_new_t, t_real, past):
    n_seq, _, t_pad, _, dh = q_rows.shape
    topn = idx.shape[-1]
    assert past % SLC_BLOCK == 0 and t_real <= SUBLANES and PAGE_SIZE == 2 * SLC_BLOCK
    last_old = past // SLC_BLOCK - 1
    logical = jnp.clip(idx, 0, last_old) // 2
    n_pages = page_table.shape[1]
    hit = logical[..., None] == jnp.arange(n_pages, dtype=jnp.int32)
    phys = jnp.sum(jnp.where(hit, page_table[:, None, None, None, :], 0), axis=-1)

    def blk_map(kk):
        def f(s, g, t, idx_ref, pg_ref):
            return (pg_ref[((s * NSA_KV + g) * t_pad + t) * topn + kk], g, 0, 0)
        return f

    blk_specs = [pl.BlockSpec((1, 1, dh, PAGE_SIZE), blk_map(kk)) for kk in range(topn)]
    new_spec = pl.BlockSpec((1, 1, dh, SUBLANES), lambda s, g, t, i_r, p_r: (s, g, 0, 0))
    q_spec = pl.BlockSpec((1, 1, 1, SUBLANES, dh), lambda s, g, t, i_r, p_r: (s, g, t, 0, 0))
    grid_spec = pltpu.PrefetchScalarGridSpec(
        num_scalar_prefetch=2,
        grid=(n_seq, NSA_KV, t_real),
        in_specs=[q_spec, new_spec, new_spec] + blk_specs + blk_specs,
        out_specs=q_spec,
    )
    return pl.pallas_call(
        functools.partial(_sample_slc_kernel, topn=topn, past=past, t_pad=t_pad),
        grid_spec=grid_spec,
        out_shape=jax.ShapeDtypeStruct((n_seq, NSA_KV, t_real, SUBLANES, dh), F32),
        compiler_params=_cparams(3),
        name="sample_slc_attention",
    )(idx.reshape(-1), phys.reshape(-1), q_rows, k_new_t, v_new_t, *([pool_k] * topn), *([pool_v] * topn))


def _sample_win_kernel(q_ref, wk_ref, wv_ref, kn_ref, vn_ref, o_ref, *, past, t_pad):
    rows = t_pad * SUBLANES
    wb = wk_ref.shape[-1]
    q = q_ref[0, 0].reshape(rows, q_ref.shape[-1]).astype(BF16)
    q_pos = past + lax.broadcasted_iota(jnp.int32, (rows, 1), 0) // SUBLANES
    k_pos = past - wb + lax.broadcasted_iota(jnp.int32, (1, wb), 1)
    dist = q_pos - k_pos
    valid = (dist >= 0) & (dist < WINDOW) & (k_pos >= 0)
    s_old = jnp.where(valid, _dot(q, wk_ref[0, 0].astype(BF16)), NEG_INF)
    n_pos = past + lax.broadcasted_iota(jnp.int32, (1, SUBLANES), 1)
    dist_n = q_pos - n_pos
    valid_n = (dist_n >= 0) & (dist_n < WINDOW)
    s_new = jnp.where(valid_n, _dot(q, kn_ref[0, 0].astype(BF16)), NEG_INF)
    m = jnp.maximum(jnp.max(s_old, axis=-1, keepdims=True), jnp.max(s_new, axis=-1, keepdims=True))
    p_old = jnp.exp(s_old - m)
    p_new = jnp.exp(s_new - m)
    l = jnp.sum(p_old, axis=-1, keepdims=True) + jnp.sum(p_new, axis=-1, keepdims=True)
    o = _nt_dot(p_old.astype(BF16), wv_ref[0, 0].astype(BF16)) + _nt_dot(p_new.astype(BF16),
                                                                      vn_ref[0, 0].astype(BF16))
    o_ref[0, 0] = (o / l).reshape(t_pad, SUBLANES, o.shape[-1])


def sample_win_attention(q_rows, win_k_t, win_v_t, k_new_t, v_new_t, past):
    n_seq, _, t_pad, _, dh = q_rows.shape
    wb = win_k_t.shape[-1]
    q_spec = pl.BlockSpec((1, 1, t_pad, SUBLANES, dh), lambda s, g: (s, g, 0, 0, 0))
    win_spec = pl.BlockSpec((1, 1, dh, wb), lambda s, g: (s, g, 0, 0))
    new_spec = pl.BlockSpec((1, 1, dh, SUBLANES), lambda s, g: (s, g, 0, 0))
    return pl.pallas_call(
        functools.partial(_sample_win_kernel, past=past, t_pad=t_pad),
        grid=(n_seq, NSA_KV),
        in_specs=[q_spec, win_spec, win_spec, new_spec, new_spec],
        out_specs=q_spec,
        out_shape=jax.ShapeDtypeStruct(q_rows.shape, F32),
        compiler_params=_cparams(2),
        name="sample_win_attention",
    )(q_rows, win_k_t, win_v_t, k_new_t, v_new_t)


def _to_slots(a):
    lead = a.shape[:-1]
    n = a.shape[-1] // NSA_DH
    a = a.reshape(*lead, n, NSA_DH)
    a = jnp.pad(a, [(0, 0)] * (a.ndim - 1) + [(0, SLOT - NSA_DH)])
    return a.reshape(*lead, n * SLOT)


def _odd_weights(w_in, w_out):
    d = w_in.shape[0]
    hq = NSA_HEADS * NSA_DH
    kvw = NSA_KV * NSA_DH
    wq = _to_slots(w_in[:, :hq] * np.float32(NSA_DH ** -0.5))
    wg = jnp.pad(w_in[:, hq + 6 * kvw:], ((0, 0), (0, LANES - 3 * NSA_HEADS)))
    w_q = jnp.concatenate([wq, wg], axis=1).astype(BF16)
    w_kvt = w_in[:, hq:hq + 6 * kvw].T.astype(BF16)
    wo = jnp.pad(w_out.reshape(NSA_HEADS, NSA_DH, d), ((0, 0), (0, SLOT - NSA_DH), (0, 0)))
    wo = wo.reshape(NSA_HEADS * SLOT, d).astype(BF16)
    k = NSA_HEADS * SLOT
    e = np.zeros((LANES, 3 * k), np.float32)
    for c in range(3):
        for h in range(NSA_HEADS):
            e[c * NSA_HEADS + h, c * k + h * SLOT:c * k + (h + 1) * SLOT] = 1.0
    return w_q, w_kvt, wo, jnp.asarray(e, dtype=BF16)


def _group_rows(q_slots, n_seq, t_pad):
    q = q_slots.reshape(n_seq, t_pad, NSA_KV, NSA_GROUP, SLOT)[..., :NSA_DH]
    q = q.transpose(0, 2, 1, 3, 4)
    return jnp.pad(q, ((0, 0), (0, 0), (0, 0), (0, SUBLANES - NSA_GROUP), (0, 0)))


def _ungroup_rows(o, n_seq, t_pad):
    t = o.shape[2]
    o = o[:, :, :, :NSA_GROUP].transpose(0, 2, 1, 3, 4)
    o = jnp.pad(o, ((0, 0), (0, t_pad - t), (0, 0), (0, 0), (0, SLOT - NSA_DH)))
    return o.reshape(n_seq * t_pad, NSA_HEADS * SLOT)


def _feature_major(cache):
    return cache.transpose(0, 2, 3, 1)


def _token_major(a_t):
    return a_t.transpose(0, 3, 1, 2)


def _pad_rows(a, t_pad):
    return jnp.pad(a, ((0, 0), (0, t_pad - a.shape[1])) + ((0, 0),) * (a.ndim - 2))


def kernel(x_prompt, x_sample, state_sconv, state_ret, cache_cmp_k, cache_cmp_v, cache_slc_k, cache_slc_v,
           cache_win_k, cache_win_v, state_ffn_conv, page_table,
           w_in_even, sconv_w, sconv_b, ret_gn_g, w_out_even,
           w_in_odd, cmp_pe, cmp_w1, cmp_w2, w_out_odd,
           ln_mix_g, ln_mix_b, ffn_w_up, ffn_conv_w, ffn_conv_b, ffn_w_down, ln_ffn_g, ln_ffn_b):
    b_p, s_p, d_model = x_prompt.shape
    b_s, t_s, _ = x_sample.shape
    n_pages = page_table.shape[1]
    past = n_pages * PAGE_SIZE
    t_pad = SUBLANES
    assert t_s <= t_pad and t_s >= SCONV_W - 1 and t_s < CMP_STRIDE and past % PAGE_SIZE == 0
    assert s_p % RET_CHUNK == 0 and s_p % PAGE_SIZE == 0
    d_sconv = sconv_w.shape[-1]
    d_ff = ffn_conv_w.shape[-1]
    gd = NSA_KV * NSA_DH
    depth = ln_mix_g.shape[0]

    xp = x_prompt.reshape(b_p * s_p, d_model)
    xs = _pad_rows(x_sample, t_pad).reshape(b_s * t_pad, d_model)
    outs = {k: [] for k in ("sconv_p", "sconv_s", "ret_p", "ret_s", "cmp_k_p", "cmp_v_p", "slc_k_p", "slc_v_p",
                            "cmp_k_s", "cmp_v_s", "slc_k_s", "slc_v_s", "win_k_p", "win_v_p", "win_k_s",
                            "win_v_s", "ffn_p", "ffn_s")}

    for layer in range(depth):
        if layer % 2 == 0:
            e = layer // 2
            w_in = w_in_even[e].astype(BF16)
            w_out = w_out_even[e].astype(BF16)
            n_in = w_in.shape[1]
            (zp,) = matmul_split(xp, w_in, [n_in], [F32])
            yp, hc, st = even_mixer(zp, jnp.zeros((b_p, SCONV_W - 1, d_sconv), F32),
                                    jnp.zeros((b_p,) + state_ret.shape[2:], F32), jnp.arange(s_p),
                                    RET_CHUNK, RET_CHUNK, sconv_w[e], sconv_b[e], ret_gn_g[e])
            outs["sconv_p"].append(hc)
            outs["ret_p"].append(st)
            xp = matmul_residual_ln(yp, w_out, xp, ln_mix_g[layer], ln_mix_b[layer])
            (zs,) = matmul_split(xs, w_in, [n_in], [F32])
            ys, hc, st = even_mixer(zs, state_sconv[e], state_ret[e], past + jnp.arange(t_pad),
                                    t_pad, t_s, sconv_w[e], sconv_b[e], ret_gn_g[e])
            outs["sconv_s"].append(hc)
            outs["ret_s"].append(st)
            xs = matmul_residual_ln(ys, w_out, xs, ln_mix_g[layer], ln_mix_b[layer])
        else:
            o = layer // 2
            w_q, w_kvt, w_out, e_gate = _odd_weights(w_in_odd[o], w_out_odd[o])
            pe, w1, w2 = cmp_pe[o], cmp_w1[o], cmp_w2[o]
            qp, gp, kc, vc, ks, vs, kw, vw = nsa_projection(xp, b_p, w_q, w_kvt, BF16)
            as_cache = lambda a_t: _token_major(a_t.reshape(b_p, NSA_KV, NSA_DH, -1))
            keep = min(WINDOW, s_p)
            outs["cmp_k_p"].append(as_cache(kc))
            outs["cmp_v_p"].append(as_cache(vc))
            outs["slc_k_p"].append(as_cache(ks))
            outs["slc_v_p"].append(as_cache(vs))
            outs["win_k_p"].append(as_cache(kw[:, :, s_p - keep:]))
            outs["win_v_p"].append(as_cache(vw[:, :, s_p - keep:]))
            kcc = compress(kc.reshape(b_p, NSA_KV, NSA_DH, s_p), None, pe[0], w1[0], w2[0])
            vcc = compress(vc.reshape(b_p, NSA_KV, NSA_DH, s_p), None, pe[1], w1[1], w2[1])
            n_cmp = s_p // CMP_STRIDE - CMP_LEN // CMP_STRIDE + 1
            n_slc = s_p // SLC_BLOCK
            oc, sel = cmp_attention_select_prompt(qp, kcc, vcc, b_p, n_cmp, n_slc, SELECT_ROWS)
            osl, ow = prompt_slc_win_attention(qp, sel, ks, vs, kw, vw, b_p, ATTN_ROWS)
            xp = nsa_merge_residual_ln(oc, osl, ow, gp, e_gate, w_out, xp, ln_mix_g[layer], ln_mix_b[layer])
            qs, gs, *kv_s = nsa_projection(xs, 1, w_q, w_kvt, F32)
            kc, vc, ks, vs, kw, vw = [a.reshape(NSA_KV, NSA_DH, b_s, t_pad).transpose(2, 0, 1, 3) for a in kv_s]
            new_rows = lambda a_t: _token_major(a_t[..., :t_s])
            outs["cmp_k_s"].append(new_rows(kc))
            outs["cmp_v_s"].append(new_rows(vc))
            outs["slc_k_s"].append(new_rows(ks))
            outs["slc_v_s"].append(new_rows(vs))
            win_k = _feature_major(cache_win_k[o])
            win_v = _feature_major(cache_win_v[o])
            wb = win_k.shape[-1]
            keep = min(WINDOW, wb + t_s)
            outs["win_k_s"].append(_token_major(jnp.concatenate([win_k, kw[..., :t_s]], axis=-1)[..., -keep:]))
            outs["win_v_s"].append(_token_major(jnp.concatenate([win_v, vw[..., :t_s]], axis=-1)[..., -keep:]))
            kcc = compress(_feature_major(cache_cmp_k[o]), page_table, pe[0], w1[0], w2[0])
            vcc = compress(_feature_major(cache_cmp_v[o]), page_table, pe[1], w1[1], w2[1])
            n_cmp = (past + t_s) // CMP_STRIDE - CMP_LEN // CMP_STRIDE + 1
            n_slc = -(-(past + t_s) // SLC_BLOCK)
            oc, idx = cmp_attention_select(qs, kcc, vcc, b_s, n_cmp, n_slc, past)
            q_rows = _group_rows(qs, b_s, t_pad)
            osl = sample_slc_attention(q_rows, idx, page_table, _feature_major(cache_slc_k[o]),
                                       _feature_major(cache_slc_v[o]), ks, vs, t_s, past)
            ow = sample_win_attention(q_rows, win_k, win_v, kw, vw, past)
            xs = nsa_merge_residual_ln(oc, _ungroup_rows(osl, b_s, t_pad), _ungroup_rows(ow, b_s, t_pad), gs,
                                       e_gate, w_out, xs, ln_mix_g[layer], ln_mix_b[layer])
        w_up = ffn_w_up[layer].astype(BF16)
        w_down = ffn_w_down[layer].astype(BF16)
        hp, hist_p = ffn_up_sequences(xp, b_p, jnp.zeros((b_p, FFN_W - 1, d_ff), F32), w_up,
                                      ffn_conv_w[layer], ffn_conv_b[layer])
        outs["ffn_p"].append(hist_p)
        xp = matmul_residual_ln(hp, w_down, xp, ln_ffn_g[layer], ln_ffn_b[layer])
        hs, a_s = ffn_up_short(xs, state_ffn_conv[layer], w_up, ffn_conv_w[layer], ffn_conv_b[layer])
        outs["ffn_s"].append(a_s.reshape(b_s, t_pad, d_ff)[:, t_s - (FFN_W - 1):t_s])
        xs = matmul_residual_ln(hs, w_down, xs, ln_ffn_g[layer], ln_ffn_b[layer])

    st = jnp.stack
    y_p = xp.reshape(b_p, s_p, d_model)
    y_s = xs.reshape(b_s, t_pad, d_model)[:, :t_s]
    order = ("sconv_p", "sconv_s", "ret_p", "ret_s", "cmp_k_p", "cmp_v_p", "slc_k_p", "slc_v_p",
             "cmp_k_s", "cmp_v_s", "slc_k_s", "slc_v_s", "win_k_p", "win_v_p", "win_k_s", "win_v_s",
             "ffn_p", "ffn_s")
    return (y_p, y_s) + tuple(st(outs[k]) for k in order)
```

```python
import functools

import numpy as np
import jax
import jax.numpy as jnp
from jax import lax
from jax.experimental import pallas as pl
from jax.experimental.pallas import tpu as pltpu

F32 = jnp.float32
BF16 = jnp.bfloat16

SUBLANES = 8
LANES = 128
VMEM_LIMIT_BYTES = 56 * 1024 * 1024
MATMUL_ROWS = 512
MERGE_ROWS = 256
ATTN_ROWS = 256
SELECT_ROWS = 1024
COMPRESS_PAGES = 32

DEPTH = 2
SCONV_W = 3
RET_HEADS = 4
RET_CHUNK = 128
ROPE_BASE = 10000.0
NSA_HEADS = 16
NSA_KV = 4
NSA_GROUP = NSA_HEADS // NSA_KV
NSA_DH = 64
CMP_LEN = 32
CMP_STRIDE = 16
SLC_BLOCK = 64
SLC_TOPN = 16
WINDOW = 512
PAGE_SIZE = 128
FFN_W = 3
ALPHA = (2.0 * DEPTH) ** 0.25
LN_EPS = 1e-5
NEG_INF = -1e30
REMOVED = -3e38
FORCE_BONUS = 1e4
SLOT = 2 * NSA_DH


def _cparams(n_grid):
    return pltpu.CompilerParams(dimension_semantics=("arbitrary",) * n_grid,
                                vmem_limit_bytes=VMEM_LIMIT_BYTES)


def _row_tile(m, want):
    t = min(m, want)
    assert m % t == 0, (m, t)
    return t


def _nt_dot(a, b):
    return lax.dot_general(a, b, (((1,), (1,)), ((), ())), preferred_element_type=F32)


def _tn_dot(a, b):
    return lax.dot_general(a, b, (((0,), (0,)), ((), ())), preferred_element_type=F32)


def _dot(a, b):
    return jnp.dot(a, b, preferred_element_type=F32)


def _gelu(x):
    return 0.5 * x * (1.0 + jnp.tanh(np.float32(np.sqrt(2.0 / np.pi)) * (x + 0.044715 * (x * x * x))))


def _layer_norm_rows(r, g, b):
    mu = jnp.mean(r, axis=-1, keepdims=True)
    d = r - mu
    var = jnp.mean(d * d, axis=-1, keepdims=True)
    return d * lax.rsqrt(var + LN_EPS) * g + b


def _mm_split_kernel(x_ref, w_ref, *o_refs, cuts):
    acc = _dot(x_ref[...].astype(BF16), w_ref[...])
    for o_ref, (lo, hi) in zip(o_refs, cuts):
        o_ref[...] = acc[:, lo:hi].astype(o_ref.dtype)


def matmul_split(x, w_bf16, widths, dtypes, tm=MATMUL_ROWS):
    m, k = x.shape
    n = w_bf16.shape[1]
    assert sum(widths) == n and all(wd % LANES == 0 for wd in widths)
    tm = _row_tile(m, tm)
    cuts, lo = [], 0
    for wd in widths:
        cuts.append((lo, lo + wd))
        lo += wd
    return pl.pallas_call(
        functools.partial(_mm_split_kernel, cuts=tuple(cuts)),
        grid=(m // tm,),
        in_specs=[pl.BlockSpec((tm, k), lambda i: (i, 0)),
                  pl.BlockSpec((k, n), lambda i: (0, 0))],
        out_specs=[pl.BlockSpec((tm, wd), lambda i: (i, 0)) for wd in widths],
        out_shape=[jax.ShapeDtypeStruct((m, wd), dt) for wd, dt in zip(widths, dtypes)],
        compiler_params=_cparams(1),
        name="matmul_split",
    )(x, w_bf16)


def _nsa_proj_kernel(x_ref, wq_ref, wkvt_ref, q_ref, g_ref, *kv_refs, nq):
    xb = x_ref[...].astype(BF16)
    acc = _dot(xb, wq_ref[...])
    q_ref[...] = acc[:, :nq].astype(q_ref.dtype)
    g_ref[...] = acc[:, nq:]
    acc_t = _nt_dot(wkvt_ref[...], xb)
    gd = acc_t.shape[0] // len(kv_refs)
    for i, r in enumerate(kv_refs):
        r[0] = acc_t[i * gd:(i + 1) * gd, :]


def nsa_projection(x, n_seq, wq_bf16, wkvt_bf16, q_dtype, n_kv=6, tm=MATMUL_ROWS):
    m, d = x.shape
    seq = m // n_seq
    tm = _row_tile(seq, tm)
    nt = seq // tm
    nq = wq_bf16.shape[1] - LANES
    gd = wkvt_bf16.shape[0] // n_kv
    return pl.pallas_call(
        functools.partial(_nsa_proj_kernel, nq=nq),
        grid=(n_seq, nt),
        in_specs=[pl.BlockSpec((tm, d), lambda s, i: (s * nt + i, 0)),
                  pl.BlockSpec((d, nq + LANES), lambda s, i: (0, 0)),
                  pl.BlockSpec((n_kv * gd, d), lambda s, i: (0, 0))],
        out_specs=[pl.BlockSpec((tm, nq), lambda s, i: (s * nt + i, 0)),
                   pl.BlockSpec((tm, LANES), lambda s, i: (s * nt + i, 0))] + [
                      pl.BlockSpec((1, gd, tm), lambda s, i: (s, 0, i)) for _ in range(n_kv)],
        out_shape=[jax.ShapeDtypeStruct((m, nq), q_dtype), jax.ShapeDtypeStruct((m, LANES), F32)] + [
            jax.ShapeDtypeStruct((n_seq, gd, seq), F32) for _ in range(n_kv)],
        compiler_params=_cparams(2),
        name="nsa_projection",
    )(x, wq_bf16, wkvt_bf16)


def _mm_res_ln_kernel(a_ref, w_ref, x_ref, g_ref, b_ref, o_ref):
    y = _dot(a_ref[...].astype(BF16), w_ref[...])
    o_ref[...] = _layer_norm_rows(ALPHA * x_ref[...] + y, g_ref[...], b_ref[...])


def matmul_residual_ln(a, w_bf16, x, g, b, tm=MATMUL_ROWS):
    m, k = a.shape
    d = w_bf16.shape[1]
    tm = _row_tile(m, tm)
    return pl.pallas_call(
        _mm_res_ln_kernel,
        grid=(m // tm,),
        in_specs=[pl.BlockSpec((tm, k), lambda i: (i, 0)),
                  pl.BlockSpec((k, d), lambda i: (0, 0)),
                  pl.BlockSpec((tm, d), lambda i: (i, 0)),
                  pl.BlockSpec((1, d), lambda i: (0, 0)),
                  pl.BlockSpec((1, d), lambda i: (0, 0))],
        out_specs=pl.BlockSpec((tm, d), lambda i: (i, 0)),
        out_shape=jax.ShapeDtypeStruct((m, d), F32),
        compiler_params=_cparams(1),
        name="matmul_residual_ln",
    )(a, w_bf16, x, g.reshape(1, d), b.reshape(1, d))


def _expand_gates(gates_raw, e_ref):
    sig = jax.nn.sigmoid(gates_raw)
    hi = sig.astype(BF16)
    lo = (sig - hi.astype(F32)).astype(BF16)
    e = e_ref[...]
    return _dot(hi, e) + _dot(lo, e)


def _nsa_merge_ln_kernel(oc_ref, os_ref, ow_ref, gt_ref, e_ref, w_ref, x_ref, g_ref, b_ref, o_ref, *, k):
    gx = _expand_gates(gt_ref[...], e_ref)
    o = gx[:, 0:k] * oc_ref[...] + gx[:, k:2 * k] * os_ref[...] + gx[:, 2 * k:3 * k] * ow_ref[...]
    y = _dot(o.astype(BF16), w_ref[...])
    o_ref[...] = _layer_norm_rows(ALPHA * x_ref[...] + y, g_ref[...], b_ref[...])


def nsa_merge_residual_ln(oc, osl, ow, gates, e_bf16, w_bf16, x, g, b, tm=MERGE_ROWS):
    m, k = oc.shape
    d = w_bf16.shape[1]
    tm = _row_tile(m, tm)
    row = lambda i: (i, 0)
    fixed = lambda i: (0, 0)
    return pl.pallas_call(
        functools.partial(_nsa_merge_ln_kernel, k=k),
        grid=(m // tm,),
        in_specs=[pl.BlockSpec((tm, k), row), pl.BlockSpec((tm, k), row), pl.BlockSpec((tm, k), row),
                  pl.BlockSpec((tm, LANES), row),
                  pl.BlockSpec((LANES, 3 * k), fixed),
                  pl.BlockSpec((k, d), fixed),
                  pl.BlockSpec((tm, d), row),
                  pl.BlockSpec((1, d), fixed), pl.BlockSpec((1, d), fixed)],
        out_specs=pl.BlockSpec((tm, d), row),
        out_shape=jax.ShapeDtypeStruct((m, d), F32),
        compiler_params=_cparams(1),
        name="nsa_merge_residual_ln",
    )(oc, osl, ow, gates, e_bf16, w_bf16, x, g.reshape(1, d), b.reshape(1, d))


def _even_mixer_kernel(z_ref, hist_ref, st_ref, cos_ref, sin_ref, decay_ref, qdec_ref, kdec_ref, sdec_ref,
                       cw_ref, cb_ref, gn_ref, y_ref, hist_out_ref, st_out_ref, carry, state,
                       *, rows, valid, dconv, dk):
    c = pl.program_id(1)
    r0 = valid - 2 - (rows - SUBLANES)

    @pl.when(c == 0)
    def _():
        carry[r0:r0 + 2, :] = hist_ref[0]
        state[...] = st_ref[0]

    d = dconv
    h = z_ref[:, 0:d]
    gate_b = z_ref[:, d:2 * d]
    gate_c = z_ref[:, 2 * d:3 * d]
    ch = gate_c * h
    row = lax.broadcasted_iota(jnp.int32, (rows, d), 0)
    h0 = carry[r0:r0 + 1, :]
    h1 = carry[r0 + 1:r0 + 2, :]
    m1 = jnp.where(row == 0, h1, pltpu.roll(ch, 1, 0))
    m2 = jnp.where(row == 0, h0, jnp.where(row == 1, h1, pltpu.roll(ch, 2, 0)))
    u = ((cb_ref[...] + m2 * cw_ref[0:1, :]) + m1 * cw_ref[1:2, :]) + ch * cw_ref[2:3, :]
    y_ref[:, 0:d] = gate_b * u
    carry[...] = ch[rows - SUBLANES:rows, :]
    hist_out_ref[0] = carry[r0:r0 + 2, :]

    cosf = cos_ref[...]
    sinf = sin_ref[...]
    scale = np.float32(dk ** -0.5)
    for hh in range(RET_HEADS):
        q = z_ref[:, 3 * d + hh * dk:3 * d + (hh + 1) * dk]
        k = z_ref[:, 4 * d + hh * dk:4 * d + (hh + 1) * dk]
        v = z_ref[:, 5 * d + hh * dk:5 * d + (hh + 1) * dk]
        gsw = z_ref[:, 6 * d + hh * dk:6 * d + (hh + 1) * dk]
        q = (q * cosf + pltpu.roll(q, dk // 2, 1) * sinf) * scale
        k = k * cosf + pltpu.roll(k, dk // 2, 1) * sinf
        qb = q.astype(BF16)
        vb = v.astype(BF16)
        s_old = state[hh]
        scores = _nt_dot(qb, k.astype(BF16)) * decay_ref[hh]
        intra = _dot(scores.astype(BF16), vb)
        cross = _dot(qb, s_old.astype(BF16)) * qdec_ref[hh]
        kd = (k * kdec_ref[hh]).astype(BF16)
        state[hh] = s_old * sdec_ref[hh] + _tn_dot(kd, vb)
        o = intra + cross
        mu = jnp.mean(o, axis=-1, keepdims=True)
        dv = o - mu
        var = jnp.mean(dv * dv, axis=-1, keepdims=True)
        on = dv * lax.rsqrt(var + LN_EPS) * gn_ref[:, hh * dk:(hh + 1) * dk]
        y_ref[:, d + hh * dk:d + (hh + 1) * dk] = (gsw * jax.nn.sigmoid(gsw)) * on
    st_out_ref[0] = state[...]


def _retention_tables(rows, valid, dk):
    log_gamma = jnp.log1p(-jnp.exp2(-5.0 - jnp.arange(RET_HEADS, dtype=F32)))
    n = jnp.arange(rows, dtype=F32)
    diff = n[:, None] - n[None, :]
    lg = log_gamma[:, None, None]
    decay = jnp.where(diff >= 0, jnp.exp(lg * jnp.maximum(diff, 0.0)), 0.0)
    q_dec = jnp.exp((n[None, :] + 1.0) * log_gamma[:, None])
    k_dec = jnp.where(n[None, :] < valid, jnp.exp((valid - 1.0 - n[None, :]) * log_gamma[:, None]), 0.0)
    s_dec = jnp.exp(valid * log_gamma)
    bc = lambda a: jnp.broadcast_to(a[:, :, None], (RET_HEADS, rows, dk))
    return decay, bc(q_dec), bc(k_dec), jnp.broadcast_to(s_dec[:, None, None], (RET_HEADS, 1, dk))


def _rope_tables(pos, dk):
    half = dk // 2
    inv = ROPE_BASE ** (-jnp.arange(half, dtype=F32) / half)
    ang = pos.astype(F32)[:, None] * inv
    cos, sin = jnp.cos(ang), jnp.sin(ang)
    return jnp.concatenate([cos, cos], axis=-1), jnp.concatenate([-sin, sin], axis=-1)


def even_mixer(z, hist, st, pos, rows, valid, conv_w, conv_b, gn_g):
    n_seq, _, dconv = hist.shape
    dk = st.shape[-1]
    n_chunks = z.shape[0] // (n_seq * rows)
    cosf, sinf = _rope_tables(pos, dk)
    decay, q_dec, k_dec, s_dec = _retention_tables(rows, valid, dk)
    fixed3 = lambda s, c: (0, 0, 0)
    fixed2 = lambda s, c: (0, 0)
    return pl.pallas_call(
        functools.partial(_even_mixer_kernel, rows=rows, valid=valid, dconv=dconv, dk=dk),
        grid=(n_seq, n_chunks),
        in_specs=[pl.BlockSpec((rows, 7 * dconv), lambda s, c: (s * n_chunks + c, 0)),
                  pl.BlockSpec((1, 2, dconv), lambda s, c: (s, 0, 0)),
                  pl.BlockSpec((1, RET_HEADS, dk, dk), lambda s, c: (s, 0, 0, 0)),
                  pl.BlockSpec((rows, dk), lambda s, c: (c, 0)),
                  pl.BlockSpec((rows, dk), lambda s, c: (c, 0)),
                  pl.BlockSpec((RET_HEADS, rows, rows), fixed3),
                  pl.BlockSpec((RET_HEADS, rows, dk), fixed3),
                  pl.BlockSpec((RET_HEADS, rows, dk), fixed3),
                  pl.BlockSpec((RET_HEADS, 1, dk), fixed3),
                  pl.BlockSpec((SCONV_W, dconv), fixed2),
                  pl.BlockSpec((1, dconv), fixed2),
                  pl.BlockSpec((1, RET_HEADS * dk), fixed2)],
        out_specs=[pl.BlockSpec((rows, 2 * dconv), lambda s, c: (s * n_chunks + c, 0)),
                   pl.BlockSpec((1, 2, dconv), lambda s, c: (s, 0, 0)),
                   pl.BlockSpec((1, RET_HEADS, dk, dk), lambda s, c: (s, 0, 0, 0))],
        out_shape=[jax.ShapeDtypeStruct((z.shape[0], 2 * dconv), F32),
                   jax.ShapeDtypeStruct((n_seq, 2, dconv), F32),
                   jax.ShapeDtypeStruct((n_seq, RET_HEADS, dk, dk), F32)],
        scratch_shapes=[pltpu.VMEM((SUBLANES, dconv), F32), pltpu.VMEM((RET_HEADS, dk, dk), F32)],
        compiler_params=_cparams(2),
        name="even_mixer",
    )(z, hist, st, cosf, sinf, decay, q_dec, k_dec, s_dec, conv_w, conv_b.reshape(1, dconv),
      gn_g.reshape(1, RET_HEADS * dk))


def _conv_gate(a, gate, m1, m2, cw_ref, cb_ref):
    conv = ((cb_ref[...] + m2 * cw_ref[0:1, :]) + m1 * cw_ref[1:2, :]) + a * cw_ref[2:3, :]
    return _gelu(conv) * gate


def _ffn_up_seq_kernel(x_ref, wa_ref, wg_ref, h_ref, cw_ref, cb_ref, o_ref, hist_out_ref, carry, *, tm):
    @pl.when(pl.program_id(2) == 0)
    def _():
        carry[SUBLANES - 2:SUBLANES, :] = h_ref[0]

    xb = x_ref[...].astype(BF16)
    a = _dot(xb, wa_ref[...])
    gate = _dot(xb, wg_ref[...])
    row = lax.broadcasted_iota(jnp.int32, a.shape, 0)
    h0 = carry[SUBLANES - 2:SUBLANES - 1, :]
    h1 = carry[SUBLANES - 1:SUBLANES, :]
    m1 = jnp.where(row == 0, h1, pltpu.roll(a, 1, 0))
    m2 = jnp.where(row == 0, h0, jnp.where(row == 1, h1, pltpu.roll(a, 2, 0)))
    o_ref[...] = _conv_gate(a, gate, m1, m2, cw_ref, cb_ref).astype(o_ref.dtype)
    carry[...] = a[tm - SUBLANES:tm, :]
    hist_out_ref[0] = carry[SUBLANES - 2:SUBLANES, :]


def ffn_up_sequences(x, n_seq, hist, w_up_bf16, conv_w, conv_b, tm=MATMUL_ROWS, n_col=2):
    m, k = x.shape
    dff = conv_w.shape[1]
    seq = m // n_seq
    tm = _row_tile(seq, tm)
    tps = seq // tm
    tn = dff // n_col
    assert tn % LANES == 0
    return pl.pallas_call(
        functools.partial(_ffn_up_seq_kernel, tm=tm),
        grid=(n_col, n_seq, tps),
        in_specs=[pl.BlockSpec((tm, k), lambda j, s, i: (s * tps + i, 0)),
                  pl.BlockSpec((k, tn), lambda j, s, i: (0, j)),
                  pl.BlockSpec((k, tn), lambda j, s, i: (0, j + n_col)),
                  pl.BlockSpec((1, 2, tn), lambda j, s, i: (s, 0, j)),
                  pl.BlockSpec((FFN_W, tn), lambda j, s, i: (0, j)),
                  pl.BlockSpec((1, tn), lambda j, s, i: (0, j))],
        out_specs=[pl.BlockSpec((tm, tn), lambda j, s, i: (s * tps + i, j)),
                   pl.BlockSpec((1, 2, tn), lambda j, s, i: (s, 0, j))],
        out_shape=[jax.ShapeDtypeStruct((m, dff), BF16),
                   jax.ShapeDtypeStruct((n_seq, 2, dff), F32)],
        scratch_shapes=[pltpu.VMEM((SUBLANES, tn), F32)],
        compiler_params=_cparams(3),
        name="ffn_up_sequences",
    )(x, w_up_bf16, w_up_bf16, hist, conv_w, conv_b.reshape(1, dff))


def _ffn_up_short_kernel(x_ref, wa_ref, wg_ref, h1_ref, h2_ref, cw_ref, cb_ref, o_ref, a_ref):
    xb = x_ref[...].astype(BF16)
    a = _dot(xb, wa_ref[...])
    gate = _dot(xb, wg_ref[...])
    t = lax.broadcasted_iota(jnp.int32, a.shape, 0) % SUBLANES
    m1 = jnp.where(t == 0, h1_ref[...], pltpu.roll(a, 1, 0))
    m2 = jnp.where(t < 2, h2_ref[...], pltpu.roll(a, 2, 0))
    o_ref[...] = _conv_gate(a, gate, m1, m2, cw_ref, cb_ref).astype(o_ref.dtype)
    a_ref[...] = a


def ffn_up_short(x, hist, w_up_bf16, conv_w, conv_b, n_col=2):
    m, k = x.shape
    dff = conv_w.shape[1]
    n_seq = m // SUBLANES
    tn = dff // n_col
    zeros = jnp.zeros((n_seq, SUBLANES, dff), F32)
    h1 = zeros.at[:, 0].set(hist[:, 1]).reshape(m, dff)
    h2 = zeros.at[:, 0].set(hist[:, 0]).at[:, 1].set(hist[:, 1]).reshape(m, dff)
    col = lambda j: (0, j)
    return pl.pallas_call(
        _ffn_up_short_kernel,
        grid=(n_col,),
        in_specs=[pl.BlockSpec((m, k), lambda j: (0, 0)),
                  pl.BlockSpec((k, tn), col),
                  pl.BlockSpec((k, tn), lambda j: (0, j + n_col)),
                  pl.BlockSpec((m, tn), col), pl.BlockSpec((m, tn), col),
                  pl.BlockSpec((FFN_W, tn), col), pl.BlockSpec((1, tn), col)],
        out_specs=[pl.BlockSpec((m, tn), col), pl.BlockSpec((m, tn), col)],
        out_shape=[jax.ShapeDtypeStruct((m, dff), F32), jax.ShapeDtypeStruct((m, dff), F32)],
        compiler_params=_cparams(1),
        name="ffn_up_short",
    )(x, w_up_bf16, w_up_bf16, h1, h2, conv_w, conv_b.reshape(1, dff))


def _compress_kernel(pt_ref, *refs, pages):
    page_refs = refs[:pages + 1]
    w2t_ref, pecol_ref, w1_ref, w2_ref, o_ref = refs[pages + 1:pages + 6]
    rows_refs = refs[pages + 6:]
    parts = len(rows_refs)
    ppp = (pages + 1) // parts
    cpp = PAGE_SIZE // CMP_STRIDE
    n = (pages + 1) * cpp
    n_p = ppp * cpp
    hidden = w1_ref.shape[1]
    gpr = LANES // NSA_DH
    pieces = NSA_KV // gpr
    for i, r in enumerate(page_refs):
        for pc in range(pieces):
            tile = r[0, pc * gpr:(pc + 1) * gpr].reshape(LANES, PAGE_SIZE)
            rows_refs[i // ppp][pc, (i % ppp) * PAGE_SIZE:(i % ppp + 1) * PAGE_SIZE, :] = tile.T
    pe_term = jnp.sum(pecol_ref[...] * w1_ref[...], axis=0, keepdims=True)
    accs = []
    for rows_ref in rows_refs:
        acc = jnp.zeros((pieces * n_p, gpr * 2 * hidden), F32)
        for tp in range(CMP_STRIDE // 2):
            lhs = jnp.concatenate(
                [jnp.concatenate([rows_ref[pc, pl.ds(2 * tp, n_p, stride=CMP_STRIDE), :],
                                  rows_ref[pc, pl.ds(2 * tp + 1, n_p, stride=CMP_STRIDE), :]], axis=1)
                 for pc in range(pieces)], axis=0)
            acc = acc + _dot(lhs.astype(BF16), w2t_ref[tp])
        accs.append(acc)
    for pc in range(pieces):
        for gl in range(gpr):
            a = jnp.concatenate([acc[pc * n_p:(pc + 1) * n_p, gl * 2 * hidden:(gl + 1) * 2 * hidden]
                                 for acc in accs], axis=0)
            nxt = pltpu.roll(a, n - 1, 0)
            pre = pe_term + a[:, 0:hidden]
            pre = pre + nxt[:, hidden:2 * hidden]
            o_ref[0, pc * gpr + gl] = _dot(_gelu(pre[0:pages * cpp]).astype(BF16), w2_ref[...])


def compress(rows_t, page_table, pe, w1, w2, pages=COMPRESS_PAGES):
    pooled = page_table is not None
    if pooled:
        n_seq, n_pages = page_table.shape
    else:
        n_seq, n_pages = rows_t.shape[0], rows_t.shape[3] // PAGE_SIZE
        page_table = jnp.zeros((1, 1), jnp.int32)
    pages = min(pages, n_pages)
    assert n_pages % pages == 0
    parts = 3 if (pages + 1) % 3 == 0 else 1
    hidden = w1.shape[1]
    cpp = PAGE_SIZE // CMP_STRIDE
    r = CMP_LEN // CMP_STRIDE
    gpr = LANES // NSA_DH
    assert r == 2 and gpr == 2
    w1p = w1.reshape(r, CMP_STRIDE, NSA_DH, hidden)
    w16 = jnp.concatenate([w1p[0], w1p[1]], axis=-1)
    zero = jnp.zeros_like(w16)
    per_tok = jnp.concatenate([jnp.concatenate([w16, zero], axis=-1),
                               jnp.concatenate([zero, w16], axis=-1)], axis=1)
    w2t = per_tok.reshape(CMP_STRIDE // 2, 2 * LANES, gpr * 2 * hidden).astype(BF16)
    w2p = jnp.pad(w2, ((0, 0), (0, SLOT - NSA_DH))).astype(BF16)
    pecol = pe.reshape(CMP_LEN * NSA_DH, 1)

    def page_map(i):
        if pooled:
            return lambda s, j, pt: (pt[s, jnp.minimum(j * pages + i, n_pages - 1)], 0, 0, 0)
        return lambda s, j, pt: (s, 0, 0, jnp.minimum(j * pages + i, n_pages - 1))

    fixed2 = lambda s, j, pt: (0, 0)
    grid_spec = pltpu.PrefetchScalarGridSpec(
        num_scalar_prefetch=1,
        grid=(n_seq, n_pages // pages),
        in_specs=[pl.BlockSpec((1, NSA_KV, NSA_DH, PAGE_SIZE), page_map(i)) for i in range(pages + 1)] + [
            pl.BlockSpec((CMP_STRIDE // 2, 2 * LANES, gpr * 2 * hidden), lambda s, j, pt: (0, 0, 0)),
            pl.BlockSpec((CMP_LEN * NSA_DH, 1), fixed2),
            pl.BlockSpec((CMP_LEN * NSA_DH, hidden), fixed2),
            pl.BlockSpec((hidden, SLOT), fixed2)],
        out_specs=pl.BlockSpec((1, NSA_KV, pages * cpp, SLOT), lambda s, j, pt: (s, 0, j, 0)),
        scratch_shapes=[pltpu.VMEM((NSA_KV // gpr, (pages + 1) // parts * PAGE_SIZE, LANES), F32)
                        for _ in range(parts)],
    )
    return pl.pallas_call(
        functools.partial(_compress_kernel, pages=pages),
        grid_spec=grid_spec,
        out_shape=jax.ShapeDtypeStruct((n_seq, NSA_KV, n_pages * cpp, SLOT), F32),
        compiler_params=_cparams(2),
        name="compress",
    )(page_table, *([rows_t] * (pages + 1)), w2t, pecol, w1, w2p)


def _cmp_select_kernel(q_ref, kc_ref, vc_ref, ov_ref, o_ref, idx_ref, *, tq, n_cmp, n_slc, pos0):
    ncp = kc_ref.shape[2]
    nsp = ov_ref.shape[0]
    hrows = NSA_GROUP * tq
    q_pos = pos0 + (lax.broadcasted_iota(jnp.int32, (hrows, ncp), 0) & (tq - 1))
    blk_i = lax.broadcasted_iota(jnp.int32, (hrows, ncp), 1)
    valid = (blk_i * CMP_STRIDE + (CMP_LEN - 1) <= q_pos) & (blk_i < n_cmp)
    ov = ov_ref[...]
    groups = range(NSA_KV)
    heads = [range(g * NSA_GROUP, (g + 1) * NSA_GROUP) for g in groups]
    scores = [_nt_dot(jnp.concatenate([q_ref[:, h * SLOT:(h + 1) * SLOT] for h in heads[g]], axis=0).astype(BF16),
                      kc_ref[0, g].astype(BF16)) for g in groups]
    probs = []
    for g in groups:
        s = jnp.where(valid, scores[g], NEG_INF)
        m = jnp.max(s, axis=-1, keepdims=True)
        e = jnp.where(valid, jnp.exp(s - m), 0.0)
        den = jnp.sum(e, axis=-1, keepdims=True)
        probs.append(e * (1.0 / jnp.where(den > 0.0, den, 1.0)))
    outs = [_dot(probs[g].astype(BF16), vc_ref[0, g].astype(BF16)) for g in groups]
    imps = []
    for g in groups:
        p_sum = jnp.zeros((tq, ncp), F32)
        for j, h in enumerate(heads[g]):
            o_ref[:, h * SLOT:(h + 1) * SLOT] = outs[g][j * tq:(j + 1) * tq]
            p_sum = p_sum + probs[g][j * tq:(j + 1) * tq]
        hi = p_sum.astype(BF16)
        lo = (p_sum - hi.astype(F32)).astype(BF16)
        imps.append(_nt_dot(ov, hi) + _nt_dot(ov, lo))
    cols = NSA_KV * tq
    imp = jnp.concatenate(imps, axis=1)
    blk = lax.broadcasted_iota(jnp.int32, (nsp, cols), 0)
    cur = (pos0 + (lax.broadcasted_iota(jnp.int32, (nsp, cols), 1) & (tq - 1))) // SLC_BLOCK
    real = blk < n_slc
    causal = real & (blk <= cur)
    forced = (blk == 0) | (blk == cur) | (blk == cur - 1)
    score = jnp.where(causal, imp + jnp.where(forced, FORCE_BONUS, 0.0), NEG_INF)
    score = jnp.where(real, score, REMOVED)
    idx = jnp.zeros((SLC_TOPN, cols), jnp.int32)
    idx_row = lax.broadcasted_iota(jnp.int32, (SLC_TOPN, cols), 0)
    for it in range(min(SLC_TOPN, n_slc)):
        m = jnp.max(score, axis=0, keepdims=True)
        first = jnp.min(jnp.where(score == m, blk, nsp), axis=0, keepdims=True)
        score = jnp.where(blk == first, REMOVED, score)
        idx = jnp.where(idx_row == it, first, idx)
    idx_ref[0] = idx


def cmp_block_overlap(n_cmp_pad, n_cmp, n_slc, n_slc_pad, lane_off):
    i = np.arange(n_cmp_pad)[:, None]
    j = np.arange(n_slc_pad)[None, :] - lane_off
    start = i * CMP_STRIDE
    hit = (start < (j + 1) * SLC_BLOCK) & (start + CMP_LEN > j * SLC_BLOCK) & (i < n_cmp) & (j >= 0) & (j < n_slc)
    return jnp.asarray(hit.astype(np.float32), dtype=BF16)


def cmp_attention_select(q_slots, kcc, vcc, n_seq, n_cmp, n_slc, pos0):
    tokens = q_slots.shape[0]
    tq = tokens // n_seq
    assert tq & (tq - 1) == 0
    ncp = kcc.shape[2]
    nsp = -(-n_slc // SUBLANES) * SUBLANES
    ov_t = cmp_block_overlap(ncp, n_cmp, n_slc, nsp, 0).T
    hw = NSA_HEADS * SLOT
    o_cmp, idx_t = pl.pallas_call(
        functools.partial(_cmp_select_kernel, tq=tq, n_cmp=n_cmp, n_slc=n_slc, pos0=pos0),
        grid=(n_seq,),
        in_specs=[pl.BlockSpec((tq, hw), lambda s: (s, 0)),
                  pl.BlockSpec((1, NSA_KV, ncp, SLOT), lambda s: (s, 0, 0, 0)),
                  pl.BlockSpec((1, NSA_KV, ncp, SLOT), lambda s: (s, 0, 0, 0)),
                  pl.BlockSpec((nsp, ncp), lambda s: (0, 0))],
        out_specs=[pl.BlockSpec((tq, hw), lambda s: (s, 0)),
                   pl.BlockSpec((1, SLC_TOPN, NSA_KV * tq), lambda s: (s, 0, 0))],
        out_shape=[jax.ShapeDtypeStruct((tokens, hw), F32),
                   jax.ShapeDtypeStruct((n_seq, SLC_TOPN, NSA_KV * tq), jnp.int32)],
        compiler_params=_cparams(1),
        name="cmp_attention_select",
    )(q_slots, kcc, vcc, ov_t)
    topn = min(SLC_TOPN, n_slc)
    idx = idx_t[:, :topn].reshape(n_seq, topn, NSA_KV, tq).transpose(0, 2, 3, 1)
    return o_cmp, idx


def _cmp_select_prompt_kernel(q_ref, kc_ref, vc_ref, ovt_ref, o_ref, sel_ref, *, tq, n_cmp, n_slc):
    t0 = pl.program_id(2) * tq
    ncp = kc_ref.shape[2]
    nsr = ovt_ref.shape[0]
    kc = kc_ref[0, 0].astype(BF16)
    vc = vc_ref[0, 0].astype(BF16)
    q_pos = t0 + lax.broadcasted_iota(jnp.int32, (ncp, tq), 1)
    blk_i = lax.broadcasted_iota(jnp.int32, (ncp, tq), 0)
    valid = (blk_i * CMP_STRIDE + (CMP_LEN - 1) <= q_pos) & (blk_i < n_cmp)
    p_sum = jnp.zeros((ncp, tq), F32)
    for j in range(NSA_GROUP):
        s = jnp.where(valid, _nt_dot(kc, q_ref[:, j * SLOT:(j + 1) * SLOT]), NEG_INF)
        m = jnp.max(s, axis=0, keepdims=True)
        e = jnp.where(valid, jnp.exp(s - m), 0.0)
        den = jnp.sum(e, axis=0, keepdims=True)
        p = e * (1.0 / jnp.where(den > 0.0, den, 1.0))
        o_ref[:, j * SLOT:(j + 1) * SLOT] = _tn_dot(p.astype(BF16), vc)
        p_sum = p_sum + p
    hi = p_sum.astype(BF16)
    lo = (p_sum - hi.astype(F32)).astype(BF16)
    ovt = ovt_ref[...]
    imp = _dot(ovt, hi) + _dot(ovt, lo)
    blk = lax.broadcasted_iota(jnp.int32, (nsr, tq), 0)
    cur = (t0 + lax.broadcasted_iota(jnp.int32, (nsr, tq), 1)) // SLC_BLOCK
    real = blk < n_slc
    causal = real & (blk <= cur)
    forced = (blk == 0) | (blk == cur) | (blk == cur - 1)
    score = jnp.where(causal, imp + jnp.where(forced, FORCE_BONUS, 0.0), NEG_INF)
    score = jnp.where(real, score, REMOVED)
    picked = jnp.zeros((nsr, tq), jnp.bool_)
    for _ in range(min(SLC_TOPN, n_slc)):
        m = jnp.max(score, axis=0, keepdims=True)
        first = jnp.min(jnp.where(score == m, blk, nsr), axis=0, keepdims=True)
        hit = blk == first
        picked = picked | hit
        score = jnp.where(hit, REMOVED, score)
    bias_t = jnp.where(real & ~(picked & causal), NEG_INF, 0.0)
    slot_t = jnp.concatenate([jnp.zeros((NSA_DH, tq), F32), bias_t], axis=0)
    sel_ref[0, 0] = slot_t.T


def cmp_attention_select_prompt(q_slots, kcc, vcc, n_seq, n_cmp, n_slc, tq):
    tokens = q_slots.shape[0]
    t = tokens // n_seq
    tq = _row_tile(t, tq)
    nt = t // tq
    ncp = kcc.shape[2]
    nsr = SLOT - NSA_DH
    assert n_slc <= nsr
    ovt = cmp_block_overlap(ncp, n_cmp, n_slc, nsr, 0).T
    gw = NSA_GROUP * SLOT
    return pl.pallas_call(
        functools.partial(_cmp_select_prompt_kernel, tq=tq, n_cmp=n_cmp, n_slc=n_slc),
        grid=(n_seq, NSA_KV, nt),
        in_specs=[pl.BlockSpec((tq, gw), lambda s, g, i: (s * nt + i, g)),
                  pl.BlockSpec((1, 1, ncp, SLOT), lambda s, g, i: (s, g, 0, 0)),
                  pl.BlockSpec((1, 1, ncp, SLOT), lambda s, g, i: (s, g, 0, 0)),
                  pl.BlockSpec((nsr, ncp), lambda s, g, i: (0, 0))],
        out_specs=[pl.BlockSpec((tq, gw), lambda s, g, i: (s * nt + i, g)),
                   pl.BlockSpec((1, 1, tq, SLOT), lambda s, g, i: (s, g, i, 0))],
        out_shape=[jax.ShapeDtypeStruct((tokens, NSA_HEADS * SLOT), F32),
                   jax.ShapeDtypeStruct((n_seq, NSA_KV, t, SLOT), F32)],
        compiler_params=_cparams(3),
        name="cmp_attention_select_prompt",
    )(q_slots, kcc, vcc, ovt)


def _prompt_slc_win_kernel(q_ref, sel_ref, ks_ref, vs_ref, kw_ref, vw_ref, oh_ref, os_ref, ow_ref,
                           m_ref, acc_ref, *, tq, seq):
    acc_rows = NSA_DH + 16
    qi = pl.program_id(2)
    t0 = qi * tq
    rows = NSA_GROUP * tq
    sel = sel_ref[0, 0]
    q_plain = jnp.concatenate([q_ref[:, j * SLOT:(j + 1) * SLOT] for j in range(NSA_GROUP)], axis=0)
    q_aug = jnp.concatenate([(q_ref[:, j * SLOT:(j + 1) * SLOT].astype(F32) + sel).astype(BF16)
                             for j in range(NSA_GROUP)], axis=0)
    zeros_k = jnp.zeros((SLOT - NSA_DH, tq), F32)
    ones_row = (lax.broadcasted_iota(jnp.int32, (acc_rows - NSA_DH, tq), 0) == 0).astype(BF16)
    rel = lax.broadcasted_iota(jnp.int32, (tq, LANES), 1) - lax.broadcasted_iota(jnp.int32, (tq, LANES), 0)

    def scores(q_rows, k_top, k_bottom):
        k_rows = jnp.concatenate([k_top, k_bottom], axis=0).T.astype(BF16)
        return [_nt_dot(k_rows, q_rows[c:c + 2 * LANES]) for c in range(0, rows, 2 * LANES)]

    def update(s_t, v_top, start, mask):
        v_t = jnp.concatenate([v_top.astype(BF16), ones_row], axis=0)
        for cg in range(rows // (2 * LANES)):
            p_parts, a_parts = [], []
            for h in range(2):
                c0 = (2 * cg + h) * LANES
                x = s_t[cg][:, h * LANES:(h + 1) * LANES]
                if mask is not None:
                    lo, hi = mask
                    off = t0 - start + (c0 & (tq - 1))
                    keep = rel >= lo - off
                    if hi is not None:
                        keep = keep & (rel < hi - off)
                    x = jnp.where(keep, x, NEG_INF)
                m_old = m_ref[:, c0:c0 + LANES]
                m_new = jnp.maximum(m_old, jnp.max(x, axis=0, keepdims=True))
                m_ref[:, c0:c0 + LANES] = m_new
                a_parts.append(jnp.exp(m_old - m_new))
                p_parts.append(jnp.exp(x - m_new).astype(BF16))
            c0 = 2 * cg * LANES
            pv = _dot(v_t, jnp.concatenate(p_parts, axis=1))
            acc_ref[:, c0:c0 + 2 * LANES] = jnp.concatenate(a_parts, axis=1) * acc_ref[:, c0:c0 + 2 * LANES] + pv

    def reset():
        m_ref[...] = jnp.full(m_ref.shape, NEG_INF, F32)
        acc_ref[...] = jnp.zeros(acc_ref.shape, F32)

    slot_pad = jnp.zeros((SLOT - NSA_DH, tq), F32)

    def finish(o_ref):
        for j in range(NSA_GROUP):
            a = acc_ref[:, j * tq:(j + 1) * tq]
            o_t = jnp.concatenate([a[0:NSA_DH] * (1.0 / a[NSA_DH:NSA_DH + 1, :]), slot_pad], axis=0)
            o_ref[:, j * SLOT:(j + 1) * SLOT] = o_t.T

    def slc_scores(start):
        return scores(q_aug, ks_ref[0, :, pl.ds(start, tq)], oh_ref[:, pl.ds(start, tq)])

    def slc_update(s_t, start, mask):
        update(s_t, vs_ref[0, :, pl.ds(start, tq)], start, mask)

    k0 = jnp.clip(t0 - WINDOW, 0, seq - WINDOW - tq)
    n_win = WINDOW // tq + 1
    win_start = [pl.multiple_of(k0 + i * tq, tq) for i in range(n_win)]
    win_mask = (0, WINDOW)

    def win_scores(i):
        return scores(q_plain, kw_ref[0, :, pl.ds(win_start[i], tq)], zeros_k)

    reset()

    def slc_tiles(first, count):
        starts = [pl.multiple_of((first + i) * tq, tq) for i in range(count)]
        s_all = [slc_scores(st) for st in starts]
        for s_t, st in zip(s_all, starts):
            slc_update(s_t, st, None)

    def quad(k4, carry):
        slc_tiles(4 * k4, 4)
        return carry

    lax.fori_loop(0, qi // 4, quad, 0)

    @pl.when(qi % 4 >= 2)
    def _():
        slc_tiles(4 * (qi // 4), 2)

    @pl.when(qi % 2 == 1)
    def _():
        slc_tiles(qi - 1, 1)

    diag = pl.multiple_of(t0, tq)
    s_cur = slc_scores(diag)
    s_next = win_scores(0)
    slc_update(s_cur, diag, (0, None))
    finish(os_ref)
    reset()
    for i in range(n_win):
        s_cur = s_next
        if i + 1 < n_win:
            s_next = win_scores(i + 1)
        update(s_cur, vw_ref[0, :, pl.ds(win_start[i], tq)], win_start[i], win_mask)
    finish(ow_ref)


def prompt_slc_win_attention(q_slots, sel, ks_t, vs_t, kw_t, vw_t, n_seq, tq):
    tokens = q_slots.shape[0]
    seq = tokens // n_seq
    tq = _row_tile(seq, tq)
    assert tq & (tq - 1) == 0 and WINDOW % tq == 0 and seq >= WINDOW + tq
    nt = seq // tq
    gw = NSA_GROUP * SLOT
    assert tq % (2 * LANES) == 0
    onehot_t = jax.nn.one_hot(jnp.arange(seq) // SLC_BLOCK, SLOT - NSA_DH, dtype=F32).T
    kv_spec = pl.BlockSpec((1, NSA_DH, seq), lambda s, g, i: (s, g, 0))
    return pl.pallas_call(
        functools.partial(_prompt_slc_win_kernel, tq=tq, seq=seq),
        grid=(n_seq, NSA_KV, nt),
        in_specs=[pl.BlockSpec((tq, gw), lambda s, g, i: (s * nt + i, g)),
                  pl.BlockSpec((1, 1, tq, SLOT), lambda s, g, i: (s, g, i, 0)),
                  kv_spec, kv_spec, kv_spec, kv_spec,
                  pl.BlockSpec((SLOT - NSA_DH, seq), lambda s, g, i: (0, 0))],
        out_specs=[pl.BlockSpec((tq, gw), lambda s, g, i: (s * nt + i, g)),
                   pl.BlockSpec((tq, gw), lambda s, g, i: (s * nt + i, g))],
        out_shape=[jax.ShapeDtypeStruct((tokens, NSA_HEADS * SLOT), F32),
                   jax.ShapeDtypeStruct((tokens, NSA_HEADS * SLOT), F32)],
        scratch_shapes=[pltpu.VMEM((1, NSA_GROUP * tq), F32),
                        pltpu.VMEM((NSA_DH + 16, NSA_GROUP * tq), F32)],
        compiler_params=_cparams(3),
        name="prompt_slc_win_attention",
    )(q_slots, sel, ks_t, vs_t, kw_t, vw_t, onehot_t)


def _sample_slc_kernel(idx_ref, pg_ref, q_ref, kn_ref, vn_ref, pool_k, pool_v, o_ref, kbuf, vbuf, sems,
                       *, topn, past, t_pad, t_real):
    s_id, g_id = pl.program_id(0), pl.program_id(1)
    step = s_id * NSA_KV + g_id
    n_steps = pl.num_programs(0) * NSA_KV

    def copies(item_step, t, slot):
        base = (item_step * t_pad + t) * topn
        g = item_step % NSA_KV
        out = []
        for kk in range(topn):
            page = pg_ref[base + kk]
            lanes = pl.ds(kk * PAGE_SIZE, PAGE_SIZE)
            out.append(pltpu.make_async_copy(pool_k.at[page, g], kbuf.at[slot, :, lanes], sems.at[slot, 0]))
            out.append(pltpu.make_async_copy(pool_v.at[page, g], vbuf.at[slot, :, lanes], sems.at[slot, 1]))
        return out

    @pl.when(step == 0)
    def _():
        for c in copies(step, 0, 0):
            c.start()

    for t in range(t_real):
        slot = t % 2
        if t + 1 < t_real:
            for c in copies(step, t + 1, 1 - slot):
                c.start()
        else:
            @pl.when(step + 1 < n_steps)
            def _():
                for c in copies(step + 1, 0, 1 - slot):
                    c.start()
        for c in copies(step, t, slot):
            c.wait()
        _sample_slc_token(idx_ref, q_ref, kn_ref, vn_ref, o_ref, kbuf[slot], vbuf[slot],
                          base=(step * t_pad + t) * topn, t=t, topn=topn, past=past)


def _sample_slc_token(idx_ref, q_ref, kn_ref, vn_ref, o_ref, k_tiles, v_tiles, *, base, t, topn, past):
    q_pos = past + t
    cur = q_pos // SLC_BLOCK
    first_new = past // SLC_BLOCK
    q = q_ref[0, 0, t].astype(BF16)
    kb = k_tiles.astype(BF16)
    vb = v_tiles.astype(BF16)
    n_keys = topn * PAGE_SIZE
    lane = lax.broadcasted_iota(jnp.int32, (1, n_keys), 1)
    slot = lane // PAGE_SIZE
    in_page = lane % PAGE_SIZE
    k_pos = in_page
    limit = jnp.zeros((1, n_keys), jnp.int32)
    n_new = jnp.int32(0)
    for kk in range(topn):
        b = idx_ref[base + kk]
        here = slot == kk
        k_pos = jnp.where(here, (b // 2) * PAGE_SIZE + in_page, k_pos)
        last = jnp.where(b <= cur, jnp.minimum(q_pos, past - 1), -1)
        limit = jnp.where(here, jnp.where(in_page // SLC_BLOCK == b % 2, last, -1), limit)
        n_new = n_new + jnp.where(b == first_new, 1, 0)
    valid = k_pos <= limit
    s_old = jnp.where(valid, _dot(q, kb), NEG_INF)
    new_lane = lax.broadcasted_iota(jnp.int32, (1, SUBLANES), 1)
    valid_new = past + new_lane <= jnp.where(n_new > 0, q_pos, past - 1)
    s_new = jnp.where(valid_new, _dot(q, kn_ref[0, 0].astype(BF16)), NEG_INF)
    m = jnp.maximum(jnp.max(s_old, axis=-1, keepdims=True), jnp.max(s_new, axis=-1, keepdims=True))
    p_old = jnp.exp(s_old - m)
    p_new = jnp.exp(s_new - m)
    l = jnp.sum(p_old, axis=-1, keepdims=True) + jnp.sum(p_new, axis=-1, keepdims=True)
    o = _nt_dot(p_old.astype(BF16), vb) + _nt_dot(p_new.astype(BF16), vn_ref[0, 0].astype(BF16))
    o_ref[0, 0, t] = o / l


def sample_slc_attention(q_rows, idx, page_table, pool_k, pool_v, k_new_t, v_new_t, t_real, past):
    n_seq, _, t_pad, _, dh = q_rows.shape
    topn = idx.shape[-1]
    assert past % SLC_BLOCK == 0 and t_real <= SUBLANES and PAGE_SIZE == 2 * SLC_BLOCK
    last_old = past // SLC_BLOCK - 1
    logical = jnp.clip(idx, 0, last_old) // 2
    n_pages = page_table.shape[1]
    hit = logical[..., None] == jnp.arange(n_pages, dtype=jnp.int32)
    phys = jnp.sum(jnp.where(hit, page_table[:, None, None, None, :], 0), axis=-1)

    assert t_real % 2 == 0
    new_spec = pl.BlockSpec((1, 1, dh, SUBLANES), lambda s, g, i_r, p_r: (s, g, 0, 0))
    hbm = pl.BlockSpec(memory_space=pl.ANY)
    grid_spec = pltpu.PrefetchScalarGridSpec(
        num_scalar_prefetch=2,
        grid=(n_seq, NSA_KV),
        in_specs=[pl.BlockSpec((1, 1, t_pad, SUBLANES, dh), lambda s, g, i_r, p_r: (s, g, 0, 0, 0)),
                  new_spec, new_spec, hbm, hbm],
        out_specs=pl.BlockSpec((1, 1, t_real, SUBLANES, dh), lambda s, g, i_r, p_r: (s, g, 0, 0, 0)),
        scratch_shapes=[pltpu.VMEM((2, dh, topn * PAGE_SIZE), F32),
                        pltpu.VMEM((2, dh, topn * PAGE_SIZE), F32),
                        pltpu.SemaphoreType.DMA((2, 2))],
    )
    return pl.pallas_call(
        functools.partial(_sample_slc_kernel, topn=topn, past=past, t_pad=t_pad, t_real=t_real),
        grid_spec=grid_spec,
        out_shape=jax.ShapeDtypeStruct((n_seq, NSA_KV, t_real, SUBLANES, dh), F32),
        compiler_params=_cparams(2),
        name="sample_slc_attention",
    )(idx.reshape(-1), phys.reshape(-1), q_rows, k_new_t, v_new_t, pool_k, pool_v)


def _sample_win_kernel(q_ref, wk_ref, wv_ref, kn_ref, vn_ref, o_ref, *, past, t_pad):
    rows = t_pad * SUBLANES
    wb = wk_ref.shape[-1]
    q = q_ref[0, 0].reshape(rows, q_ref.shape[-1]).astype(BF16)
    q_pos = past + lax.broadcasted_iota(jnp.int32, (rows, 1), 0) // SUBLANES
    k_pos = past - wb + lax.broadcasted_iota(jnp.int32, (1, wb), 1)
    dist = q_pos - k_pos
    valid = (dist >= 0) & (dist < WINDOW) & (k_pos >= 0)
    s_old = jnp.where(valid, _dot(q, wk_ref[0, 0].astype(BF16)), NEG_INF)
    n_pos = past + lax.broadcasted_iota(jnp.int32, (1, SUBLANES), 1)
    dist_n = q_pos - n_pos
    valid_n = (dist_n >= 0) & (dist_n < WINDOW)
    s_new = jnp.where(valid_n, _dot(q, kn_ref[0, 0].astype(BF16)), NEG_INF)
    m = jnp.maximum(jnp.max(s_old, axis=-1, keepdims=True), jnp.max(s_new, axis=-1, keepdims=True))
    p_old = jnp.exp(s_old - m)
    p_new = jnp.exp(s_new - m)
    l = jnp.sum(p_old, axis=-1, keepdims=True) + jnp.sum(p_new, axis=-1, keepdims=True)
    o = _nt_dot(p_old.astype(BF16), wv_ref[0, 0].astype(BF16)) + _nt_dot(p_new.astype(BF16),
                                                                      vn_ref[0, 0].astype(BF16))
    o_ref[0, 0] = (o / l).reshape(t_pad, SUBLANES, o.shape[-1])


def sample_win_attention(q_rows, win_k_t, win_v_t, k_new_t, v_new_t, past):
    n_seq, _, t_pad, _, dh = q_rows.shape
    wb = win_k_t.shape[-1]
    q_spec = pl.BlockSpec((1, 1, t_pad, SUBLANES, dh), lambda s, g: (s, g, 0, 0, 0))
    win_spec = pl.BlockSpec((1, 1, dh, wb), lambda s, g: (s, g, 0, 0))
    new_spec = pl.BlockSpec((1, 1, dh, SUBLANES), lambda s, g: (s, g, 0, 0))
    return pl.pallas_call(
        functools.partial(_sample_win_kernel, past=past, t_pad=t_pad),
        grid=(n_seq, NSA_KV),
        in_specs=[q_spec, win_spec, win_spec, new_spec, new_spec],
        out_specs=q_spec,
        out_shape=jax.ShapeDtypeStruct(q_rows.shape, F32),
        compiler_params=_cparams(2),
        name="sample_win_attention",
    )(q_rows, win_k_t, win_v_t, k_new_t, v_new_t)


def _to_slots(a):
    lead = a.shape[:-1]
    n = a.shape[-1] // NSA_DH
    a = a.reshape(*lead, n, NSA_DH)
    a = jnp.pad(a, [(0, 0)] * (a.ndim - 1) + [(0, SLOT - NSA_DH)])
    return a.reshape(*lead, n * SLOT)


def _odd_weights(w_in, w_out):
    d = w_in.shape[0]
    hq = NSA_HEADS * NSA_DH
    kvw = NSA_KV * NSA_DH
    wq = _to_slots(w_in[:, :hq] * np.float32(NSA_DH ** -0.5))
    wg = jnp.pad(w_in[:, hq + 6 * kvw:], ((0, 0), (0, LANES - 3 * NSA_HEADS)))
    w_q = jnp.concatenate([wq, wg], axis=1).astype(BF16)
    w_kvt = w_in[:, hq:hq + 6 * kvw].T.astype(BF16)
    wo = jnp.pad(w_out.reshape(NSA_HEADS, NSA_DH, d), ((0, 0), (0, SLOT - NSA_DH), (0, 0)))
    wo = wo.reshape(NSA_HEADS * SLOT, d).astype(BF16)
    k = NSA_HEADS * SLOT
    e = np.zeros((LANES, 3 * k), np.float32)
    for c in range(3):
        for h in range(NSA_HEADS):
            e[c * NSA_HEADS + h, c * k + h * SLOT:c * k + (h + 1) * SLOT] = 1.0
    return w_q, w_kvt, wo, jnp.asarray(e, dtype=BF16)


def _group_rows(q_slots, n_seq, t_pad):
    q = q_slots.reshape(n_seq, t_pad, NSA_KV, NSA_GROUP, SLOT)[..., :NSA_DH]
    q = q.transpose(0, 2, 1, 3, 4)
    return jnp.pad(q, ((0, 0), (0, 0), (0, 0), (0, SUBLANES - NSA_GROUP), (0, 0)))


def _ungroup_rows(o, n_seq, t_pad):
    t = o.shape[2]
    o = o[:, :, :, :NSA_GROUP].transpose(0, 2, 1, 3, 4)
    o = jnp.pad(o, ((0, 0), (0, t_pad - t), (0, 0), (0, 0), (0, SLOT - NSA_DH)))
    return o.reshape(n_seq * t_pad, NSA_HEADS * SLOT)


def _feature_major(cache):
    return cache.transpose(0, 2, 3, 1)


def _token_major(a_t):
    return a_t.transpose(0, 3, 1, 2)


def _pad_rows(a, t_pad):
    return jnp.pad(a, ((0, 0), (0, t_pad - a.shape[1])) + ((0, 0),) * (a.ndim - 2))


def kernel(x_prompt, x_sample, state_sconv, state_ret, cache_cmp_k, cache_cmp_v, cache_slc_k, cache_slc_v,
           cache_win_k, cache_win_v, state_ffn_conv, page_table,
           w_in_even, sconv_w, sconv_b, ret_gn_g, w_out_even,
           w_in_odd, cmp_pe, cmp_w1, cmp_w2, w_out_odd,
           ln_mix_g, ln_mix_b, ffn_w_up, ffn_conv_w, ffn_conv_b, ffn_w_down, ln_ffn_g, ln_ffn_b):
    b_p, s_p, d_model = x_prompt.shape
    b_s, t_s, _ = x_sample.shape
    n_pages = page_table.shape[1]
    past = n_pages * PAGE_SIZE
    t_pad = SUBLANES
    assert t_s <= t_pad and t_s >= SCONV_W - 1 and t_s < CMP_STRIDE and past % PAGE_SIZE == 0
    assert s_p % RET_CHUNK == 0 and s_p % PAGE_SIZE == 0
    d_sconv = sconv_w.shape[-1]
    d_ff = ffn_conv_w.shape[-1]
    gd = NSA_KV * NSA_DH
    depth = ln_mix_g.shape[0]

    xp = x_prompt.reshape(b_p * s_p, d_model)
    xs = _pad_rows(x_sample, t_pad).reshape(b_s * t_pad, d_model)
    outs = {k: [] for k in ("sconv_p", "sconv_s", "ret_p", "ret_s", "cmp_k_p", "cmp_v_p", "slc_k_p", "slc_v_p",
                            "cmp_k_s", "cmp_v_s", "slc_k_s", "slc_v_s", "win_k_p", "win_v_p", "win_k_s",
                            "win_v_s", "ffn_p", "ffn_s")}

    for layer in range(depth):
        if layer % 2 == 0:
            e = layer // 2
            w_in = w_in_even[e].astype(BF16)
            w_out = w_out_even[e].astype(BF16)
            n_in = w_in.shape[1]
            (zp,) = matmul_split(xp, w_in, [n_in], [F32])
            yp, hc, st = even_mixer(zp, jnp.zeros((b_p, SCONV_W - 1, d_sconv), F32),
                                    jnp.zeros((b_p,) + state_ret.shape[2:], F32), jnp.arange(s_p),
                                    RET_CHUNK, RET_CHUNK, sconv_w[e], sconv_b[e], ret_gn_g[e])
            outs["sconv_p"].append(hc)
            outs["ret_p"].append(st)
            xp = matmul_residual_ln(yp, w_out, xp, ln_mix_g[layer], ln_mix_b[layer])
            (zs,) = matmul_split(xs, w_in, [n_in], [F32])
            ys, hc, st = even_mixer(zs, state_sconv[e], state_ret[e], past + jnp.arange(t_pad),
                                    t_pad, t_s, sconv_w[e], sconv_b[e], ret_gn_g[e])
            outs["sconv_s"].append(hc)
            outs["ret_s"].append(st)
            xs = matmul_residual_ln(ys, w_out, xs, ln_mix_g[layer], ln_mix_b[layer])
        else:
            o = layer // 2
            w_q, w_kvt, w_out, e_gate = _odd_weights(w_in_odd[o], w_out_odd[o])
            pe, w1, w2 = cmp_pe[o], cmp_w1[o], cmp_w2[o]
            qp, gp, kc, vc, ks, vs, kw, vw = nsa_projection(xp, b_p, w_q, w_kvt, BF16)
            as_cache = lambda a_t: _token_major(a_t.reshape(b_p, NSA_KV, NSA_DH, -1))
            keep = min(WINDOW, s_p)
            outs["cmp_k_p"].append(as_cache(kc))
            outs["cmp_v_p"].append(as_cache(vc))
            outs["slc_k_p"].append(as_cache(ks))
            outs["slc_v_p"].append(as_cache(vs))
            outs["win_k_p"].append(as_cache(kw[:, :, s_p - keep:]))
            outs["win_v_p"].append(as_cache(vw[:, :, s_p - keep:]))
            kcc = compress(kc.reshape(b_p, NSA_KV, NSA_DH, s_p), None, pe[0], w1[0], w2[0])
            vcc = compress(vc.reshape(b_p, NSA_KV, NSA_DH, s_p), None, pe[1], w1[1], w2[1])
            n_cmp = s_p // CMP_STRIDE - CMP_LEN // CMP_STRIDE + 1
            n_slc = s_p // SLC_BLOCK
            oc, sel = cmp_attention_select_prompt(qp, kcc, vcc, b_p, n_cmp, n_slc, SELECT_ROWS)
            osl, ow = prompt_slc_win_attention(qp, sel, ks, vs, kw, vw, b_p, ATTN_ROWS)
            xp = nsa_merge_residual_ln(oc, osl, ow, gp, e_gate, w_out, xp, ln_mix_g[layer], ln_mix_b[layer])
            qs, gs, *kv_s = nsa_projection(xs, 1, w_q, w_kvt, F32)
            kc, vc, ks, vs, kw, vw = [a.reshape(NSA_KV, NSA_DH, b_s, t_pad).transpose(2, 0, 1, 3) for a in kv_s]
            new_rows = lambda a_t: _token_major(a_t[..., :t_s])
            outs["cmp_k_s"].append(new_rows(kc))
            outs["cmp_v_s"].append(new_rows(vc))
            outs["slc_k_s"].append(new_rows(ks))
            outs["slc_v_s"].append(new_rows(vs))
            win_k = _feature_major(cache_win_k[o])
            win_v = _feature_major(cache_win_v[o])
            wb = win_k.shape[-1]
            keep = min(WINDOW, wb + t_s)
            outs["win_k_s"].append(_token_major(jnp.concatenate([win_k, kw[..., :t_s]], axis=-1)[..., -keep:]))
            outs["win_v_s"].append(_token_major(jnp.concatenate([win_v, vw[..., :t_s]], axis=-1)[..., -keep:]))
            kcc = compress(_feature_major(cache_cmp_k[o]), page_table, pe[0], w1[0], w2[0])
            vcc = compress(_feature_major(cache_cmp_v[o]), page_table, pe[1], w1[1], w2[1])
            n_cmp = (past + t_s) // CMP_STRIDE - CMP_LEN // CMP_STRIDE + 1
            n_slc = -(-(past + t_s) // SLC_BLOCK)
            oc, idx = cmp_attention_select(qs, kcc, vcc, b_s, n_cmp, n_slc, past)
            q_rows = _group_rows(qs, b_s, t_pad)
            osl = sample_slc_attention(q_rows, idx, page_table, _feature_major(cache_slc_k[o]),
                                       _feature_major(cache_slc_v[o]), ks, vs, t_s, past)
            ow = sample_win_attention(q_rows, win_k, win_v, kw, vw, past)
            xs = nsa_merge_residual_ln(oc, _ungroup_rows(osl, b_s, t_pad), _ungroup_rows(ow, b_s, t_pad), gs,
                                       e_gate, w_out, xs, ln_mix_g[layer], ln_mix_b[layer])
        w_up = ffn_w_up[layer].astype(BF16)
        w_down = ffn_w_down[layer].astype(BF16)
        hp, hist_p = ffn_up_sequences(xp, b_p, jnp.zeros((b_p, FFN_W - 1, d_ff), F32), w_up,
                                      ffn_conv_w[layer], ffn_conv_b[layer])
        outs["ffn_p"].append(hist_p)
        xp = matmul_residual_ln(hp, w_down, xp, ln_ffn_g[layer], ln_ffn_b[layer])
        hs, a_s = ffn_up_short(xs, state_ffn_conv[layer], w_up, ffn_conv_w[layer], ffn_conv_b[layer])
        outs["ffn_s"].append(a_s.reshape(b_s, t_pad, d_ff)[:, t_s - (FFN_W - 1):t_s])
        xs = matmul_residual_ln(hs, w_down, xs, ln_ffn_g[layer], ln_ffn_b[layer])

    st = jnp.stack
    y_p = xp.reshape(b_p, s_p, d_model)
    y_s = xs.reshape(b_s, t_pad, d_model)[:, :t_s]
    order = ("sconv_p", "sconv_s", "ret_p", "ret_s", "cmp_k_p", "cmp_v_p", "slc_k_p", "slc_v_p",
             "cmp_k_s", "cmp_v_s", "slc_k_s", "slc_v_s", "win_k_p", "win_v_p", "win_k_s", "win_v_s",
             "ffn_p", "ffn_s")
    return (y_p, y_s) + tuple(st(outs[k]) for k in order)
```

```python
import functools

import numpy as np
import jax
import jax.numpy as jnp
from jax import lax
from jax.experimental import pallas as pl
from jax.experimental.pallas import tpu as pltpu

F32 = jnp.float32
BF16 = jnp.bfloat16

SUBLANES = 8
LANES = 128
VMEM_LIMIT_BYTES = 56 * 1024 * 1024
MATMUL_ROWS = 512
MERGE_ROWS = 256
ATTN_ROWS = 256
SELECT_ROWS = 1024
COMPRESS_PAGES = 32

DEPTH = 2
SCONV_W = 3
RET_HEADS = 4
RET_CHUNK = 128
ROPE_BASE = 10000.0
NSA_HEADS = 16
NSA_KV = 4
NSA_GROUP = NSA_HEADS // NSA_KV
NSA_DH = 64
CMP_LEN = 32
CMP_STRIDE = 16
SLC_BLOCK = 64
SLC_TOPN = 16
WINDOW = 512
PAGE_SIZE = 128
FFN_W = 3
ALPHA = (2.0 * DEPTH) ** 0.25
LN_EPS = 1e-5
NEG_INF = -1e30
REMOVED = -3e38
FORCE_BONUS = 1e4
SLOT = 2 * NSA_DH


def _cparams(n_grid):
    return pltpu.CompilerParams(dimension_semantics=("arbitrary",) * n_grid,
                                vmem_limit_bytes=VMEM_LIMIT_BYTES)


def _row_tile(m, want):
    t = min(m, want)
    assert m % t == 0, (m, t)
    return t


def _nt_dot(a, b):
    return lax.dot_general(a, b, (((1,), (1,)), ((), ())), preferred_element_type=F32)


def _tn_dot(a, b):
    return lax.dot_general(a, b, (((0,), (0,)), ((), ())), preferred_element_type=F32)


def _dot(a, b):
    return jnp.dot(a, b, preferred_element_type=F32)


def _gelu(x):
    return 0.5 * x * (1.0 + jnp.tanh(np.float32(np.sqrt(2.0 / np.pi)) * (x + 0.044715 * (x * x * x))))


def _layer_norm_rows(r, g, b):
    mu = jnp.mean(r, axis=-1, keepdims=True)
    d = r - mu
    var = jnp.mean(d * d, axis=-1, keepdims=True)
    return d * lax.rsqrt(var + LN_EPS) * g + b


def _mm_split_kernel(x_ref, w_ref, *o_refs, cuts):
    acc = _dot(x_ref[...].astype(BF16), w_ref[...])
    for o_ref, (lo, hi) in zip(o_refs, cuts):
        o_ref[...] = acc[:, lo:hi].astype(o_ref.dtype)


def matmul_split(x, w_bf16, widths, dtypes, tm=MATMUL_ROWS):
    m, k = x.shape
    n = w_bf16.shape[1]
    assert sum(widths) == n and all(wd % LANES == 0 for wd in widths)
    tm = _row_tile(m, tm)
    cuts, lo = [], 0
    for wd in widths:
        cuts.append((lo, lo + wd))
        lo += wd
    return pl.pallas_call(
        functools.partial(_mm_split_kernel, cuts=tuple(cuts)),
        grid=(m // tm,),
        in_specs=[pl.BlockSpec((tm, k), lambda i: (i, 0)),
                  pl.BlockSpec((k, n), lambda i: (0, 0))],
        out_specs=[pl.BlockSpec((tm, wd), lambda i: (i, 0)) for wd in widths],
        out_shape=[jax.ShapeDtypeStruct((m, wd), dt) for wd, dt in zip(widths, dtypes)],
        compiler_params=_cparams(1),
        name="matmul_split",
    )(x, w_bf16)


def _nsa_proj_kernel(x_ref, wq_ref, wt_ref, q_ref, g_ref, gt_ref, *kv_refs, nq, gd):
    xb = x_ref[...].astype(BF16)
    acc = _dot(xb, wq_ref[...])
    q_ref[...] = acc[:, :nq].astype(q_ref.dtype)
    g_ref[...] = acc[:, nq:]
    acc_t = _nt_dot(wt_ref[...], xb)
    for i, r in enumerate(kv_refs):
        r[0] = acc_t[i * gd:(i + 1) * gd, :]
    gt_ref[0] = acc_t[len(kv_refs) * gd:, :]


def nsa_projection(x, n_seq, wq_bf16, wt_bf16, q_dtype, n_kv=6, tm=MATMUL_ROWS):
    m, d = x.shape
    seq = m // n_seq
    tm = _row_tile(seq, tm)
    nt = seq // tm
    nq = wq_bf16.shape[1] - LANES
    gd = (wt_bf16.shape[0] - LANES) // n_kv
    fm = lambda rows: pl.BlockSpec((1, rows, tm), lambda s, i: (s, 0, i))
    return pl.pallas_call(
        functools.partial(_nsa_proj_kernel, nq=nq, gd=gd),
        grid=(n_seq, nt),
        in_specs=[pl.BlockSpec((tm, d), lambda s, i: (s * nt + i, 0)),
                  pl.BlockSpec((d, nq + LANES), lambda s, i: (0, 0)),
                  pl.BlockSpec((n_kv * gd + LANES, d), lambda s, i: (0, 0))],
        out_specs=[pl.BlockSpec((tm, nq), lambda s, i: (s * nt + i, 0)),
                   pl.BlockSpec((tm, LANES), lambda s, i: (s * nt + i, 0)),
                   fm(LANES)] + [fm(gd) for _ in range(n_kv)],
        out_shape=[jax.ShapeDtypeStruct((m, nq), q_dtype), jax.ShapeDtypeStruct((m, LANES), F32),
                   jax.ShapeDtypeStruct((n_seq, LANES, seq), F32)] + [
            jax.ShapeDtypeStruct((n_seq, gd, seq), F32) for _ in range(n_kv)],
        compiler_params=_cparams(2),
        name="nsa_projection",
    )(x, wq_bf16, wt_bf16)


def _mm_res_ln_kernel(a_ref, w_ref, x_ref, g_ref, b_ref, o_ref):
    y = _dot(a_ref[...].astype(BF16), w_ref[...])
    o_ref[...] = _layer_norm_rows(ALPHA * x_ref[...] + y, g_ref[...], b_ref[...])


def matmul_residual_ln(a, w_bf16, x, g, b, tm=MATMUL_ROWS):
    m, k = a.shape
    d = w_bf16.shape[1]
    tm = _row_tile(m, tm)
    return pl.pallas_call(
        _mm_res_ln_kernel,
        grid=(m // tm,),
        in_specs=[pl.BlockSpec((tm, k), lambda i: (i, 0)),
                  pl.BlockSpec((k, d), lambda i: (0, 0)),
                  pl.BlockSpec((tm, d), lambda i: (i, 0)),
                  pl.BlockSpec((1, d), lambda i: (0, 0)),
                  pl.BlockSpec((1, d), lambda i: (0, 0))],
        out_specs=pl.BlockSpec((tm, d), lambda i: (i, 0)),
        out_shape=jax.ShapeDtypeStruct((m, d), F32),
        compiler_params=_cparams(1),
        name="matmul_residual_ln",
    )(a, w_bf16, x, g.reshape(1, d), b.reshape(1, d))


def _sum2_mm_res_ln_kernel(a_ref, b2_ref, w_ref, x_ref, g_ref, b_ref, o_ref):
    y = _dot((a_ref[...] + b2_ref[...]).astype(BF16), w_ref[...])
    o_ref[...] = _layer_norm_rows(ALPHA * x_ref[...] + y, g_ref[...], b_ref[...])


def sum2_matmul_residual_ln(a, b2, w_bf16, x, g, b, tm=MATMUL_ROWS):
    m, k = a.shape
    d = w_bf16.shape[1]
    tm = _row_tile(m, tm)
    row = lambda i: (i, 0)
    fixed = lambda i: (0, 0)
    return pl.pallas_call(
        _sum2_mm_res_ln_kernel,
        grid=(m // tm,),
        in_specs=[pl.BlockSpec((tm, k), row), pl.BlockSpec((tm, k), row),
                  pl.BlockSpec((k, d), fixed),
                  pl.BlockSpec((tm, d), row),
                  pl.BlockSpec((1, d), fixed), pl.BlockSpec((1, d), fixed)],
        out_specs=pl.BlockSpec((tm, d), row),
        out_shape=jax.ShapeDtypeStruct((m, d), F32),
        compiler_params=_cparams(1),
        name="sum2_matmul_residual_ln",
    )(a, b2, w_bf16, x, g.reshape(1, d), b.reshape(1, d))


def _expand_gates(gates_raw, e_ref):
    sig = jax.nn.sigmoid(gates_raw)
    hi = sig.astype(BF16)
    lo = (sig - hi.astype(F32)).astype(BF16)
    e = e_ref[...]
    return _dot(hi, e) + _dot(lo, e)


def _nsa_merge_ln_kernel(oc_ref, os_ref, ow_ref, gt_ref, e_ref, w_ref, x_ref, g_ref, b_ref, o_ref, *, k):
    gx = _expand_gates(gt_ref[...], e_ref)
    o = gx[:, 0:k] * oc_ref[...] + gx[:, k:2 * k] * os_ref[...] + gx[:, 2 * k:3 * k] * ow_ref[...]
    y = _dot(o.astype(BF16), w_ref[...])
    o_ref[...] = _layer_norm_rows(ALPHA * x_ref[...] + y, g_ref[...], b_ref[...])


def nsa_merge_residual_ln(oc, osl, ow, gates, e_bf16, w_bf16, x, g, b, tm=MERGE_ROWS):
    m, k = oc.shape
    d = w_bf16.shape[1]
    tm = _row_tile(m, tm)
    row = lambda i: (i, 0)
    fixed = lambda i: (0, 0)
    return pl.pallas_call(
        functools.partial(_nsa_merge_ln_kernel, k=k),
        grid=(m // tm,),
        in_specs=[pl.BlockSpec((tm, k), row), pl.BlockSpec((tm, k), row), pl.BlockSpec((tm, k), row),
                  pl.BlockSpec((tm, LANES), row),
                  pl.BlockSpec((LANES, 3 * k), fixed),
                  pl.BlockSpec((k, d), fixed),
                  pl.BlockSpec((tm, d), row),
                  pl.BlockSpec((1, d), fixed), pl.BlockSpec((1, d), fixed)],
        out_specs=pl.BlockSpec((tm, d), row),
        out_shape=jax.ShapeDtypeStruct((m, d), F32),
        compiler_params=_cparams(1),
        name="nsa_merge_residual_ln",
    )(oc, osl, ow, gates, e_bf16, w_bf16, x, g.reshape(1, d), b.reshape(1, d))


def _even_mixer_kernel(z_ref, hist_ref, st_ref, cos_ref, sin_ref, decay_ref, qdec_ref, kdec_ref, sdec_ref,
                       cw_ref, cb_ref, gn_ref, y_ref, hist_out_ref, st_out_ref, carry, state,
                       *, rows, valid, dconv, dk):
    c = pl.program_id(1)
    r0 = valid - 2 - (rows - SUBLANES)

    @pl.when(c == 0)
    def _():
        carry[r0:r0 + 2, :] = hist_ref[0]
        state[...] = st_ref[0]

    d = dconv
    h = z_ref[:, 0:d]
    gate_b = z_ref[:, d:2 * d]
    gate_c = z_ref[:, 2 * d:3 * d]
    ch = gate_c * h
    row = lax.broadcasted_iota(jnp.int32, (rows, d), 0)
    h0 = carry[r0:r0 + 1, :]
    h1 = carry[r0 + 1:r0 + 2, :]
    m1 = jnp.where(row == 0, h1, pltpu.roll(ch, 1, 0))
    m2 = jnp.where(row == 0, h0, jnp.where(row == 1, h1, pltpu.roll(ch, 2, 0)))
    u = ((cb_ref[...] + m2 * cw_ref[0:1, :]) + m1 * cw_ref[1:2, :]) + ch * cw_ref[2:3, :]
    y_ref[:, 0:d] = gate_b * u
    carry[...] = ch[rows - SUBLANES:rows, :]
    hist_out_ref[0] = carry[r0:r0 + 2, :]

    cosf = cos_ref[...]
    sinf = sin_ref[...]
    scale = np.float32(dk ** -0.5)
    heads = range(RET_HEADS)
    col = lambda part, hh: slice(part * d + hh * dk, part * d + (hh + 1) * dk)
    qs, ks, vbs = [], [], []
    for hh in heads:
        q = z_ref[:, col(3, hh)]
        k = z_ref[:, col(4, hh)]
        qs.append(((q * cosf + pltpu.roll(q, dk // 2, 1) * sinf) * scale).astype(BF16))
        ks.append(k * cosf + pltpu.roll(k, dk // 2, 1) * sinf)
        vbs.append(z_ref[:, col(5, hh)].astype(BF16))
    s_old = [state[hh] for hh in heads]
    scores = [_nt_dot(qs[hh], ks[hh].astype(BF16)) * decay_ref[hh] for hh in heads]
    cross = [_dot(qs[hh], s_old[hh].astype(BF16)) * qdec_ref[hh] for hh in heads]
    intra = [_dot(scores[hh].astype(BF16), vbs[hh]) for hh in heads]
    for hh in heads:
        kd = (ks[hh] * kdec_ref[hh]).astype(BF16)
        state[hh] = s_old[hh] * sdec_ref[hh] + _tn_dot(kd, vbs[hh])
    for hh in heads:
        o = intra[hh] + cross[hh]
        mu = jnp.mean(o, axis=-1, keepdims=True)
        dv = o - mu
        var = jnp.mean(dv * dv, axis=-1, keepdims=True)
        on = dv * lax.rsqrt(var + LN_EPS) * gn_ref[:, hh * dk:(hh + 1) * dk]
        gsw = z_ref[:, col(6, hh)]
        y_ref[:, d + hh * dk:d + (hh + 1) * dk] = (gsw * jax.nn.sigmoid(gsw)) * on
    st_out_ref[0] = state[...]


def _retention_tables(rows, valid, dk):
    log_gamma = jnp.log1p(-jnp.exp2(-5.0 - jnp.arange(RET_HEADS, dtype=F32)))
    n = jnp.arange(rows, dtype=F32)
    diff = n[:, None] - n[None, :]
    lg = log_gamma[:, None, None]
    decay = jnp.where(diff >= 0, jnp.exp(lg * jnp.maximum(diff, 0.0)), 0.0)
    q_dec = jnp.exp((n[None, :] + 1.0) * log_gamma[:, None])
    k_dec = jnp.where(n[None, :] < valid, jnp.exp((valid - 1.0 - n[None, :]) * log_gamma[:, None]), 0.0)
    s_dec = jnp.exp(valid * log_gamma)
    bc = lambda a: jnp.broadcast_to(a[:, :, None], (RET_HEADS, rows, dk))
    return decay, bc(q_dec), bc(k_dec), jnp.broadcast_to(s_dec[:, None, None], (RET_HEADS, 1, dk))


def _rope_tables(pos, dk):
    half = dk // 2
    inv = ROPE_BASE ** (-jnp.arange(half, dtype=F32) / half)
    ang = pos.astype(F32)[:, None] * inv
    cos, sin = jnp.cos(ang), jnp.sin(ang)
    return jnp.concatenate([cos, cos], axis=-1), jnp.concatenate([-sin, sin], axis=-1)


def even_mixer(z, hist, st, pos, rows, valid, conv_w, conv_b, gn_g):
    n_seq, _, dconv = hist.shape
    dk = st.shape[-1]
    n_chunks = z.shape[0] // (n_seq * rows)
    cosf, sinf = _rope_tables(pos, dk)
    decay, q_dec, k_dec, s_dec = _retention_tables(rows, valid, dk)
    fixed3 = lambda s, c: (0, 0, 0)
    fixed2 = lambda s, c: (0, 0)
    return pl.pallas_call(
        functools.partial(_even_mixer_kernel, rows=rows, valid=valid, dconv=dconv, dk=dk),
        grid=(n_seq, n_chunks),
        in_specs=[pl.BlockSpec((rows, 7 * dconv), lambda s, c: (s * n_chunks + c, 0)),
                  pl.BlockSpec((1, 2, dconv), lambda s, c: (s, 0, 0)),
                  pl.BlockSpec((1, RET_HEADS, dk, dk), lambda s, c: (s, 0, 0, 0)),
                  pl.BlockSpec((rows, dk), lambda s, c: (c, 0)),
                  pl.BlockSpec((rows, dk), lambda s, c: (c, 0)),
                  pl.BlockSpec((RET_HEADS, rows, rows), fixed3),
                  pl.BlockSpec((RET_HEADS, rows, dk), fixed3),
                  pl.BlockSpec((RET_HEADS, rows, dk), fixed3),
                  pl.BlockSpec((RET_HEADS, 1, dk), fixed3),
                  pl.BlockSpec((SCONV_W, dconv), fixed2),
                  pl.BlockSpec((1, dconv), fixed2),
                  pl.BlockSpec((1, RET_HEADS * dk), fixed2)],
        out_specs=[pl.BlockSpec((rows, 2 * dconv), lambda s, c: (s * n_chunks + c, 0)),
                   pl.BlockSpec((1, 2, dconv), lambda s, c: (s, 0, 0)),
                   pl.BlockSpec((1, RET_HEADS, dk, dk), lambda s, c: (s, 0, 0, 0))],
        out_shape=[jax.ShapeDtypeStruct((z.shape[0], 2 * dconv), F32),
                   jax.ShapeDtypeStruct((n_seq, 2, dconv), F32),
                   jax.ShapeDtypeStruct((n_seq, RET_HEADS, dk, dk), F32)],
        scratch_shapes=[pltpu.VMEM((SUBLANES, dconv), F32), pltpu.VMEM((RET_HEADS, dk, dk), F32)],
        compiler_params=_cparams(2),
        name="even_mixer",
    )(z, hist, st, cosf, sinf, decay, q_dec, k_dec, s_dec, conv_w, conv_b.reshape(1, dconv),
      gn_g.reshape(1, RET_HEADS * dk))


def _conv_gate(a, gate, m1, m2, cw_ref, cb_ref):
    conv = ((cb_ref[...] + m2 * cw_ref[0:1, :]) + m1 * cw_ref[1:2, :]) + a * cw_ref[2:3, :]
    return _gelu(conv) * gate


def _ffn_up_seq_kernel(x_ref, wa_ref, wg_ref, h_ref, cw_ref, cb_ref, o_ref, hist_out_ref, carry, *, tm):
    @pl.when(pl.program_id(2) == 0)
    def _():
        carry[SUBLANES - 2:SUBLANES, :] = h_ref[0]

    xb = x_ref[...].astype(BF16)
    a = _dot(xb, wa_ref[...])
    gate = _dot(xb, wg_ref[...])
    row = lax.broadcasted_iota(jnp.int32, a.shape, 0)
    h0 = carry[SUBLANES - 2:SUBLANES - 1, :]
    h1 = carry[SUBLANES - 1:SUBLANES, :]
    m1 = jnp.where(row == 0, h1, pltpu.roll(a, 1, 0))
    m2 = jnp.where(row == 0, h0, jnp.where(row == 1, h1, pltpu.roll(a, 2, 0)))
    o_ref[...] = _conv_gate(a, gate, m1, m2, cw_ref, cb_ref).astype(o_ref.dtype)
    carry[...] = a[tm - SUBLANES:tm, :]
    hist_out_ref[0] = carry[SUBLANES - 2:SUBLANES, :]


def ffn_up_sequences(x, n_seq, hist, w_up_bf16, conv_w, conv_b, tm=MATMUL_ROWS, n_col=2):
    m, k = x.shape
    dff = conv_w.shape[1]
    seq = m // n_seq
    tm = _row_tile(seq, tm)
    tps = seq // tm
    tn = dff // n_col
    assert tn % LANES == 0
    return pl.pallas_call(
        functools.partial(_ffn_up_seq_kernel, tm=tm),
        grid=(n_col, n_seq, tps),
        in_specs=[pl.BlockSpec((tm, k), lambda j, s, i: (s * tps + i, 0)),
                  pl.BlockSpec((k, tn), lambda j, s, i: (0, j)),
                  pl.BlockSpec((k, tn), lambda j, s, i: (0, j + n_col)),
                  pl.BlockSpec((1, 2, tn), lambda j, s, i: (s, 0, j)),
                  pl.BlockSpec((FFN_W, tn), lambda j, s, i: (0, j)),
                  pl.BlockSpec((1, tn), lambda j, s, i: (0, j))],
        out_specs=[pl.BlockSpec((tm, tn), lambda j, s, i: (s * tps + i, j)),
                   pl.BlockSpec((1, 2, tn), lambda j, s, i: (s, 0, j))],
        out_shape=[jax.ShapeDtypeStruct((m, dff), BF16),
                   jax.ShapeDtypeStruct((n_seq, 2, dff), F32)],
        scratch_shapes=[pltpu.VMEM((SUBLANES, tn), F32)],
        compiler_params=_cparams(3),
        name="ffn_up_sequences",
    )(x, w_up_bf16, w_up_bf16, hist, conv_w, conv_b.reshape(1, dff))


def _ffn_up_short_kernel(x_ref, wa_ref, wg_ref, h1_ref, h2_ref, cw_ref, cb_ref, o_ref, a_ref):
    xb = x_ref[...].astype(BF16)
    a = _dot(xb, wa_ref[...])
    gate = _dot(xb, wg_ref[...])
    t = lax.broadcasted_iota(jnp.int32, a.shape, 0) % SUBLANES
    m1 = jnp.where(t == 0, h1_ref[...], pltpu.roll(a, 1, 0))
    m2 = jnp.where(t < 2, h2_ref[...], pltpu.roll(a, 2, 0))
    o_ref[...] = _conv_gate(a, gate, m1, m2, cw_ref, cb_ref).astype(o_ref.dtype)
    a_ref[...] = a


def ffn_up_short(x, hist, w_up_bf16, conv_w, conv_b, n_col=2):
    m, k = x.shape
    dff = conv_w.shape[1]
    n_seq = m // SUBLANES
    tn = dff // n_col
    zeros = jnp.zeros((n_seq, SUBLANES, dff), F32)
    h1 = zeros.at[:, 0].set(hist[:, 1]).reshape(m, dff)
    h2 = zeros.at[:, 0].set(hist[:, 0]).at[:, 1].set(hist[:, 1]).reshape(m, dff)
    col = lambda j: (0, j)
    return pl.pallas_call(
        _ffn_up_short_kernel,
        grid=(n_col,),
        in_specs=[pl.BlockSpec((m, k), lambda j: (0, 0)),
                  pl.BlockSpec((k, tn), col),
                  pl.BlockSpec((k, tn), lambda j: (0, j + n_col)),
                  pl.BlockSpec((m, tn), col), pl.BlockSpec((m, tn), col),
                  pl.BlockSpec((FFN_W, tn), col), pl.BlockSpec((1, tn), col)],
        out_specs=[pl.BlockSpec((m, tn), col), pl.BlockSpec((m, tn), col)],
        out_shape=[jax.ShapeDtypeStruct((m, dff), F32), jax.ShapeDtypeStruct((m, dff), F32)],
        compiler_params=_cparams(1),
        name="ffn_up_short",
    )(x, w_up_bf16, w_up_bf16, h1, h2, conv_w, conv_b.reshape(1, dff))


def _compress_kernel(pt_ref, *refs, pages):
    page_refs = refs[:pages + 1]
    w2t_ref, pecol_ref, w1_ref, w2_ref, o_ref = refs[pages + 1:pages + 6]
    rows_refs = refs[pages + 6:]
    parts = len(rows_refs)
    ppp = (pages + 1) // parts
    cpp = PAGE_SIZE // CMP_STRIDE
    n = (pages + 1) * cpp
    n_p = ppp * cpp
    hidden = w1_ref.shape[1]
    gpr = LANES // NSA_DH
    pieces = NSA_KV // gpr
    for i, r in enumerate(page_refs):
        for pc in range(pieces):
            tile = r[0, pc * gpr:(pc + 1) * gpr].reshape(LANES, PAGE_SIZE)
            rows_refs[i // ppp][pc, (i % ppp) * PAGE_SIZE:(i % ppp + 1) * PAGE_SIZE, :] = tile.T
    pe_term = jnp.sum(pecol_ref[...] * w1_ref[...], axis=0, keepdims=True)
    accs = []
    for rows_ref in rows_refs:
        acc = jnp.zeros((pieces * n_p, gpr * 2 * hidden), F32)
        for tp in range(CMP_STRIDE // 2):
            lhs = jnp.concatenate(
                [jnp.concatenate([rows_ref[pc, pl.ds(2 * tp, n_p, stride=CMP_STRIDE), :],
                                  rows_ref[pc, pl.ds(2 * tp + 1, n_p, stride=CMP_STRIDE), :]], axis=1)
                 for pc in range(pieces)], axis=0)
            acc = acc + _dot(lhs.astype(BF16), w2t_ref[tp])
        accs.append(acc)
    for pc in range(pieces):
        for gl in range(gpr):
            a = jnp.concatenate([acc[pc * n_p:(pc + 1) * n_p, gl * 2 * hidden:(gl + 1) * 2 * hidden]
                                 for acc in accs], axis=0)
            nxt = pltpu.roll(a, n - 1, 0)
            pre = pe_term + a[:, 0:hidden]
            pre = pre + nxt[:, hidden:2 * hidden]
            o_ref[0, pc * gpr + gl] = _dot(_gelu(pre[0:pages * cpp]).astype(BF16), w2_ref[...])


def compress(rows_t, page_table, pe, w1, w2, pages=COMPRESS_PAGES):
    pooled = page_table is not None
    if pooled:
        n_seq, n_pages = page_table.shape
    else:
        n_seq, n_pages = rows_t.shape[0], rows_t.shape[3] // PAGE_SIZE
        page_table = jnp.zeros((1, 1), jnp.int32)
    pages = min(pages, n_pages)
    assert n_pages % pages == 0
    parts = 3 if (pages + 1) % 3 == 0 else 1
    hidden = w1.shape[1]
    cpp = PAGE_SIZE // CMP_STRIDE
    r = CMP_LEN // CMP_STRIDE
    gpr = LANES // NSA_DH
    assert r == 2 and gpr == 2
    w1p = w1.reshape(r, CMP_STRIDE, NSA_DH, hidden)
    w16 = jnp.concatenate([w1p[0], w1p[1]], axis=-1)
    zero = jnp.zeros_like(w16)
    per_tok = jnp.concatenate([jnp.concatenate([w16, zero], axis=-1),
                               jnp.concatenate([zero, w16], axis=-1)], axis=1)
    w2t = per_tok.reshape(CMP_STRIDE // 2, 2 * LANES, gpr * 2 * hidden).astype(BF16)
    w2p = jnp.pad(w2, ((0, 0), (0, SLOT - NSA_DH))).astype(BF16)
    pecol = pe.reshape(CMP_LEN * NSA_DH, 1)

    def page_map(i):
        if pooled:
            return lambda s, j, pt: (pt[s, jnp.minimum(j * pages + i, n_pages - 1)], 0, 0, 0)
        return lambda s, j, pt: (s, 0, 0, jnp.minimum(j * pages + i, n_pages - 1))

    fixed2 = lambda s, j, pt: (0, 0)
    grid_spec = pltpu.PrefetchScalarGridSpec(
        num_scalar_prefetch=1,
        grid=(n_seq, n_pages // pages),
        in_specs=[pl.BlockSpec((1, NSA_KV, NSA_DH, PAGE_SIZE), page_map(i)) for i in range(pages + 1)] + [
            pl.BlockSpec((CMP_STRIDE // 2, 2 * LANES, gpr * 2 * hidden), lambda s, j, pt: (0, 0, 0)),
            pl.BlockSpec((CMP_LEN * NSA_DH, 1), fixed2),
            pl.BlockSpec((CMP_LEN * NSA_DH, hidden), fixed2),
            pl.BlockSpec((hidden, SLOT), fixed2)],
        out_specs=pl.BlockSpec((1, NSA_KV, pages * cpp, SLOT), lambda s, j, pt: (s, 0, j, 0)),
        scratch_shapes=[pltpu.VMEM((NSA_KV // gpr, (pages + 1) // parts * PAGE_SIZE, LANES), F32)
                        for _ in range(parts)],
    )
    return pl.pallas_call(
        functools.partial(_compress_kernel, pages=pages),
        grid_spec=grid_spec,
        out_shape=jax.ShapeDtypeStruct((n_seq, NSA_KV, n_pages * cpp, SLOT), F32),
        compiler_params=_cparams(2),
        name="compress",
    )(page_table, *([rows_t] * (pages + 1)), w2t, pecol, w1, w2p)


def _cmp_select_kernel(q_ref, kc_ref, vc_ref, ov_ref, o_ref, idx_ref, *, tq, n_cmp, n_slc, pos0):
    ncp = kc_ref.shape[2]
    nsp = ov_ref.shape[0]
    hrows = NSA_GROUP * tq
    q_pos = pos0 + (lax.broadcasted_iota(jnp.int32, (hrows, ncp), 0) & (tq - 1))
    blk_i = lax.broadcasted_iota(jnp.int32, (hrows, ncp), 1)
    valid = (blk_i * CMP_STRIDE + (CMP_LEN - 1) <= q_pos) & (blk_i < n_cmp)
    ov = ov_ref[...]
    groups = range(NSA_KV)
    heads = [range(g * NSA_GROUP, (g + 1) * NSA_GROUP) for g in groups]
    scores = [_nt_dot(jnp.concatenate([q_ref[:, h * SLOT:(h + 1) * SLOT] for h in heads[g]], axis=0).astype(BF16),
                      kc_ref[0, g].astype(BF16)) for g in groups]
    probs = []
    for g in groups:
        s = jnp.where(valid, scores[g], NEG_INF)
        m = jnp.max(s, axis=-1, keepdims=True)
        e = jnp.where(valid, jnp.exp(s - m), 0.0)
        den = jnp.sum(e, axis=-1, keepdims=True)
        probs.append(e * (1.0 / jnp.where(den > 0.0, den, 1.0)))
    outs = [_dot(probs[g].astype(BF16), vc_ref[0, g].astype(BF16)) for g in groups]
    imps = []
    for g in groups:
        p_sum = jnp.zeros((tq, ncp), F32)
        for j, h in enumerate(heads[g]):
            o_ref[:, h * SLOT:(h + 1) * SLOT] = outs[g][j * tq:(j + 1) * tq]
            p_sum = p_sum + probs[g][j * tq:(j + 1) * tq]
        hi = p_sum.astype(BF16)
        lo = (p_sum - hi.astype(F32)).astype(BF16)
        imps.append(_nt_dot(ov, hi) + _nt_dot(ov, lo))
    cols = NSA_KV * tq
    imp = jnp.concatenate(imps, axis=1)
    blk = lax.broadcasted_iota(jnp.int32, (nsp, cols), 0)
    cur = (pos0 + (lax.broadcasted_iota(jnp.int32, (nsp, cols), 1) & (tq - 1))) // SLC_BLOCK
    real = blk < n_slc
    causal = real & (blk <= cur)
    forced = (blk == 0) | (blk == cur) | (blk == cur - 1)
    score = jnp.where(causal, imp + jnp.where(forced, FORCE_BONUS, 0.0), NEG_INF)
    score = jnp.where(real, score, REMOVED)
    idx = jnp.zeros((SLC_TOPN, cols), jnp.int32)
    idx_row = lax.broadcasted_iota(jnp.int32, (SLC_TOPN, cols), 0)
    for it in range(min(SLC_TOPN, n_slc)):
        m = jnp.max(score, axis=0, keepdims=True)
        first = jnp.min(jnp.where(score == m, blk, nsp), axis=0, keepdims=True)
        score = jnp.where(blk == first, REMOVED, score)
        idx = jnp.where(idx_row == it, first, idx)
    idx_ref[0] = idx


def cmp_block_overlap(n_cmp_pad, n_cmp, n_slc, n_slc_pad, lane_off):
    i = np.arange(n_cmp_pad)[:, None]
    j = np.arange(n_slc_pad)[None, :] - lane_off
    start = i * CMP_STRIDE
    hit = (start < (j + 1) * SLC_BLOCK) & (start + CMP_LEN > j * SLC_BLOCK) & (i < n_cmp) & (j >= 0) & (j < n_slc)
    return jnp.asarray(hit.astype(np.float32), dtype=BF16)


def cmp_attention_select(q_slots, kcc, vcc, n_seq, n_cmp, n_slc, pos0):
    tokens = q_slots.shape[0]
    tq = tokens // n_seq
    assert tq & (tq - 1) == 0
    ncp = kcc.shape[2]
    nsp = -(-n_slc // SUBLANES) * SUBLANES
    ov_t = cmp_block_overlap(ncp, n_cmp, n_slc, nsp, 0).T
    hw = NSA_HEADS * SLOT
    o_cmp, idx_t = pl.pallas_call(
        functools.partial(_cmp_select_kernel, tq=tq, n_cmp=n_cmp, n_slc=n_slc, pos0=pos0),
        grid=(n_seq,),
        in_specs=[pl.BlockSpec((tq, hw), lambda s: (s, 0)),
                  pl.BlockSpec((1, NSA_KV, ncp, SLOT), lambda s: (s, 0, 0, 0)),
                  pl.BlockSpec((1, NSA_KV, ncp, SLOT), lambda s: (s, 0, 0, 0)),
                  pl.BlockSpec((nsp, ncp), lambda s: (0, 0))],
        out_specs=[pl.BlockSpec((tq, hw), lambda s: (s, 0)),
                   pl.BlockSpec((1, SLC_TOPN, NSA_KV * tq), lambda s: (s, 0, 0))],
        out_shape=[jax.ShapeDtypeStruct((tokens, hw), F32),
                   jax.ShapeDtypeStruct((n_seq, SLC_TOPN, NSA_KV * tq), jnp.int32)],
        compiler_params=_cparams(1),
        name="cmp_attention_select",
    )(q_slots, kcc, vcc, ov_t)
    topn = min(SLC_TOPN, n_slc)
    idx = idx_t[:, :topn].reshape(n_seq, topn, NSA_KV, tq).transpose(0, 2, 3, 1)
    return o_cmp, idx


def _cmp_select_prompt_kernel(q_ref, kc_ref, vc_ref, ovt_ref, gt_ref, o_ref, sel_ref, *, tq, n_cmp, n_slc):
    t0 = pl.program_id(2) * tq
    ncp = kc_ref.shape[2]
    nsr = ovt_ref.shape[0]
    kc = kc_ref[0, 0].astype(BF16)
    vc = vc_ref[0, 0].astype(BF16)
    q_pos = t0 + lax.broadcasted_iota(jnp.int32, (ncp, tq), 1)
    blk_i = lax.broadcasted_iota(jnp.int32, (ncp, tq), 0)
    valid = (blk_i * CMP_STRIDE + (CMP_LEN - 1) <= q_pos) & (blk_i < n_cmp)
    p_sum = jnp.zeros((ncp, tq), F32)
    for j in range(NSA_GROUP):
        s = jnp.where(valid, _nt_dot(kc, q_ref[:, j * SLOT:(j + 1) * SLOT]), NEG_INF)
        m = jnp.max(s, axis=0, keepdims=True)
        e = jnp.where(valid, jnp.exp(s - m), 0.0)
        den = jnp.sum(e, axis=0, keepdims=True)
        p = e * (1.0 / jnp.where(den > 0.0, den, 1.0))
        gate = jax.nn.sigmoid(gt_ref[0, pl.ds(pl.program_id(1) * NSA_GROUP + j, 1), :])
        o_ref[:, j * SLOT:(j + 1) * SLOT] = _tn_dot((p * gate).astype(BF16), vc)
        p_sum = p_sum + p
    hi = p_sum.astype(BF16)
    lo = (p_sum - hi.astype(F32)).astype(BF16)
    ovt = ovt_ref[...]
    imp = _dot(ovt, hi) + _dot(ovt, lo)
    blk = lax.broadcasted_iota(jnp.int32, (nsr, tq), 0)
    cur = (t0 + lax.broadcasted_iota(jnp.int32, (nsr, tq), 1)) // SLC_BLOCK
    real = blk < n_slc
    causal = real & (blk <= cur)
    forced = (blk == 0) | (blk == cur) | (blk == cur - 1)
    score = jnp.where(causal, imp + jnp.where(forced, FORCE_BONUS, 0.0), NEG_INF)
    score = jnp.where(real, score, REMOVED)
    picked = jnp.zeros((nsr, tq), jnp.bool_)
    for _ in range(min(SLC_TOPN, n_slc)):
        m = jnp.max(score, axis=0, keepdims=True)
        first = jnp.min(jnp.where(score == m, blk, nsr), axis=0, keepdims=True)
        hit = blk == first
        picked = picked | hit
        score = jnp.where(hit, REMOVED, score)
    bias_t = jnp.where(real & ~(picked & causal), NEG_INF, 0.0)
    slot_t = jnp.concatenate([jnp.zeros((NSA_DH, tq), F32), bias_t], axis=0)
    sel_ref[0, 0] = slot_t.T


def cmp_attention_select_prompt(q_slots, kcc, vcc, gates_t, n_seq, n_cmp, n_slc, tq):
    tokens = q_slots.shape[0]
    t = tokens // n_seq
    tq = _row_tile(t, tq)
    nt = t // tq
    ncp = kcc.shape[2]
    nsr = SLOT - NSA_DH
    assert n_slc <= nsr
    ovt = cmp_block_overlap(ncp, n_cmp, n_slc, nsr, 0).T
    gw = NSA_GROUP * SLOT
    return pl.pallas_call(
        functools.partial(_cmp_select_prompt_kernel, tq=tq, n_cmp=n_cmp, n_slc=n_slc),
        grid=(n_seq, NSA_KV, nt),
        in_specs=[pl.BlockSpec((tq, gw), lambda s, g, i: (s * nt + i, g)),
                  pl.BlockSpec((1, 1, ncp, SLOT), lambda s, g, i: (s, g, 0, 0)),
                  pl.BlockSpec((1, 1, ncp, SLOT), lambda s, g, i: (s, g, 0, 0)),
                  pl.BlockSpec((nsr, ncp), lambda s, g, i: (0, 0)),
                  pl.BlockSpec((1, LANES, tq), lambda s, g, i: (s, 0, i))],
        out_specs=[pl.BlockSpec((tq, gw), lambda s, g, i: (s * nt + i, g)),
                   pl.BlockSpec((1, 1, tq, SLOT), lambda s, g, i: (s, g, i, 0))],
        out_shape=[jax.ShapeDtypeStruct((tokens, NSA_HEADS * SLOT), F32),
                   jax.ShapeDtypeStruct((n_seq, NSA_KV, t, SLOT), F32)],
        compiler_params=_cparams(3),
        name="cmp_attention_select_prompt",
    )(q_slots, kcc, vcc, ovt, gates_t)


def _prompt_slc_win_kernel(q_ref, sel_ref, ks_ref, vs_ref, kw_ref, vw_ref, oh_ref, gt_ref, o_ref,
                           m_ref, acc_ref, *, tq, seq):
    acc_rows = NSA_DH + 16
    qi = pl.program_id(2)
    t0 = qi * tq
    rows = NSA_GROUP * tq
    sel = sel_ref[0, 0]
    q_plain = jnp.concatenate([q_ref[:, j * SLOT:(j + 1) * SLOT] for j in range(NSA_GROUP)], axis=0)
    q_aug = jnp.concatenate([(q_ref[:, j * SLOT:(j + 1) * SLOT].astype(F32) + sel).astype(BF16)
                             for j in range(NSA_GROUP)], axis=0)
    zeros_k = jnp.zeros((SLOT - NSA_DH, tq), F32)
    ones_row = (lax.broadcasted_iota(jnp.int32, (acc_rows - NSA_DH, tq), 0) == 0).astype(BF16)
    rel = lax.broadcasted_iota(jnp.int32, (tq, LANES), 1) - lax.broadcasted_iota(jnp.int32, (tq, LANES), 0)

    def scores(q_rows, k_top, k_bottom):
        k_rows = jnp.concatenate([k_top, k_bottom], axis=0).T.astype(BF16)
        return [_nt_dot(k_rows, q_rows[c:c + 2 * LANES]) for c in range(0, rows, 2 * LANES)]

    def update(s_t, v_top, start, mask):
        v_t = jnp.concatenate([v_top.astype(BF16), ones_row], axis=0)
        for cg in range(rows // (2 * LANES)):
            p_parts, a_parts = [], []
            for h in range(2):
                c0 = (2 * cg + h) * LANES
                x = s_t[cg][:, h * LANES:(h + 1) * LANES]
                if mask is not None:
                    lo, hi = mask
                    off = t0 - start + (c0 & (tq - 1))
                    keep = rel >= lo - off
                    if hi is not None:
                        keep = keep & (rel < hi - off)
                    x = jnp.where(keep, x, NEG_INF)
                m_old = m_ref[:, c0:c0 + LANES]
                m_new = jnp.maximum(m_old, jnp.max(x, axis=0, keepdims=True))
                m_ref[:, c0:c0 + LANES] = m_new
                a_parts.append(jnp.exp(m_old - m_new))
                p_parts.append(jnp.exp(x - m_new).astype(BF16))
            c0 = 2 * cg * LANES
            pv = _dot(v_t, jnp.concatenate(p_parts, axis=1))
            acc_ref[:, c0:c0 + 2 * LANES] = jnp.concatenate(a_parts, axis=1) * acc_ref[:, c0:c0 + 2 * LANES] + pv

    def reset():
        m_ref[...] = jnp.full(m_ref.shape, NEG_INF, F32)
        acc_ref[...] = jnp.zeros(acc_ref.shape, F32)

    slot_pad = jnp.zeros((SLOT - NSA_DH, tq), F32)

    def finish(branch, accumulate):
        for j in range(NSA_GROUP):
            a = acc_ref[:, j * tq:(j + 1) * tq]
            head = pl.program_id(1) * NSA_GROUP + j
            gate = jax.nn.sigmoid(gt_ref[0, pl.ds(branch * NSA_HEADS + head, 1), :])
            o_t = jnp.concatenate([a[0:NSA_DH] * (gate * (1.0 / a[NSA_DH:NSA_DH + 1, :])), slot_pad], axis=0)
            if accumulate:
                o_ref[:, j * SLOT:(j + 1) * SLOT] += o_t.T
            else:
                o_ref[:, j * SLOT:(j + 1) * SLOT] = o_t.T

    def slc_scores(start):
        return scores(q_aug, ks_ref[0, :, pl.ds(start, tq)], oh_ref[:, pl.ds(start, tq)])

    def slc_update(s_t, start, mask):
        update(s_t, vs_ref[0, :, pl.ds(start, tq)], start, mask)

    k0 = jnp.clip(t0 - WINDOW, 0, seq - WINDOW - tq)
    n_win = WINDOW // tq + 1
    win_start = [pl.multiple_of(k0 + i * tq, tq) for i in range(n_win)]
    win_mask = (0, WINDOW)

    def win_scores(i):
        return scores(q_plain, kw_ref[0, :, pl.ds(win_start[i], tq)], zeros_k)

    reset()

    def slc_tiles(first, count):
        starts = [pl.multiple_of((first + i) * tq, tq) for i in range(count)]
        s_all = [slc_scores(st) for st in starts]
        for s_t, st in zip(s_all, starts):
            slc_update(s_t, st, None)

    def quad(k4, carry):
        slc_tiles(4 * k4, 4)
        return carry

    lax.fori_loop(0, qi // 4, quad, 0)

    @pl.when(qi % 4 >= 2)
    def _():
        slc_tiles(4 * (qi // 4), 2)

    @pl.when(qi % 2 == 1)
    def _():
        slc_tiles(qi - 1, 1)

    diag = pl.multiple_of(t0, tq)
    s_diag = slc_scores(diag)
    s_win = [win_scores(i) for i in range(n_win)]
    slc_update(s_diag, diag, (0, None))
    finish(1, False)
    reset()
    for i in range(n_win):
        update(s_win[i], vw_ref[0, :, pl.ds(win_start[i], tq)], win_start[i], win_mask)
    finish(2, True)


def prompt_slc_win_attention(q_slots, sel, ks_t, vs_t, kw_t, vw_t, gates_t, n_seq, tq):
    tokens = q_slots.shape[0]
    seq = tokens // n_seq
    tq = _row_tile(seq, tq)
    assert tq & (tq - 1) == 0 and WINDOW % tq == 0 and seq >= WINDOW + tq
    nt = seq // tq
    gw = NSA_GROUP * SLOT
    assert tq % (2 * LANES) == 0
    onehot_t = jax.nn.one_hot(jnp.arange(seq) // SLC_BLOCK, SLOT - NSA_DH, dtype=F32).T
    kv_spec = pl.BlockSpec((1, NSA_DH, seq), lambda s, g, i: (s, g, 0))
    return pl.pallas_call(
        functools.partial(_prompt_slc_win_kernel, tq=tq, seq=seq),
        grid=(n_seq, NSA_KV, nt),
        in_specs=[pl.BlockSpec((tq, gw), lambda s, g, i: (s * nt + i, g)),
                  pl.BlockSpec((1, 1, tq, SLOT), lambda s, g, i: (s, g, i, 0)),
                  kv_spec, kv_spec, kv_spec, kv_spec,
                  pl.BlockSpec((SLOT - NSA_DH, seq), lambda s, g, i: (0, 0)),
                  pl.BlockSpec((1, LANES, tq), lambda s, g, i: (s, 0, i))],
        out_specs=pl.BlockSpec((tq, gw), lambda s, g, i: (s * nt + i, g)),
        out_shape=jax.ShapeDtypeStruct((tokens, NSA_HEADS * SLOT), F32),
        scratch_shapes=[pltpu.VMEM((1, NSA_GROUP * tq), F32),
                        pltpu.VMEM((NSA_DH + 16, NSA_GROUP * tq), F32)],
        compiler_params=_cparams(3),
        name="prompt_slc_win_attention",
    )(q_slots, sel, ks_t, vs_t, kw_t, vw_t, onehot_t, gates_t)


def _sample_slc_kernel(idx_ref, pg_ref, q_ref, kn_ref, vn_ref, pool_k, pool_v, o_ref, kbuf, vbuf, sems,
                       *, topn, past, t_pad, t_real):
    s_id, g_id = pl.program_id(0), pl.program_id(1)
    step = s_id * NSA_KV + g_id
    n_steps = pl.num_programs(0) * NSA_KV

    def copies(item_step, t, slot):
        base = (item_step * t_pad + t) * topn
        g = item_step % NSA_KV
        out = []
        for kk in range(topn):
            page = pg_ref[base + kk]
            out.append(pltpu.make_async_copy(pool_k.at[page, g], kbuf.at[slot, kk], sems.at[slot, 0]))
            out.append(pltpu.make_async_copy(pool_v.at[page, g], vbuf.at[slot, kk], sems.at[slot, 1]))
        return out

    @pl.when(step == 0)
    def _():
        for c in copies(step, 0, 0):
            c.start()

    for t in range(t_real):
        slot = t % 2
        if t + 1 < t_real:
            for c in copies(step, t + 1, 1 - slot):
                c.start()
        else:
            @pl.when(step + 1 < n_steps)
            def _():
                for c in copies(step + 1, 0, 1 - slot):
                    c.start()
        for c in copies(step, t, slot):
            c.wait()
        _sample_slc_token(idx_ref, q_ref, kn_ref, vn_ref, o_ref, kbuf[slot], vbuf[slot],
                          base=(step * t_pad + t) * topn, t=t, topn=topn, past=past)


def _sample_slc_token(idx_ref, q_ref, kn_ref, vn_ref, o_ref, k_tiles, v_tiles, *, base, t, topn, past):
    q_pos = past + t
    cur = q_pos // SLC_BLOCK
    first_new = past // SLC_BLOCK
    q = q_ref[0, 0, t].astype(BF16)
    kb = jnp.concatenate([k_tiles[kk] for kk in range(topn)], axis=1).astype(BF16)
    vb = jnp.concatenate([v_tiles[kk] for kk in range(topn)], axis=1).astype(BF16)
    n_keys = topn * PAGE_SIZE
    lane = lax.broadcasted_iota(jnp.int32, (1, n_keys), 1)
    slot = lane // PAGE_SIZE
    in_page = lane % PAGE_SIZE
    k_pos = in_page
    limit = jnp.zeros((1, n_keys), jnp.int32)
    n_new = jnp.int32(0)
    for kk in range(topn):
        b = idx_ref[base + kk]
        here = slot == kk
        k_pos = jnp.where(here, (b // 2) * PAGE_SIZE + in_page, k_pos)
        last = jnp.where(b <= cur, jnp.minimum(q_pos, past - 1), -1)
        limit = jnp.where(here, jnp.where(in_page // SLC_BLOCK == b % 2, last, -1), limit)
        n_new = n_new + jnp.where(b == first_new, 1, 0)
    valid = k_pos <= limit
    s_old = jnp.where(valid, _dot(q, kb), NEG_INF)
    new_lane = lax.broadcasted_iota(jnp.int32, (1, SUBLANES), 1)
    valid_new = past + new_lane <= jnp.where(n_new > 0, q_pos, past - 1)
    s_new = jnp.where(valid_new, _dot(q, kn_ref[0, 0].astype(BF16)), NEG_INF)
    m = jnp.maximum(jnp.max(s_old, axis=-1, keepdims=True), jnp.max(s_new, axis=-1, keepdims=True))
    p_old = jnp.exp(s_old - m)
    p_new = jnp.exp(s_new - m)
    l = jnp.sum(p_old, axis=-1, keepdims=True) + jnp.sum(p_new, axis=-1, keepdims=True)
    o = _nt_dot(p_old.astype(BF16), vb) + _nt_dot(p_new.astype(BF16), vn_ref[0, 0].astype(BF16))
    o_ref[0, 0, t] = o / l


def sample_slc_attention(q_rows, idx, page_table, pool_k, pool_v, k_new_t, v_new_t, t_real, past):
    n_seq, _, t_pad, _, dh = q_rows.shape
    topn = idx.shape[-1]
    assert past % SLC_BLOCK == 0 and t_real <= SUBLANES and PAGE_SIZE == 2 * SLC_BLOCK
    last_old = past // SLC_BLOCK - 1
    logical = jnp.clip(idx, 0, last_old) // 2
    n_pages = page_table.shape[1]
    hit = logical[..., None] == jnp.arange(n_pages, dtype=jnp.int32)
    phys = jnp.sum(jnp.where(hit, page_table[:, None, None, None, :], 0), axis=-1)

    assert t_real % 2 == 0
    new_spec = pl.BlockSpec((1, 1, dh, SUBLANES), lambda s, g, i_r, p_r: (s, g, 0, 0))
    hbm = pl.BlockSpec(memory_space=pl.ANY)
    grid_spec = pltpu.PrefetchScalarGridSpec(
        num_scalar_prefetch=2,
        grid=(n_seq, NSA_KV),
        in_specs=[pl.BlockSpec((1, 1, t_pad, SUBLANES, dh), lambda s, g, i_r, p_r: (s, g, 0, 0, 0)),
                  new_spec, new_spec, hbm, hbm],
        out_specs=pl.BlockSpec((1, 1, t_real, SUBLANES, dh), lambda s, g, i_r, p_r: (s, g, 0, 0, 0)),
        scratch_shapes=[pltpu.VMEM((2, topn, dh, PAGE_SIZE), F32),
                        pltpu.VMEM((2, topn, dh, PAGE_SIZE), F32),
                        pltpu.SemaphoreType.DMA((2, 2))],
    )
    return pl.pallas_call(
        functools.partial(_sample_slc_kernel, topn=topn, past=past, t_pad=t_pad, t_real=t_real),
        grid_spec=grid_spec,
        out_shape=jax.ShapeDtypeStruct((n_seq, NSA_KV, t_real, SUBLANES, dh), F32),
        compiler_params=_cparams(2),
        name="sample_slc_attention",
    )(idx.reshape(-1), phys.reshape(-1), q_rows, k_new_t, v_new_t, pool_k, pool_v)


def _sample_win_kernel(q_ref, wk_ref, wv_ref, kn_ref, vn_ref, o_ref, *, past, t_pad):
    rows = t_pad * SUBLANES
    wb = wk_ref.shape[-1]
    q = q_ref[0, 0].reshape(rows, q_ref.shape[-1]).astype(BF16)
    q_pos = past + lax.broadcasted_iota(jnp.int32, (rows, 1), 0) // SUBLANES
    k_pos = past - wb + lax.broadcasted_iota(jnp.int32, (1, wb), 1)
    dist = q_pos - k_pos
    valid = (dist >= 0) & (dist < WINDOW) & (k_pos >= 0)
    s_old = jnp.where(valid, _dot(q, wk_ref[0, 0].astype(BF16)), NEG_INF)
    n_pos = past + lax.broadcasted_iota(jnp.int32, (1, SUBLANES), 1)
    dist_n = q_pos - n_pos
    valid_n = (dist_n >= 0) & (dist_n < WINDOW)
    s_new = jnp.where(valid_n, _dot(q, kn_ref[0, 0].astype(BF16)), NEG_INF)
    m = jnp.maximum(jnp.max(s_old, axis=-1, keepdims=True), jnp.max(s_new, axis=-1, keepdims=True))
    p_old = jnp.exp(s_old - m)
    p_new = jnp.exp(s_new - m)
    l = jnp.sum(p_old, axis=-1, keepdims=True) + jnp.sum(p_new, axis=-1, keepdims=True)
    o = _nt_dot(p_old.astype(BF16), wv_ref[0, 0].astype(BF16)) + _nt_dot(p_new.astype(BF16),
                                                                      vn_ref[0, 0].astype(BF16))
    o_ref[0, 0] = (o / l).reshape(t_pad, SUBLANES, o.shape[-1])


def sample_win_attention(q_rows, win_k_t, win_v_t, k_new_t, v_new_t, past):
    n_seq, _, t_pad, _, dh = q_rows.shape
    wb = win_k_t.shape[-1]
    q_spec = pl.BlockSpec((1, 1, t_pad, SUBLANES, dh), lambda s, g: (s, g, 0, 0, 0))
    win_spec = pl.BlockSpec((1, 1, dh, wb), lambda s, g: (s, g, 0, 0))
    new_spec = pl.BlockSpec((1, 1, dh, SUBLANES), lambda s, g: (s, g, 0, 0))
    return pl.pallas_call(
        functools.partial(_sample_win_kernel, past=past, t_pad=t_pad),
        grid=(n_seq, NSA_KV),
        in_specs=[q_spec, win_spec, win_spec, new_spec, new_spec],
        out_specs=q_spec,
        out_shape=jax.ShapeDtypeStruct(q_rows.shape, F32),
        compiler_params=_cparams(2),
        name="sample_win_attention",
    )(q_rows, win_k_t, win_v_t, k_new_t, v_new_t)


def _to_slots(a):
    lead = a.shape[:-1]
    n = a.shape[-1] // NSA_DH
    a = a.reshape(*lead, n, NSA_DH)
    a = jnp.pad(a, [(0, 0)] * (a.ndim - 1) + [(0, SLOT - NSA_DH)])
    return a.reshape(*lead, n * SLOT)


def _odd_weights(w_in, w_out):
    d = w_in.shape[0]
    hq = NSA_HEADS * NSA_DH
    kvw = NSA_KV * NSA_DH
    wq = _to_slots(w_in[:, :hq] * np.float32(NSA_DH ** -0.5))
    wg = jnp.pad(w_in[:, hq + 6 * kvw:], ((0, 0), (0, LANES - 3 * NSA_HEADS)))
    w_q = jnp.concatenate([wq, wg], axis=1).astype(BF16)
    w_kvt = jnp.concatenate([w_in[:, hq:hq + 6 * kvw], wg], axis=1).T.astype(BF16)
    wo = jnp.pad(w_out.reshape(NSA_HEADS, NSA_DH, d), ((0, 0), (0, SLOT - NSA_DH), (0, 0)))
    wo = wo.reshape(NSA_HEADS * SLOT, d).astype(BF16)
    k = NSA_HEADS * SLOT
    e = np.zeros((LANES, 3 * k), np.float32)
    for c in range(3):
        for h in range(NSA_HEADS):
            e[c * NSA_HEADS + h, c * k + h * SLOT:c * k + (h + 1) * SLOT] = 1.0
    return w_q, w_kvt, wo, jnp.asarray(e, dtype=BF16)


def _group_rows(q_slots, n_seq, t_pad):
    q = q_slots.reshape(n_seq, t_pad, NSA_KV, NSA_GROUP, SLOT)[..., :NSA_DH]
    q = q.transpose(0, 2, 1, 3, 4)
    return jnp.pad(q, ((0, 0), (0, 0), (0, 0), (0, SUBLANES - NSA_GROUP), (0, 0)))


def _ungroup_rows(o, n_seq, t_pad):
    t = o.shape[2]
    o = o[:, :, :, :NSA_GROUP].transpose(0, 2, 1, 3, 4)
    o = jnp.pad(o, ((0, 0), (0, t_pad - t), (0, 0), (0, 0), (0, SLOT - NSA_DH)))
    return o.reshape(n_seq * t_pad, NSA_HEADS * SLOT)


def _feature_major(cache):
    return cache.transpose(0, 2, 3, 1)


def _token_major(a_t):
    return a_t.transpose(0, 3, 1, 2)


def _pad_rows(a, t_pad):
    return jnp.pad(a, ((0, 0), (0, t_pad - a.shape[1])) + ((0, 0),) * (a.ndim - 2))


def kernel(x_prompt, x_sample, state_sconv, state_ret, cache_cmp_k, cache_cmp_v, cache_slc_k, cache_slc_v,
           cache_win_k, cache_win_v, state_ffn_conv, page_table,
           w_in_even, sconv_w, sconv_b, ret_gn_g, w_out_even,
           w_in_odd, cmp_pe, cmp_w1, cmp_w2, w_out_odd,
           ln_mix_g, ln_mix_b, ffn_w_up, ffn_conv_w, ffn_conv_b, ffn_w_down, ln_ffn_g, ln_ffn_b):
    b_p, s_p, d_model = x_prompt.shape
    b_s, t_s, _ = x_sample.shape
    n_pages = page_table.shape[1]
    past = n_pages * PAGE_SIZE
    t_pad = SUBLANES
    assert t_s <= t_pad and t_s >= SCONV_W - 1 and t_s < CMP_STRIDE and past % PAGE_SIZE == 0
    assert s_p % RET_CHUNK == 0 and s_p % PAGE_SIZE == 0
    d_sconv = sconv_w.shape[-1]
    d_ff = ffn_conv_w.shape[-1]
    gd = NSA_KV * NSA_DH
    depth = ln_mix_g.shape[0]

    xp = x_prompt.reshape(b_p * s_p, d_model)
    xs = _pad_rows(x_sample, t_pad).reshape(b_s * t_pad, d_model)
    outs = {k: [] for k in ("sconv_p", "sconv_s", "ret_p", "ret_s", "cmp_k_p", "cmp_v_p", "slc_k_p", "slc_v_p",
                            "cmp_k_s", "cmp_v_s", "slc_k_s", "slc_v_s", "win_k_p", "win_v_p", "win_k_s",
                            "win_v_s", "ffn_p", "ffn_s")}

    for layer in range(depth):
        if layer % 2 == 0:
            e = layer // 2
            w_in = w_in_even[e].astype(BF16)
            w_out = w_out_even[e].astype(BF16)
            n_in = w_in.shape[1]
            (zp,) = matmul_split(xp, w_in, [n_in], [F32])
            yp, hc, st = even_mixer(zp, jnp.zeros((b_p, SCONV_W - 1, d_sconv), F32),
                                    jnp.zeros((b_p,) + state_ret.shape[2:], F32), jnp.arange(s_p),
                                    RET_CHUNK, RET_CHUNK, sconv_w[e], sconv_b[e], ret_gn_g[e])
            outs["sconv_p"].append(hc)
            outs["ret_p"].append(st)
            xp = matmul_residual_ln(yp, w_out, xp, ln_mix_g[layer], ln_mix_b[layer])
            (zs,) = matmul_split(xs, w_in, [n_in], [F32])
            ys, hc, st = even_mixer(zs, state_sconv[e], state_ret[e], past + jnp.arange(t_pad),
                                    t_pad, t_s, sconv_w[e], sconv_b[e], ret_gn_g[e])
            outs["sconv_s"].append(hc)
            outs["ret_s"].append(st)
            xs = matmul_residual_ln(ys, w_out, xs, ln_mix_g[layer], ln_mix_b[layer])
        else:
            o = layer // 2
            w_q, w_kvt, w_out, e_gate = _odd_weights(w_in_odd[o], w_out_odd[o])
            pe, w1, w2 = cmp_pe[o], cmp_w1[o], cmp_w2[o]
            qp, _, gtp, kc, vc, ks, vs, kw, vw = nsa_projection(xp, b_p, w_q, w_kvt, BF16)
            as_cache = lambda a_t: _token_major(a_t.reshape(b_p, NSA_KV, NSA_DH, -1))
            keep = min(WINDOW, s_p)
            outs["cmp_k_p"].append(as_cache(kc))
            outs["cmp_v_p"].append(as_cache(vc))
            outs["slc_k_p"].append(as_cache(ks))
            outs["slc_v_p"].append(as_cache(vs))
            outs["win_k_p"].append(as_cache(kw[:, :, s_p - keep:]))
            outs["win_v_p"].append(as_cache(vw[:, :, s_p - keep:]))
            kcc = compress(kc.reshape(b_p, NSA_KV, NSA_DH, s_p), None, pe[0], w1[0], w2[0])
            vcc = compress(vc.reshape(b_p, NSA_KV, NSA_DH, s_p), None, pe[1], w1[1], w2[1])
            n_cmp = s_p // CMP_STRIDE - CMP_LEN // CMP_STRIDE + 1
            n_slc = s_p // SLC_BLOCK
            oc, sel = cmp_attention_select_prompt(qp, kcc, vcc, gtp, b_p, n_cmp, n_slc, SELECT_ROWS)
            osw = prompt_slc_win_attention(qp, sel, ks, vs, kw, vw, gtp, b_p, ATTN_ROWS)
            xp = sum2_matmul_residual_ln(oc, osw, w_out, xp, ln_mix_g[layer], ln_mix_b[layer])
            qs, gs, _, *kv_s = nsa_projection(xs, 1, w_q, w_kvt, F32)
            kc, vc, ks, vs, kw, vw = [a.reshape(NSA_KV, NSA_DH, b_s, t_pad).transpose(2, 0, 1, 3) for a in kv_s]
            new_rows = lambda a_t: _token_major(a_t[..., :t_s])
            outs["cmp_k_s"].append(new_rows(kc))
            outs["cmp_v_s"].append(new_rows(vc))
            outs["slc_k_s"].append(new_rows(ks))
            outs["slc_v_s"].append(new_rows(vs))
            win_k = _feature_major(cache_win_k[o])
            win_v = _feature_major(cache_win_v[o])
            wb = win_k.shape[-1]
            keep = min(WINDOW, wb + t_s)
            outs["win_k_s"].append(_token_major(jnp.concatenate([win_k, kw[..., :t_s]], axis=-1)[..., -keep:]))
            outs["win_v_s"].append(_token_major(jnp.concatenate([win_v, vw[..., :t_s]], axis=-1)[..., -keep:]))
            kcc = compress(_feature_major(cache_cmp_k[o]), page_table, pe[0], w1[0], w2[0])
            vcc = compress(_feature_major(cache_cmp_v[o]), page_table, pe[1], w1[1], w2[1])
            n_cmp = (past + t_s) // CMP_STRIDE - CMP_LEN // CMP_STRIDE + 1
            n_slc = -(-(past + t_s) // SLC_BLOCK)
            oc, idx = cmp_attention_select(qs, kcc, vcc, b_s, n_cmp, n_slc, past)
            q_rows = _group_rows(qs, b_s, t_pad)
            osl = sample_slc_attention(q_rows, idx, page_table, _feature_major(cache_slc_k[o]),
                                       _feature_major(cache_slc_v[o]), ks, vs, t_s, past)
            ow = sample_win_attention(q_rows, win_k, win_v, kw, vw, past)
            xs = nsa_merge_residual_ln(oc, _ungroup_rows(osl, b_s, t_pad), _ungroup_rows(ow, b_s, t_pad), gs,
                                       e_gate, w_out, xs, ln_mix_g[layer], ln_mix_b[layer])
        w_up = ffn_w_up[layer].astype(BF16)
        w_down = ffn_w_down[layer].astype(BF16)
        hp, hist_p = ffn_up_sequences(xp, b_p, jnp.zeros((b_p, FFN_W - 1, d_ff), F32), w_up,
                                      ffn_conv_w[layer], ffn_conv_b[layer])
        outs["ffn_p"].append(hist_p)
        xp = matmul_residual_ln(hp, w_down, xp, ln_ffn_g[layer], ln_ffn_b[layer])
        hs, a_s = ffn_up_short(xs, state_ffn_conv[layer], w_up, ffn_conv_w[layer], ffn_conv_b[layer])
        outs["ffn_s"].append(a_s.reshape(b_s, t_pad, d_ff)[:, t_s - (FFN_W - 1):t_s])
        xs = matmul_residual_ln(hs, w_down, xs, ln_ffn_g[layer], ln_ffn_b[layer])

    st = jnp.stack
    y_p = xp.reshape(b_p, s_p, d_model)
    y_s = xs.reshape(b_s, t_pad, d_model)[:, :t_s]
    order = ("sconv_p", "sconv_s", "ret_p", "ret_s", "cmp_k_p", "cmp_v_p", "slc_k_p", "slc_v_p",
             "cmp_k_s", "cmp_v_s", "slc_k_s", "slc_v_s", "win_k_p", "win_v_p", "win_k_s", "win_v_s",
             "ffn_p", "ffn_s")
    return (y_p, y_s) + tuple(st(outs[k]) for k in order)
```

```python
import functools

import numpy as np
import jax
import jax.numpy as jnp
from jax import lax
from jax.experimental import pallas as pl
from jax.experimental.pallas import tpu as pltpu

F32 = jnp.float32
BF16 = jnp.bfloat16

SUBLANES = 8
LANES = 128
VMEM_LIMIT_BYTES = 56 * 1024 * 1024
MATMUL_ROWS = 512
MERGE_ROWS = 256
ATTN_ROWS = 256
SELECT_ROWS = 1024
COMPRESS_PAGES = 64

DEPTH = 2
SCONV_W = 3
RET_HEADS = 4
RET_CHUNK = 128
ROPE_BASE = 10000.0
NSA_HEADS = 16
NSA_KV = 4
NSA_GROUP = NSA_HEADS // NSA_KV
NSA_DH = 64
CMP_LEN = 32
CMP_STRIDE = 16
SLC_BLOCK = 64
SLC_TOPN = 16
WINDOW = 512
PAGE_SIZE = 128
FFN_W = 3
ALPHA = (2.0 * DEPTH) ** 0.25
LN_EPS = 1e-5
NEG_INF = -1e30
REMOVED = -3e38
FORCE_BONUS = 1e4
SLOT = 2 * NSA_DH


def _cparams(n_grid):
    return pltpu.CompilerParams(dimension_semantics=("arbitrary",) * n_grid,
                                vmem_limit_bytes=VMEM_LIMIT_BYTES)


def _row_tile(m, want):
    t = min(m, want)
    assert m % t == 0, (m, t)
    return t


def _nt_dot(a, b):
    return lax.dot_general(a, b, (((1,), (1,)), ((), ())), preferred_element_type=F32)


def _tn_dot(a, b):
    return lax.dot_general(a, b, (((0,), (0,)), ((), ())), preferred_element_type=F32)


def _dot(a, b):
    return jnp.dot(a, b, preferred_element_type=F32)


def _gelu(x):
    return 0.5 * x * (1.0 + jnp.tanh(np.float32(np.sqrt(2.0 / np.pi)) * (x + 0.044715 * (x * x * x))))


def _layer_norm_rows(r, g, b):
    mu = jnp.mean(r, axis=-1, keepdims=True)
    d = r - mu
    var = jnp.mean(d * d, axis=-1, keepdims=True)
    return d * lax.rsqrt(var + LN_EPS) * g + b


def _mm_split_kernel(x_ref, w_ref, *o_refs, cuts):
    acc = _dot(x_ref[...].astype(BF16), w_ref[...])
    for o_ref, (lo, hi) in zip(o_refs, cuts):
        o_ref[...] = acc[:, lo:hi].astype(o_ref.dtype)


def matmul_split(x, w_bf16, widths, dtypes, tm=MATMUL_ROWS):
    m, k = x.shape
    n = w_bf16.shape[1]
    assert sum(widths) == n and all(wd % LANES == 0 for wd in widths)
    tm = _row_tile(m, tm)
    cuts, lo = [], 0
    for wd in widths:
        cuts.append((lo, lo + wd))
        lo += wd
    return pl.pallas_call(
        functools.partial(_mm_split_kernel, cuts=tuple(cuts)),
        grid=(m // tm,),
        in_specs=[pl.BlockSpec((tm, k), lambda i: (i, 0)),
                  pl.BlockSpec((k, n), lambda i: (0, 0))],
        out_specs=[pl.BlockSpec((tm, wd), lambda i: (i, 0)) for wd in widths],
        out_shape=[jax.ShapeDtypeStruct((m, wd), dt) for wd, dt in zip(widths, dtypes)],
        compiler_params=_cparams(1),
        name="matmul_split",
    )(x, w_bf16)


def _nsa_proj_kernel(x_ref, wq_ref, wt_ref, q_ref, g_ref, gt_ref, *kv_refs, nq, gd):
    xb = x_ref[...].astype(BF16)
    acc = _dot(xb, wq_ref[...])
    q_ref[...] = acc[:, :nq].astype(q_ref.dtype)
    g_ref[...] = acc[:, nq:]
    acc_t = _nt_dot(wt_ref[...], xb)
    for i, r in enumerate(kv_refs):
        r[0] = acc_t[i * gd:(i + 1) * gd, :]
    gt_ref[0] = acc_t[len(kv_refs) * gd:, :]


def nsa_projection(x, n_seq, wq_bf16, wt_bf16, q_dtype, n_kv=6, tm=MATMUL_ROWS):
    m, d = x.shape
    seq = m // n_seq
    tm = _row_tile(seq, tm)
    nt = seq // tm
    nq = wq_bf16.shape[1] - LANES
    gd = (wt_bf16.shape[0] - LANES) // n_kv
    fm = lambda rows: pl.BlockSpec((1, rows, tm), lambda s, i: (s, 0, i))
    return pl.pallas_call(
        functools.partial(_nsa_proj_kernel, nq=nq, gd=gd),
        grid=(n_seq, nt),
        in_specs=[pl.BlockSpec((tm, d), lambda s, i: (s * nt + i, 0)),
                  pl.BlockSpec((d, nq + LANES), lambda s, i: (0, 0)),
                  pl.BlockSpec((n_kv * gd + LANES, d), lambda s, i: (0, 0))],
        out_specs=[pl.BlockSpec((tm, nq), lambda s, i: (s * nt + i, 0)),
                   pl.BlockSpec((tm, LANES), lambda s, i: (s * nt + i, 0)),
                   fm(LANES)] + [fm(gd) for _ in range(n_kv)],
        out_shape=[jax.ShapeDtypeStruct((m, nq), q_dtype), jax.ShapeDtypeStruct((m, LANES), F32),
                   jax.ShapeDtypeStruct((n_seq, LANES, seq), F32)] + [
            jax.ShapeDtypeStruct((n_seq, gd, seq), F32) for _ in range(n_kv)],
        compiler_params=_cparams(2),
        name="nsa_projection",
    )(x, wq_bf16, wt_bf16)


def _mm_res_ln_kernel(a_ref, w_ref, x_ref, g_ref, b_ref, o_ref):
    y = _dot(a_ref[...].astype(BF16), w_ref[...])
    o_ref[...] = _layer_norm_rows(ALPHA * x_ref[...] + y, g_ref[...], b_ref[...])


def matmul_residual_ln(a, w_bf16, x, g, b, tm=MATMUL_ROWS):
    m, k = a.shape
    d = w_bf16.shape[1]
    tm = _row_tile(m, tm)
    return pl.pallas_call(
        _mm_res_ln_kernel,
        grid=(m // tm,),
        in_specs=[pl.BlockSpec((tm, k), lambda i: (i, 0)),
                  pl.BlockSpec((k, d), lambda i: (0, 0)),
                  pl.BlockSpec((tm, d), lambda i: (i, 0)),
                  pl.BlockSpec((1, d), lambda i: (0, 0)),
                  pl.BlockSpec((1, d), lambda i: (0, 0))],
        out_specs=pl.BlockSpec((tm, d), lambda i: (i, 0)),
        out_shape=jax.ShapeDtypeStruct((m, d), F32),
        compiler_params=_cparams(1),
        name="matmul_residual_ln",
    )(a, w_bf16, x, g.reshape(1, d), b.reshape(1, d))


def _sum2_mm_res_ln_kernel(a_ref, b2_ref, w_ref, x_ref, g_ref, b_ref, o_ref):
    y = _dot((a_ref[...] + b2_ref[...]).astype(BF16), w_ref[...])
    o_ref[...] = _layer_norm_rows(ALPHA * x_ref[...] + y, g_ref[...], b_ref[...])


def sum2_matmul_residual_ln(a, b2, w_bf16, x, g, b, tm=MATMUL_ROWS):
    m, k = a.shape
    d = w_bf16.shape[1]
    tm = _row_tile(m, tm)
    row = lambda i: (i, 0)
    fixed = lambda i: (0, 0)
    return pl.pallas_call(
        _sum2_mm_res_ln_kernel,
        grid=(m // tm,),
        in_specs=[pl.BlockSpec((tm, k), row), pl.BlockSpec((tm, k), row),
                  pl.BlockSpec((k, d), fixed),
                  pl.BlockSpec((tm, d), row),
                  pl.BlockSpec((1, d), fixed), pl.BlockSpec((1, d), fixed)],
        out_specs=pl.BlockSpec((tm, d), row),
        out_shape=jax.ShapeDtypeStruct((m, d), F32),
        compiler_params=_cparams(1),
        name="sum2_matmul_residual_ln",
    )(a, b2, w_bf16, x, g.reshape(1, d), b.reshape(1, d))


def _expand_gates(gates_raw, e_ref):
    sig = jax.nn.sigmoid(gates_raw)
    hi = sig.astype(BF16)
    lo = (sig - hi.astype(F32)).astype(BF16)
    e = e_ref[...]
    return _dot(hi, e) + _dot(lo, e)


def _nsa_merge_ln_kernel(oc_ref, os_ref, ow_ref, gt_ref, e_ref, w_ref, x_ref, g_ref, b_ref, o_ref, *, k):
    gx = _expand_gates(gt_ref[...], e_ref)
    o = gx[:, 0:k] * oc_ref[...] + gx[:, k:2 * k] * os_ref[...] + gx[:, 2 * k:3 * k] * ow_ref[...]
    y = _dot(o.astype(BF16), w_ref[...])
    o_ref[...] = _layer_norm_rows(ALPHA * x_ref[...] + y, g_ref[...], b_ref[...])


def nsa_merge_residual_ln(oc, osl, ow, gates, e_bf16, w_bf16, x, g, b, tm=MERGE_ROWS):
    m, k = oc.shape
    d = w_bf16.shape[1]
    tm = _row_tile(m, tm)
    row = lambda i: (i, 0)
    fixed = lambda i: (0, 0)
    return pl.pallas_call(
        functools.partial(_nsa_merge_ln_kernel, k=k),
        grid=(m // tm,),
        in_specs=[pl.BlockSpec((tm, k), row), pl.BlockSpec((tm, k), row), pl.BlockSpec((tm, k), row),
                  pl.BlockSpec((tm, LANES), row),
                  pl.BlockSpec((LANES, 3 * k), fixed),
                  pl.BlockSpec((k, d), fixed),
                  pl.BlockSpec((tm, d), row),
                  pl.BlockSpec((1, d), fixed), pl.BlockSpec((1, d), fixed)],
        out_specs=pl.BlockSpec((tm, d), row),
        out_shape=jax.ShapeDtypeStruct((m, d), F32),
        compiler_params=_cparams(1),
        name="nsa_merge_residual_ln",
    )(oc, osl, ow, gates, e_bf16, w_bf16, x, g.reshape(1, d), b.reshape(1, d))


def _even_mixer_kernel(z_ref, hist_ref, st_ref, cos_ref, sin_ref, decay_ref, qdec_ref, kdec_ref, sdec_ref,
                       cw_ref, cb_ref, gn_ref, y_ref, hist_out_ref, st_out_ref, carry, state,
                       *, rows, valid, dconv, dk):
    c = pl.program_id(1)
    r0 = valid - 2 - (rows - SUBLANES)

    @pl.when(c == 0)
    def _():
        carry[r0:r0 + 2, :] = hist_ref[0]
        state[...] = st_ref[0]

    d = dconv
    h = z_ref[:, 0:d]
    gate_b = z_ref[:, d:2 * d]
    gate_c = z_ref[:, 2 * d:3 * d]
    ch = gate_c * h
    row = lax.broadcasted_iota(jnp.int32, (rows, d), 0)
    h0 = carry[r0:r0 + 1, :]
    h1 = carry[r0 + 1:r0 + 2, :]
    m1 = jnp.where(row == 0, h1, pltpu.roll(ch, 1, 0))
    m2 = jnp.where(row == 0, h0, jnp.where(row == 1, h1, pltpu.roll(ch, 2, 0)))
    u = ((cb_ref[...] + m2 * cw_ref[0:1, :]) + m1 * cw_ref[1:2, :]) + ch * cw_ref[2:3, :]
    y_ref[:, 0:d] = gate_b * u
    carry[...] = ch[rows - SUBLANES:rows, :]
    hist_out_ref[0] = carry[r0:r0 + 2, :]

    cosf = cos_ref[...]
    sinf = sin_ref[...]
    scale = np.float32(dk ** -0.5)
    heads = range(RET_HEADS)
    col = lambda part, hh: slice(part * d + hh * dk, part * d + (hh + 1) * dk)
    qs, ks, vbs = [], [], []
    for hh in heads:
        q = z_ref[:, col(3, hh)]
        k = z_ref[:, col(4, hh)]
        qs.append(((q * cosf + pltpu.roll(q, dk // 2, 1) * sinf) * scale).astype(BF16))
        ks.append(k * cosf + pltpu.roll(k, dk // 2, 1) * sinf)
        vbs.append(z_ref[:, col(5, hh)].astype(BF16))
    s_old = [state[hh] for hh in heads]
    scores = [_nt_dot(qs[hh], ks[hh].astype(BF16)) * decay_ref[hh] for hh in heads]
    cross = [_dot(qs[hh], s_old[hh].astype(BF16)) * qdec_ref[hh] for hh in heads]
    intra = [_dot(scores[hh].astype(BF16), vbs[hh]) for hh in heads]
    for hh in heads:
        kd = (ks[hh] * kdec_ref[hh]).astype(BF16)
        state[hh] = s_old[hh] * sdec_ref[hh] + _tn_dot(kd, vbs[hh])
    for hh in heads:
        o = intra[hh] + cross[hh]
        mu = jnp.mean(o, axis=-1, keepdims=True)
        dv = o - mu
        var = jnp.mean(dv * dv, axis=-1, keepdims=True)
        on = dv * lax.rsqrt(var + LN_EPS) * gn_ref[:, hh * dk:(hh + 1) * dk]
        gsw = z_ref[:, col(6, hh)]
        y_ref[:, d + hh * dk:d + (hh + 1) * dk] = (gsw * jax.nn.sigmoid(gsw)) * on
    st_out_ref[0] = state[...]


def _retention_tables(rows, valid, dk):
    log_gamma = jnp.log1p(-jnp.exp2(-5.0 - jnp.arange(RET_HEADS, dtype=F32)))
    n = jnp.arange(rows, dtype=F32)
    diff = n[:, None] - n[None, :]
    lg = log_gamma[:, None, None]
    decay = jnp.where(diff >= 0, jnp.exp(lg * jnp.maximum(diff, 0.0)), 0.0)
    q_dec = jnp.exp((n[None, :] + 1.0) * log_gamma[:, None])
    k_dec = jnp.where(n[None, :] < valid, jnp.exp((valid - 1.0 - n[None, :]) * log_gamma[:, None]), 0.0)
    s_dec = jnp.exp(valid * log_gamma)
    bc = lambda a: jnp.broadcast_to(a[:, :, None], (RET_HEADS, rows, dk))
    return decay, bc(q_dec), bc(k_dec), jnp.broadcast_to(s_dec[:, None, None], (RET_HEADS, 1, dk))


def _rope_tables(pos, dk):
    half = dk // 2
    inv = ROPE_BASE ** (-jnp.arange(half, dtype=F32) / half)
    ang = pos.astype(F32)[:, None] * inv
    cos, sin = jnp.cos(ang), jnp.sin(ang)
    return jnp.concatenate([cos, cos], axis=-1), jnp.concatenate([-sin, sin], axis=-1)


def even_mixer(z, hist, st, pos, rows, valid, conv_w, conv_b, gn_g):
    n_seq, _, dconv = hist.shape
    dk = st.shape[-1]
    n_chunks = z.shape[0] // (n_seq * rows)
    cosf, sinf = _rope_tables(pos, dk)
    decay, q_dec, k_dec, s_dec = _retention_tables(rows, valid, dk)
    fixed3 = lambda s, c: (0, 0, 0)
    fixed2 = lambda s, c: (0, 0)
    return pl.pallas_call(
        functools.partial(_even_mixer_kernel, rows=rows, valid=valid, dconv=dconv, dk=dk),
        grid=(n_seq, n_chunks),
        in_specs=[pl.BlockSpec((rows, 7 * dconv), lambda s, c: (s * n_chunks + c, 0)),
                  pl.BlockSpec((1, 2, dconv), lambda s, c: (s, 0, 0)),
                  pl.BlockSpec((1, RET_HEADS, dk, dk), lambda s, c: (s, 0, 0, 0)),
                  pl.BlockSpec((rows, dk), lambda s, c: (c, 0)),
                  pl.BlockSpec((rows, dk), lambda s, c: (c, 0)),
                  pl.BlockSpec((RET_HEADS, rows, rows), fixed3),
                  pl.BlockSpec((RET_HEADS, rows, dk), fixed3),
                  pl.BlockSpec((RET_HEADS, rows, dk), fixed3),
                  pl.BlockSpec((RET_HEADS, 1, dk), fixed3),
                  pl.BlockSpec((SCONV_W, dconv), fixed2),
                  pl.BlockSpec((1, dconv), fixed2),
                  pl.BlockSpec((1, RET_HEADS * dk), fixed2)],
        out_specs=[pl.BlockSpec((rows, 2 * dconv), lambda s, c: (s * n_chunks + c, 0)),
                   pl.BlockSpec((1, 2, dconv), lambda s, c: (s, 0, 0)),
                   pl.BlockSpec((1, RET_HEADS, dk, dk), lambda s, c: (s, 0, 0, 0))],
        out_shape=[jax.ShapeDtypeStruct((z.shape[0], 2 * dconv), F32),
                   jax.ShapeDtypeStruct((n_seq, 2, dconv), F32),
                   jax.ShapeDtypeStruct((n_seq, RET_HEADS, dk, dk), F32)],
        scratch_shapes=[pltpu.VMEM((SUBLANES, dconv), F32), pltpu.VMEM((RET_HEADS, dk, dk), F32)],
        compiler_params=_cparams(2),
        name="even_mixer",
    )(z, hist, st, cosf, sinf, decay, q_dec, k_dec, s_dec, conv_w, conv_b.reshape(1, dconv),
      gn_g.reshape(1, RET_HEADS * dk))


def _conv_gate(a, gate, m1, m2, cw_ref, cb_ref):
    conv = ((cb_ref[...] + m2 * cw_ref[0:1, :]) + m1 * cw_ref[1:2, :]) + a * cw_ref[2:3, :]
    return _gelu(conv) * gate


def _ffn_up_seq_kernel(x_ref, wa_ref, wg_ref, h_ref, cw_ref, cb_ref, o_ref, hist_out_ref, carry, *, tm):
    @pl.when(pl.program_id(2) == 0)
    def _():
        carry[SUBLANES - 2:SUBLANES, :] = h_ref[0]

    xb = x_ref[...].astype(BF16)
    a = _dot(xb, wa_ref[...])
    gate = _dot(xb, wg_ref[...])
    row = lax.broadcasted_iota(jnp.int32, a.shape, 0)
    h0 = carry[SUBLANES - 2:SUBLANES - 1, :]
    h1 = carry[SUBLANES - 1:SUBLANES, :]
    m1 = jnp.where(row == 0, h1, pltpu.roll(a, 1, 0))
    m2 = jnp.where(row == 0, h0, jnp.where(row == 1, h1, pltpu.roll(a, 2, 0)))
    o_ref[...] = _conv_gate(a, gate, m1, m2, cw_ref, cb_ref).astype(o_ref.dtype)
    carry[...] = a[tm - SUBLANES:tm, :]
    hist_out_ref[0] = carry[SUBLANES - 2:SUBLANES, :]


def ffn_up_sequences(x, n_seq, hist, w_up_bf16, conv_w, conv_b, tm=MATMUL_ROWS, n_col=2):
    m, k = x.shape
    dff = conv_w.shape[1]
    seq = m // n_seq
    tm = _row_tile(seq, tm)
    tps = seq // tm
    tn = dff // n_col
    assert tn % LANES == 0
    return pl.pallas_call(
        functools.partial(_ffn_up_seq_kernel, tm=tm),
        grid=(n_col, n_seq, tps),
        in_specs=[pl.BlockSpec((tm, k), lambda j, s, i: (s * tps + i, 0)),
                  pl.BlockSpec((k, tn), lambda j, s, i: (0, j)),
                  pl.BlockSpec((k, tn), lambda j, s, i: (0, j + n_col)),
                  pl.BlockSpec((1, 2, tn), lambda j, s, i: (s, 0, j)),
                  pl.BlockSpec((FFN_W, tn), lambda j, s, i: (0, j)),
                  pl.BlockSpec((1, tn), lambda j, s, i: (0, j))],
        out_specs=[pl.BlockSpec((tm, tn), lambda j, s, i: (s * tps + i, j)),
                   pl.BlockSpec((1, 2, tn), lambda j, s, i: (s, 0, j))],
        out_shape=[jax.ShapeDtypeStruct((m, dff), BF16),
                   jax.ShapeDtypeStruct((n_seq, 2, dff), F32)],
        scratch_shapes=[pltpu.VMEM((SUBLANES, tn), F32)],
        compiler_params=_cparams(3),
        name="ffn_up_sequences",
    )(x, w_up_bf16, w_up_bf16, hist, conv_w, conv_b.reshape(1, dff))


def _ffn_up_short_kernel(x_ref, wa_ref, wg_ref, h1_ref, h2_ref, cw_ref, cb_ref, o_ref, a_ref):
    xb = x_ref[...].astype(BF16)
    a = _dot(xb, wa_ref[...])
    gate = _dot(xb, wg_ref[...])
    t = lax.broadcasted_iota(jnp.int32, a.shape, 0) % SUBLANES
    m1 = jnp.where(t == 0, h1_ref[...], pltpu.roll(a, 1, 0))
    m2 = jnp.where(t < 2, h2_ref[...], pltpu.roll(a, 2, 0))
    o_ref[...] = _conv_gate(a, gate, m1, m2, cw_ref, cb_ref).astype(o_ref.dtype)
    a_ref[...] = a


def ffn_up_short(x, hist, w_up_bf16, conv_w, conv_b, n_col=2):
    m, k = x.shape
    dff = conv_w.shape[1]
    n_seq = m // SUBLANES
    tn = dff // n_col
    zeros = jnp.zeros((n_seq, SUBLANES, dff), F32)
    h1 = zeros.at[:, 0].set(hist[:, 1]).reshape(m, dff)
    h2 = zeros.at[:, 0].set(hist[:, 0]).at[:, 1].set(hist[:, 1]).reshape(m, dff)
    col = lambda j: (0, j)
    return pl.pallas_call(
        _ffn_up_short_kernel,
        grid=(n_col,),
        in_specs=[pl.BlockSpec((m, k), lambda j: (0, 0)),
                  pl.BlockSpec((k, tn), col),
                  pl.BlockSpec((k, tn), lambda j: (0, j + n_col)),
                  pl.BlockSpec((m, tn), col), pl.BlockSpec((m, tn), col),
                  pl.BlockSpec((FFN_W, tn), col), pl.BlockSpec((1, tn), col)],
        out_specs=[pl.BlockSpec((m, tn), col), pl.BlockSpec((m, tn), col)],
        out_shape=[jax.ShapeDtypeStruct((m, dff), F32), jax.ShapeDtypeStruct((m, dff), F32)],
        compiler_params=_cparams(1),
        name="ffn_up_short",
    )(x, w_up_bf16, w_up_bf16, h1, h2, conv_w, conv_b.reshape(1, dff))


def _compress_kernel(pt_ref, *refs, pages):
    page_refs = refs[:pages + 1]
    w_ref, pecol_ref, w1_ref, w2_ref, o_ref = refs[pages + 1:pages + 6]
    rows_refs = refs[pages + 6:]
    parts = len(rows_refs)
    ppp = (pages + 1) // parts
    cpp = PAGE_SIZE // CMP_STRIDE
    n = (pages + 1) * cpp
    n_p = ppp * cpp
    hidden = w1_ref.shape[1]
    gpr = LANES // NSA_DH
    pieces = NSA_KV // gpr
    for i, r in enumerate(page_refs):
        for pc in range(pieces):
            tile = r[0, pc * gpr:(pc + 1) * gpr].reshape(LANES, PAGE_SIZE)
            rows_refs[i // ppp][pc, (i % ppp) * PAGE_SIZE:(i % ppp + 1) * PAGE_SIZE, :] = tile.T
    pe_term = jnp.sum(pecol_ref[...] * w1_ref[...], axis=0, keepdims=True)
    accs = []
    for rows_ref in rows_refs:
        lhs = jnp.concatenate([rows_ref[pc].reshape(n_p, CMP_STRIDE * LANES) for pc in range(pieces)], axis=0)
        accs.append(_dot(lhs.astype(BF16), w_ref[...]))
    for pc in range(pieces):
        for gl in range(gpr):
            a = jnp.concatenate([acc[pc * n_p:(pc + 1) * n_p, gl * 2 * hidden:(gl + 1) * 2 * hidden]
                                 for acc in accs], axis=0)
            nxt = pltpu.roll(a, n - 1, 0)
            pre = pe_term + a[:, 0:hidden]
            pre = pre + nxt[:, hidden:2 * hidden]
            o_ref[0, pc * gpr + gl] = _dot(_gelu(pre[0:pages * cpp]).astype(BF16), w2_ref[...])


def compress(rows_t, page_table, pe, w1, w2, pages=COMPRESS_PAGES):
    pooled = page_table is not None
    if pooled:
        n_seq, n_pages = page_table.shape
    else:
        n_seq, n_pages = rows_t.shape[0], rows_t.shape[3] // PAGE_SIZE
        page_table = jnp.zeros((1, 1), jnp.int32)
    pages = min(pages, n_pages)
    assert n_pages % pages == 0
    parts = next(p for p in (3, 5, 1) if (pages + 1) % p == 0)
    hidden = w1.shape[1]
    cpp = PAGE_SIZE // CMP_STRIDE
    r = CMP_LEN // CMP_STRIDE
    gpr = LANES // NSA_DH
    assert r == 2 and gpr == 2
    w1p = w1.reshape(r, CMP_STRIDE, NSA_DH, hidden)
    w16 = jnp.concatenate([w1p[0], w1p[1]], axis=-1)
    zero = jnp.zeros_like(w16)
    per_tok = jnp.concatenate([jnp.concatenate([w16, zero], axis=-1),
                               jnp.concatenate([zero, w16], axis=-1)], axis=1)
    w_chunk = per_tok.reshape(CMP_STRIDE * LANES, gpr * 2 * hidden).astype(BF16)
    w2p = jnp.pad(w2, ((0, 0), (0, SLOT - NSA_DH))).astype(BF16)
    pecol = pe.reshape(CMP_LEN * NSA_DH, 1)

    def page_map(i):
        if pooled:
            return lambda s, j, pt: (pt[s, jnp.minimum(j * pages + i, n_pages - 1)], 0, 0, 0)
        return lambda s, j, pt: (s, 0, 0, jnp.minimum(j * pages + i, n_pages - 1))

    fixed2 = lambda s, j, pt: (0, 0)
    grid_spec = pltpu.PrefetchScalarGridSpec(
        num_scalar_prefetch=1,
        grid=(n_seq, n_pages // pages),
        in_specs=[pl.BlockSpec((1, NSA_KV, NSA_DH, PAGE_SIZE), page_map(i)) for i in range(pages + 1)] + [
            pl.BlockSpec((CMP_STRIDE * LANES, gpr * 2 * hidden), fixed2),
            pl.BlockSpec((CMP_LEN * NSA_DH, 1), fixed2),
            pl.BlockSpec((CMP_LEN * NSA_DH, hidden), fixed2),
            pl.BlockSpec((hidden, SLOT), fixed2)],
        out_specs=pl.BlockSpec((1, NSA_KV, pages * cpp, SLOT), lambda s, j, pt: (s, 0, j, 0)),
        scratch_shapes=[pltpu.VMEM((NSA_KV // gpr, (pages + 1) // parts * PAGE_SIZE, LANES), F32)
                        for _ in range(parts)],
    )
    return pl.pallas_call(
        functools.partial(_compress_kernel, pages=pages),
        grid_spec=grid_spec,
        out_shape=jax.ShapeDtypeStruct((n_seq, NSA_KV, n_pages * cpp, SLOT), F32),
        compiler_params=_cparams(2),
        name="compress",
    )(page_table, *([rows_t] * (pages + 1)), w_chunk, pecol, w1, w2p)


def _cmp_select_kernel(q_ref, kc_ref, vc_ref, ov_ref, o_ref, idx_ref, *, tq, n_cmp, n_slc, pos0):
    ncp = kc_ref.shape[2]
    nsp = ov_ref.shape[0]
    hrows = NSA_GROUP * tq
    q_pos = pos0 + (lax.broadcasted_iota(jnp.int32, (hrows, ncp), 0) & (tq - 1))
    blk_i = lax.broadcasted_iota(jnp.int32, (hrows, ncp), 1)
    valid = (blk_i * CMP_STRIDE + (CMP_LEN - 1) <= q_pos) & (blk_i < n_cmp)
    ov = ov_ref[...]
    groups = range(NSA_KV)
    heads = [range(g * NSA_GROUP, (g + 1) * NSA_GROUP) for g in groups]
    scores = [_nt_dot(jnp.concatenate([q_ref[:, h * SLOT:(h + 1) * SLOT] for h in heads[g]], axis=0).astype(BF16),
                      kc_ref[0, g].astype(BF16)) for g in groups]
    probs = []
    for g in groups:
        s = jnp.where(valid, scores[g], NEG_INF)
        m = jnp.max(s, axis=-1, keepdims=True)
        e = jnp.where(valid, jnp.exp(s - m), 0.0)
        den = jnp.sum(e, axis=-1, keepdims=True)
        probs.append(e * (1.0 / jnp.where(den > 0.0, den, 1.0)))
    outs = [_dot(probs[g].astype(BF16), vc_ref[0, g].astype(BF16)) for g in groups]
    imps = []
    for g in groups:
        p_sum = jnp.zeros((tq, ncp), F32)
        for j, h in enumerate(heads[g]):
            o_ref[:, h * SLOT:(h + 1) * SLOT] = outs[g][j * tq:(j + 1) * tq]
            p_sum = p_sum + probs[g][j * tq:(j + 1) * tq]
        hi = p_sum.astype(BF16)
        lo = (p_sum - hi.astype(F32)).astype(BF16)
        imps.append(_nt_dot(ov, hi) + _nt_dot(ov, lo))
    cols = NSA_KV * tq
    imp = jnp.concatenate(imps, axis=1)
    blk = lax.broadcasted_iota(jnp.int32, (nsp, cols), 0)
    cur = (pos0 + (lax.broadcasted_iota(jnp.int32, (nsp, cols), 1) & (tq - 1))) // SLC_BLOCK
    real = blk < n_slc
    causal = real & (blk <= cur)
    forced = (blk == 0) | (blk == cur) | (blk == cur - 1)
    score = jnp.where(causal, imp + jnp.where(forced, FORCE_BONUS, 0.0), NEG_INF)
    score = jnp.where(real, score, REMOVED)
    idx = jnp.zeros((SLC_TOPN, cols), jnp.int32)
    idx_row = lax.broadcasted_iota(jnp.int32, (SLC_TOPN, cols), 0)
    for it in range(min(SLC_TOPN, n_slc)):
        m = jnp.max(score, axis=0, keepdims=True)
        first = jnp.min(jnp.where(score == m, blk, nsp), axis=0, keepdims=True)
        score = jnp.where(blk == first, REMOVED, score)
        idx = jnp.where(idx_row == it, first, idx)
    idx_ref[0] = idx


def cmp_block_overlap(n_cmp_pad, n_cmp, n_slc, n_slc_pad, lane_off):
    i = np.arange(n_cmp_pad)[:, None]
    j = np.arange(n_slc_pad)[None, :] - lane_off
    start = i * CMP_STRIDE
    hit = (start < (j + 1) * SLC_BLOCK) & (start + CMP_LEN > j * SLC_BLOCK) & (i < n_cmp) & (j >= 0) & (j < n_slc)
    return jnp.asarray(hit.astype(np.float32), dtype=BF16)


def cmp_attention_select(q_slots, kcc, vcc, n_seq, n_cmp, n_slc, pos0):
    tokens = q_slots.shape[0]
    tq = tokens // n_seq
    assert tq & (tq - 1) == 0
    ncp = kcc.shape[2]
    nsp = -(-n_slc // SUBLANES) * SUBLANES
    ov_t = cmp_block_overlap(ncp, n_cmp, n_slc, nsp, 0).T
    hw = NSA_HEADS * SLOT
    o_cmp, idx_t = pl.pallas_call(
        functools.partial(_cmp_select_kernel, tq=tq, n_cmp=n_cmp, n_slc=n_slc, pos0=pos0),
        grid=(n_seq,),
        in_specs=[pl.BlockSpec((tq, hw), lambda s: (s, 0)),
                  pl.BlockSpec((1, NSA_KV, ncp, SLOT), lambda s: (s, 0, 0, 0)),
                  pl.BlockSpec((1, NSA_KV, ncp, SLOT), lambda s: (s, 0, 0, 0)),
                  pl.BlockSpec((nsp, ncp), lambda s: (0, 0))],
        out_specs=[pl.BlockSpec((tq, hw), lambda s: (s, 0)),
                   pl.BlockSpec((1, SLC_TOPN, NSA_KV * tq), lambda s: (s, 0, 0))],
        out_shape=[jax.ShapeDtypeStruct((tokens, hw), F32),
                   jax.ShapeDtypeStruct((n_seq, SLC_TOPN, NSA_KV * tq), jnp.int32)],
        compiler_params=_cparams(1),
        name="cmp_attention_select",
    )(q_slots, kcc, vcc, ov_t)
    topn = min(SLC_TOPN, n_slc)
    idx = idx_t[:, :topn].reshape(n_seq, topn, NSA_KV, tq).transpose(0, 2, 3, 1)
    return o_cmp, idx


def _cmp_select_prompt_kernel(q_ref, kc_ref, vc_ref, ovt_ref, gt_ref, o_ref, sel_ref, *, tq, n_cmp, n_slc):
    t0 = pl.program_id(2) * tq
    ncp = kc_ref.shape[2]
    nsr = ovt_ref.shape[0]
    kc = kc_ref[0, 0].astype(BF16)
    vc = vc_ref[0, 0].astype(BF16)
    q_pos = t0 + lax.broadcasted_iota(jnp.int32, (ncp, tq), 1)
    blk_i = lax.broadcasted_iota(jnp.int32, (ncp, tq), 0)
    valid = (blk_i * CMP_STRIDE + (CMP_LEN - 1) <= q_pos) & (blk_i < n_cmp)
    p_sum = jnp.zeros((ncp, tq), F32)
    for j in range(NSA_GROUP):
        s = jnp.where(valid, _nt_dot(kc, q_ref[:, j * SLOT:(j + 1) * SLOT]), NEG_INF)
        m = jnp.max(s, axis=0, keepdims=True)
        e = jnp.where(valid, jnp.exp(s - m), 0.0)
        den = jnp.sum(e, axis=0, keepdims=True)
        p = e * (1.0 / jnp.where(den > 0.0, den, 1.0))
        gate = jax.nn.sigmoid(gt_ref[0, pl.ds(pl.program_id(1) * NSA_GROUP + j, 1), :])
        o_ref[:, j * SLOT:(j + 1) * SLOT] = _tn_dot((p * gate).astype(BF16), vc)
        p_sum = p_sum + p
    hi = p_sum.astype(BF16)
    lo = (p_sum - hi.astype(F32)).astype(BF16)
    ovt = ovt_ref[...]
    imp = _dot(ovt, hi) + _dot(ovt, lo)
    blk = lax.broadcasted_iota(jnp.int32, (nsr, tq), 0)
    cur = (t0 + lax.broadcasted_iota(jnp.int32, (nsr, tq), 1)) // SLC_BLOCK
    real = blk < n_slc
    causal = real & (blk <= cur)
    forced = (blk == 0) | (blk == cur) | (blk == cur - 1)
    score = jnp.where(causal, imp + jnp.where(forced, FORCE_BONUS, 0.0), NEG_INF)
    score = jnp.where(real, score, REMOVED)
    picked = jnp.zeros((nsr, tq), jnp.bool_)
    for _ in range(min(SLC_TOPN, n_slc)):
        m = jnp.max(score, axis=0, keepdims=True)
        first = jnp.min(jnp.where(score == m, blk, nsr), axis=0, keepdims=True)
        hit = blk == first
        picked = picked | hit
        score = jnp.where(hit, REMOVED, score)
    bias_t = jnp.where(real & ~(picked & causal), NEG_INF, 0.0)
    slot_t = jnp.concatenate([jnp.zeros((NSA_DH, tq), F32), bias_t], axis=0)
    sel_ref[0, 0] = slot_t.T


def cmp_attention_select_prompt(q_slots, kcc, vcc, gates_t, n_seq, n_cmp, n_slc, tq):
    tokens = q_slots.shape[0]
    t = tokens // n_seq
    tq = _row_tile(t, tq)
    nt = t // tq
    ncp = kcc.shape[2]
    nsr = SLOT - NSA_DH
    assert n_slc <= nsr
    ovt = cmp_block_overlap(ncp, n_cmp, n_slc, nsr, 0).T
    gw = NSA_GROUP * SLOT
    return pl.pallas_call(
        functools.partial(_cmp_select_prompt_kernel, tq=tq, n_cmp=n_cmp, n_slc=n_slc),
        grid=(n_seq, NSA_KV, nt),
        in_specs=[pl.BlockSpec((tq, gw), lambda s, g, i: (s * nt + i, g)),
                  pl.BlockSpec((1, 1, ncp, SLOT), lambda s, g, i: (s, g, 0, 0)),
                  pl.BlockSpec((1, 1, ncp, SLOT), lambda s, g, i: (s, g, 0, 0)),
                  pl.BlockSpec((nsr, ncp), lambda s, g, i: (0, 0)),
                  pl.BlockSpec((1, LANES, tq), lambda s, g, i: (s, 0, i))],
        out_specs=[pl.BlockSpec((tq, gw), lambda s, g, i: (s * nt + i, g)),
                   pl.BlockSpec((1, 1, tq, SLOT), lambda s, g, i: (s, g, i, 0))],
        out_shape=[jax.ShapeDtypeStruct((tokens, NSA_HEADS * SLOT), F32),
                   jax.ShapeDtypeStruct((n_seq, NSA_KV, t, SLOT), F32)],
        compiler_params=_cparams(3),
        name="cmp_attention_select_prompt",
    )(q_slots, kcc, vcc, ovt, gates_t)


def _prompt_slc_win_kernel(q_ref, sel_ref, ks_ref, vs_ref, kw_ref, vw_ref, oh_ref, gt_ref, o_ref,
                           m_ref, acc_ref, *, tq, seq):
    acc_rows = NSA_DH + 16
    qi = pl.program_id(2)
    t0 = qi * tq
    rows = NSA_GROUP * tq
    sel = sel_ref[0, 0]
    q_plain = jnp.concatenate([q_ref[:, j * SLOT:(j + 1) * SLOT] for j in range(NSA_GROUP)], axis=0)
    q_aug = jnp.concatenate([(q_ref[:, j * SLOT:(j + 1) * SLOT].astype(F32) + sel).astype(BF16)
                             for j in range(NSA_GROUP)], axis=0)
    zeros_k = jnp.zeros((SLOT - NSA_DH, tq), F32)
    ones_row = (lax.broadcasted_iota(jnp.int32, (acc_rows - NSA_DH, tq), 0) == 0).astype(BF16)
    rel = lax.broadcasted_iota(jnp.int32, (tq, LANES), 1) - lax.broadcasted_iota(jnp.int32, (tq, LANES), 0)

    def scores(q_rows, k_top, k_bottom):
        k_rows = jnp.concatenate([k_top, k_bottom], axis=0).T.astype(BF16)
        return [_nt_dot(k_rows, q_rows[c:c + 2 * LANES]) for c in range(0, rows, 2 * LANES)]

    def update(s_t, v_top, start, mask):
        v_t = jnp.concatenate([v_top.astype(BF16), ones_row], axis=0)
        for cg in range(rows // (2 * LANES)):
            p_parts, a_parts = [], []
            for h in range(2):
                c0 = (2 * cg + h) * LANES
                x = s_t[cg][:, h * LANES:(h + 1) * LANES]
                if mask is not None:
                    lo, hi = mask
                    off = t0 - start + (c0 & (tq - 1))
                    keep = rel >= lo - off
                    if hi is not None:
                        keep = keep & (rel < hi - off)
                    x = jnp.where(keep, x, NEG_INF)
                m_old = m_ref[:, c0:c0 + LANES]
                m_new = jnp.maximum(m_old, jnp.max(x, axis=0, keepdims=True))
                m_ref[:, c0:c0 + LANES] = m_new
                a_parts.append(jnp.exp(m_old - m_new))
                p_parts.append(jnp.exp(x - m_new).astype(BF16))
            c0 = 2 * cg * LANES
            pv = _dot(v_t, jnp.concatenate(p_parts, axis=1))
            acc_ref[:, c0:c0 + 2 * LANES] = jnp.concatenate(a_parts, axis=1) * acc_ref[:, c0:c0 + 2 * LANES] + pv

    def reset():
        m_ref[...] = jnp.full(m_ref.shape, NEG_INF, F32)
        acc_ref[...] = jnp.zeros(acc_ref.shape, F32)

    slot_pad = jnp.zeros((SLOT - NSA_DH, tq), F32)

    def finish(branch, accumulate):
        for j in range(NSA_GROUP):
            a = acc_ref[:, j * tq:(j + 1) * tq]
            head = pl.program_id(1) * NSA_GROUP + j
            gate = jax.nn.sigmoid(gt_ref[0, pl.ds(branch * NSA_HEADS + head, 1), :])
            o_t = jnp.concatenate([a[0:NSA_DH] * (gate * (1.0 / a[NSA_DH:NSA_DH + 1, :])), slot_pad], axis=0)
            if accumulate:
                o_ref[:, j * SLOT:(j + 1) * SLOT] += o_t.T
            else:
                o_ref[:, j * SLOT:(j + 1) * SLOT] = o_t.T

    def slc_scores(start):
        return scores(q_aug, ks_ref[0, :, pl.ds(start, tq)], oh_ref[:, pl.ds(start, tq)])

    def slc_update(s_t, start, mask):
        update(s_t, vs_ref[0, :, pl.ds(start, tq)], start, mask)

    k0 = jnp.clip(t0 - WINDOW, 0, seq - WINDOW - tq)
    n_win = WINDOW // tq + 1
    win_start = [pl.multiple_of(k0 + i * tq, tq) for i in range(n_win)]
    win_mask = (0, WINDOW)

    def win_scores(i):
        return scores(q_plain, kw_ref[0, :, pl.ds(win_start[i], tq)], zeros_k)

    reset()

    def slc_tiles(first, count):
        starts = [pl.multiple_of((first + i) * tq, tq) for i in range(count)]
        s_all = [slc_scores(st) for st in starts]
        for s_t, st in zip(s_all, starts):
            slc_update(s_t, st, None)

    def quad(k4, carry):
        slc_tiles(4 * k4, 4)
        return carry

    lax.fori_loop(0, qi // 4, quad, 0)

    @pl.when(qi % 4 >= 2)
    def _():
        slc_tiles(4 * (qi // 4), 2)

    @pl.when(qi % 2 == 1)
    def _():
        slc_tiles(qi - 1, 1)

    diag = pl.multiple_of(t0, tq)
    s_diag = slc_scores(diag)
    s_win = [win_scores(i) for i in range(n_win)]
    slc_update(s_diag, diag, (0, None))
    finish(1, False)
    reset()
    for i in range(n_win):
        update(s_win[i], vw_ref[0, :, pl.ds(win_start[i], tq)], win_start[i], win_mask)
    finish(2, True)


def prompt_slc_win_attention(q_slots, sel, ks_t, vs_t, kw_t, vw_t, gates_t, n_seq, tq):
    tokens = q_slots.shape[0]
    seq = tokens // n_seq
    tq = _row_tile(seq, tq)
    assert tq & (tq - 1) == 0 and WINDOW % tq == 0 and seq >= WINDOW + tq
    nt = seq // tq
    gw = NSA_GROUP * SLOT
    assert tq % (2 * LANES) == 0
    onehot_t = jax.nn.one_hot(jnp.arange(seq) // SLC_BLOCK, SLOT - NSA_DH, dtype=F32).T
    kv_spec = pl.BlockSpec((1, NSA_DH, seq), lambda s, g, i: (s, g, 0))
    return pl.pallas_call(
        functools.partial(_prompt_slc_win_kernel, tq=tq, seq=seq),
        grid=(n_seq, NSA_KV, nt),
        in_specs=[pl.BlockSpec((tq, gw), lambda s, g, i: (s * nt + i, g)),
                  pl.BlockSpec((1, 1, tq, SLOT), lambda s, g, i: (s, g, i, 0)),
                  kv_spec, kv_spec, kv_spec, kv_spec,
                  pl.BlockSpec((SLOT - NSA_DH, seq), lambda s, g, i: (0, 0)),
                  pl.BlockSpec((1, LANES, tq), lambda s, g, i: (s, 0, i))],
        out_specs=pl.BlockSpec((tq, gw), lambda s, g, i: (s * nt + i, g)),
        out_shape=jax.ShapeDtypeStruct((tokens, NSA_HEADS * SLOT), F32),
        scratch_shapes=[pltpu.VMEM((1, NSA_GROUP * tq), F32),
                        pltpu.VMEM((NSA_DH + 16, NSA_GROUP * tq), F32)],
        compiler_params=_cparams(3),
        name="prompt_slc_win_attention",
    )(q_slots, sel, ks_t, vs_t, kw_t, vw_t, onehot_t, gates_t)


def _sample_slc_kernel(idx_ref, pg_ref, q_ref, kn_ref, vn_ref, pool_k, pool_v, o_ref, kbuf, vbuf, sems,
                       *, topn, past, t_pad, t_real):
    s_id, g_id = pl.program_id(0), pl.program_id(1)
    step = s_id * NSA_KV + g_id
    n_steps = pl.num_programs(0) * NSA_KV

    def copies(item_step, t, slot):
        base = (item_step * t_pad + t) * topn
        g = item_step % NSA_KV
        out = []
        for kk in range(topn):
            page = pg_ref[base + kk]
            out.append(pltpu.make_async_copy(pool_k.at[page, g], kbuf.at[slot, kk], sems.at[slot, 0]))
            out.append(pltpu.make_async_copy(pool_v.at[page, g], vbuf.at[slot, kk], sems.at[slot, 1]))
        return out

    @pl.when(step == 0)
    def _():
        for c in copies(step, 0, 0):
            c.start()

    for t in range(t_real):
        slot = t % 2
        if t + 1 < t_real:
            for c in copies(step, t + 1, 1 - slot):
                c.start()
        else:
            @pl.when(step + 1 < n_steps)
            def _():
                for c in copies(step + 1, 0, 1 - slot):
                    c.start()
        for c in copies(step, t, slot):
            c.wait()
        _sample_slc_token(idx_ref, q_ref, kn_ref, vn_ref, o_ref, kbuf[slot], vbuf[slot],
                          base=(step * t_pad + t) * topn, t=t, topn=topn, past=past)


def _sample_slc_token(idx_ref, q_ref, kn_ref, vn_ref, o_ref, k_tiles, v_tiles, *, base, t, topn, past):
    q_pos = past + t
    cur = q_pos // SLC_BLOCK
    first_new = past // SLC_BLOCK
    q = q_ref[0, 0, t].astype(BF16)
    kb = jnp.concatenate([k_tiles[kk] for kk in range(topn)], axis=1).astype(BF16)
    vb = jnp.concatenate([v_tiles[kk] for kk in range(topn)], axis=1).astype(BF16)
    n_keys = topn * PAGE_SIZE
    lane = lax.broadcasted_iota(jnp.int32, (1, n_keys), 1)
    slot = lane // PAGE_SIZE
    in_page = lane % PAGE_SIZE
    k_pos = in_page
    limit = jnp.zeros((1, n_keys), jnp.int32)
    n_new = jnp.int32(0)
    for kk in range(topn):
        b = idx_ref[base + kk]
        here = slot == kk
        k_pos = jnp.where(here, (b // 2) * PAGE_SIZE + in_page, k_pos)
        last = jnp.where(b <= cur, jnp.minimum(q_pos, past - 1), -1)
        limit = jnp.where(here, jnp.where(in_page // SLC_BLOCK == b % 2, last, -1), limit)
        n_new = n_new + jnp.where(b == first_new, 1, 0)
    valid = k_pos <= limit
    s_old = jnp.where(valid, _dot(q, kb), NEG_INF)
    new_lane = lax.broadcasted_iota(jnp.int32, (1, SUBLANES), 1)
    valid_new = past + new_lane <= jnp.where(n_new > 0, q_pos, past - 1)
    s_new = jnp.where(valid_new, _dot(q, kn_ref[0, 0].astype(BF16)), NEG_INF)
    m = jnp.maximum(jnp.max(s_old, axis=-1, keepdims=True), jnp.max(s_new, axis=-1, keepdims=True))
    p_old = jnp.exp(s_old - m)
    p_new = jnp.exp(s_new - m)
    l = jnp.sum(p_old, axis=-1, keepdims=True) + jnp.sum(p_new, axis=-1, keepdims=True)
    o = _nt_dot(p_old.astype(BF16), vb) + _nt_dot(p_new.astype(BF16), vn_ref[0, 0].astype(BF16))
    o_ref[0, 0, t] = o / l


def sample_slc_attention(q_rows, idx, page_table, pool_k, pool_v, k_new_t, v_new_t, t_real, past):
    n_seq, _, t_pad, _, dh = q_rows.shape
    topn = idx.shape[-1]
    assert past % SLC_BLOCK == 0 and t_real <= SUBLANES and PAGE_SIZE == 2 * SLC_BLOCK
    last_old = past // SLC_BLOCK - 1
    logical = jnp.clip(idx, 0, last_old) // 2
    n_pages = page_table.shape[1]
    hit = logical[..., None] == jnp.arange(n_pages, dtype=jnp.int32)
    phys = jnp.sum(jnp.where(hit, page_table[:, None, None, None, :], 0), axis=-1)

    assert t_real % 2 == 0
    new_spec = pl.BlockSpec((1, 1, dh, SUBLANES), lambda s, g, i_r, p_r: (s, g, 0, 0))
    hbm = pl.BlockSpec(memory_space=pl.ANY)
    grid_spec = pltpu.PrefetchScalarGridSpec(
        num_scalar_prefetch=2,
        grid=(n_seq, NSA_KV),
        in_specs=[pl.BlockSpec((1, 1, t_pad, SUBLANES, dh), lambda s, g, i_r, p_r: (s, g, 0, 0, 0)),
                  new_spec, new_spec, hbm, hbm],
        out_specs=pl.BlockSpec((1, 1, t_real, SUBLANES, dh), lambda s, g, i_r, p_r: (s, g, 0, 0, 0)),
        scratch_shapes=[pltpu.VMEM((2, topn, dh, PAGE_SIZE), F32),
                        pltpu.VMEM((2, topn, dh, PAGE_SIZE), F32),
                        pltpu.SemaphoreType.DMA((2, 2))],
    )
    return pl.pallas_call(
        functools.partial(_sample_slc_kernel, topn=topn, past=past, t_pad=t_pad, t_real=t_real),
        grid_spec=grid_spec,
        out_shape=jax.ShapeDtypeStruct((n_seq, NSA_KV, t_real, SUBLANES, dh), F32),
        compiler_params=_cparams(2),
        name="sample_slc_attention",
    )(idx.reshape(-1), phys.reshape(-1), q_rows, k_new_t, v_new_t, pool_k, pool_v)


def _sample_win_kernel(q_ref, wk_ref, wv_ref, kn_ref, vn_ref, o_ref, *, past, t_pad):
    rows = t_pad * SUBLANES
    wb = wk_ref.shape[-1]
    q = q_ref[0, 0].reshape(rows, q_ref.shape[-1]).astype(BF16)
    q_pos = past + lax.broadcasted_iota(jnp.int32, (rows, 1), 0) // SUBLANES
    k_pos = past - wb + lax.broadcasted_iota(jnp.int32, (1, wb), 1)
    dist = q_pos - k_pos
    valid = (dist >= 0) & (dist < WINDOW) & (k_pos >= 0)
    s_old = jnp.where(valid, _dot(q, wk_ref[0, 0].astype(BF16)), NEG_INF)
    n_pos = past + lax.broadcasted_iota(jnp.int32, (1, SUBLANES), 1)
    dist_n = q_pos - n_pos
    valid_n = (dist_n >= 0) & (dist_n < WINDOW)
    s_new = jnp.where(valid_n, _dot(q, kn_ref[0, 0].astype(BF16)), NEG_INF)
    m = jnp.maximum(jnp.max(s_old, axis=-1, keepdims=True), jnp.max(s_new, axis=-1, keepdims=True))
    p_old = jnp.exp(s_old - m)
    p_new = jnp.exp(s_new - m)
    l = jnp.sum(p_old, axis=-1, keepdims=True) + jnp.sum(p_new, axis=-1, keepdims=True)
    o = _nt_dot(p_old.astype(BF16), wv_ref[0, 0].astype(BF16)) + _nt_dot(p_new.astype(BF16),
                                                                      vn_ref[0, 0].astype(BF16))
    o_ref[0, 0] = (o / l).reshape(t_pad, SUBLANES, o.shape[-1])


def sample_win_attention(q_rows, win_k_t, win_v_t, k_new_t, v_new_t, past):
    n_seq, _, t_pad, _, dh = q_rows.shape
    wb = win_k_t.shape[-1]
    q_spec = pl.BlockSpec((1, 1, t_pad, SUBLANES, dh), lambda s, g: (s, g, 0, 0, 0))
    win_spec = pl.BlockSpec((1, 1, dh, wb), lambda s, g: (s, g, 0, 0))
    new_spec = pl.BlockSpec((1, 1, dh, SUBLANES), lambda s, g: (s, g, 0, 0))
    return pl.pallas_call(
        functools.partial(_sample_win_kernel, past=past, t_pad=t_pad),
        grid=(n_seq, NSA_KV),
        in_specs=[q_spec, win_spec, win_spec, new_spec, new_spec],
        out_specs=q_spec,
        out_shape=jax.ShapeDtypeStruct(q_rows.shape, F32),
        compiler_params=_cparams(2),
        name="sample_win_attention",
    )(q_rows, win_k_t, win_v_t, k_new_t, v_new_t)


def _to_slots(a):
    lead = a.shape[:-1]
    n = a.shape[-1] // NSA_DH
    a = a.reshape(*lead, n, NSA_DH)
    a = jnp.pad(a, [(0, 0)] * (a.ndim - 1) + [(0, SLOT - NSA_DH)])
    return a.reshape(*lead, n * SLOT)


def _odd_weights(w_in, w_out):
    d = w_in.shape[0]
    hq = NSA_HEADS * NSA_DH
    kvw = NSA_KV * NSA_DH
    wq = _to_slots(w_in[:, :hq] * np.float32(NSA_DH ** -0.5))
    wg = jnp.pad(w_in[:, hq + 6 * kvw:], ((0, 0), (0, LANES - 3 * NSA_HEADS)))
    w_q = jnp.concatenate([wq, wg], axis=1).astype(BF16)
    w_kvt = jnp.concatenate([w_in[:, hq:hq + 6 * kvw], wg], axis=1).T.astype(BF16)
    wo = jnp.pad(w_out.reshape(NSA_HEADS, NSA_DH, d), ((0, 0), (0, SLOT - NSA_DH), (0, 0)))
    wo = wo.reshape(NSA_HEADS * SLOT, d).astype(BF16)
    k = NSA_HEADS * SLOT
    e = np.zeros((LANES, 3 * k), np.float32)
    for c in range(3):
        for h in range(NSA_HEADS):
            e[c * NSA_HEADS + h, c * k + h * SLOT:c * k + (h + 1) * SLOT] = 1.0
    return w_q, w_kvt, wo, jnp.asarray(e, dtype=BF16)


def _group_rows(q_slots, n_seq, t_pad):
    q = q_slots.reshape(n_seq, t_pad, NSA_KV, NSA_GROUP, SLOT)[..., :NSA_DH]
    q = q.transpose(0, 2, 1, 3, 4)
    return jnp.pad(q, ((0, 0), (0, 0), (0, 0), (0, SUBLANES - NSA_GROUP), (0, 0)))


def _ungroup_rows(o, n_seq, t_pad):
    t = o.shape[2]
    o = o[:, :, :, :NSA_GROUP].transpose(0, 2, 1, 3, 4)
    o = jnp.pad(o, ((0, 0), (0, t_pad - t), (0, 0), (0, 0), (0, SLOT - NSA_DH)))
    return o.reshape(n_seq * t_pad, NSA_HEADS * SLOT)


def _feature_major(cache):
    return cache.transpose(0, 2, 3, 1)


def _token_major(a_t):
    return a_t.transpose(0, 3, 1, 2)


def _pad_rows(a, t_pad):
    return jnp.pad(a, ((0, 0), (0, t_pad - a.shape[1])) + ((0, 0),) * (a.ndim - 2))


def kernel(x_prompt, x_sample, state_sconv, state_ret, cache_cmp_k, cache_cmp_v, cache_slc_k, cache_slc_v,
           cache_win_k, cache_win_v, state_ffn_conv, page_table,
           w_in_even, sconv_w, sconv_b, ret_gn_g, w_out_even,
           w_in_odd, cmp_pe, cmp_w1, cmp_w2, w_out_odd,
           ln_mix_g, ln_mix_b, ffn_w_up, ffn_conv_w, ffn_conv_b, ffn_w_down, ln_ffn_g, ln_ffn_b):
    b_p, s_p, d_model = x_prompt.shape
    b_s, t_s, _ = x_sample.shape
    n_pages = page_table.shape[1]
    past = n_pages * PAGE_SIZE
    t_pad = SUBLANES
    assert t_s <= t_pad and t_s >= SCONV_W - 1 and t_s < CMP_STRIDE and past % PAGE_SIZE == 0
    assert s_p % RET_CHUNK == 0 and s_p % PAGE_SIZE == 0
    d_sconv = sconv_w.shape[-1]
    d_ff = ffn_conv_w.shape[-1]
    gd = NSA_KV * NSA_DH
    depth = ln_mix_g.shape[0]

    xp = x_prompt.reshape(b_p * s_p, d_model)
    xs = _pad_rows(x_sample, t_pad).reshape(b_s * t_pad, d_model)
    outs = {k: [] for k in ("sconv_p", "sconv_s", "ret_p", "ret_s", "cmp_k_p", "cmp_v_p", "slc_k_p", "slc_v_p",
                            "cmp_k_s", "cmp_v_s", "slc_k_s", "slc_v_s", "win_k_p", "win_v_p", "win_k_s",
                            "win_v_s", "ffn_p", "ffn_s")}

    for layer in range(depth):
        if layer % 2 == 0:
            e = layer // 2
            w_in = w_in_even[e].astype(BF16)
            w_out = w_out_even[e].astype(BF16)
            n_in = w_in.shape[1]
            (zp,) = matmul_split(xp, w_in, [n_in], [F32])
            yp, hc, st = even_mixer(zp, jnp.zeros((b_p, SCONV_W - 1, d_sconv), F32),
                                    jnp.zeros((b_p,) + state_ret.shape[2:], F32), jnp.arange(s_p),
                                    RET_CHUNK, RET_CHUNK, sconv_w[e], sconv_b[e], ret_gn_g[e])
            outs["sconv_p"].append(hc)
            outs["ret_p"].append(st)
            xp = matmul_residual_ln(yp, w_out, xp, ln_mix_g[layer], ln_mix_b[layer])
            (zs,) = matmul_split(xs, w_in, [n_in], [F32])
            ys, hc, st = even_mixer(zs, state_sconv[e], state_ret[e], past + jnp.arange(t_pad),
                                    t_pad, t_s, sconv_w[e], sconv_b[e], ret_gn_g[e])
            outs["sconv_s"].append(hc)
            outs["ret_s"].append(st)
            xs = matmul_residual_ln(ys, w_out, xs, ln_mix_g[layer], ln_mix_b[layer])
        else:
            o = layer // 2
            w_q, w_kvt, w_out, e_gate = _odd_weights(w_in_odd[o], w_out_odd[o])
            pe, w1, w2 = cmp_pe[o], cmp_w1[o], cmp_w2[o]
            qp, _, gtp, kc, vc, ks, vs, kw, vw = nsa_projection(xp, b_p, w_q, w_kvt, BF16)
            as_cache = lambda a_t: _token_major(a_t.reshape(b_p, NSA_KV, NSA_DH, -1))
            keep = min(WINDOW, s_p)
            outs["cmp_k_p"].append(as_cache(kc))
            outs["cmp_v_p"].append(as_cache(vc))
            outs["slc_k_p"].append(as_cache(ks))
            outs["slc_v_p"].append(as_cache(vs))
            outs["win_k_p"].append(as_cache(kw[:, :, s_p - keep:]))
            outs["win_v_p"].append(as_cache(vw[:, :, s_p - keep:]))
            kcc = compress(kc.reshape(b_p, NSA_KV, NSA_DH, s_p), None, pe[0], w1[0], w2[0])
            vcc = compress(vc.reshape(b_p, NSA_KV, NSA_DH, s_p), None, pe[1], w1[1], w2[1])
            n_cmp = s_p // CMP_STRIDE - CMP_LEN // CMP_STRIDE + 1
            n_slc = s_p // SLC_BLOCK
            oc, sel = cmp_attention_select_prompt(qp, kcc, vcc, gtp, b_p, n_cmp, n_slc, SELECT_ROWS)
            osw = prompt_slc_win_attention(qp, sel, ks, vs, kw, vw, gtp, b_p, ATTN_ROWS)
            xp = sum2_matmul_residual_ln(oc, osw, w_out, xp, ln_mix_g[layer], ln_mix_b[layer])
            qs, gs, _, *kv_s = nsa_projection(xs, 1, w_q, w_kvt, F32)
            kc, vc, ks, vs, kw, vw = [a.reshape(NSA_KV, NSA_DH, b_s, t_pad).transpose(2, 0, 1, 3) for a in kv_s]
            new_rows = lambda a_t: _token_major(a_t[..., :t_s])
            outs["cmp_k_s"].append(new_rows(kc))
            outs["cmp_v_s"].append(new_rows(vc))
            outs["slc_k_s"].append(new_rows(ks))
            outs["slc_v_s"].append(new_rows(vs))
            win_k = _feature_major(cache_win_k[o])
            win_v = _feature_major(cache_win_v[o])
            wb = win_k.shape[-1]
            keep = min(WINDOW, wb + t_s)
            outs["win_k_s"].append(_token_major(jnp.concatenate([win_k, kw[..., :t_s]], axis=-1)[..., -keep:]))
            outs["win_v_s"].append(_token_major(jnp.concatenate([win_v, vw[..., :t_s]], axis=-1)[..., -keep:]))
            kcc = compress(_feature_major(cache_cmp_k[o]), page_table, pe[0], w1[0], w2[0])
            vcc = compress(_feature_major(cache_cmp_v[o]), page_table, pe[1], w1[1], w2[1])
            n_cmp = (past + t_s) // CMP_STRIDE - CMP_LEN // CMP_STRIDE + 1
            n_slc = -(-(past + t_s) // SLC_BLOCK)
            oc, idx = cmp_attention_select(qs, kcc, vcc, b_s, n_cmp, n_slc, past)
            q_rows = _group_rows(qs, b_s, t_pad)
            osl = sample_slc_attention(q_rows, idx, page_table, _feature_major(cache_slc_k[o]),
                                       _feature_major(cache_slc_v[o]), ks, vs, t_s, past)
            ow = sample_win_attention(q_rows, win_k, win_v, kw, vw, past)
            xs = nsa_merge_residual_ln(oc, _ungroup_rows(osl, b_s, t_pad), _ungroup_rows(ow, b_s, t_pad), gs,
                                       e_gate, w_out, xs, ln_mix_g[layer], ln_mix_b[layer])
        w_up = ffn_w_up[layer].astype(BF16)
        w_down = ffn_w_down[layer].astype(BF16)
        hp, hist_p = ffn_up_sequences(xp, b_p, jnp.zeros((b_p, FFN_W - 1, d_ff), F32), w_up,
                                      ffn_conv_w[layer], ffn_conv_b[layer])
        outs["ffn_p"].append(hist_p)
        xp = matmul_residual_ln(hp, w_down, xp, ln_ffn_g[layer], ln_ffn_b[layer])
        hs, a_s = ffn_up_short(xs, state_ffn_conv[layer], w_up, ffn_conv_w[layer], ffn_conv_b[layer])
        outs["ffn_s"].append(a_s.reshape(b_s, t_pad, d_ff)[:, t_s - (FFN_W - 1):t_s])
        xs = matmul_residual_ln(hs, w_down, xs, ln_ffn_g[layer], ln_ffn_b[layer])

    st = jnp.stack
    y_p = xp.reshape(b_p, s_p, d_model)
    y_s = xs.reshape(b_s, t_pad, d_model)[:, :t_s]
    order = ("sconv_p", "sconv_s", "ret_p", "ret_s", "cmp_k_p", "cmp_v_p", "slc_k_p", "slc_v_p",
             "cmp_k_s", "cmp_v_s", "slc_k_s", "slc_v_s", "win_k_p", "win_v_p", "win_k_s", "win_v_s",
             "ffn_p", "ffn_s")
    return (y_p, y_s) + tuple(st(outs[k]) for k in order)
```

```python
import functools

import numpy as np
import jax
import jax.numpy as jnp
from jax import lax
from jax.experimental import pallas as pl
from jax.experimental.pallas import tpu as pltpu

F32 = jnp.float32
BF16 = jnp.bfloat16

SUBLANES = 8
LANES = 128
VMEM_LIMIT_BYTES = 56 * 1024 * 1024
MATMUL_ROWS = 512
MERGE_ROWS = 256
ATTN_ROWS = 256
SELECT_ROWS = 1024
COMPRESS_PAGES = 64
MIXER_CHUNKS = 4

DEPTH = 2
SCONV_W = 3
RET_HEADS = 4
RET_CHUNK = 128
ROPE_BASE = 10000.0
NSA_HEADS = 16
NSA_KV = 4
NSA_GROUP = NSA_HEADS // NSA_KV
NSA_DH = 64
CMP_LEN = 32
CMP_STRIDE = 16
SLC_BLOCK = 64
SLC_TOPN = 16
WINDOW = 512
PAGE_SIZE = 128
FFN_W = 3
ALPHA = (2.0 * DEPTH) ** 0.25
LN_EPS = 1e-5
NEG_INF = -1e30
REMOVED = -3e38
FORCE_BONUS = 1e4
SLOT = 2 * NSA_DH


def _cparams(n_grid):
    return pltpu.CompilerParams(dimension_semantics=("arbitrary",) * n_grid,
                                vmem_limit_bytes=VMEM_LIMIT_BYTES)


def _row_tile(m, want):
    t = min(m, want)
    assert m % t == 0, (m, t)
    return t


def _nt_dot(a, b):
    return lax.dot_general(a, b, (((1,), (1,)), ((), ())), preferred_element_type=F32)


def _tn_dot(a, b):
    return lax.dot_general(a, b, (((0,), (0,)), ((), ())), preferred_element_type=F32)


def _dot(a, b):
    return jnp.dot(a, b, preferred_element_type=F32)


def _gelu(x):
    return 0.5 * x * (1.0 + jnp.tanh(np.float32(np.sqrt(2.0 / np.pi)) * (x + 0.044715 * (x * x * x))))


def _layer_norm_rows(r, g, b):
    mu = jnp.mean(r, axis=-1, keepdims=True)
    d = r - mu
    var = jnp.mean(d * d, axis=-1, keepdims=True)
    return d * lax.rsqrt(var + LN_EPS) * g + b


def _mm_split_kernel(x_ref, w_ref, *o_refs, cuts):
    acc = _dot(x_ref[...].astype(BF16), w_ref[...])
    for o_ref, (lo, hi) in zip(o_refs, cuts):
        o_ref[...] = acc[:, lo:hi].astype(o_ref.dtype)


def matmul_split(x, w_bf16, widths, dtypes, tm=MATMUL_ROWS):
    m, k = x.shape
    n = w_bf16.shape[1]
    assert sum(widths) == n and all(wd % LANES == 0 for wd in widths)
    tm = _row_tile(m, tm)
    cuts, lo = [], 0
    for wd in widths:
        cuts.append((lo, lo + wd))
        lo += wd
    return pl.pallas_call(
        functools.partial(_mm_split_kernel, cuts=tuple(cuts)),
        grid=(m // tm,),
        in_specs=[pl.BlockSpec((tm, k), lambda i: (i, 0)),
                  pl.BlockSpec((k, n), lambda i: (0, 0))],
        out_specs=[pl.BlockSpec((tm, wd), lambda i: (i, 0)) for wd in widths],
        out_shape=[jax.ShapeDtypeStruct((m, wd), dt) for wd, dt in zip(widths, dtypes)],
        compiler_params=_cparams(1),
        name="matmul_split",
    )(x, w_bf16)


def _nsa_proj_kernel(x_ref, wq_ref, wt_ref, q_ref, g_ref, gt_ref, *kv_refs, nq, gd):
    xb = x_ref[...].astype(BF16)
    acc = _dot(xb, wq_ref[...])
    q_ref[...] = acc[:, :nq].astype(q_ref.dtype)
    g_ref[...] = acc[:, nq:]
    acc_t = _nt_dot(wt_ref[...], xb)
    for i, r in enumerate(kv_refs):
        r[0] = acc_t[i * gd:(i + 1) * gd, :]
    gt_ref[0] = acc_t[len(kv_refs) * gd:, :]


def nsa_projection(x, n_seq, wq_bf16, wt_bf16, q_dtype, n_kv=6, tm=MATMUL_ROWS):
    m, d = x.shape
    seq = m // n_seq
    tm = _row_tile(seq, tm)
    nt = seq // tm
    nq = wq_bf16.shape[1] - LANES
    gd = (wt_bf16.shape[0] - LANES) // n_kv
    fm = lambda rows: pl.BlockSpec((1, rows, tm), lambda s, i: (s, 0, i))
    return pl.pallas_call(
        functools.partial(_nsa_proj_kernel, nq=nq, gd=gd),
        grid=(n_seq, nt),
        in_specs=[pl.BlockSpec((tm, d), lambda s, i: (s * nt + i, 0)),
                  pl.BlockSpec((d, nq + LANES), lambda s, i: (0, 0)),
                  pl.BlockSpec((n_kv * gd + LANES, d), lambda s, i: (0, 0))],
        out_specs=[pl.BlockSpec((tm, nq), lambda s, i: (s * nt + i, 0)),
                   pl.BlockSpec((tm, LANES), lambda s, i: (s * nt + i, 0)),
                   fm(LANES)] + [fm(gd) for _ in range(n_kv)],
        out_shape=[jax.ShapeDtypeStruct((m, nq), q_dtype), jax.ShapeDtypeStruct((m, LANES), F32),
                   jax.ShapeDtypeStruct((n_seq, LANES, seq), F32)] + [
            jax.ShapeDtypeStruct((n_seq, gd, seq), F32) for _ in range(n_kv)],
        compiler_params=_cparams(2),
        name="nsa_projection",
    )(x, wq_bf16, wt_bf16)


def _mm_res_ln_kernel(a_ref, w_ref, x_ref, g_ref, b_ref, o_ref):
    y = _dot(a_ref[...].astype(BF16), w_ref[...])
    o_ref[...] = _layer_norm_rows(ALPHA * x_ref[...] + y, g_ref[...], b_ref[...])


def matmul_residual_ln(a, w_bf16, x, g, b, tm=MATMUL_ROWS):
    m, k = a.shape
    d = w_bf16.shape[1]
    tm = _row_tile(m, tm)
    return pl.pallas_call(
        _mm_res_ln_kernel,
        grid=(m // tm,),
        in_specs=[pl.BlockSpec((tm, k), lambda i: (i, 0)),
                  pl.BlockSpec((k, d), lambda i: (0, 0)),
                  pl.BlockSpec((tm, d), lambda i: (i, 0)),
                  pl.BlockSpec((1, d), lambda i: (0, 0)),
                  pl.BlockSpec((1, d), lambda i: (0, 0))],
        out_specs=pl.BlockSpec((tm, d), lambda i: (i, 0)),
        out_shape=jax.ShapeDtypeStruct((m, d), F32),
        compiler_params=_cparams(1),
        name="matmul_residual_ln",
    )(a, w_bf16, x, g.reshape(1, d), b.reshape(1, d))


def _sum2_mm_res_ln_kernel(a_ref, b2_ref, w_ref, x_ref, g_ref, b_ref, o_ref):
    y = _dot((a_ref[...] + b2_ref[...]).astype(BF16), w_ref[...])
    o_ref[...] = _layer_norm_rows(ALPHA * x_ref[...] + y, g_ref[...], b_ref[...])


def sum2_matmul_residual_ln(a, b2, w_bf16, x, g, b, tm=MATMUL_ROWS):
    m, k = a.shape
    d = w_bf16.shape[1]
    tm = _row_tile(m, tm)
    row = lambda i: (i, 0)
    fixed = lambda i: (0, 0)
    return pl.pallas_call(
        _sum2_mm_res_ln_kernel,
        grid=(m // tm,),
        in_specs=[pl.BlockSpec((tm, k), row), pl.BlockSpec((tm, k), row),
                  pl.BlockSpec((k, d), fixed),
                  pl.BlockSpec((tm, d), row),
                  pl.BlockSpec((1, d), fixed), pl.BlockSpec((1, d), fixed)],
        out_specs=pl.BlockSpec((tm, d), row),
        out_shape=jax.ShapeDtypeStruct((m, d), F32),
        compiler_params=_cparams(1),
        name="sum2_matmul_residual_ln",
    )(a, b2, w_bf16, x, g.reshape(1, d), b.reshape(1, d))


def _expand_gates(gates_raw, e_ref):
    sig = jax.nn.sigmoid(gates_raw)
    hi = sig.astype(BF16)
    lo = (sig - hi.astype(F32)).astype(BF16)
    e = e_ref[...]
    return _dot(hi, e) + _dot(lo, e)


def _nsa_merge_ln_kernel(oc_ref, os_ref, ow_ref, gt_ref, e_ref, w_ref, x_ref, g_ref, b_ref, o_ref, *, k):
    gx = _expand_gates(gt_ref[...], e_ref)
    o = gx[:, 0:k] * oc_ref[...] + gx[:, k:2 * k] * os_ref[...] + gx[:, 2 * k:3 * k] * ow_ref[...]
    y = _dot(o.astype(BF16), w_ref[...])
    o_ref[...] = _layer_norm_rows(ALPHA * x_ref[...] + y, g_ref[...], b_ref[...])


def nsa_merge_residual_ln(oc, osl, ow, gates, e_bf16, w_bf16, x, g, b, tm=MERGE_ROWS):
    m, k = oc.shape
    d = w_bf16.shape[1]
    tm = _row_tile(m, tm)
    row = lambda i: (i, 0)
    fixed = lambda i: (0, 0)
    return pl.pallas_call(
        functools.partial(_nsa_merge_ln_kernel, k=k),
        grid=(m // tm,),
        in_specs=[pl.BlockSpec((tm, k), row), pl.BlockSpec((tm, k), row), pl.BlockSpec((tm, k), row),
                  pl.BlockSpec((tm, LANES), row),
                  pl.BlockSpec((LANES, 3 * k), fixed),
                  pl.BlockSpec((k, d), fixed),
                  pl.BlockSpec((tm, d), row),
                  pl.BlockSpec((1, d), fixed), pl.BlockSpec((1, d), fixed)],
        out_specs=pl.BlockSpec((tm, d), row),
        out_shape=jax.ShapeDtypeStruct((m, d), F32),
        compiler_params=_cparams(1),
        name="nsa_merge_residual_ln",
    )(oc, osl, ow, gates, e_bf16, w_bf16, x, g.reshape(1, d), b.reshape(1, d))


def _even_mixer_kernel(*refs, rows, valid, dconv, dk, chunks, project):
    if project:
        x_ref, w_ref, *refs = refs
    else:
        z_ref, *refs = refs
    (hist_ref, st_ref, cos_ref, sin_ref, decay_ref, qdec_ref, kdec_ref, sdec_ref, cw_ref, cb_ref, gn_ref,
     y_ref, hist_out_ref, st_out_ref, carry, state) = refs
    r0 = valid - 2 - (rows - SUBLANES)

    @pl.when(pl.program_id(1) == 0)
    def _():
        carry[r0:r0 + 2, :] = hist_ref[0]
        state[...] = st_ref[0]

    z_all = _dot(x_ref[...].astype(BF16), w_ref[...]) if project else z_ref[...]
    d = dconv
    scale = np.float32(dk ** -0.5)
    heads = range(RET_HEADS)
    col = lambda part, hh: slice(part * d + hh * dk, part * d + (hh + 1) * dk)
    row = lax.broadcasted_iota(jnp.int32, (rows, d), 0)
    for ci in range(chunks):
        at = slice(ci * rows, (ci + 1) * rows)
        z = z_all[at]
        ch = z[:, 2 * d:3 * d] * z[:, 0:d]
        h0 = carry[r0:r0 + 1, :]
        h1 = carry[r0 + 1:r0 + 2, :]
        m1 = jnp.where(row == 0, h1, pltpu.roll(ch, 1, 0))
        m2 = jnp.where(row == 0, h0, jnp.where(row == 1, h1, pltpu.roll(ch, 2, 0)))
        u = ((cb_ref[...] + m2 * cw_ref[0:1, :]) + m1 * cw_ref[1:2, :]) + ch * cw_ref[2:3, :]
        y_ref[at, 0:d] = z[:, d:2 * d] * u
        carry[...] = ch[rows - SUBLANES:rows, :]

        cosf = cos_ref[at, :]
        sinf = sin_ref[at, :]
        qs, ks, vbs = [], [], []
        for hh in heads:
            q = z[:, col(3, hh)]
            k = z[:, col(4, hh)]
            qs.append(((q * cosf + pltpu.roll(q, dk // 2, 1) * sinf) * scale).astype(BF16))
            ks.append(k * cosf + pltpu.roll(k, dk // 2, 1) * sinf)
            vbs.append(z[:, col(5, hh)].astype(BF16))
        s_old = [state[hh] for hh in heads]
        scores = [_nt_dot(qs[hh], ks[hh].astype(BF16)) * decay_ref[hh] for hh in heads]
        cross = [_dot(qs[hh], s_old[hh].astype(BF16)) * qdec_ref[hh] for hh in heads]
        intra = [_dot(scores[hh].astype(BF16), vbs[hh]) for hh in heads]
        for hh in heads:
            kd = (ks[hh] * kdec_ref[hh]).astype(BF16)
            state[hh] = s_old[hh] * sdec_ref[hh] + _tn_dot(kd, vbs[hh])
        for hh in heads:
            o = intra[hh] + cross[hh]
            mu = jnp.mean(o, axis=-1, keepdims=True)
            dv = o - mu
            var = jnp.mean(dv * dv, axis=-1, keepdims=True)
            on = dv * lax.rsqrt(var + LN_EPS) * gn_ref[:, hh * dk:(hh + 1) * dk]
            gsw = z[:, col(6, hh)]
            y_ref[at, d + hh * dk:d + (hh + 1) * dk] = (gsw * jax.nn.sigmoid(gsw)) * on
    hist_out_ref[0] = carry[r0:r0 + 2, :]
    st_out_ref[0] = state[...]


def _retention_tables(rows, valid, dk):
    log_gamma = jnp.log1p(-jnp.exp2(-5.0 - jnp.arange(RET_HEADS, dtype=F32)))
    n = jnp.arange(rows, dtype=F32)
    diff = n[:, None] - n[None, :]
    lg = log_gamma[:, None, None]
    decay = jnp.where(diff >= 0, jnp.exp(lg * jnp.maximum(diff, 0.0)), 0.0)
    q_dec = jnp.exp((n[None, :] + 1.0) * log_gamma[:, None])
    k_dec = jnp.where(n[None, :] < valid, jnp.exp((valid - 1.0 - n[None, :]) * log_gamma[:, None]), 0.0)
    s_dec = jnp.exp(valid * log_gamma)
    bc = lambda a: jnp.broadcast_to(a[:, :, None], (RET_HEADS, rows, dk))
    return decay, bc(q_dec), bc(k_dec), jnp.broadcast_to(s_dec[:, None, None], (RET_HEADS, 1, dk))


def _rope_tables(pos, dk):
    half = dk // 2
    inv = ROPE_BASE ** (-jnp.arange(half, dtype=F32) / half)
    ang = pos.astype(F32)[:, None] * inv
    cos, sin = jnp.cos(ang), jnp.sin(ang)
    return jnp.concatenate([cos, cos], axis=-1), jnp.concatenate([-sin, sin], axis=-1)


def even_mixer(x, w_in_bf16, hist, st, pos, rows, valid, conv_w, conv_b, gn_g, chunks=1):
    n_seq, _, dconv = hist.shape
    dk = st.shape[-1]
    project = w_in_bf16 is not None
    tile = chunks * rows
    n_steps = x.shape[0] // (n_seq * tile)
    cosf, sinf = _rope_tables(pos, dk)
    decay, q_dec, k_dec, s_dec = _retention_tables(rows, valid, dk)
    fixed3 = lambda s, c: (0, 0, 0)
    fixed2 = lambda s, c: (0, 0)
    lead_specs = [pl.BlockSpec((tile, x.shape[1]), lambda s, c: (s * n_steps + c, 0))]
    lead_args = [x]
    if project:
        lead_specs.append(pl.BlockSpec(w_in_bf16.shape, fixed2))
        lead_args.append(w_in_bf16)
    n_chunks = n_steps
    rows_out = x.shape[0]
    return pl.pallas_call(
        functools.partial(_even_mixer_kernel, rows=rows, valid=valid, dconv=dconv, dk=dk, chunks=chunks,
                          project=project),
        grid=(n_seq, n_steps),
        in_specs=lead_specs + [
                  pl.BlockSpec((1, 2, dconv), lambda s, c: (s, 0, 0)),
                  pl.BlockSpec((1, RET_HEADS, dk, dk), lambda s, c: (s, 0, 0, 0)),
                  pl.BlockSpec((tile, dk), lambda s, c: (c, 0)),
                  pl.BlockSpec((tile, dk), lambda s, c: (c, 0)),
                  pl.BlockSpec((RET_HEADS, rows, rows), fixed3),
                  pl.BlockSpec((RET_HEADS, rows, dk), fixed3),
                  pl.BlockSpec((RET_HEADS, rows, dk), fixed3),
                  pl.BlockSpec((RET_HEADS, 1, dk), fixed3),
                  pl.BlockSpec((SCONV_W, dconv), fixed2),
                  pl.BlockSpec((1, dconv), fixed2),
                  pl.BlockSpec((1, RET_HEADS * dk), fixed2)],
        out_specs=[pl.BlockSpec((tile, 2 * dconv), lambda s, c: (s * n_chunks + c, 0)),
                   pl.BlockSpec((1, 2, dconv), lambda s, c: (s, 0, 0)),
                   pl.BlockSpec((1, RET_HEADS, dk, dk), lambda s, c: (s, 0, 0, 0))],
        out_shape=[jax.ShapeDtypeStruct((rows_out, 2 * dconv), F32),
                   jax.ShapeDtypeStruct((n_seq, 2, dconv), F32),
                   jax.ShapeDtypeStruct((n_seq, RET_HEADS, dk, dk), F32)],
        scratch_shapes=[pltpu.VMEM((SUBLANES, dconv), F32), pltpu.VMEM((RET_HEADS, dk, dk), F32)],
        compiler_params=_cparams(2),
        name="even_mixer",
    )(*lead_args, hist, st, cosf, sinf, decay, q_dec, k_dec, s_dec, conv_w, conv_b.reshape(1, dconv),
      gn_g.reshape(1, RET_HEADS * dk))


def _conv_gate(a, gate, m1, m2, cw_ref, cb_ref):
    conv = ((cb_ref[...] + m2 * cw_ref[0:1, :]) + m1 * cw_ref[1:2, :]) + a * cw_ref[2:3, :]
    return _gelu(conv) * gate


def _ffn_up_seq_kernel(x_ref, wa_ref, wg_ref, h_ref, cw_ref, cb_ref, o_ref, hist_out_ref, carry, *, tm):
    @pl.when(pl.program_id(2) == 0)
    def _():
        carry[SUBLANES - 2:SUBLANES, :] = h_ref[0]

    xb = x_ref[...].astype(BF16)
    a = _dot(xb, wa_ref[...])
    gate = _dot(xb, wg_ref[...])
    row = lax.broadcasted_iota(jnp.int32, a.shape, 0)
    h0 = carry[SUBLANES - 2:SUBLANES - 1, :]
    h1 = carry[SUBLANES - 1:SUBLANES, :]
    m1 = jnp.where(row == 0, h1, pltpu.roll(a, 1, 0))
    m2 = jnp.where(row == 0, h0, jnp.where(row == 1, h1, pltpu.roll(a, 2, 0)))
    o_ref[...] = _conv_gate(a, gate, m1, m2, cw_ref, cb_ref).astype(o_ref.dtype)
    carry[...] = a[tm - SUBLANES:tm, :]
    hist_out_ref[0] = carry[SUBLANES - 2:SUBLANES, :]


def ffn_up_sequences(x, n_seq, hist, w_up_bf16, conv_w, conv_b, tm=MATMUL_ROWS, n_col=2):
    m, k = x.shape
    dff = conv_w.shape[1]
    seq = m // n_seq
    tm = _row_tile(seq, tm)
    tps = seq // tm
    tn = dff // n_col
    assert tn % LANES == 0
    return pl.pallas_call(
        functools.partial(_ffn_up_seq_kernel, tm=tm),
        grid=(n_col, n_seq, tps),
        in_specs=[pl.BlockSpec((tm, k), lambda j, s, i: (s * tps + i, 0)),
                  pl.BlockSpec((k, tn), lambda j, s, i: (0, j)),
                  pl.BlockSpec((k, tn), lambda j, s, i: (0, j + n_col)),
                  pl.BlockSpec((1, 2, tn), lambda j, s, i: (s, 0, j)),
                  pl.BlockSpec((FFN_W, tn), lambda j, s, i: (0, j)),
                  pl.BlockSpec((1, tn), lambda j, s, i: (0, j))],
        out_specs=[pl.BlockSpec((tm, tn), lambda j, s, i: (s * tps + i, j)),
                   pl.BlockSpec((1, 2, tn), lambda j, s, i: (s, 0, j))],
        out_shape=[jax.ShapeDtypeStruct((m, dff), BF16),
                   jax.ShapeDtypeStruct((n_seq, 2, dff), F32)],
        scratch_shapes=[pltpu.VMEM((SUBLANES, tn), F32)],
        compiler_params=_cparams(3),
        name="ffn_up_sequences",
    )(x, w_up_bf16, w_up_bf16, hist, conv_w, conv_b.reshape(1, dff))


def _ffn_up_short_kernel(x_ref, wa_ref, wg_ref, h1_ref, h2_ref, cw_ref, cb_ref, o_ref, a_ref):
    xb = x_ref[...].astype(BF16)
    a = _dot(xb, wa_ref[...])
    gate = _dot(xb, wg_ref[...])
    t = lax.broadcasted_iota(jnp.int32, a.shape, 0) % SUBLANES
    m1 = jnp.where(t == 0, h1_ref[...], pltpu.roll(a, 1, 0))
    m2 = jnp.where(t < 2, h2_ref[...], pltpu.roll(a, 2, 0))
    o_ref[...] = _conv_gate(a, gate, m1, m2, cw_ref, cb_ref).astype(o_ref.dtype)
    a_ref[...] = a


def ffn_up_short(x, hist, w_up_bf16, conv_w, conv_b, n_col=2):
    m, k = x.shape
    dff = conv_w.shape[1]
    n_seq = m // SUBLANES
    tn = dff // n_col
    zeros = jnp.zeros((n_seq, SUBLANES, dff), F32)
    h1 = zeros.at[:, 0].set(hist[:, 1]).reshape(m, dff)
    h2 = zeros.at[:, 0].set(hist[:, 0]).at[:, 1].set(hist[:, 1]).reshape(m, dff)
    col = lambda j: (0, j)
    return pl.pallas_call(
        _ffn_up_short_kernel,
        grid=(n_col,),
        in_specs=[pl.BlockSpec((m, k), lambda j: (0, 0)),
                  pl.BlockSpec((k, tn), col),
                  pl.BlockSpec((k, tn), lambda j: (0, j + n_col)),
                  pl.BlockSpec((m, tn), col), pl.BlockSpec((m, tn), col),
                  pl.BlockSpec((FFN_W, tn), col), pl.BlockSpec((1, tn), col)],
        out_specs=[pl.BlockSpec((m, tn), col), pl.BlockSpec((m, tn), col)],
        out_shape=[jax.ShapeDtypeStruct((m, dff), F32), jax.ShapeDtypeStruct((m, dff), F32)],
        compiler_params=_cparams(1),
        name="ffn_up_short",
    )(x, w_up_bf16, w_up_bf16, h1, h2, conv_w, conv_b.reshape(1, dff))


def _compress_kernel(pt_ref, *refs, pages):
    page_refs = refs[:pages + 1]
    w_ref, pecol_ref, w1_ref, w2_ref, o_ref = refs[pages + 1:pages + 6]
    rows_refs = refs[pages + 6:]
    parts = len(rows_refs)
    ppp = (pages + 1) // parts
    cpp = PAGE_SIZE // CMP_STRIDE
    n = (pages + 1) * cpp
    n_p = ppp * cpp
    hidden = w1_ref.shape[1]
    gpr = LANES // NSA_DH
    pieces = NSA_KV // gpr
    for i, r in enumerate(page_refs):
        for pc in range(pieces):
            tile = r[0, pc * gpr:(pc + 1) * gpr].reshape(LANES, PAGE_SIZE)
            rows_refs[i // ppp][pc, (i % ppp) * PAGE_SIZE:(i % ppp + 1) * PAGE_SIZE, :] = tile.T
    pe_term = jnp.sum(pecol_ref[...] * w1_ref[...], axis=0, keepdims=True)
    accs = []
    for rows_ref in rows_refs:
        lhs = jnp.concatenate([rows_ref[pc].reshape(n_p, CMP_STRIDE * LANES) for pc in range(pieces)], axis=0)
        accs.append(_dot(lhs.astype(BF16), w_ref[...]))
    for pc in range(pieces):
        for gl in range(gpr):
            a = jnp.concatenate([acc[pc * n_p:(pc + 1) * n_p, gl * 2 * hidden:(gl + 1) * 2 * hidden]
                                 for acc in accs], axis=0)
            nxt = pltpu.roll(a, n - 1, 0)
            pre = pe_term + a[:, 0:hidden]
            pre = pre + nxt[:, hidden:2 * hidden]
            o_ref[0, pc * gpr + gl] = _dot(_gelu(pre[0:pages * cpp]).astype(BF16), w2_ref[...])


def compress(rows_t, page_table, pe, w1, w2, pages=COMPRESS_PAGES):
    pooled = page_table is not None
    if pooled:
        n_seq, n_pages = page_table.shape
    else:
        n_seq, n_pages = rows_t.shape[0], rows_t.shape[3] // PAGE_SIZE
        page_table = jnp.zeros((1, 1), jnp.int32)
    pages = min(pages, n_pages)
    assert n_pages % pages == 0
    parts = next(p for p in (3, 5, 1) if (pages + 1) % p == 0)
    hidden = w1.shape[1]
    cpp = PAGE_SIZE // CMP_STRIDE
    r = CMP_LEN // CMP_STRIDE
    gpr = LANES // NSA_DH
    assert r == 2 and gpr == 2
    w1p = w1.reshape(r, CMP_STRIDE, NSA_DH, hidden)
    w16 = jnp.concatenate([w1p[0], w1p[1]], axis=-1)
    zero = jnp.zeros_like(w16)
    per_tok = jnp.concatenate([jnp.concatenate([w16, zero], axis=-1),
                               jnp.concatenate([zero, w16], axis=-1)], axis=1)
    w_chunk = per_tok.reshape(CMP_STRIDE * LANES, gpr * 2 * hidden).astype(BF16)
    w2p = jnp.pad(w2, ((0, 0), (0, SLOT - NSA_DH))).astype(BF16)
    pecol = pe.reshape(CMP_LEN * NSA_DH, 1)

    def page_map(i):
        if pooled:
            return lambda s, j, pt: (pt[s, jnp.minimum(j * pages + i, n_pages - 1)], 0, 0, 0)
        return lambda s, j, pt: (s, 0, 0, jnp.minimum(j * pages + i, n_pages - 1))

    fixed2 = lambda s, j, pt: (0, 0)
    grid_spec = pltpu.PrefetchScalarGridSpec(
        num_scalar_prefetch=1,
        grid=(n_seq, n_pages // pages),
        in_specs=[pl.BlockSpec((1, NSA_KV, NSA_DH, PAGE_SIZE), page_map(i)) for i in range(pages + 1)] + [
            pl.BlockSpec((CMP_STRIDE * LANES, gpr * 2 * hidden), fixed2),
            pl.BlockSpec((CMP_LEN * NSA_DH, 1), fixed2),
            pl.BlockSpec((CMP_LEN * NSA_DH, hidden), fixed2),
            pl.BlockSpec((hidden, SLOT), fixed2)],
        out_specs=pl.BlockSpec((1, NSA_KV, pages * cpp, SLOT), lambda s, j, pt: (s, 0, j, 0)),
        scratch_shapes=[pltpu.VMEM((NSA_KV // gpr, (pages + 1) // parts * PAGE_SIZE, LANES), F32)
                        for _ in range(parts)],
    )
    return pl.pallas_call(
        functools.partial(_compress_kernel, pages=pages),
        grid_spec=grid_spec,
        out_shape=jax.ShapeDtypeStruct((n_seq, NSA_KV, n_pages * cpp, SLOT), F32),
        compiler_params=_cparams(2),
        name="compress",
    )(page_table, *([rows_t] * (pages + 1)), w_chunk, pecol, w1, w2p)


def _cmp_select_kernel(q_ref, kc_ref, vc_ref, ov_ref, o_ref, idx_ref, *, tq, n_cmp, n_slc, pos0):
    ncp = kc_ref.shape[2]
    nsp = ov_ref.shape[0]
    hrows = NSA_GROUP * tq
    q_pos = pos0 + (lax.broadcasted_iota(jnp.int32, (hrows, ncp), 0) & (tq - 1))
    blk_i = lax.broadcasted_iota(jnp.int32, (hrows, ncp), 1)
    valid = (blk_i * CMP_STRIDE + (CMP_LEN - 1) <= q_pos) & (blk_i < n_cmp)
    ov = ov_ref[...]
    groups = range(NSA_KV)
    heads = [range(g * NSA_GROUP, (g + 1) * NSA_GROUP) for g in groups]
    scores = [_nt_dot(jnp.concatenate([q_ref[:, h * SLOT:(h + 1) * SLOT] for h in heads[g]], axis=0).astype(BF16),
                      kc_ref[0, g].astype(BF16)) for g in groups]
    probs = []
    for g in groups:
        s = jnp.where(valid, scores[g], NEG_INF)
        m = jnp.max(s, axis=-1, keepdims=True)
        e = jnp.where(valid, jnp.exp(s - m), 0.0)
        den = jnp.sum(e, axis=-1, keepdims=True)
        probs.append(e * (1.0 / jnp.where(den > 0.0, den, 1.0)))
    outs = [_dot(probs[g].astype(BF16), vc_ref[0, g].astype(BF16)) for g in groups]
    imps = []
    for g in groups:
        p_sum = jnp.zeros((tq, ncp), F32)
        for j, h in enumerate(heads[g]):
            o_ref[:, h * SLOT:(h + 1) * SLOT] = outs[g][j * tq:(j + 1) * tq]
            p_sum = p_sum + probs[g][j * tq:(j + 1) * tq]
        hi = p_sum.astype(BF16)
        lo = (p_sum - hi.astype(F32)).astype(BF16)
        imps.append(_nt_dot(ov, hi) + _nt_dot(ov, lo))
    cols = NSA_KV * tq
    imp = jnp.concatenate(imps, axis=1)
    blk = lax.broadcasted_iota(jnp.int32, (nsp, cols), 0)
    cur = (pos0 + (lax.broadcasted_iota(jnp.int32, (nsp, cols), 1) & (tq - 1))) // SLC_BLOCK
    real = blk < n_slc
    causal = real & (blk <= cur)
    forced = (blk == 0) | (blk == cur) | (blk == cur - 1)
    score = jnp.where(causal, imp + jnp.where(forced, FORCE_BONUS, 0.0), NEG_INF)
    score = jnp.where(real, score, REMOVED)
    idx = jnp.zeros((SLC_TOPN, cols), jnp.int32)
    idx_row = lax.broadcasted_iota(jnp.int32, (SLC_TOPN, cols), 0)
    for it in range(min(SLC_TOPN, n_slc)):
        m = jnp.max(score, axis=0, keepdims=True)
        first = jnp.min(jnp.where(score == m, blk, nsp), axis=0, keepdims=True)
        score = jnp.where(blk == first, REMOVED, score)
        idx = jnp.where(idx_row == it, first, idx)
    idx_ref[0] = idx


def cmp_block_overlap(n_cmp_pad, n_cmp, n_slc, n_slc_pad, lane_off):
    i = np.arange(n_cmp_pad)[:, None]
    j = np.arange(n_slc_pad)[None, :] - lane_off
    start = i * CMP_STRIDE
    hit = (start < (j + 1) * SLC_BLOCK) & (start + CMP_LEN > j * SLC_BLOCK) & (i < n_cmp) & (j >= 0) & (j < n_slc)
    return jnp.asarray(hit.astype(np.float32), dtype=BF16)


def cmp_attention_select(q_slots, kcc, vcc, n_seq, n_cmp, n_slc, pos0):
    tokens = q_slots.shape[0]
    tq = tokens // n_seq
    assert tq & (tq - 1) == 0
    ncp = kcc.shape[2]
    nsp = -(-n_slc // SUBLANES) * SUBLANES
    ov_t = cmp_block_overlap(ncp, n_cmp, n_slc, nsp, 0).T
    hw = NSA_HEADS * SLOT
    o_cmp, idx_t = pl.pallas_call(
        functools.partial(_cmp_select_kernel, tq=tq, n_cmp=n_cmp, n_slc=n_slc, pos0=pos0),
        grid=(n_seq,),
        in_specs=[pl.BlockSpec((tq, hw), lambda s: (s, 0)),
                  pl.BlockSpec((1, NSA_KV, ncp, SLOT), lambda s: (s, 0, 0, 0)),
                  pl.BlockSpec((1, NSA_KV, ncp, SLOT), lambda s: (s, 0, 0, 0)),
                  pl.BlockSpec((nsp, ncp), lambda s: (0, 0))],
        out_specs=[pl.BlockSpec((tq, hw), lambda s: (s, 0)),
                   pl.BlockSpec((1, SLC_TOPN, NSA_KV * tq), lambda s: (s, 0, 0))],
        out_shape=[jax.ShapeDtypeStruct((tokens, hw), F32),
                   jax.ShapeDtypeStruct((n_seq, SLC_TOPN, NSA_KV * tq), jnp.int32)],
        compiler_params=_cparams(1),
        name="cmp_attention_select",
    )(q_slots, kcc, vcc, ov_t)
    topn = min(SLC_TOPN, n_slc)
    idx = idx_t[:, :topn].reshape(n_seq, topn, NSA_KV, tq).transpose(0, 2, 3, 1)
    return o_cmp, idx


def _cmp_select_prompt_kernel(q_ref, kc_ref, vc_ref, ovt_ref, gt_ref, o_ref, sel_ref, *, tq, n_cmp, n_slc):
    t0 = pl.program_id(2) * tq
    ncp = kc_ref.shape[2]
    nsr = ovt_ref.shape[0]
    kc = kc_ref[0, 0].astype(BF16)
    vc = vc_ref[0, 0].astype(BF16)
    q_pos = t0 + lax.broadcasted_iota(jnp.int32, (ncp, tq), 1)
    blk_i = lax.broadcasted_iota(jnp.int32, (ncp, tq), 0)
    valid = (blk_i * CMP_STRIDE + (CMP_LEN - 1) <= q_pos) & (blk_i < n_cmp)
    p_sum = jnp.zeros((ncp, tq), F32)
    for j in range(NSA_GROUP):
        s = jnp.where(valid, _nt_dot(kc, q_ref[:, j * SLOT:(j + 1) * SLOT]), NEG_INF)
        m = jnp.max(s, axis=0, keepdims=True)
        e = jnp.where(valid, jnp.exp(s - m), 0.0)
        den = jnp.sum(e, axis=0, keepdims=True)
        p = e * (1.0 / jnp.where(den > 0.0, den, 1.0))
        gate = jax.nn.sigmoid(gt_ref[0, pl.ds(pl.program_id(1) * NSA_GROUP + j, 1), :])
        o_ref[:, j * SLOT:(j + 1) * SLOT] = _tn_dot((p * gate).astype(BF16), vc)
        p_sum = p_sum + p
    hi = p_sum.astype(BF16)
    lo = (p_sum - hi.astype(F32)).astype(BF16)
    ovt = ovt_ref[...]
    imp = _dot(ovt, hi) + _dot(ovt, lo)
    blk = lax.broadcasted_iota(jnp.int32, (nsr, tq), 0)
    cur = (t0 + lax.broadcasted_iota(jnp.int32, (nsr, tq), 1)) // SLC_BLOCK
    real = blk < n_slc
    causal = real & (blk <= cur)
    forced = (blk == 0) | (blk == cur) | (blk == cur - 1)
    score = jnp.where(causal, imp + jnp.where(forced, FORCE_BONUS, 0.0), NEG_INF)
    score = jnp.where(real, score, REMOVED)
    picked = jnp.zeros((nsr, tq), jnp.bool_)
    for _ in range(min(SLC_TOPN, n_slc)):
        m = jnp.max(score, axis=0, keepdims=True)
        first = jnp.min(jnp.where(score == m, blk, nsr), axis=0, keepdims=True)
        hit = blk == first
        picked = picked | hit
        score = jnp.where(hit, REMOVED, score)
    bias_t = jnp.where(real & ~(picked & causal), NEG_INF, 0.0)
    slot_t = jnp.concatenate([jnp.zeros((NSA_DH, tq), F32), bias_t], axis=0)
    sel_ref[0, 0] = slot_t.T


def cmp_attention_select_prompt(q_slots, kcc, vcc, gates_t, n_seq, n_cmp, n_slc, tq):
    tokens = q_slots.shape[0]
    t = tokens // n_seq
    tq = _row_tile(t, tq)
    nt = t // tq
    ncp = kcc.shape[2]
    nsr = SLOT - NSA_DH
    assert n_slc <= nsr
    ovt = cmp_block_overlap(ncp, n_cmp, n_slc, nsr, 0).T
    gw = NSA_GROUP * SLOT
    return pl.pallas_call(
        functools.partial(_cmp_select_prompt_kernel, tq=tq, n_cmp=n_cmp, n_slc=n_slc),
        grid=(n_seq, NSA_KV, nt),
        in_specs=[pl.BlockSpec((tq, gw), lambda s, g, i: (s * nt + i, g)),
                  pl.BlockSpec((1, 1, ncp, SLOT), lambda s, g, i: (s, g, 0, 0)),
                  pl.BlockSpec((1, 1, ncp, SLOT), lambda s, g, i: (s, g, 0, 0)),
                  pl.BlockSpec((nsr, ncp), lambda s, g, i: (0, 0)),
                  pl.BlockSpec((1, LANES, tq), lambda s, g, i: (s, 0, i))],
        out_specs=[pl.BlockSpec((tq, gw), lambda s, g, i: (s * nt + i, g)),
                   pl.BlockSpec((1, 1, tq, SLOT), lambda s, g, i: (s, g, i, 0))],
        out_shape=[jax.ShapeDtypeStruct((tokens, NSA_HEADS * SLOT), F32),
                   jax.ShapeDtypeStruct((n_seq, NSA_KV, t, SLOT), F32)],
        compiler_params=_cparams(3),
        name="cmp_attention_select_prompt",
    )(q_slots, kcc, vcc, ovt, gates_t)


def _prompt_slc_win_kernel(q_ref, sel_ref, ks_ref, vs_ref, kw_ref, vw_ref, oh_ref, gt_ref, o_ref,
                           m_ref, acc_ref, *, tq, seq):
    acc_rows = NSA_DH + 16
    qi = pl.program_id(2)
    t0 = qi * tq
    rows = NSA_GROUP * tq
    sel = sel_ref[0, 0]
    q_plain = jnp.concatenate([q_ref[:, j * SLOT:(j + 1) * SLOT] for j in range(NSA_GROUP)], axis=0)
    q_aug = jnp.concatenate([(q_ref[:, j * SLOT:(j + 1) * SLOT].astype(F32) + sel).astype(BF16)
                             for j in range(NSA_GROUP)], axis=0)
    zeros_k = jnp.zeros((SLOT - NSA_DH, tq), F32)
    ones_row = (lax.broadcasted_iota(jnp.int32, (acc_rows - NSA_DH, tq), 0) == 0).astype(BF16)
    rel = lax.broadcasted_iota(jnp.int32, (tq, LANES), 1) - lax.broadcasted_iota(jnp.int32, (tq, LANES), 0)

    def scores(q_rows, k_top, k_bottom):
        k_rows = jnp.concatenate([k_top, k_bottom], axis=0).T.astype(BF16)
        return [_nt_dot(k_rows, q_rows[c:c + 2 * LANES]) for c in range(0, rows, 2 * LANES)]

    def update(s_t, v_top, start, mask):
        v_t = jnp.concatenate([v_top.astype(BF16), ones_row], axis=0)
        for cg in range(rows // (2 * LANES)):
            p_parts, a_parts = [], []
            for h in range(2):
                c0 = (2 * cg + h) * LANES
                x = s_t[cg][:, h * LANES:(h + 1) * LANES]
                if mask is not None:
                    lo, hi = mask
                    off = t0 - start + (c0 & (tq - 1))
                    keep = rel >= lo - off
                    if hi is not None:
                        keep = keep & (rel < hi - off)
                    x = jnp.where(keep, x, NEG_INF)
                m_old = m_ref[:, c0:c0 + LANES]
                m_new = jnp.maximum(m_old, jnp.max(x, axis=0, keepdims=True))
                m_ref[:, c0:c0 + LANES] = m_new
                a_parts.append(jnp.exp(m_old - m_new))
                p_parts.append(jnp.exp(x - m_new).astype(BF16))
            c0 = 2 * cg * LANES
            pv = _dot(v_t, jnp.concatenate(p_parts, axis=1))
            acc_ref[:, c0:c0 + 2 * LANES] = jnp.concatenate(a_parts, axis=1) * acc_ref[:, c0:c0 + 2 * LANES] + pv

    def reset():
        m_ref[...] = jnp.full(m_ref.shape, NEG_INF, F32)
        acc_ref[...] = jnp.zeros(acc_ref.shape, F32)

    slot_pad = jnp.zeros((SLOT - NSA_DH, tq), F32)

    def finish(branch, accumulate):
        for j in range(NSA_GROUP):
            a = acc_ref[:, j * tq:(j + 1) * tq]
            head = pl.program_id(1) * NSA_GROUP + j
            gate = jax.nn.sigmoid(gt_ref[0, pl.ds(branch * NSA_HEADS + head, 1), :])
            o_t = jnp.concatenate([a[0:NSA_DH] * (gate * (1.0 / a[NSA_DH:NSA_DH + 1, :])), slot_pad], axis=0)
            if accumulate:
                o_ref[:, j * SLOT:(j + 1) * SLOT] += o_t.T
            else:
                o_ref[:, j * SLOT:(j + 1) * SLOT] = o_t.T

    def slc_scores(start):
        return scores(q_aug, ks_ref[0, :, pl.ds(start, tq)], oh_ref[:, pl.ds(start, tq)])

    def slc_update(s_t, start, mask):
        update(s_t, vs_ref[0, :, pl.ds(start, tq)], start, mask)

    k0 = jnp.clip(t0 - WINDOW, 0, seq - WINDOW - tq)
    n_win = WINDOW // tq + 1
    win_start = [pl.multiple_of(k0 + i * tq, tq) for i in range(n_win)]
    win_mask = (0, WINDOW)

    def win_scores(i):
        return scores(q_plain, kw_ref[0, :, pl.ds(win_start[i], tq)], zeros_k)

    reset()

    def slc_tiles(first, count):
        starts = [pl.multiple_of((first + i) * tq, tq) for i in range(count)]
        s_all = [slc_scores(st) for st in starts]
        for s_t, st in zip(s_all, starts):
            slc_update(s_t, st, None)

    def quad(k4, carry):
        slc_tiles(4 * k4, 4)
        return carry

    lax.fori_loop(0, qi // 4, quad, 0)

    @pl.when(qi % 4 >= 2)
    def _():
        slc_tiles(4 * (qi // 4), 2)

    @pl.when(qi % 2 == 1)
    def _():
        slc_tiles(qi - 1, 1)

    diag = pl.multiple_of(t0, tq)
    s_diag = slc_scores(diag)
    s_win = [win_scores(i) for i in range(n_win)]
    slc_update(s_diag, diag, (0, None))
    finish(1, False)
    reset()
    for i in range(n_win):
        update(s_win[i], vw_ref[0, :, pl.ds(win_start[i], tq)], win_start[i], win_mask)
    finish(2, True)


def prompt_slc_win_attention(q_slots, sel, ks_t, vs_t, kw_t, vw_t, gates_t, n_seq, tq):
    tokens = q_slots.shape[0]
    seq = tokens // n_seq
    tq = _row_tile(seq, tq)
    assert tq & (tq - 1) == 0 and WINDOW % tq == 0 and seq >= WINDOW + tq
    nt = seq // tq
    gw = NSA_GROUP * SLOT
    assert tq % (2 * LANES) == 0
    onehot_t = jax.nn.one_hot(jnp.arange(seq) // SLC_BLOCK, SLOT - NSA_DH, dtype=F32).T
    kv_spec = pl.BlockSpec((1, NSA_DH, seq), lambda s, g, i: (s, g, 0))
    return pl.pallas_call(
        functools.partial(_prompt_slc_win_kernel, tq=tq, seq=seq),
        grid=(n_seq, NSA_KV, nt),
        in_specs=[pl.BlockSpec((tq, gw), lambda s, g, i: (s * nt + i, g)),
                  pl.BlockSpec((1, 1, tq, SLOT), lambda s, g, i: (s, g, i, 0)),
                  kv_spec, kv_spec, kv_spec, kv_spec,
                  pl.BlockSpec((SLOT - NSA_DH, seq), lambda s, g, i: (0, 0)),
                  pl.BlockSpec((1, LANES, tq), lambda s, g, i: (s, 0, i))],
        out_specs=pl.BlockSpec((tq, gw), lambda s, g, i: (s * nt + i, g)),
        out_shape=jax.ShapeDtypeStruct((tokens, NSA_HEADS * SLOT), F32),
        scratch_shapes=[pltpu.VMEM((1, NSA_GROUP * tq), F32),
                        pltpu.VMEM((NSA_DH + 16, NSA_GROUP * tq), F32)],
        compiler_params=_cparams(3),
        name="prompt_slc_win_attention",
    )(q_slots, sel, ks_t, vs_t, kw_t, vw_t, onehot_t, gates_t)


def _sample_slc_kernel(idx_ref, pg_ref, q_ref, kn_ref, vn_ref, pool_k, pool_v, o_ref, kbuf, vbuf, sems,
                       *, topn, past, t_pad, t_real):
    s_id, g_id = pl.program_id(0), pl.program_id(1)
    step = s_id * NSA_KV + g_id
    n_steps = pl.num_programs(0) * NSA_KV

    def copies(item_step, t, slot):
        base = (item_step * t_pad + t) * topn
        g = item_step % NSA_KV
        out = []
        for kk in range(topn):
            page = pg_ref[base + kk]
            out.append(pltpu.make_async_copy(pool_k.at[page, g], kbuf.at[slot, kk], sems.at[slot, 0]))
            out.append(pltpu.make_async_copy(pool_v.at[page, g], vbuf.at[slot, kk], sems.at[slot, 1]))
        return out

    @pl.when(step == 0)
    def _():
        for c in copies(step, 0, 0):
            c.start()

    for t in range(t_real):
        slot = t % 2
        if t + 1 < t_real:
            for c in copies(step, t + 1, 1 - slot):
                c.start()
        else:
            @pl.when(step + 1 < n_steps)
            def _():
                for c in copies(step + 1, 0, 1 - slot):
                    c.start()
        for c in copies(step, t, slot):
            c.wait()
        _sample_slc_token(idx_ref, q_ref, kn_ref, vn_ref, o_ref, kbuf[slot], vbuf[slot],
                          base=(step * t_pad + t) * topn, t=t, topn=topn, past=past)


def _sample_slc_token(idx_ref, q_ref, kn_ref, vn_ref, o_ref, k_tiles, v_tiles, *, base, t, topn, past):
    q_pos = past + t
    cur = q_pos // SLC_BLOCK
    first_new = past // SLC_BLOCK
    q = q_ref[0, 0, t].astype(BF16)
    kb = jnp.concatenate([k_tiles[kk] for kk in range(topn)], axis=1).astype(BF16)
    vb = jnp.concatenate([v_tiles[kk] for kk in range(topn)], axis=1).astype(BF16)
    n_keys = topn * PAGE_SIZE
    lane = lax.broadcasted_iota(jnp.int32, (1, n_keys), 1)
    slot = lane // PAGE_SIZE
    in_page = lane % PAGE_SIZE
    k_pos = in_page
    limit = jnp.zeros((1, n_keys), jnp.int32)
    n_new = jnp.int32(0)
    for kk in range(topn):
        b = idx_ref[base + kk]
        here = slot == kk
        k_pos = jnp.where(here, (b // 2) * PAGE_SIZE + in_page, k_pos)
        last = jnp.where(b <= cur, jnp.minimum(q_pos, past - 1), -1)
        limit = jnp.where(here, jnp.where(in_page // SLC_BLOCK == b % 2, last, -1), limit)
        n_new = n_new + jnp.where(b == first_new, 1, 0)
    valid = k_pos <= limit
    s_old = jnp.where(valid, _dot(q, kb), NEG_INF)
    new_lane = lax.broadcasted_iota(jnp.int32, (1, SUBLANES), 1)
    valid_new = past + new_lane <= jnp.where(n_new > 0, q_pos, past - 1)
    s_new = jnp.where(valid_new, _dot(q, kn_ref[0, 0].astype(BF16)), NEG_INF)
    m = jnp.maximum(jnp.max(s_old, axis=-1, keepdims=True), jnp.max(s_new, axis=-1, keepdims=True))
    p_old = jnp.exp(s_old - m)
    p_new = jnp.exp(s_new - m)
    l = jnp.sum(p_old, axis=-1, keepdims=True) + jnp.sum(p_new, axis=-1, keepdims=True)
    o = _nt_dot(p_old.astype(BF16), vb) + _nt_dot(p_new.astype(BF16), vn_ref[0, 0].astype(BF16))
    o_ref[0, 0, t] = o / l


def sample_slc_attention(q_rows, idx, page_table, pool_k, pool_v, k_new_t, v_new_t, t_real, past):
    n_seq, _, t_pad, _, dh = q_rows.shape
    topn = idx.shape[-1]
    assert past % SLC_BLOCK == 0 and t_real <= SUBLANES and PAGE_SIZE == 2 * SLC_BLOCK
    last_old = past // SLC_BLOCK - 1
    logical = jnp.clip(idx, 0, last_old) // 2
    n_pages = page_table.shape[1]
    hit = logical[..., None] == jnp.arange(n_pages, dtype=jnp.int32)
    phys = jnp.sum(jnp.where(hit, page_table[:, None, None, None, :], 0), axis=-1)

    assert t_real % 2 == 0
    new_spec = pl.BlockSpec((1, 1, dh, SUBLANES), lambda s, g, i_r, p_r: (s, g, 0, 0))
    hbm = pl.BlockSpec(memory_space=pl.ANY)
    grid_spec = pltpu.PrefetchScalarGridSpec(
        num_scalar_prefetch=2,
        grid=(n_seq, NSA_KV),
        in_specs=[pl.BlockSpec((1, 1, t_pad, SUBLANES, dh), lambda s, g, i_r, p_r: (s, g, 0, 0, 0)),
                  new_spec, new_spec, hbm, hbm],
        out_specs=pl.BlockSpec((1, 1, t_real, SUBLANES, dh), lambda s, g, i_r, p_r: (s, g, 0, 0, 0)),
        scratch_shapes=[pltpu.VMEM((2, topn, dh, PAGE_SIZE), F32),
                        pltpu.VMEM((2, topn, dh, PAGE_SIZE), F32),
                        pltpu.SemaphoreType.DMA((2, 2))],
    )
    return pl.pallas_call(
        functools.partial(_sample_slc_kernel, topn=topn, past=past, t_pad=t_pad, t_real=t_real),
        grid_spec=grid_spec,
        out_shape=jax.ShapeDtypeStruct((n_seq, NSA_KV, t_real, SUBLANES, dh), F32),
        compiler_params=_cparams(2),
        name="sample_slc_attention",
    )(idx.reshape(-1), phys.reshape(-1), q_rows, k_new_t, v_new_t, pool_k, pool_v)


def _sample_win_kernel(q_ref, wk_ref, wv_ref, kn_ref, vn_ref, o_ref, *, past, t_pad):
    rows = t_pad * SUBLANES
    wb = wk_ref.shape[-1]
    groups = range(NSA_KV)
    q_pos = past + lax.broadcasted_iota(jnp.int32, (rows, 1), 0) // SUBLANES
    k_pos = past - wb + lax.broadcasted_iota(jnp.int32, (1, wb), 1)
    dist = q_pos - k_pos
    valid = (dist >= 0) & (dist < WINDOW) & (k_pos >= 0)
    n_pos = past + lax.broadcasted_iota(jnp.int32, (1, SUBLANES), 1)
    dist_n = q_pos - n_pos
    valid_n = (dist_n >= 0) & (dist_n < WINDOW)
    qs = [q_ref[0, g].reshape(rows, q_ref.shape[-1]).astype(BF16) for g in groups]
    s_old = [_dot(qs[g], wk_ref[0, g].astype(BF16)) for g in groups]
    s_new = [_dot(qs[g], kn_ref[0, g].astype(BF16)) for g in groups]
    p_old, p_new, inv_l = [], [], []
    for g in groups:
        so = jnp.where(valid, s_old[g], NEG_INF)
        sn = jnp.where(valid_n, s_new[g], NEG_INF)
        m = jnp.maximum(jnp.max(so, axis=-1, keepdims=True), jnp.max(sn, axis=-1, keepdims=True))
        po = jnp.exp(so - m)
        pn = jnp.exp(sn - m)
        inv_l.append(1.0 / (jnp.sum(po, axis=-1, keepdims=True) + jnp.sum(pn, axis=-1, keepdims=True)))
        p_old.append(po.astype(BF16))
        p_new.append(pn.astype(BF16))
    for g in groups:
        o = _nt_dot(p_old[g], wv_ref[0, g].astype(BF16)) + _nt_dot(p_new[g], vn_ref[0, g].astype(BF16))
        o_ref[0, g] = (o * inv_l[g]).reshape(t_pad, SUBLANES, o.shape[-1])


def sample_win_attention(q_rows, win_k_t, win_v_t, k_new_t, v_new_t, past):
    n_seq, _, t_pad, _, dh = q_rows.shape
    wb = win_k_t.shape[-1]
    q_spec = pl.BlockSpec((1, NSA_KV, t_pad, SUBLANES, dh), lambda s: (s, 0, 0, 0, 0))
    win_spec = pl.BlockSpec((1, NSA_KV, dh, wb), lambda s: (s, 0, 0, 0))
    new_spec = pl.BlockSpec((1, NSA_KV, dh, SUBLANES), lambda s: (s, 0, 0, 0))
    return pl.pallas_call(
        functools.partial(_sample_win_kernel, past=past, t_pad=t_pad),
        grid=(n_seq,),
        in_specs=[q_spec, win_spec, win_spec, new_spec, new_spec],
        out_specs=q_spec,
        out_shape=jax.ShapeDtypeStruct(q_rows.shape, F32),
        compiler_params=_cparams(1),
        name="sample_win_attention",
    )(q_rows, win_k_t, win_v_t, k_new_t, v_new_t)


def _to_slots(a):
    lead = a.shape[:-1]
    n = a.shape[-1] // NSA_DH
    a = a.reshape(*lead, n, NSA_DH)
    a = jnp.pad(a, [(0, 0)] * (a.ndim - 1) + [(0, SLOT - NSA_DH)])
    return a.reshape(*lead, n * SLOT)


def _odd_weights(w_in, w_out):
    d = w_in.shape[0]
    hq = NSA_HEADS * NSA_DH
    kvw = NSA_KV * NSA_DH
    wq = _to_slots(w_in[:, :hq] * np.float32(NSA_DH ** -0.5))
    wg = jnp.pad(w_in[:, hq + 6 * kvw:], ((0, 0), (0, LANES - 3 * NSA_HEADS)))
    w_q = jnp.concatenate([wq, wg], axis=1).astype(BF16)
    w_kvt = jnp.concatenate([w_in[:, hq:hq + 6 * kvw], wg], axis=1).T.astype(BF16)
    wo = jnp.pad(w_out.reshape(NSA_HEADS, NSA_DH, d), ((0, 0), (0, SLOT - NSA_DH), (0, 0)))
    wo = wo.reshape(NSA_HEADS * SLOT, d).astype(BF16)
    k = NSA_HEADS * SLOT
    e = np.zeros((LANES, 3 * k), np.float32)
    for c in range(3):
        for h in range(NSA_HEADS):
            e[c * NSA_HEADS + h, c * k + h * SLOT:c * k + (h + 1) * SLOT] = 1.0
    return w_q, w_kvt, wo, jnp.asarray(e, dtype=BF16)


def _group_rows(q_slots, n_seq, t_pad):
    q = q_slots.reshape(n_seq, t_pad, NSA_KV, NSA_GROUP, SLOT)[..., :NSA_DH]
    q = q.transpose(0, 2, 1, 3, 4)
    return jnp.pad(q, ((0, 0), (0, 0), (0, 0), (0, SUBLANES - NSA_GROUP), (0, 0)))


def _ungroup_rows(o, n_seq, t_pad):
    t = o.shape[2]
    o = o[:, :, :, :NSA_GROUP].transpose(0, 2, 1, 3, 4)
    o = jnp.pad(o, ((0, 0), (0, t_pad - t), (0, 0), (0, 0), (0, SLOT - NSA_DH)))
    return o.reshape(n_seq * t_pad, NSA_HEADS * SLOT)


def _feature_major(cache):
    return cache.transpose(0, 2, 3, 1)


def _token_major(a_t):
    return a_t.transpose(0, 3, 1, 2)


def _pad_rows(a, t_pad):
    return jnp.pad(a, ((0, 0), (0, t_pad - a.shape[1])) + ((0, 0),) * (a.ndim - 2))


def kernel(x_prompt, x_sample, state_sconv, state_ret, cache_cmp_k, cache_cmp_v, cache_slc_k, cache_slc_v,
           cache_win_k, cache_win_v, state_ffn_conv, page_table,
           w_in_even, sconv_w, sconv_b, ret_gn_g, w_out_even,
           w_in_odd, cmp_pe, cmp_w1, cmp_w2, w_out_odd,
           ln_mix_g, ln_mix_b, ffn_w_up, ffn_conv_w, ffn_conv_b, ffn_w_down, ln_ffn_g, ln_ffn_b):
    b_p, s_p, d_model = x_prompt.shape
    b_s, t_s, _ = x_sample.shape
    n_pages = page_table.shape[1]
    past = n_pages * PAGE_SIZE
    t_pad = SUBLANES
    assert t_s <= t_pad and t_s >= SCONV_W - 1 and t_s < CMP_STRIDE and past % PAGE_SIZE == 0
    assert s_p % RET_CHUNK == 0 and s_p % PAGE_SIZE == 0
    d_sconv = sconv_w.shape[-1]
    d_ff = ffn_conv_w.shape[-1]
    gd = NSA_KV * NSA_DH
    depth = ln_mix_g.shape[0]

    xp = x_prompt.reshape(b_p * s_p, d_model)
    xs = _pad_rows(x_sample, t_pad).reshape(b_s * t_pad, d_model)
    outs = {k: [] for k in ("sconv_p", "sconv_s", "ret_p", "ret_s", "cmp_k_p", "cmp_v_p", "slc_k_p", "slc_v_p",
                            "cmp_k_s", "cmp_v_s", "slc_k_s", "slc_v_s", "win_k_p", "win_v_p", "win_k_s",
                            "win_v_s", "ffn_p", "ffn_s")}

    for layer in range(depth):
        if layer % 2 == 0:
            e = layer // 2
            w_in = w_in_even[e].astype(BF16)
            w_out = w_out_even[e].astype(BF16)
            n_in = w_in.shape[1]
            yp, hc, st = even_mixer(xp, w_in, jnp.zeros((b_p, SCONV_W - 1, d_sconv), F32),
                                    jnp.zeros((b_p,) + state_ret.shape[2:], F32), jnp.arange(s_p),
                                    RET_CHUNK, RET_CHUNK, sconv_w[e], sconv_b[e], ret_gn_g[e],
                                    chunks=MIXER_CHUNKS if s_p % (MIXER_CHUNKS * RET_CHUNK) == 0 else 1)
            outs["sconv_p"].append(hc)
            outs["ret_p"].append(st)
            xp = matmul_residual_ln(yp, w_out, xp, ln_mix_g[layer], ln_mix_b[layer])
            (zs,) = matmul_split(xs, w_in, [n_in], [F32])
            ys, hc, st = even_mixer(zs, None, state_sconv[e], state_ret[e], past + jnp.arange(t_pad),
                                    t_pad, t_s, sconv_w[e], sconv_b[e], ret_gn_g[e])
            outs["sconv_s"].append(hc)
            outs["ret_s"].append(st)
            xs = matmul_residual_ln(ys, w_out, xs, ln_mix_g[layer], ln_mix_b[layer])
        else:
            o = layer // 2
            w_q, w_kvt, w_out, e_gate = _odd_weights(w_in_odd[o], w_out_odd[o])
            pe, w1, w2 = cmp_pe[o], cmp_w1[o], cmp_w2[o]
            qp, _, gtp, kc, vc, ks, vs, kw, vw = nsa_projection(xp, b_p, w_q, w_kvt, BF16)
            as_cache = lambda a_t: _token_major(a_t.reshape(b_p, NSA_KV, NSA_DH, -1))
            keep = min(WINDOW, s_p)
            outs["cmp_k_p"].append(as_cache(kc))
            outs["cmp_v_p"].append(as_cache(vc))
            outs["slc_k_p"].append(as_cache(ks))
            outs["slc_v_p"].append(as_cache(vs))
            outs["win_k_p"].append(as_cache(kw[:, :, s_p - keep:]))
            outs["win_v_p"].append(as_cache(vw[:, :, s_p - keep:]))
            kcc = compress(kc.reshape(b_p, NSA_KV, NSA_DH, s_p), None, pe[0], w1[0], w2[0])
            vcc = compress(vc.reshape(b_p, NSA_KV, NSA_DH, s_p), None, pe[1], w1[1], w2[1])
            n_cmp = s_p // CMP_STRIDE - CMP_LEN // CMP_STRIDE + 1
            n_slc = s_p // SLC_BLOCK
            oc, sel = cmp_attention_select_prompt(qp, kcc, vcc, gtp, b_p, n_cmp, n_slc, SELECT_ROWS)
            osw = prompt_slc_win_attention(qp, sel, ks, vs, kw, vw, gtp, b_p, ATTN_ROWS)
            xp = sum2_matmul_residual_ln(oc, osw, w_out, xp, ln_mix_g[layer], ln_mix_b[layer])
            qs, gs, _, *kv_s = nsa_projection(xs, 1, w_q, w_kvt, F32)
            kc, vc, ks, vs, kw, vw = [a.reshape(NSA_KV, NSA_DH, b_s, t_pad).transpose(2, 0, 1, 3) for a in kv_s]
            new_rows = lambda a_t: _token_major(a_t[..., :t_s])
            outs["cmp_k_s"].append(new_rows(kc))
            outs["cmp_v_s"].append(new_rows(vc))
            outs["slc_k_s"].append(new_rows(ks))
            outs["slc_v_s"].append(new_rows(vs))
            win_k = _feature_major(cache_win_k[o])
            win_v = _feature_major(cache_win_v[o])
            wb = win_k.shape[-1]
            keep = min(WINDOW, wb + t_s)
            outs["win_k_s"].append(_token_major(jnp.concatenate([win_k, kw[..., :t_s]], axis=-1)[..., -keep:]))
            outs["win_v_s"].append(_token_major(jnp.concatenate([win_v, vw[..., :t_s]], axis=-1)[..., -keep:]))
            kcc = compress(_feature_major(cache_cmp_k[o]), page_table, pe[0], w1[0], w2[0])
            vcc = compress(_feature_major(cache_cmp_v[o]), page_table, pe[1], w1[1], w2[1])
            n_cmp = (past + t_s) // CMP_STRIDE - CMP_LEN // CMP_STRIDE + 1
            n_slc = -(-(past + t_s) // SLC_BLOCK)
            oc, idx = cmp_attention_select(qs, kcc, vcc, b_s, n_cmp, n_slc, past)
            q_rows = _group_rows(qs, b_s, t_pad)
            osl = sample_slc_attention(q_rows, idx, page_table, _feature_major(cache_slc_k[o]),
                                       _feature_major(cache_slc_v[o]), ks, vs, t_s, past)
            ow = sample_win_attention(q_rows, win_k, win_v, kw, vw, past)
            xs = nsa_merge_residual_ln(oc, _ungroup_rows(osl, b_s, t_pad), _ungroup_rows(ow, b_s, t_pad), gs,
                                       e_gate, w_out, xs, ln_mix_g[layer], ln_mix_b[layer])
        w_up = ffn_w_up[layer].astype(BF16)
        w_down = ffn_w_down[layer].astype(BF16)
        hp, hist_p = ffn_up_sequences(xp, b_p, jnp.zeros((b_p, FFN_W - 1, d_ff), F32), w_up,
                                      ffn_conv_w[layer], ffn_conv_b[layer])
        outs["ffn_p"].append(hist_p)
        xp = matmul_residual_ln(hp, w_down, xp, ln_ffn_g[layer], ln_ffn_b[layer])
        hs, a_s = ffn_up_short(xs, state_ffn_conv[layer], w_up, ffn_conv_w[layer], ffn_conv_b[layer])
        outs["ffn_s"].append(a_s.reshape(b_s, t_pad, d_ff)[:, t_s - (FFN_W - 1):t_s])
        xs = matmul_residual_ln(hs, w_down, xs, ln_ffn_g[layer], ln_ffn_b[layer])

    st = jnp.stack
    y_p = xp.reshape(b_p, s_p, d_model)
    y_s = xs.reshape(b_s, t_pad, d_model)[:, :t_s]
    order = ("sconv_p", "sconv_s", "ret_p", "ret_s", "cmp_k_p", "cmp_v_p", "slc_k_p", "slc_v_p",
             "cmp_k_s", "cmp_v_s", "slc_k_s", "slc_v_s", "win_k_p", "win_v_p", "win_k_s", "win_v_s",
             "ffn_p", "ffn_s")
    return (y_p, y_s) + tuple(st(outs[k]) for k in order)
```

```python
import functools

import numpy as np
import jax
import jax.numpy as jnp
from jax import lax
from jax.experimental import pallas as pl
from jax.experimental.pallas import tpu as pltpu

F32 = jnp.float32
BF16 = jnp.bfloat16

SUBLANES = 8
LANES = 128
VMEM_LIMIT_BYTES = 56 * 1024 * 1024
MATMUL_ROWS = 512
MERGE_ROWS = 256
ATTN_ROWS = 256
SELECT_ROWS = 1024
COMPRESS_PAGES = 64
MIXER_CHUNKS = 4

DEPTH = 2
SCONV_W = 3
RET_HEADS = 4
RET_CHUNK = 128
ROPE_BASE = 10000.0
NSA_HEADS = 16
NSA_KV = 4
NSA_GROUP = NSA_HEADS // NSA_KV
NSA_DH = 64
CMP_LEN = 32
CMP_STRIDE = 16
SLC_BLOCK = 64
SLC_TOPN = 16
WINDOW = 512
PAGE_SIZE = 128
FFN_W = 3
ALPHA = (2.0 * DEPTH) ** 0.25
LN_EPS = 1e-5
NEG_INF = -1e30
REMOVED = -3e38
FORCE_BONUS = 1e4
SLOT = 2 * NSA_DH


def _cparams(n_grid):
    return pltpu.CompilerParams(dimension_semantics=("arbitrary",) * n_grid,
                                vmem_limit_bytes=VMEM_LIMIT_BYTES)


def _row_tile(m, want):
    t = min(m, want)
    assert m % t == 0, (m, t)
    return t


def _nt_dot(a, b):
    return lax.dot_general(a, b, (((1,), (1,)), ((), ())), preferred_element_type=F32)


def _tn_dot(a, b):
    return lax.dot_general(a, b, (((0,), (0,)), ((), ())), preferred_element_type=F32)


def _dot(a, b):
    return jnp.dot(a, b, preferred_element_type=F32)


def _gelu(x):
    return 0.5 * x * (1.0 + jnp.tanh(np.float32(np.sqrt(2.0 / np.pi)) * (x + 0.044715 * (x * x * x))))


def _layer_norm_rows(r, g, b):
    mu = jnp.mean(r, axis=-1, keepdims=True)
    d = r - mu
    var = jnp.mean(d * d, axis=-1, keepdims=True)
    return d * lax.rsqrt(var + LN_EPS) * g + b


def _mm_split_kernel(x_ref, w_ref, *o_refs, cuts):
    acc = _dot(x_ref[...].astype(BF16), w_ref[...])
    for o_ref, (lo, hi) in zip(o_refs, cuts):
        o_ref[...] = acc[:, lo:hi].astype(o_ref.dtype)


def matmul_split(x, w_bf16, widths, dtypes, tm=MATMUL_ROWS):
    m, k = x.shape
    n = w_bf16.shape[1]
    assert sum(widths) == n and all(wd % LANES == 0 for wd in widths)
    tm = _row_tile(m, tm)
    cuts, lo = [], 0
    for wd in widths:
        cuts.append((lo, lo + wd))
        lo += wd
    return pl.pallas_call(
        functools.partial(_mm_split_kernel, cuts=tuple(cuts)),
        grid=(m // tm,),
        in_specs=[pl.BlockSpec((tm, k), lambda i: (i, 0)),
                  pl.BlockSpec((k, n), lambda i: (0, 0))],
        out_specs=[pl.BlockSpec((tm, wd), lambda i: (i, 0)) for wd in widths],
        out_shape=[jax.ShapeDtypeStruct((m, wd), dt) for wd, dt in zip(widths, dtypes)],
        compiler_params=_cparams(1),
        name="matmul_split",
    )(x, w_bf16)


def _nsa_proj_kernel(x_ref, wq_ref, wt_ref, q_ref, g_ref, gt_ref, *kv_refs, nq, gd):
    xb = x_ref[...].astype(BF16)
    acc = _dot(xb, wq_ref[...])
    q_ref[...] = acc[:, :nq].astype(q_ref.dtype)
    g_ref[...] = acc[:, nq:]
    acc_t = _nt_dot(wt_ref[...], xb)
    for i, r in enumerate(kv_refs):
        r[0] = acc_t[i * gd:(i + 1) * gd, :]
    gt_ref[0] = acc_t[len(kv_refs) * gd:, :]


def nsa_projection(x, n_seq, wq_bf16, wt_bf16, q_dtype, n_kv=6, tm=MATMUL_ROWS):
    m, d = x.shape
    seq = m // n_seq
    tm = _row_tile(seq, tm)
    nt = seq // tm
    nq = wq_bf16.shape[1] - LANES
    gd = (wt_bf16.shape[0] - LANES) // n_kv
    fm = lambda rows: pl.BlockSpec((1, rows, tm), lambda s, i: (s, 0, i))
    return pl.pallas_call(
        functools.partial(_nsa_proj_kernel, nq=nq, gd=gd),
        grid=(n_seq, nt),
        in_specs=[pl.BlockSpec((tm, d), lambda s, i: (s * nt + i, 0)),
                  pl.BlockSpec((d, nq + LANES), lambda s, i: (0, 0)),
                  pl.BlockSpec((n_kv * gd + LANES, d), lambda s, i: (0, 0))],
        out_specs=[pl.BlockSpec((tm, nq), lambda s, i: (s * nt + i, 0)),
                   pl.BlockSpec((tm, LANES), lambda s, i: (s * nt + i, 0)),
                   fm(LANES)] + [fm(gd) for _ in range(n_kv)],
        out_shape=[jax.ShapeDtypeStruct((m, nq), q_dtype), jax.ShapeDtypeStruct((m, LANES), F32),
                   jax.ShapeDtypeStruct((n_seq, LANES, seq), F32)] + [
            jax.ShapeDtypeStruct((n_seq, gd, seq), F32) for _ in range(n_kv)],
        compiler_params=_cparams(2),
        name="nsa_projection",
    )(x, wq_bf16, wt_bf16)


def _mm_res_ln_kernel(a_ref, w_ref, x_ref, g_ref, b_ref, o_ref):
    y = _dot(a_ref[...].astype(BF16), w_ref[...])
    o_ref[...] = _layer_norm_rows(ALPHA * x_ref[...] + y, g_ref[...], b_ref[...])


def matmul_residual_ln(a, w_bf16, x, g, b, tm=MATMUL_ROWS):
    m, k = a.shape
    d = w_bf16.shape[1]
    tm = _row_tile(m, tm)
    return pl.pallas_call(
        _mm_res_ln_kernel,
        grid=(m // tm,),
        in_specs=[pl.BlockSpec((tm, k), lambda i: (i, 0)),
                  pl.BlockSpec((k, d), lambda i: (0, 0)),
                  pl.BlockSpec((tm, d), lambda i: (i, 0)),
                  pl.BlockSpec((1, d), lambda i: (0, 0)),
                  pl.BlockSpec((1, d), lambda i: (0, 0))],
        out_specs=pl.BlockSpec((tm, d), lambda i: (i, 0)),
        out_shape=jax.ShapeDtypeStruct((m, d), F32),
        compiler_params=_cparams(1),
        name="matmul_residual_ln",
    )(a, w_bf16, x, g.reshape(1, d), b.reshape(1, d))


def _sum2_mm_res_ln_kernel(a_ref, b2_ref, w_ref, x_ref, g_ref, b_ref, o_ref):
    y = _dot((a_ref[...] + b2_ref[...]).astype(BF16), w_ref[...])
    o_ref[...] = _layer_norm_rows(ALPHA * x_ref[...] + y, g_ref[...], b_ref[...])


def sum2_matmul_residual_ln(a, b2, w_bf16, x, g, b, tm=MATMUL_ROWS):
    m, k = a.shape
    d = w_bf16.shape[1]
    tm = _row_tile(m, tm)
    row = lambda i: (i, 0)
    fixed = lambda i: (0, 0)
    return pl.pallas_call(
        _sum2_mm_res_ln_kernel,
        grid=(m // tm,),
        in_specs=[pl.BlockSpec((tm, k), row), pl.BlockSpec((tm, k), row),
                  pl.BlockSpec((k, d), fixed),
                  pl.BlockSpec((tm, d), row),
                  pl.BlockSpec((1, d), fixed), pl.BlockSpec((1, d), fixed)],
        out_specs=pl.BlockSpec((tm, d), row),
        out_shape=jax.ShapeDtypeStruct((m, d), F32),
        compiler_params=_cparams(1),
        name="sum2_matmul_residual_ln",
    )(a, b2, w_bf16, x, g.reshape(1, d), b.reshape(1, d))


def _expand_gates(gates_raw, e_ref):
    sig = jax.nn.sigmoid(gates_raw)
    hi = sig.astype(BF16)
    lo = (sig - hi.astype(F32)).astype(BF16)
    e = e_ref[...]
    return _dot(hi, e) + _dot(lo, e)


def _nsa_merge_ln_kernel(oc_ref, os_ref, ow_ref, gt_ref, e_ref, w_ref, x_ref, g_ref, b_ref, o_ref, *, k):
    gx = _expand_gates(gt_ref[...], e_ref)
    o = gx[:, 0:k] * oc_ref[...] + gx[:, k:2 * k] * os_ref[...] + gx[:, 2 * k:3 * k] * ow_ref[...]
    y = _dot(o.astype(BF16), w_ref[...])
    o_ref[...] = _layer_norm_rows(ALPHA * x_ref[...] + y, g_ref[...], b_ref[...])


def nsa_merge_residual_ln(oc, osl, ow, gates, e_bf16, w_bf16, x, g, b, tm=MERGE_ROWS):
    m, k = oc.shape
    d = w_bf16.shape[1]
    tm = _row_tile(m, tm)
    row = lambda i: (i, 0)
    fixed = lambda i: (0, 0)
    return pl.pallas_call(
        functools.partial(_nsa_merge_ln_kernel, k=k),
        grid=(m // tm,),
        in_specs=[pl.BlockSpec((tm, k), row), pl.BlockSpec((tm, k), row), pl.BlockSpec((tm, k), row),
                  pl.BlockSpec((tm, LANES), row),
                  pl.BlockSpec((LANES, 3 * k), fixed),
                  pl.BlockSpec((k, d), fixed),
                  pl.BlockSpec((tm, d), row),
                  pl.BlockSpec((1, d), fixed), pl.BlockSpec((1, d), fixed)],
        out_specs=pl.BlockSpec((tm, d), row),
        out_shape=jax.ShapeDtypeStruct((m, d), F32),
        compiler_params=_cparams(1),
        name="nsa_merge_residual_ln",
    )(oc, osl, ow, gates, e_bf16, w_bf16, x, g.reshape(1, d), b.reshape(1, d))


def _even_mixer_kernel(*refs, rows, valid, dconv, dk, chunks, project):
    if project:
        x_ref, w_ref, *refs = refs
    else:
        z_ref, *refs = refs
    (hist_ref, st_ref, cos_ref, sin_ref, decay_ref, qdec_ref, kdec_ref, sdec_ref, cw_ref, cb_ref, gn_ref,
     y_ref, hist_out_ref, st_out_ref, carry, state) = refs
    r0 = valid - 2 - (rows - SUBLANES)

    @pl.when(pl.program_id(1) == 0)
    def _():
        carry[r0:r0 + 2, :] = hist_ref[0]
        state[...] = st_ref[0]

    z_all = _dot(x_ref[...].astype(BF16), w_ref[...]) if project else z_ref[...]
    d = dconv
    scale = np.float32(dk ** -0.5)
    heads = range(RET_HEADS)
    col = lambda part, hh: slice(part * d + hh * dk, part * d + (hh + 1) * dk)
    row = lax.broadcasted_iota(jnp.int32, (rows, d), 0)
    for ci in range(chunks):
        at = slice(ci * rows, (ci + 1) * rows)
        z = z_all[at]
        ch = z[:, 2 * d:3 * d] * z[:, 0:d]
        h0 = carry[r0:r0 + 1, :]
        h1 = carry[r0 + 1:r0 + 2, :]
        m1 = jnp.where(row == 0, h1, pltpu.roll(ch, 1, 0))
        m2 = jnp.where(row == 0, h0, jnp.where(row == 1, h1, pltpu.roll(ch, 2, 0)))
        u = ((cb_ref[...] + m2 * cw_ref[0:1, :]) + m1 * cw_ref[1:2, :]) + ch * cw_ref[2:3, :]
        y_ref[at, 0:d] = z[:, d:2 * d] * u
        carry[...] = ch[rows - SUBLANES:rows, :]

        cosf = cos_ref[at, :]
        sinf = sin_ref[at, :]
        qs, ks, vbs = [], [], []
        for hh in heads:
            q = z[:, col(3, hh)]
            k = z[:, col(4, hh)]
            qs.append(((q * cosf + pltpu.roll(q, dk // 2, 1) * sinf) * scale).astype(BF16))
            ks.append(k * cosf + pltpu.roll(k, dk // 2, 1) * sinf)
            vbs.append(z[:, col(5, hh)].astype(BF16))
        s_old = [state[hh] for hh in heads]
        scores = [_nt_dot(qs[hh], ks[hh].astype(BF16)) * decay_ref[hh] for hh in heads]
        cross = [_dot(qs[hh], s_old[hh].astype(BF16)) * qdec_ref[hh] for hh in heads]
        intra = [_dot(scores[hh].astype(BF16), vbs[hh]) for hh in heads]
        for hh in heads:
            kd = (ks[hh] * kdec_ref[hh]).astype(BF16)
            state[hh] = s_old[hh] * sdec_ref[hh] + _tn_dot(kd, vbs[hh])
        for hh in heads:
            o = intra[hh] + cross[hh]
            mu = jnp.mean(o, axis=-1, keepdims=True)
            dv = o - mu
            var = jnp.mean(dv * dv, axis=-1, keepdims=True)
            on = dv * lax.rsqrt(var + LN_EPS) * gn_ref[:, hh * dk:(hh + 1) * dk]
            gsw = z[:, col(6, hh)]
            y_ref[at, d + hh * dk:d + (hh + 1) * dk] = (gsw * jax.nn.sigmoid(gsw)) * on
    hist_out_ref[0] = carry[r0:r0 + 2, :]
    st_out_ref[0] = state[...]


def _retention_tables(rows, valid, dk):
    log_gamma = jnp.log1p(-jnp.exp2(-5.0 - jnp.arange(RET_HEADS, dtype=F32)))
    n = jnp.arange(rows, dtype=F32)
    diff = n[:, None] - n[None, :]
    lg = log_gamma[:, None, None]
    decay = jnp.where(diff >= 0, jnp.exp(lg * jnp.maximum(diff, 0.0)), 0.0)
    q_dec = jnp.exp((n[None, :] + 1.0) * log_gamma[:, None])
    k_dec = jnp.where(n[None, :] < valid, jnp.exp((valid - 1.0 - n[None, :]) * log_gamma[:, None]), 0.0)
    s_dec = jnp.exp(valid * log_gamma)
    bc = lambda a: jnp.broadcast_to(a[:, :, None], (RET_HEADS, rows, dk))
    return decay, bc(q_dec), bc(k_dec), jnp.broadcast_to(s_dec[:, None, None], (RET_HEADS, 1, dk))


def _rope_tables(pos, dk):
    half = dk // 2
    inv = ROPE_BASE ** (-jnp.arange(half, dtype=F32) / half)
    ang = pos.astype(F32)[:, None] * inv
    cos, sin = jnp.cos(ang), jnp.sin(ang)
    return jnp.concatenate([cos, cos], axis=-1), jnp.concatenate([-sin, sin], axis=-1)


def even_mixer(x, w_in_bf16, hist, st, pos, rows, valid, conv_w, conv_b, gn_g, chunks=1):
    n_seq, _, dconv = hist.shape
    dk = st.shape[-1]
    project = w_in_bf16 is not None
    tile = chunks * rows
    n_steps = x.shape[0] // (n_seq * tile)
    cosf, sinf = _rope_tables(pos, dk)
    decay, q_dec, k_dec, s_dec = _retention_tables(rows, valid, dk)
    fixed3 = lambda s, c: (0, 0, 0)
    fixed2 = lambda s, c: (0, 0)
    lead_specs = [pl.BlockSpec((tile, x.shape[1]), lambda s, c: (s * n_steps + c, 0))]
    lead_args = [x]
    if project:
        lead_specs.append(pl.BlockSpec(w_in_bf16.shape, fixed2))
        lead_args.append(w_in_bf16)
    n_chunks = n_steps
    rows_out = x.shape[0]
    return pl.pallas_call(
        functools.partial(_even_mixer_kernel, rows=rows, valid=valid, dconv=dconv, dk=dk, chunks=chunks,
                          project=project),
        grid=(n_seq, n_steps),
        in_specs=lead_specs + [
                  pl.BlockSpec((1, 2, dconv), lambda s, c: (s, 0, 0)),
                  pl.BlockSpec((1, RET_HEADS, dk, dk), lambda s, c: (s, 0, 0, 0)),
                  pl.BlockSpec((tile, dk), lambda s, c: (c, 0)),
                  pl.BlockSpec((tile, dk), lambda s, c: (c, 0)),
                  pl.BlockSpec((RET_HEADS, rows, rows), fixed3),
                  pl.BlockSpec((RET_HEADS, rows, dk), fixed3),
                  pl.BlockSpec((RET_HEADS, rows, dk), fixed3),
                  pl.BlockSpec((RET_HEADS, 1, dk), fixed3),
                  pl.BlockSpec((SCONV_W, dconv), fixed2),
                  pl.BlockSpec((1, dconv), fixed2),
                  pl.BlockSpec((1, RET_HEADS * dk), fixed2)],
        out_specs=[pl.BlockSpec((tile, 2 * dconv), lambda s, c: (s * n_chunks + c, 0)),
                   pl.BlockSpec((1, 2, dconv), lambda s, c: (s, 0, 0)),
                   pl.BlockSpec((1, RET_HEADS, dk, dk), lambda s, c: (s, 0, 0, 0))],
        out_shape=[jax.ShapeDtypeStruct((rows_out, 2 * dconv), F32),
                   jax.ShapeDtypeStruct((n_seq, 2, dconv), F32),
                   jax.ShapeDtypeStruct((n_seq, RET_HEADS, dk, dk), F32)],
        scratch_shapes=[pltpu.VMEM((SUBLANES, dconv), F32), pltpu.VMEM((RET_HEADS, dk, dk), F32)],
        compiler_params=_cparams(2),
        name="even_mixer",
    )(*lead_args, hist, st, cosf, sinf, decay, q_dec, k_dec, s_dec, conv_w, conv_b.reshape(1, dconv),
      gn_g.reshape(1, RET_HEADS * dk))


def _conv_gate(a, gate, m1, m2, cw_ref, cb_ref):
    conv = ((cb_ref[...] + m2 * cw_ref[0:1, :]) + m1 * cw_ref[1:2, :]) + a * cw_ref[2:3, :]
    return _gelu(conv) * gate


def _ffn_up_seq_kernel(x_ref, wa_ref, wg_ref, h_ref, cw_ref, cb_ref, o_ref, hist_out_ref, carry, *, tm):
    @pl.when(pl.program_id(2) == 0)
    def _():
        carry[SUBLANES - 2:SUBLANES, :] = h_ref[0]

    xb = x_ref[...].astype(BF16)
    a = _dot(xb, wa_ref[...])
    gate = _dot(xb, wg_ref[...])
    row = lax.broadcasted_iota(jnp.int32, a.shape, 0)
    h0 = carry[SUBLANES - 2:SUBLANES - 1, :]
    h1 = carry[SUBLANES - 1:SUBLANES, :]
    m1 = jnp.where(row == 0, h1, pltpu.roll(a, 1, 0))
    m2 = jnp.where(row == 0, h0, jnp.where(row == 1, h1, pltpu.roll(a, 2, 0)))
    o_ref[...] = _conv_gate(a, gate, m1, m2, cw_ref, cb_ref).astype(o_ref.dtype)
    carry[...] = a[tm - SUBLANES:tm, :]
    hist_out_ref[0] = carry[SUBLANES - 2:SUBLANES, :]


def ffn_up_sequences(x, n_seq, hist, w_up_bf16, conv_w, conv_b, tm=MATMUL_ROWS, n_col=1):
    m, k = x.shape
    dff = conv_w.shape[1]
    seq = m // n_seq
    tm = _row_tile(seq, tm)
    tps = seq // tm
    tn = dff // n_col
    assert tn % LANES == 0
    return pl.pallas_call(
        functools.partial(_ffn_up_seq_kernel, tm=tm),
        grid=(n_col, n_seq, tps),
        in_specs=[pl.BlockSpec((tm, k), lambda j, s, i: (s * tps + i, 0)),
                  pl.BlockSpec((k, tn), lambda j, s, i: (0, j)),
                  pl.BlockSpec((k, tn), lambda j, s, i: (0, j + n_col)),
                  pl.BlockSpec((1, 2, tn), lambda j, s, i: (s, 0, j)),
                  pl.BlockSpec((FFN_W, tn), lambda j, s, i: (0, j)),
                  pl.BlockSpec((1, tn), lambda j, s, i: (0, j))],
        out_specs=[pl.BlockSpec((tm, tn), lambda j, s, i: (s * tps + i, j)),
                   pl.BlockSpec((1, 2, tn), lambda j, s, i: (s, 0, j))],
        out_shape=[jax.ShapeDtypeStruct((m, dff), BF16),
                   jax.ShapeDtypeStruct((n_seq, 2, dff), F32)],
        scratch_shapes=[pltpu.VMEM((SUBLANES, tn), F32)],
        compiler_params=_cparams(3),
        name="ffn_up_sequences",
    )(x, w_up_bf16, w_up_bf16, hist, conv_w, conv_b.reshape(1, dff))


def _ffn_up_short_kernel(x_ref, wa_ref, wg_ref, h1_ref, h2_ref, cw_ref, cb_ref, o_ref, a_ref):
    xb = x_ref[...].astype(BF16)
    a = _dot(xb, wa_ref[...])
    gate = _dot(xb, wg_ref[...])
    t = lax.broadcasted_iota(jnp.int32, a.shape, 0) % SUBLANES
    m1 = jnp.where(t == 0, h1_ref[...], pltpu.roll(a, 1, 0))
    m2 = jnp.where(t < 2, h2_ref[...], pltpu.roll(a, 2, 0))
    o_ref[...] = _conv_gate(a, gate, m1, m2, cw_ref, cb_ref).astype(o_ref.dtype)
    a_ref[...] = a


def ffn_up_short(x, hist, w_up_bf16, conv_w, conv_b, n_col=2):
    m, k = x.shape
    dff = conv_w.shape[1]
    n_seq = m // SUBLANES
    tn = dff // n_col
    zeros = jnp.zeros((n_seq, SUBLANES, dff), F32)
    h1 = zeros.at[:, 0].set(hist[:, 1]).reshape(m, dff)
    h2 = zeros.at[:, 0].set(hist[:, 0]).at[:, 1].set(hist[:, 1]).reshape(m, dff)
    col = lambda j: (0, j)
    return pl.pallas_call(
        _ffn_up_short_kernel,
        grid=(n_col,),
        in_specs=[pl.BlockSpec((m, k), lambda j: (0, 0)),
                  pl.BlockSpec((k, tn), col),
                  pl.BlockSpec((k, tn), lambda j: (0, j + n_col)),
                  pl.BlockSpec((m, tn), col), pl.BlockSpec((m, tn), col),
                  pl.BlockSpec((FFN_W, tn), col), pl.BlockSpec((1, tn), col)],
        out_specs=[pl.BlockSpec((m, tn), col), pl.BlockSpec((m, tn), col)],
        out_shape=[jax.ShapeDtypeStruct((m, dff), F32), jax.ShapeDtypeStruct((m, dff), F32)],
        compiler_params=_cparams(1),
        name="ffn_up_short",
    )(x, w_up_bf16, w_up_bf16, h1, h2, conv_w, conv_b.reshape(1, dff))


def _compress_kernel(pt_ref, *refs, pages):
    page_refs = refs[:pages + 1]
    w_ref, pecol_ref, w1_ref, w2_ref, o_ref = refs[pages + 1:pages + 6]
    rows_refs = refs[pages + 6:]
    parts = len(rows_refs)
    ppp = (pages + 1) // parts
    cpp = PAGE_SIZE // CMP_STRIDE
    n = (pages + 1) * cpp
    n_p = ppp * cpp
    hidden = w1_ref.shape[1]
    gpr = LANES // NSA_DH
    pieces = NSA_KV // gpr
    for i, r in enumerate(page_refs):
        for pc in range(pieces):
            tile = r[0, pc * gpr:(pc + 1) * gpr].reshape(LANES, PAGE_SIZE)
            rows_refs[i // ppp][pc, (i % ppp) * PAGE_SIZE:(i % ppp + 1) * PAGE_SIZE, :] = tile.T
    pe_term = jnp.sum(pecol_ref[...] * w1_ref[...], axis=0, keepdims=True)
    accs = []
    for rows_ref in rows_refs:
        lhs = jnp.concatenate([rows_ref[pc].reshape(n_p, CMP_STRIDE * LANES) for pc in range(pieces)], axis=0)
        accs.append(_dot(lhs.astype(BF16), w_ref[...]))
    for pc in range(pieces):
        for gl in range(gpr):
            a = jnp.concatenate([acc[pc * n_p:(pc + 1) * n_p, gl * 2 * hidden:(gl + 1) * 2 * hidden]
                                 for acc in accs], axis=0)
            nxt = pltpu.roll(a, n - 1, 0)
            pre = pe_term + a[:, 0:hidden]
            pre = pre + nxt[:, hidden:2 * hidden]
            o_ref[0, pc * gpr + gl] = _dot(_gelu(pre[0:pages * cpp]).astype(BF16), w2_ref[...])


def compress(rows_t, page_table, pe, w1, w2, pages=COMPRESS_PAGES):
    pooled = page_table is not None
    if pooled:
        n_seq, n_pages = page_table.shape
    else:
        n_seq, n_pages = rows_t.shape[0], rows_t.shape[3] // PAGE_SIZE
        page_table = jnp.zeros((1, 1), jnp.int32)
    pages = min(pages, n_pages)
    assert n_pages % pages == 0
    parts = next(p for p in (3, 5, 1) if (pages + 1) % p == 0)
    hidden = w1.shape[1]
    cpp = PAGE_SIZE // CMP_STRIDE
    r = CMP_LEN // CMP_STRIDE
    gpr = LANES // NSA_DH
    assert r == 2 and gpr == 2
    w1p = w1.reshape(r, CMP_STRIDE, NSA_DH, hidden)
    w16 = jnp.concatenate([w1p[0], w1p[1]], axis=-1)
    zero = jnp.zeros_like(w16)
    per_tok = jnp.concatenate([jnp.concatenate([w16, zero], axis=-1),
                               jnp.concatenate([zero, w16], axis=-1)], axis=1)
    w_chunk = per_tok.reshape(CMP_STRIDE * LANES, gpr * 2 * hidden).astype(BF16)
    w2p = jnp.pad(w2, ((0, 0), (0, SLOT - NSA_DH))).astype(BF16)
    pecol = pe.reshape(CMP_LEN * NSA_DH, 1)

    def page_map(i):
        if pooled:
            return lambda s, j, pt: (pt[s, jnp.minimum(j * pages + i, n_pages - 1)], 0, 0, 0)
        return lambda s, j, pt: (s, 0, 0, jnp.minimum(j * pages + i, n_pages - 1))

    fixed2 = lambda s, j, pt: (0, 0)
    grid_spec = pltpu.PrefetchScalarGridSpec(
        num_scalar_prefetch=1,
        grid=(n_seq, n_pages // pages),
        in_specs=[pl.BlockSpec((1, NSA_KV, NSA_DH, PAGE_SIZE), page_map(i)) for i in range(pages + 1)] + [
            pl.BlockSpec((CMP_STRIDE * LANES, gpr * 2 * hidden), fixed2),
            pl.BlockSpec((CMP_LEN * NSA_DH, 1), fixed2),
            pl.BlockSpec((CMP_LEN * NSA_DH, hidden), fixed2),
            pl.BlockSpec((hidden, SLOT), fixed2)],
        out_specs=pl.BlockSpec((1, NSA_KV, pages * cpp, SLOT), lambda s, j, pt: (s, 0, j, 0)),
        scratch_shapes=[pltpu.VMEM((NSA_KV // gpr, (pages + 1) // parts * PAGE_SIZE, LANES), F32)
                        for _ in range(parts)],
    )
    return pl.pallas_call(
        functools.partial(_compress_kernel, pages=pages),
        grid_spec=grid_spec,
        out_shape=jax.ShapeDtypeStruct((n_seq, NSA_KV, n_pages * cpp, SLOT), F32),
        compiler_params=_cparams(2),
        name="compress",
    )(page_table, *([rows_t] * (pages + 1)), w_chunk, pecol, w1, w2p)


def _cmp_select_kernel(q_ref, kc_ref, vc_ref, ov_ref, o_ref, idx_ref, *, tq, n_cmp, n_slc, pos0):
    ncp = kc_ref.shape[2]
    nsp = ov_ref.shape[0]
    hrows = NSA_GROUP * tq
    q_pos = pos0 + (lax.broadcasted_iota(jnp.int32, (hrows, ncp), 0) & (tq - 1))
    blk_i = lax.broadcasted_iota(jnp.int32, (hrows, ncp), 1)
    valid = (blk_i * CMP_STRIDE + (CMP_LEN - 1) <= q_pos) & (blk_i < n_cmp)
    ov = ov_ref[...]
    groups = range(NSA_KV)
    heads = [range(g * NSA_GROUP, (g + 1) * NSA_GROUP) for g in groups]
    scores = [_nt_dot(jnp.concatenate([q_ref[:, h * SLOT:(h + 1) * SLOT] for h in heads[g]], axis=0).astype(BF16),
                      kc_ref[0, g].astype(BF16)) for g in groups]
    probs = []
    for g in groups:
        s = jnp.where(valid, scores[g], NEG_INF)
        m = jnp.max(s, axis=-1, keepdims=True)
        e = jnp.where(valid, jnp.exp(s - m), 0.0)
        den = jnp.sum(e, axis=-1, keepdims=True)
        probs.append(e * (1.0 / jnp.where(den > 0.0, den, 1.0)))
    outs = [_dot(probs[g].astype(BF16), vc_ref[0, g].astype(BF16)) for g in groups]
    imps = []
    for g in groups:
        p_sum = jnp.zeros((tq, ncp), F32)
        for j, h in enumerate(heads[g]):
            o_ref[:, h * SLOT:(h + 1) * SLOT] = outs[g][j * tq:(j + 1) * tq]
            p_sum = p_sum + probs[g][j * tq:(j + 1) * tq]
        hi = p_sum.astype(BF16)
        lo = (p_sum - hi.astype(F32)).astype(BF16)
        imps.append(_nt_dot(ov, hi) + _nt_dot(ov, lo))
    cols = NSA_KV * tq
    imp = jnp.concatenate(imps, axis=1)
    blk = lax.broadcasted_iota(jnp.int32, (nsp, cols), 0)
    cur = (pos0 + (lax.broadcasted_iota(jnp.int32, (nsp, cols), 1) & (tq - 1))) // SLC_BLOCK
    real = blk < n_slc
    causal = real & (blk <= cur)
    forced = (blk == 0) | (blk == cur) | (blk == cur - 1)
    score = jnp.where(causal, imp + jnp.where(forced, FORCE_BONUS, 0.0), NEG_INF)
    score = jnp.where(real, score, REMOVED)
    idx = jnp.zeros((SLC_TOPN, cols), jnp.int32)
    idx_row = lax.broadcasted_iota(jnp.int32, (SLC_TOPN, cols), 0)
    for it in range(min(SLC_TOPN, n_slc)):
        m = jnp.max(score, axis=0, keepdims=True)
        first = jnp.min(jnp.where(score == m, blk, nsp), axis=0, keepdims=True)
        score = jnp.where(blk == first, REMOVED, score)
        idx = jnp.where(idx_row == it, first, idx)
    idx_ref[0] = idx


def cmp_block_overlap(n_cmp_pad, n_cmp, n_slc, n_slc_pad, lane_off):
    i = np.arange(n_cmp_pad)[:, None]
    j = np.arange(n_slc_pad)[None, :] - lane_off
    start = i * CMP_STRIDE
    hit = (start < (j + 1) * SLC_BLOCK) & (start + CMP_LEN > j * SLC_BLOCK) & (i < n_cmp) & (j >= 0) & (j < n_slc)
    return jnp.asarray(hit.astype(np.float32), dtype=BF16)


def cmp_attention_select(q_slots, kcc, vcc, n_seq, n_cmp, n_slc, pos0):
    tokens = q_slots.shape[0]
    tq = tokens // n_seq
    assert tq & (tq - 1) == 0
    ncp = kcc.shape[2]
    nsp = -(-n_slc // SUBLANES) * SUBLANES
    ov_t = cmp_block_overlap(ncp, n_cmp, n_slc, nsp, 0).T
    hw = NSA_HEADS * SLOT
    o_cmp, idx_t = pl.pallas_call(
        functools.partial(_cmp_select_kernel, tq=tq, n_cmp=n_cmp, n_slc=n_slc, pos0=pos0),
        grid=(n_seq,),
        in_specs=[pl.BlockSpec((tq, hw), lambda s: (s, 0)),
                  pl.BlockSpec((1, NSA_KV, ncp, SLOT), lambda s: (s, 0, 0, 0)),
                  pl.BlockSpec((1, NSA_KV, ncp, SLOT), lambda s: (s, 0, 0, 0)),
                  pl.BlockSpec((nsp, ncp), lambda s: (0, 0))],
        out_specs=[pl.BlockSpec((tq, hw), lambda s: (s, 0)),
                   pl.BlockSpec((1, SLC_TOPN, NSA_KV * tq), lambda s: (s, 0, 0))],
        out_shape=[jax.ShapeDtypeStruct((tokens, hw), F32),
                   jax.ShapeDtypeStruct((n_seq, SLC_TOPN, NSA_KV * tq), jnp.int32)],
        compiler_params=_cparams(1),
        name="cmp_attention_select",
    )(q_slots, kcc, vcc, ov_t)
    topn = min(SLC_TOPN, n_slc)
    idx = idx_t[:, :topn].reshape(n_seq, topn, NSA_KV, tq).transpose(0, 2, 3, 1)
    return o_cmp, idx


def _cmp_select_prompt_kernel(q_ref, kc_ref, vc_ref, ovt_ref, gt_ref, o_ref, sel_ref, *, tq, n_cmp, n_slc):
    t0 = pl.program_id(2) * tq
    ncp = kc_ref.shape[2]
    nsr = ovt_ref.shape[0]
    kc = kc_ref[0, 0].astype(BF16)
    vc = vc_ref[0, 0].astype(BF16)
    q_pos = t0 + lax.broadcasted_iota(jnp.int32, (ncp, tq), 1)
    blk_i = lax.broadcasted_iota(jnp.int32, (ncp, tq), 0)
    valid = (blk_i * CMP_STRIDE + (CMP_LEN - 1) <= q_pos) & (blk_i < n_cmp)
    p_sum = jnp.zeros((ncp, tq), F32)
    for j in range(NSA_GROUP):
        s = jnp.where(valid, _nt_dot(kc, q_ref[:, j * SLOT:(j + 1) * SLOT]), NEG_INF)
        m = jnp.max(s, axis=0, keepdims=True)
        e = jnp.where(valid, jnp.exp(s - m), 0.0)
        den = jnp.sum(e, axis=0, keepdims=True)
        p = e * (1.0 / jnp.where(den > 0.0, den, 1.0))
        gate = jax.nn.sigmoid(gt_ref[0, pl.ds(pl.program_id(1) * NSA_GROUP + j, 1), :])
        o_ref[:, j * SLOT:(j + 1) * SLOT] = _tn_dot((p * gate).astype(BF16), vc)
        p_sum = p_sum + p
    hi = p_sum.astype(BF16)
    lo = (p_sum - hi.astype(F32)).astype(BF16)
    ovt = ovt_ref[...]
    imp = _dot(ovt, hi) + _dot(ovt, lo)
    blk = lax.broadcasted_iota(jnp.int32, (nsr, tq), 0)
    cur = (t0 + lax.broadcasted_iota(jnp.int32, (nsr, tq), 1)) // SLC_BLOCK
    real = blk < n_slc
    causal = real & (blk <= cur)
    forced = (blk == 0) | (blk == cur) | (blk == cur - 1)
    score = jnp.where(causal, imp + jnp.where(forced, FORCE_BONUS, 0.0), NEG_INF)
    score = jnp.where(real, score, REMOVED)
    picked = jnp.zeros((nsr, tq), jnp.bool_)
    for _ in range(min(SLC_TOPN, n_slc)):
        m = jnp.max(score, axis=0, keepdims=True)
        first = jnp.min(jnp.where(score == m, blk, nsr), axis=0, keepdims=True)
        hit = blk == first
        picked = picked | hit
        score = jnp.where(hit, REMOVED, score)
    bias_t = jnp.where(real & ~(picked & causal), NEG_INF, 0.0)
    slot_t = jnp.concatenate([jnp.zeros((NSA_DH, tq), F32), bias_t], axis=0)
    sel_ref[0, 0] = slot_t.T


def cmp_attention_select_prompt(q_slots, kcc, vcc, gates_t, n_seq, n_cmp, n_slc, tq):
    tokens = q_slots.shape[0]
    t = tokens // n_seq
    tq = _row_tile(t, tq)
    nt = t // tq
    ncp = kcc.shape[2]
    nsr = SLOT - NSA_DH
    assert n_slc <= nsr
    ovt = cmp_block_overlap(ncp, n_cmp, n_slc, nsr, 0).T
    gw = NSA_GROUP * SLOT
    return pl.pallas_call(
        functools.partial(_cmp_select_prompt_kernel, tq=tq, n_cmp=n_cmp, n_slc=n_slc),
        grid=(n_seq, NSA_KV, nt),
        in_specs=[pl.BlockSpec((tq, gw), lambda s, g, i: (s * nt + i, g)),
                  pl.BlockSpec((1, 1, ncp, SLOT), lambda s, g, i: (s, g, 0, 0)),
                  pl.BlockSpec((1, 1, ncp, SLOT), lambda s, g, i: (s, g, 0, 0)),
                  pl.BlockSpec((nsr, ncp), lambda s, g, i: (0, 0)),
                  pl.BlockSpec((1, LANES, tq), lambda s, g, i: (s, 0, i))],
        out_specs=[pl.BlockSpec((tq, gw), lambda s, g, i: (s * nt + i, g)),
                   pl.BlockSpec((1, 1, tq, SLOT), lambda s, g, i: (s, g, i, 0))],
        out_shape=[jax.ShapeDtypeStruct((tokens, NSA_HEADS * SLOT), F32),
                   jax.ShapeDtypeStruct((n_seq, NSA_KV, t, SLOT), F32)],
        compiler_params=_cparams(3),
        name="cmp_attention_select_prompt",
    )(q_slots, kcc, vcc, ovt, gates_t)


def _prompt_slc_win_kernel(q_ref, sel_ref, ks_ref, vs_ref, kw_ref, vw_ref, oh_ref, gt_ref, o_ref,
                           m_slc, acc_slc, m_win, acc_win, *, tq, seq):
    acc_rows = NSA_DH + 16
    qi = pl.program_id(2)
    t0 = qi * tq
    rows = NSA_GROUP * tq
    sel = sel_ref[0, 0]
    q_plain = jnp.concatenate([q_ref[:, j * SLOT:(j + 1) * SLOT] for j in range(NSA_GROUP)], axis=0)
    q_aug = jnp.concatenate([(q_ref[:, j * SLOT:(j + 1) * SLOT].astype(F32) + sel).astype(BF16)
                             for j in range(NSA_GROUP)], axis=0)
    zeros_k = jnp.zeros((SLOT - NSA_DH, tq), F32)
    ones_row = (lax.broadcasted_iota(jnp.int32, (acc_rows - NSA_DH, tq), 0) == 0).astype(BF16)
    rel = lax.broadcasted_iota(jnp.int32, (tq, LANES), 1) - lax.broadcasted_iota(jnp.int32, (tq, LANES), 0)

    def scores(q_rows, k_top, k_bottom):
        k_rows = jnp.concatenate([k_top, k_bottom], axis=0).T.astype(BF16)
        return [_nt_dot(k_rows, q_rows[c:c + 2 * LANES]) for c in range(0, rows, 2 * LANES)]

    def update(state, s_t, v_top, start, mask):
        m_ref, acc_ref = state
        v_t = jnp.concatenate([v_top.astype(BF16), ones_row], axis=0)
        for cg in range(rows // (2 * LANES)):
            p_parts, a_parts = [], []
            for h in range(2):
                c0 = (2 * cg + h) * LANES
                x = s_t[cg][:, h * LANES:(h + 1) * LANES]
                if mask is not None:
                    lo, hi = mask
                    off = t0 - start + (c0 & (tq - 1))
                    keep = rel >= lo - off
                    if hi is not None:
                        keep = keep & (rel < hi - off)
                    x = jnp.where(keep, x, NEG_INF)
                m_old = m_ref[:, c0:c0 + LANES]
                m_new = jnp.maximum(m_old, jnp.max(x, axis=0, keepdims=True))
                m_ref[:, c0:c0 + LANES] = m_new
                a_parts.append(jnp.exp(m_old - m_new))
                p_parts.append(jnp.exp(x - m_new).astype(BF16))
            c0 = 2 * cg * LANES
            pv = _dot(v_t, jnp.concatenate(p_parts, axis=1))
            acc_ref[:, c0:c0 + 2 * LANES] = jnp.concatenate(a_parts, axis=1) * acc_ref[:, c0:c0 + 2 * LANES] + pv

    slc_state = (m_slc, acc_slc)
    win_state = (m_win, acc_win)
    for m_ref, acc_ref in (slc_state, win_state):
        m_ref[...] = jnp.full(m_ref.shape, NEG_INF, F32)
        acc_ref[...] = jnp.zeros(acc_ref.shape, F32)

    slot_pad = jnp.zeros((SLOT - NSA_DH, tq), F32)

    def finish(state, branch, accumulate):
        for j in range(NSA_GROUP):
            a = state[1][:, j * tq:(j + 1) * tq]
            head = pl.program_id(1) * NSA_GROUP + j
            gate = jax.nn.sigmoid(gt_ref[0, pl.ds(branch * NSA_HEADS + head, 1), :])
            o_t = jnp.concatenate([a[0:NSA_DH] * (gate * (1.0 / a[NSA_DH:NSA_DH + 1, :])), slot_pad], axis=0)
            if accumulate:
                o_ref[:, j * SLOT:(j + 1) * SLOT] += o_t.T
            else:
                o_ref[:, j * SLOT:(j + 1) * SLOT] = o_t.T

    def slc_scores(start):
        return scores(q_aug, ks_ref[0, :, pl.ds(start, tq)], oh_ref[:, pl.ds(start, tq)])

    def slc_update(s_t, start, mask):
        update(slc_state, s_t, vs_ref[0, :, pl.ds(start, tq)], start, mask)

    k0 = jnp.clip(t0 - WINDOW, 0, seq - WINDOW - tq)
    n_win = WINDOW // tq + 1
    win_start = [pl.multiple_of(k0 + i * tq, tq) for i in range(n_win)]
    win_mask = (0, WINDOW)

    def win_scores(i):
        return scores(q_plain, kw_ref[0, :, pl.ds(win_start[i], tq)], zeros_k)

    def slc_tiles(first, count):
        starts = [pl.multiple_of((first + i) * tq, tq) for i in range(count)]
        s_all = [slc_scores(st) for st in starts]
        for s_t, st in zip(s_all, starts):
            slc_update(s_t, st, None)

    def quad(k4, carry):
        slc_tiles(4 * k4, 4)
        return carry

    lax.fori_loop(0, qi // 4, quad, 0)

    @pl.when(qi % 4 >= 2)
    def _():
        slc_tiles(4 * (qi // 4), 2)

    @pl.when(qi % 2 == 1)
    def _():
        slc_tiles(qi - 1, 1)

    diag = pl.multiple_of(t0, tq)
    s_diag = slc_scores(diag)
    s_win = [win_scores(i) for i in range(n_win)]
    slc_update(s_diag, diag, (0, None))
    for i in range(n_win):
        update(win_state, s_win[i], vw_ref[0, :, pl.ds(win_start[i], tq)], win_start[i], win_mask)
    finish(slc_state, 1, False)
    finish(win_state, 2, True)


def prompt_slc_win_attention(q_slots, sel, ks_t, vs_t, kw_t, vw_t, gates_t, n_seq, tq):
    tokens = q_slots.shape[0]
    seq = tokens // n_seq
    tq = _row_tile(seq, tq)
    assert tq & (tq - 1) == 0 and WINDOW % tq == 0 and seq >= WINDOW + tq
    nt = seq // tq
    gw = NSA_GROUP * SLOT
    assert tq % (2 * LANES) == 0
    onehot_t = jax.nn.one_hot(jnp.arange(seq) // SLC_BLOCK, SLOT - NSA_DH, dtype=F32).T
    kv_spec = pl.BlockSpec((1, NSA_DH, seq), lambda s, g, i: (s, g, 0))
    return pl.pallas_call(
        functools.partial(_prompt_slc_win_kernel, tq=tq, seq=seq),
        grid=(n_seq, NSA_KV, nt),
        in_specs=[pl.BlockSpec((tq, gw), lambda s, g, i: (s * nt + i, g)),
                  pl.BlockSpec((1, 1, tq, SLOT), lambda s, g, i: (s, g, i, 0)),
                  kv_spec, kv_spec, kv_spec, kv_spec,
                  pl.BlockSpec((SLOT - NSA_DH, seq), lambda s, g, i: (0, 0)),
                  pl.BlockSpec((1, LANES, tq), lambda s, g, i: (s, 0, i))],
        out_specs=pl.BlockSpec((tq, gw), lambda s, g, i: (s * nt + i, g)),
        out_shape=jax.ShapeDtypeStruct((tokens, NSA_HEADS * SLOT), F32),
        scratch_shapes=[pltpu.VMEM((1, NSA_GROUP * tq), F32),
                        pltpu.VMEM((NSA_DH + 16, NSA_GROUP * tq), F32),
                        pltpu.VMEM((1, NSA_GROUP * tq), F32),
                        pltpu.VMEM((NSA_DH + 16, NSA_GROUP * tq), F32)],
        compiler_params=_cparams(3),
        name="prompt_slc_win_attention",
    )(q_slots, sel, ks_t, vs_t, kw_t, vw_t, onehot_t, gates_t)


def _sample_slc_kernel(idx_ref, pg_ref, q_ref, kn_ref, vn_ref, pool_k, pool_v, o_ref, kbuf, vbuf, sems,
                       *, topn, past, t_pad, t_real):
    s_id, g_id = pl.program_id(0), pl.program_id(1)
    step = s_id * NSA_KV + g_id
    n_steps = pl.num_programs(0) * NSA_KV

    def copies(item_step, t, slot):
        base = (item_step * t_pad + t) * topn
        g = item_step % NSA_KV
        out = []
        for kk in range(topn):
            page = pg_ref[base + kk]
            out.append(pltpu.make_async_copy(pool_k.at[page, g], kbuf.at[slot, kk], sems.at[slot, 0]))
            out.append(pltpu.make_async_copy(pool_v.at[page, g], vbuf.at[slot, kk], sems.at[slot, 1]))
        return out

    @pl.when(step == 0)
    def _():
        for c in copies(step, 0, 0):
            c.start()

    for t in range(t_real):
        slot = t % 2
        if t + 1 < t_real:
            for c in copies(step, t + 1, 1 - slot):
                c.start()
        else:
            @pl.when(step + 1 < n_steps)
            def _():
                for c in copies(step + 1, 0, 1 - slot):
                    c.start()
        for c in copies(step, t, slot):
            c.wait()
        _sample_slc_token(idx_ref, q_ref, kn_ref, vn_ref, o_ref, kbuf[slot], vbuf[slot],
                          base=(step * t_pad + t) * topn, t=t, topn=topn, past=past)


def _sample_slc_token(idx_ref, q_ref, kn_ref, vn_ref, o_ref, k_tiles, v_tiles, *, base, t, topn, past):
    q_pos = past + t
    cur = q_pos // SLC_BLOCK
    first_new = past // SLC_BLOCK
    q = q_ref[0, 0, t].astype(BF16)
    kb = jnp.concatenate([k_tiles[kk] for kk in range(topn)], axis=1).astype(BF16)
    vb = jnp.concatenate([v_tiles[kk] for kk in range(topn)], axis=1).astype(BF16)
    n_keys = topn * PAGE_SIZE
    lane = lax.broadcasted_iota(jnp.int32, (1, n_keys), 1)
    slot = lane // PAGE_SIZE
    in_page = lane % PAGE_SIZE
    k_pos = in_page
    limit = jnp.zeros((1, n_keys), jnp.int32)
    n_new = jnp.int32(0)
    for kk in range(topn):
        b = idx_ref[base + kk]
        here = slot == kk
        k_pos = jnp.where(here, (b // 2) * PAGE_SIZE + in_page, k_pos)
        last = jnp.where(b <= cur, jnp.minimum(q_pos, past - 1), -1)
        limit = jnp.where(here, jnp.where(in_page // SLC_BLOCK == b % 2, last, -1), limit)
        n_new = n_new + jnp.where(b == first_new, 1, 0)
    valid = k_pos <= limit
    s_old = jnp.where(valid, _dot(q, kb), NEG_INF)
    new_lane = lax.broadcasted_iota(jnp.int32, (1, SUBLANES), 1)
    valid_new = past + new_lane <= jnp.where(n_new > 0, q_pos, past - 1)
    s_new = jnp.where(valid_new, _dot(q, kn_ref[0, 0].astype(BF16)), NEG_INF)
    m = jnp.maximum(jnp.max(s_old, axis=-1, keepdims=True), jnp.max(s_new, axis=-1, keepdims=True))
    p_old = jnp.exp(s_old - m)
    p_new = jnp.exp(s_new - m)
    l = jnp.sum(p_old, axis=-1, keepdims=True) + jnp.sum(p_new, axis=-1, keepdims=True)
    o = _nt_dot(p_old.astype(BF16), vb) + _nt_dot(p_new.astype(BF16), vn_ref[0, 0].astype(BF16))
    o_ref[0, 0, t] = o / l


def sample_slc_attention(q_rows, idx, page_table, pool_k, pool_v, k_new_t, v_new_t, t_real, past):
    n_seq, _, t_pad, _, dh = q_rows.shape
    topn = idx.shape[-1]
    assert past % SLC_BLOCK == 0 and t_real <= SUBLANES and PAGE_SIZE == 2 * SLC_BLOCK
    last_old = past // SLC_BLOCK - 1
    logical = jnp.clip(idx, 0, last_old) // 2
    n_pages = page_table.shape[1]
    hit = logical[..., None] == jnp.arange(n_pages, dtype=jnp.int32)
    phys = jnp.sum(jnp.where(hit, page_table[:, None, None, None, :], 0), axis=-1)

    assert t_real % 2 == 0
    new_spec = pl.BlockSpec((1, 1, dh, SUBLANES), lambda s, g, i_r, p_r: (s, g, 0, 0))
    hbm = pl.BlockSpec(memory_space=pl.ANY)
    grid_spec = pltpu.PrefetchScalarGridSpec(
        num_scalar_prefetch=2,
        grid=(n_seq, NSA_KV),
        in_specs=[pl.BlockSpec((1, 1, t_pad, SUBLANES, dh), lambda s, g, i_r, p_r: (s, g, 0, 0, 0)),
                  new_spec, new_spec, hbm, hbm],
        out_specs=pl.BlockSpec((1, 1, t_real, SUBLANES, dh), lambda s, g, i_r, p_r: (s, g, 0, 0, 0)),
        scratch_shapes=[pltpu.VMEM((2, topn, dh, PAGE_SIZE), F32),
                        pltpu.VMEM((2, topn, dh, PAGE_SIZE), F32),
                        pltpu.SemaphoreType.DMA((2, 2))],
    )
    return pl.pallas_call(
        functools.partial(_sample_slc_kernel, topn=topn, past=past, t_pad=t_pad, t_real=t_real),
        grid_spec=grid_spec,
        out_shape=jax.ShapeDtypeStruct((n_seq, NSA_KV, t_real, SUBLANES, dh), F32),
        compiler_params=_cparams(2),
        name="sample_slc_attention",
    )(idx.reshape(-1), phys.reshape(-1), q_rows, k_new_t, v_new_t, pool_k, pool_v)


def _sample_win_kernel(q_ref, wk_ref, wv_ref, kn_ref, vn_ref, o_ref, *, past, t_pad):
    rows = t_pad * SUBLANES
    wb = wk_ref.shape[-1]
    groups = range(NSA_KV)
    q_pos = past + lax.broadcasted_iota(jnp.int32, (rows, 1), 0) // SUBLANES
    k_pos = past - wb + lax.broadcasted_iota(jnp.int32, (1, wb), 1)
    dist = q_pos - k_pos
    valid = (dist >= 0) & (dist < WINDOW) & (k_pos >= 0)
    n_pos = past + lax.broadcasted_iota(jnp.int32, (1, SUBLANES), 1)
    dist_n = q_pos - n_pos
    valid_n = (dist_n >= 0) & (dist_n < WINDOW)
    qs = [q_ref[0, g].reshape(rows, q_ref.shape[-1]).astype(BF16) for g in groups]
    s_old = [_dot(qs[g], wk_ref[0, g].astype(BF16)) for g in groups]
    s_new = [_dot(qs[g], kn_ref[0, g].astype(BF16)) for g in groups]
    p_old, p_new, inv_l = [], [], []
    for g in groups:
        so = jnp.where(valid, s_old[g], NEG_INF)
        sn = jnp.where(valid_n, s_new[g], NEG_INF)
        m = jnp.maximum(jnp.max(so, axis=-1, keepdims=True), jnp.max(sn, axis=-1, keepdims=True))
        po = jnp.exp(so - m)
        pn = jnp.exp(sn - m)
        inv_l.append(1.0 / (jnp.sum(po, axis=-1, keepdims=True) + jnp.sum(pn, axis=-1, keepdims=True)))
        p_old.append(po.astype(BF16))
        p_new.append(pn.astype(BF16))
    for g in groups:
        o = _nt_dot(p_old[g], wv_ref[0, g].astype(BF16)) + _nt_dot(p_new[g], vn_ref[0, g].astype(BF16))
        o_ref[0, g] = (o * inv_l[g]).reshape(t_pad, SUBLANES, o.shape[-1])


def sample_win_attention(q_rows, win_k_t, win_v_t, k_new_t, v_new_t, past):
    n_seq, _, t_pad, _, dh = q_rows.shape
    wb = win_k_t.shape[-1]
    q_spec = pl.BlockSpec((1, NSA_KV, t_pad, SUBLANES, dh), lambda s: (s, 0, 0, 0, 0))
    win_spec = pl.BlockSpec((1, NSA_KV, dh, wb), lambda s: (s, 0, 0, 0))
    new_spec = pl.BlockSpec((1, NSA_KV, dh, SUBLANES), lambda s: (s, 0, 0, 0))
    return pl.pallas_call(
        functools.partial(_sample_win_kernel, past=past, t_pad=t_pad),
        grid=(n_seq,),
        in_specs=[q_spec, win_spec, win_spec, new_spec, new_spec],
        out_specs=q_spec,
        out_shape=jax.ShapeDtypeStruct(q_rows.shape, F32),
        compiler_params=_cparams(1),
        name="sample_win_attention",
    )(q_rows, win_k_t, win_v_t, k_new_t, v_new_t)


def _to_slots(a):
    lead = a.shape[:-1]
    n = a.shape[-1] // NSA_DH
    a = a.reshape(*lead, n, NSA_DH)
    a = jnp.pad(a, [(0, 0)] * (a.ndim - 1) + [(0, SLOT - NSA_DH)])
    return a.reshape(*lead, n * SLOT)


def _odd_weights(w_in, w_out):
    d = w_in.shape[0]
    hq = NSA_HEADS * NSA_DH
    kvw = NSA_KV * NSA_DH
    wq = _to_slots(w_in[:, :hq] * np.float32(NSA_DH ** -0.5))
    wg = jnp.pad(w_in[:, hq + 6 * kvw:], ((0, 0), (0, LANES - 3 * NSA_HEADS)))
    w_q = jnp.concatenate([wq, wg], axis=1).astype(BF16)
    w_kvt = jnp.concatenate([w_in[:, hq:hq + 6 * kvw], wg], axis=1).T.astype(BF16)
    wo = jnp.pad(w_out.reshape(NSA_HEADS, NSA_DH, d), ((0, 0), (0, SLOT - NSA_DH), (0, 0)))
    wo = wo.reshape(NSA_HEADS * SLOT, d).astype(BF16)
    k = NSA_HEADS * SLOT
    e = np.zeros((LANES, 3 * k), np.float32)
    for c in range(3):
        for h in range(NSA_HEADS):
            e[c * NSA_HEADS + h, c * k + h * SLOT:c * k + (h + 1) * SLOT] = 1.0
    return w_q, w_kvt, wo, jnp.asarray(e, dtype=BF16)


def _group_rows(q_slots, n_seq, t_pad):
    q = q_slots.reshape(n_seq, t_pad, NSA_KV, NSA_GROUP, SLOT)[..., :NSA_DH]
    q = q.transpose(0, 2, 1, 3, 4)
    return jnp.pad(q, ((0, 0), (0, 0), (0, 0), (0, SUBLANES - NSA_GROUP), (0, 0)))


def _ungroup_rows(o, n_seq, t_pad):
    t = o.shape[2]
    o = o[:, :, :, :NSA_GROUP].transpose(0, 2, 1, 3, 4)
    o = jnp.pad(o, ((0, 0), (0, t_pad - t), (0, 0), (0, 0), (0, SLOT - NSA_DH)))
    return o.reshape(n_seq * t_pad, NSA_HEADS * SLOT)


def _feature_major(cache):
    return cache.transpose(0, 2, 3, 1)


def _token_major(a_t):
    return a_t.transpose(0, 3, 1, 2)


def _pad_rows(a, t_pad):
    return jnp.pad(a, ((0, 0), (0, t_pad - a.shape[1])) + ((0, 0),) * (a.ndim - 2))


def kernel(x_prompt, x_sample, state_sconv, state_ret, cache_cmp_k, cache_cmp_v, cache_slc_k, cache_slc_v,
           cache_win_k, cache_win_v, state_ffn_conv, page_table,
           w_in_even, sconv_w, sconv_b, ret_gn_g, w_out_even,
           w_in_odd, cmp_pe, cmp_w1, cmp_w2, w_out_odd,
           ln_mix_g, ln_mix_b, ffn_w_up, ffn_conv_w, ffn_conv_b, ffn_w_down, ln_ffn_g, ln_ffn_b):
    b_p, s_p, d_model = x_prompt.shape
    b_s, t_s, _ = x_sample.shape
    n_pages = page_table.shape[1]
    past = n_pages * PAGE_SIZE
    t_pad = SUBLANES
    assert t_s <= t_pad and t_s >= SCONV_W - 1 and t_s < CMP_STRIDE and past % PAGE_SIZE == 0
    assert s_p % RET_CHUNK == 0 and s_p % PAGE_SIZE == 0
    d_sconv = sconv_w.shape[-1]
    d_ff = ffn_conv_w.shape[-1]
    gd = NSA_KV * NSA_DH
    depth = ln_mix_g.shape[0]

    xp = x_prompt.reshape(b_p * s_p, d_model)
    xs = _pad_rows(x_sample, t_pad).reshape(b_s * t_pad, d_model)
    outs = {k: [] for k in ("sconv_p", "sconv_s", "ret_p", "ret_s", "cmp_k_p", "cmp_v_p", "slc_k_p", "slc_v_p",
                            "cmp_k_s", "cmp_v_s", "slc_k_s", "slc_v_s", "win_k_p", "win_v_p", "win_k_s",
                            "win_v_s", "ffn_p", "ffn_s")}

    for layer in range(depth):
        if layer % 2 == 0:
            e = layer // 2
            w_in = w_in_even[e].astype(BF16)
            w_out = w_out_even[e].astype(BF16)
            n_in = w_in.shape[1]
            yp, hc, st = even_mixer(xp, w_in, jnp.zeros((b_p, SCONV_W - 1, d_sconv), F32),
                                    jnp.zeros((b_p,) + state_ret.shape[2:], F32), jnp.arange(s_p),
                                    RET_CHUNK, RET_CHUNK, sconv_w[e], sconv_b[e], ret_gn_g[e],
                                    chunks=MIXER_CHUNKS if s_p % (MIXER_CHUNKS * RET_CHUNK) == 0 else 1)
            outs["sconv_p"].append(hc)
            outs["ret_p"].append(st)
            xp = matmul_residual_ln(yp, w_out, xp, ln_mix_g[layer], ln_mix_b[layer])
            (zs,) = matmul_split(xs, w_in, [n_in], [F32])
            ys, hc, st = even_mixer(zs, None, state_sconv[e], state_ret[e], past + jnp.arange(t_pad),
                                    t_pad, t_s, sconv_w[e], sconv_b[e], ret_gn_g[e])
            outs["sconv_s"].append(hc)
            outs["ret_s"].append(st)
            xs = matmul_residual_ln(ys, w_out, xs, ln_mix_g[layer], ln_mix_b[layer])
        else:
            o = layer // 2
            w_q, w_kvt, w_out, e_gate = _odd_weights(w_in_odd[o], w_out_odd[o])
            pe, w1, w2 = cmp_pe[o], cmp_w1[o], cmp_w2[o]
            qp, _, gtp, kc, vc, ks, vs, kw, vw = nsa_projection(xp, b_p, w_q, w_kvt, BF16)
            as_cache = lambda a_t: _token_major(a_t.reshape(b_p, NSA_KV, NSA_DH, -1))
            keep = min(WINDOW, s_p)
            outs["cmp_k_p"].append(as_cache(kc))
            outs["cmp_v_p"].append(as_cache(vc))
            outs["slc_k_p"].append(as_cache(ks))
            outs["slc_v_p"].append(as_cache(vs))
            outs["win_k_p"].append(as_cache(kw[:, :, s_p - keep:]))
            outs["win_v_p"].append(as_cache(vw[:, :, s_p - keep:]))
            kcc = compress(kc.reshape(b_p, NSA_KV, NSA_DH, s_p), None, pe[0], w1[0], w2[0])
            vcc = compress(vc.reshape(b_p, NSA_KV, NSA_DH, s_p), None, pe[1], w1[1], w2[1])
            n_cmp = s_p // CMP_STRIDE - CMP_LEN // CMP_STRIDE + 1
            n_slc = s_p // SLC_BLOCK
            oc, sel = cmp_attention_select_prompt(qp, kcc, vcc, gtp, b_p, n_cmp, n_slc, SELECT_ROWS)
            osw = prompt_slc_win_attention(qp, sel, ks, vs, kw, vw, gtp, b_p, ATTN_ROWS)
            xp = sum2_matmul_residual_ln(oc, osw, w_out, xp, ln_mix_g[layer], ln_mix_b[layer])
            qs, gs, _, *kv_s = nsa_projection(xs, 1, w_q, w_kvt, F32)
            kc, vc, ks, vs, kw, vw = [a.reshape(NSA_KV, NSA_DH, b_s, t_pad).transpose(2, 0, 1, 3) for a in kv_s]
            new_rows = lambda a_t: _token_major(a_t[..., :t_s])
            outs["cmp_k_s"].append(new_rows(kc))
            outs["cmp_v_s"].append(new_rows(vc))
            outs["slc_k_s"].append(new_rows(ks))
            outs["slc_v_s"].append(new_rows(vs))
            win_k = _feature_major(cache_win_k[o])
            win_v = _feature_major(cache_win_v[o])
            wb = win_k.shape[-1]
            keep = min(WINDOW, wb + t_s)
            outs["win_k_s"].append(_token_major(jnp.concatenate([win_k, kw[..., :t_s]], axis=-1)[..., -keep:]))
            outs["win_v_s"].append(_token_major(jnp.concatenate([win_v, vw[..., :t_s]], axis=-1)[..., -keep:]))
            kcc = compress(_feature_major(cache_cmp_k[o]), page_table, pe[0], w1[0], w2[0])
            vcc = compress(_feature_major(cache_cmp_v[o]), page_table, pe[1], w1[1], w2[1])
            n_cmp = (past + t_s) // CMP_STRIDE - CMP_LEN // CMP_STRIDE + 1
            n_slc = -(-(past + t_s) // SLC_BLOCK)
            oc, idx = cmp_attention_select(qs, kcc, vcc, b_s, n_cmp, n_slc, past)
            q_rows = _group_rows(qs, b_s, t_pad)
            osl = sample_slc_attention(q_rows, idx, page_table, _feature_major(cache_slc_k[o]),
                                       _feature_major(cache_slc_v[o]), ks, vs, t_s, past)
            ow = sample_win_attention(q_rows, win_k, win_v, kw, vw, past)
            xs = nsa_merge_residual_ln(oc, _ungroup_rows(osl, b_s, t_pad), _ungroup_rows(ow, b_s, t_pad), gs,
                                       e_gate, w_out, xs, ln_mix_g[layer], ln_mix_b[layer])
        w_up = ffn_w_up[layer].astype(BF16)
        w_down = ffn_w_down[layer].astype(BF16)
        hp, hist_p = ffn_up_sequences(xp, b_p, jnp.zeros((b_p, FFN_W - 1, d_ff), F32), w_up,
                                      ffn_conv_w[layer], ffn_conv_b[layer])
        outs["ffn_p"].append(hist_p)
        xp = matmul_residual_ln(hp, w_down, xp, ln_ffn_g[layer], ln_ffn_b[layer])
        hs, a_s = ffn_up_short(xs, state_ffn_conv[layer], w_up, ffn_conv_w[layer], ffn_conv_b[layer])
        outs["ffn_s"].append(a_s.reshape(b_s, t_pad, d_ff)[:, t_s - (FFN_W - 1):t_s])
        xs = matmul_residual_ln(hs, w_down, xs, ln_ffn_g[layer], ln_ffn_b[layer])

    st = jnp.stack
    y_p = xp.reshape(b_p, s_p, d_model)
    y_s = xs.reshape(b_s, t_pad, d_model)[:, :t_s]
    order = ("sconv_p", "sconv_s", "ret_p", "ret_s", "cmp_k_p", "cmp_v_p", "slc_k_p", "slc_v_p",
             "cmp_k_s", "cmp_v_s", "slc_k_s", "slc_v_s", "win_k_p", "win_v_p", "win_k_s", "win_v_s",
             "ffn_p", "ffn_s")
    return (y_p, y_s) + tuple(st(outs[k]) for k in order)
```

```python
import functools

import numpy as np
import jax
import jax.numpy as jnp
from jax import lax
from jax.experimental import pallas as pl
from jax.experimental.pallas import tpu as pltpu

F32 = jnp.float32
BF16 = jnp.bfloat16

SUBLANES = 8
LANES = 128
VMEM_LIMIT_BYTES = 56 * 1024 * 1024
MATMUL_ROWS = 512
MERGE_ROWS = 256
ATTN_ROWS = 256
SELECT_ROWS = 1024
COMPRESS_PAGES = 64
MIXER_CHUNKS = 4

DEPTH = 2
SCONV_W = 3
RET_HEADS = 4
RET_CHUNK = 128
ROPE_BASE = 10000.0
NSA_HEADS = 16
NSA_KV = 4
NSA_GROUP = NSA_HEADS // NSA_KV
NSA_DH = 64
CMP_LEN = 32
CMP_STRIDE = 16
SLC_BLOCK = 64
SLC_TOPN = 16
WINDOW = 512
PAGE_SIZE = 128
FFN_W = 3
ALPHA = (2.0 * DEPTH) ** 0.25
LN_EPS = 1e-5
NEG_INF = -1e30
REMOVED = -3e38
FORCE_BONUS = 1e4
SLOT = 2 * NSA_DH


def _cparams(n_grid):
    return pltpu.CompilerParams(dimension_semantics=("arbitrary",) * n_grid,
                                vmem_limit_bytes=VMEM_LIMIT_BYTES)


def _row_tile(m, want):
    t = min(m, want)
    assert m % t == 0, (m, t)
    return t


def _nt_dot(a, b):
    return lax.dot_general(a, b, (((1,), (1,)), ((), ())), preferred_element_type=F32)


def _tn_dot(a, b):
    return lax.dot_general(a, b, (((0,), (0,)), ((), ())), preferred_element_type=F32)


def _dot(a, b):
    return jnp.dot(a, b, preferred_element_type=F32)


def _gelu(x):
    return 0.5 * x * (1.0 + jnp.tanh(np.float32(np.sqrt(2.0 / np.pi)) * (x + 0.044715 * (x * x * x))))


def _layer_norm_rows(r, g, b):
    mu = jnp.mean(r, axis=-1, keepdims=True)
    d = r - mu
    var = jnp.mean(d * d, axis=-1, keepdims=True)
    return d * lax.rsqrt(var + LN_EPS) * g + b


def _mm_split_kernel(x_ref, w_ref, *o_refs, cuts):
    acc = _dot(x_ref[...].astype(BF16), w_ref[...])
    for o_ref, (lo, hi) in zip(o_refs, cuts):
        o_ref[...] = acc[:, lo:hi].astype(o_ref.dtype)


def matmul_split(x, w_bf16, widths, dtypes, tm=MATMUL_ROWS):
    m, k = x.shape
    n = w_bf16.shape[1]
    assert sum(widths) == n and all(wd % LANES == 0 for wd in widths)
    tm = _row_tile(m, tm)
    cuts, lo = [], 0
    for wd in widths:
        cuts.append((lo, lo + wd))
        lo += wd
    return pl.pallas_call(
        functools.partial(_mm_split_kernel, cuts=tuple(cuts)),
        grid=(m // tm,),
        in_specs=[pl.BlockSpec((tm, k), lambda i: (i, 0)),
                  pl.BlockSpec((k, n), lambda i: (0, 0))],
        out_specs=[pl.BlockSpec((tm, wd), lambda i: (i, 0)) for wd in widths],
        out_shape=[jax.ShapeDtypeStruct((m, wd), dt) for wd, dt in zip(widths, dtypes)],
        compiler_params=_cparams(1),
        name="matmul_split",
    )(x, w_bf16)


def _nsa_proj_kernel(x_ref, wq_ref, wt_ref, q_ref, g_ref, gt_ref, *kv_refs, nq, gd):
    xb = x_ref[...].astype(BF16)
    acc = _dot(xb, wq_ref[...])
    q_ref[...] = acc[:, :nq].astype(q_ref.dtype)
    g_ref[...] = acc[:, nq:]
    acc_t = _nt_dot(wt_ref[...], xb)
    for i, r in enumerate(kv_refs):
        r[0] = acc_t[i * gd:(i + 1) * gd, :]
    gt_ref[0] = acc_t[len(kv_refs) * gd:, :]


def nsa_projection(x, n_seq, wq_bf16, wt_bf16, q_dtype, n_kv=6, tm=MATMUL_ROWS):
    m, d = x.shape
    seq = m // n_seq
    tm = _row_tile(seq, tm)
    nt = seq // tm
    nq = wq_bf16.shape[1] - LANES
    gd = (wt_bf16.shape[0] - LANES) // n_kv
    fm = lambda rows: pl.BlockSpec((1, rows, tm), lambda s, i: (s, 0, i))
    return pl.pallas_call(
        functools.partial(_nsa_proj_kernel, nq=nq, gd=gd),
        grid=(n_seq, nt),
        in_specs=[pl.BlockSpec((tm, d), lambda s, i: (s * nt + i, 0)),
                  pl.BlockSpec((d, nq + LANES), lambda s, i: (0, 0)),
                  pl.BlockSpec((n_kv * gd + LANES, d), lambda s, i: (0, 0))],
        out_specs=[pl.BlockSpec((tm, nq), lambda s, i: (s * nt + i, 0)),
                   pl.BlockSpec((tm, LANES), lambda s, i: (s * nt + i, 0)),
                   fm(LANES)] + [fm(gd) for _ in range(n_kv)],
        out_shape=[jax.ShapeDtypeStruct((m, nq), q_dtype), jax.ShapeDtypeStruct((m, LANES), F32),
                   jax.ShapeDtypeStruct((n_seq, LANES, seq), F32)] + [
            jax.ShapeDtypeStruct((n_seq, gd, seq), F32) for _ in range(n_kv)],
        compiler_params=_cparams(2),
        name="nsa_projection",
    )(x, wq_bf16, wt_bf16)


def _mm_res_ln_kernel(a_ref, w_ref, x_ref, g_ref, b_ref, o_ref):
    y = _dot(a_ref[...].astype(BF16), w_ref[...])
    o_ref[...] = _layer_norm_rows(ALPHA * x_ref[...] + y, g_ref[...], b_ref[...])


def matmul_residual_ln(a, w_bf16, x, g, b, tm=MATMUL_ROWS):
    m, k = a.shape
    d = w_bf16.shape[1]
    tm = _row_tile(m, tm)
    return pl.pallas_call(
        _mm_res_ln_kernel,
        grid=(m // tm,),
        in_specs=[pl.BlockSpec((tm, k), lambda i: (i, 0)),
                  pl.BlockSpec((k, d), lambda i: (0, 0)),
                  pl.BlockSpec((tm, d), lambda i: (i, 0)),
                  pl.BlockSpec((1, d), lambda i: (0, 0)),
                  pl.BlockSpec((1, d), lambda i: (0, 0))],
        out_specs=pl.BlockSpec((tm, d), lambda i: (i, 0)),
        out_shape=jax.ShapeDtypeStruct((m, d), F32),
        compiler_params=_cparams(1),
        name="matmul_residual_ln",
    )(a, w_bf16, x, g.reshape(1, d), b.reshape(1, d))


def _sum2_mm_res_ln_kernel(a_ref, b2_ref, w_ref, x_ref, g_ref, b_ref, o_ref):
    y = _dot((a_ref[...].astype(F32) + b2_ref[...].astype(F32)).astype(BF16), w_ref[...])
    o_ref[...] = _layer_norm_rows(ALPHA * x_ref[...] + y, g_ref[...], b_ref[...])


def sum2_matmul_residual_ln(a, b2, w_bf16, x, g, b, tm=MATMUL_ROWS):
    m, k = a.shape
    d = w_bf16.shape[1]
    tm = _row_tile(m, tm)
    row = lambda i: (i, 0)
    fixed = lambda i: (0, 0)
    return pl.pallas_call(
        _sum2_mm_res_ln_kernel,
        grid=(m // tm,),
        in_specs=[pl.BlockSpec((tm, k), row), pl.BlockSpec((tm, k), row),
                  pl.BlockSpec((k, d), fixed),
                  pl.BlockSpec((tm, d), row),
                  pl.BlockSpec((1, d), fixed), pl.BlockSpec((1, d), fixed)],
        out_specs=pl.BlockSpec((tm, d), row),
        out_shape=jax.ShapeDtypeStruct((m, d), F32),
        compiler_params=_cparams(1),
        name="sum2_matmul_residual_ln",
    )(a, b2, w_bf16, x, g.reshape(1, d), b.reshape(1, d))


def _expand_gates(gates_raw, e_ref):
    sig = jax.nn.sigmoid(gates_raw)
    hi = sig.astype(BF16)
    lo = (sig - hi.astype(F32)).astype(BF16)
    e = e_ref[...]
    return _dot(hi, e) + _dot(lo, e)


def _nsa_merge_ln_kernel(oc_ref, os_ref, ow_ref, gt_ref, e_ref, w_ref, x_ref, g_ref, b_ref, o_ref, *, k):
    gx = _expand_gates(gt_ref[...], e_ref)
    o = gx[:, 0:k] * oc_ref[...] + gx[:, k:2 * k] * os_ref[...] + gx[:, 2 * k:3 * k] * ow_ref[...]
    y = _dot(o.astype(BF16), w_ref[...])
    o_ref[...] = _layer_norm_rows(ALPHA * x_ref[...] + y, g_ref[...], b_ref[...])


def nsa_merge_residual_ln(oc, osl, ow, gates, e_bf16, w_bf16, x, g, b, tm=MERGE_ROWS):
    m, k = oc.shape
    d = w_bf16.shape[1]
    tm = _row_tile(m, tm)
    row = lambda i: (i, 0)
    fixed = lambda i: (0, 0)
    return pl.pallas_call(
        functools.partial(_nsa_merge_ln_kernel, k=k),
        grid=(m // tm,),
        in_specs=[pl.BlockSpec((tm, k), row), pl.BlockSpec((tm, k), row), pl.BlockSpec((tm, k), row),
                  pl.BlockSpec((tm, LANES), row),
                  pl.BlockSpec((LANES, 3 * k), fixed),
                  pl.BlockSpec((k, d), fixed),
                  pl.BlockSpec((tm, d), row),
                  pl.BlockSpec((1, d), fixed), pl.BlockSpec((1, d), fixed)],
        out_specs=pl.BlockSpec((tm, d), row),
        out_shape=jax.ShapeDtypeStruct((m, d), F32),
        compiler_params=_cparams(1),
        name="nsa_merge_residual_ln",
    )(oc, osl, ow, gates, e_bf16, w_bf16, x, g.reshape(1, d), b.reshape(1, d))


def _even_mixer_kernel(*refs, rows, valid, dconv, dk, chunks, project):
    if project:
        x_ref, w_ref, *refs = refs
    else:
        z_ref, *refs = refs
    (hist_ref, st_ref, cos_ref, sin_ref, decay_ref, qdec_ref, kdec_ref, sdec_ref, cw_ref, cb_ref, gn_ref,
     y_ref, hist_out_ref, st_out_ref, carry, state) = refs
    r0 = valid - 2 - (rows - SUBLANES)

    @pl.when(pl.program_id(1) == 0)
    def _():
        carry[r0:r0 + 2, :] = hist_ref[0]
        state[...] = st_ref[0]

    z_all = _dot(x_ref[...].astype(BF16), w_ref[...]) if project else z_ref[...]
    d = dconv
    scale = np.float32(dk ** -0.5)
    heads = range(RET_HEADS)
    col = lambda part, hh: slice(part * d + hh * dk, part * d + (hh + 1) * dk)
    row = lax.broadcasted_iota(jnp.int32, (rows, d), 0)
    for ci in range(chunks):
        at = slice(ci * rows, (ci + 1) * rows)
        z = z_all[at]
        ch = z[:, 2 * d:3 * d] * z[:, 0:d]
        h0 = carry[r0:r0 + 1, :]
        h1 = carry[r0 + 1:r0 + 2, :]
        m1 = jnp.where(row == 0, h1, pltpu.roll(ch, 1, 0))
        m2 = jnp.where(row == 0, h0, jnp.where(row == 1, h1, pltpu.roll(ch, 2, 0)))
        u = ((cb_ref[...] + m2 * cw_ref[0:1, :]) + m1 * cw_ref[1:2, :]) + ch * cw_ref[2:3, :]
        y_ref[at, 0:d] = z[:, d:2 * d] * u
        carry[...] = ch[rows - SUBLANES:rows, :]

        cosf = cos_ref[at, :]
        sinf = sin_ref[at, :]
        qs, ks, vbs = [], [], []
        for hh in heads:
            q = z[:, col(3, hh)]
            k = z[:, col(4, hh)]
            qs.append(((q * cosf + pltpu.roll(q, dk // 2, 1) * sinf) * scale).astype(BF16))
            ks.append(k * cosf + pltpu.roll(k, dk // 2, 1) * sinf)
            vbs.append(z[:, col(5, hh)].astype(BF16))
        s_old = [state[hh] for hh in heads]
        scores = [_nt_dot(qs[hh], ks[hh].astype(BF16)) * decay_ref[hh] for hh in heads]
        cross = [_dot(qs[hh], s_old[hh].astype(BF16)) * qdec_ref[hh] for hh in heads]
        intra = [_dot(scores[hh].astype(BF16), vbs[hh]) for hh in heads]
        for hh in heads:
            kd = (ks[hh] * kdec_ref[hh]).astype(BF16)
            state[hh] = s_old[hh] * sdec_ref[hh] + _tn_dot(kd, vbs[hh])
        for hh in heads:
            o = intra[hh] + cross[hh]
            mu = jnp.mean(o, axis=-1, keepdims=True)
            dv = o - mu
            var = jnp.mean(dv * dv, axis=-1, keepdims=True)
            on = dv * lax.rsqrt(var + LN_EPS) * gn_ref[:, hh * dk:(hh + 1) * dk]
            gsw = z[:, col(6, hh)]
            y_ref[at, d + hh * dk:d + (hh + 1) * dk] = (gsw * jax.nn.sigmoid(gsw)) * on
    hist_out_ref[0] = carry[r0:r0 + 2, :]
    st_out_ref[0] = state[...]


def _retention_tables(rows, valid, dk):
    log_gamma = jnp.log1p(-jnp.exp2(-5.0 - jnp.arange(RET_HEADS, dtype=F32)))
    n = jnp.arange(rows, dtype=F32)
    diff = n[:, None] - n[None, :]
    lg = log_gamma[:, None, None]
    decay = jnp.where(diff >= 0, jnp.exp(lg * jnp.maximum(diff, 0.0)), 0.0)
    q_dec = jnp.exp((n[None, :] + 1.0) * log_gamma[:, None])
    k_dec = jnp.where(n[None, :] < valid, jnp.exp((valid - 1.0 - n[None, :]) * log_gamma[:, None]), 0.0)
    s_dec = jnp.exp(valid * log_gamma)
    bc = lambda a: jnp.broadcast_to(a[:, :, None], (RET_HEADS, rows, dk))
    return decay, bc(q_dec), bc(k_dec), jnp.broadcast_to(s_dec[:, None, None], (RET_HEADS, 1, dk))


def _rope_tables(pos, dk):
    half = dk // 2
    inv = ROPE_BASE ** (-jnp.arange(half, dtype=F32) / half)
    ang = pos.astype(F32)[:, None] * inv
    cos, sin = jnp.cos(ang), jnp.sin(ang)
    return jnp.concatenate([cos, cos], axis=-1), jnp.concatenate([-sin, sin], axis=-1)


def even_mixer(x, w_in_bf16, hist, st, pos, rows, valid, conv_w, conv_b, gn_g, chunks=1):
    n_seq, _, dconv = hist.shape
    dk = st.shape[-1]
    project = w_in_bf16 is not None
    tile = chunks * rows
    n_steps = x.shape[0] // (n_seq * tile)
    cosf, sinf = _rope_tables(pos, dk)
    decay, q_dec, k_dec, s_dec = _retention_tables(rows, valid, dk)
    fixed3 = lambda s, c: (0, 0, 0)
    fixed2 = lambda s, c: (0, 0)
    lead_specs = [pl.BlockSpec((tile, x.shape[1]), lambda s, c: (s * n_steps + c, 0))]
    lead_args = [x]
    if project:
        lead_specs.append(pl.BlockSpec(w_in_bf16.shape, fixed2))
        lead_args.append(w_in_bf16)
    n_chunks = n_steps
    rows_out = x.shape[0]
    return pl.pallas_call(
        functools.partial(_even_mixer_kernel, rows=rows, valid=valid, dconv=dconv, dk=dk, chunks=chunks,
                          project=project),
        grid=(n_seq, n_steps),
        in_specs=lead_specs + [
                  pl.BlockSpec((1, 2, dconv), lambda s, c: (s, 0, 0)),
                  pl.BlockSpec((1, RET_HEADS, dk, dk), lambda s, c: (s, 0, 0, 0)),
                  pl.BlockSpec((tile, dk), lambda s, c: (c, 0)),
                  pl.BlockSpec((tile, dk), lambda s, c: (c, 0)),
                  pl.BlockSpec((RET_HEADS, rows, rows), fixed3),
                  pl.BlockSpec((RET_HEADS, rows, dk), fixed3),
                  pl.BlockSpec((RET_HEADS, rows, dk), fixed3),
                  pl.BlockSpec((RET_HEADS, 1, dk), fixed3),
                  pl.BlockSpec((SCONV_W, dconv), fixed2),
                  pl.BlockSpec((1, dconv), fixed2),
                  pl.BlockSpec((1, RET_HEADS * dk), fixed2)],
        out_specs=[pl.BlockSpec((tile, 2 * dconv), lambda s, c: (s * n_chunks + c, 0)),
                   pl.BlockSpec((1, 2, dconv), lambda s, c: (s, 0, 0)),
                   pl.BlockSpec((1, RET_HEADS, dk, dk), lambda s, c: (s, 0, 0, 0))],
        out_shape=[jax.ShapeDtypeStruct((rows_out, 2 * dconv), F32),
                   jax.ShapeDtypeStruct((n_seq, 2, dconv), F32),
                   jax.ShapeDtypeStruct((n_seq, RET_HEADS, dk, dk), F32)],
        scratch_shapes=[pltpu.VMEM((SUBLANES, dconv), F32), pltpu.VMEM((RET_HEADS, dk, dk), F32)],
        compiler_params=_cparams(2),
        name="even_mixer",
    )(*lead_args, hist, st, cosf, sinf, decay, q_dec, k_dec, s_dec, conv_w, conv_b.reshape(1, dconv),
      gn_g.reshape(1, RET_HEADS * dk))


def _conv_gate(a, gate, m1, m2, cw_ref, cb_ref):
    conv = ((cb_ref[...] + m2 * cw_ref[0:1, :]) + m1 * cw_ref[1:2, :]) + a * cw_ref[2:3, :]
    return _gelu(conv) * gate


def _ffn_up_seq_kernel(x_ref, wa_ref, wg_ref, h_ref, cw_ref, cb_ref, o_ref, hist_out_ref, carry, *, tm):
    @pl.when(pl.program_id(2) == 0)
    def _():
        carry[SUBLANES - 2:SUBLANES, :] = h_ref[0]

    xb = x_ref[...].astype(BF16)
    a = _dot(xb, wa_ref[...])
    gate = _dot(xb, wg_ref[...])
    row = lax.broadcasted_iota(jnp.int32, a.shape, 0)
    h0 = carry[SUBLANES - 2:SUBLANES - 1, :]
    h1 = carry[SUBLANES - 1:SUBLANES, :]
    m1 = jnp.where(row == 0, h1, pltpu.roll(a, 1, 0))
    m2 = jnp.where(row == 0, h0, jnp.where(row == 1, h1, pltpu.roll(a, 2, 0)))
    o_ref[...] = _conv_gate(a, gate, m1, m2, cw_ref, cb_ref).astype(o_ref.dtype)
    carry[...] = a[tm - SUBLANES:tm, :]
    hist_out_ref[0] = carry[SUBLANES - 2:SUBLANES, :]


def ffn_up_sequences(x, n_seq, hist, w_up_bf16, conv_w, conv_b, tm=MATMUL_ROWS, n_col=1):
    m, k = x.shape
    dff = conv_w.shape[1]
    seq = m // n_seq
    tm = _row_tile(seq, tm)
    tps = seq // tm
    tn = dff // n_col
    assert tn % LANES == 0
    return pl.pallas_call(
        functools.partial(_ffn_up_seq_kernel, tm=tm),
        grid=(n_col, n_seq, tps),
        in_specs=[pl.BlockSpec((tm, k), lambda j, s, i: (s * tps + i, 0)),
                  pl.BlockSpec((k, tn), lambda j, s, i: (0, j)),
                  pl.BlockSpec((k, tn), lambda j, s, i: (0, j + n_col)),
                  pl.BlockSpec((1, 2, tn), lambda j, s, i: (s, 0, j)),
                  pl.BlockSpec((FFN_W, tn), lambda j, s, i: (0, j)),
                  pl.BlockSpec((1, tn), lambda j, s, i: (0, j))],
        out_specs=[pl.BlockSpec((tm, tn), lambda j, s, i: (s * tps + i, j)),
                   pl.BlockSpec((1, 2, tn), lambda j, s, i: (s, 0, j))],
        out_shape=[jax.ShapeDtypeStruct((m, dff), BF16),
                   jax.ShapeDtypeStruct((n_seq, 2, dff), F32)],
        scratch_shapes=[pltpu.VMEM((SUBLANES, tn), F32)],
        compiler_params=_cparams(3),
        name="ffn_up_sequences",
    )(x, w_up_bf16, w_up_bf16, hist, conv_w, conv_b.reshape(1, dff))


def _ffn_up_short_kernel(x_ref, wa_ref, wg_ref, h1_ref, h2_ref, cw_ref, cb_ref, o_ref, a_ref):
    xb = x_ref[...].astype(BF16)
    a = _dot(xb, wa_ref[...])
    gate = _dot(xb, wg_ref[...])
    t = lax.broadcasted_iota(jnp.int32, a.shape, 0) % SUBLANES
    m1 = jnp.where(t == 0, h1_ref[...], pltpu.roll(a, 1, 0))
    m2 = jnp.where(t < 2, h2_ref[...], pltpu.roll(a, 2, 0))
    o_ref[...] = _conv_gate(a, gate, m1, m2, cw_ref, cb_ref).astype(o_ref.dtype)
    a_ref[...] = a


def ffn_up_short(x, hist, w_up_bf16, conv_w, conv_b, n_col=2):
    m, k = x.shape
    dff = conv_w.shape[1]
    n_seq = m // SUBLANES
    tn = dff // n_col
    zeros = jnp.zeros((n_seq, SUBLANES, dff), F32)
    h1 = zeros.at[:, 0].set(hist[:, 1]).reshape(m, dff)
    h2 = zeros.at[:, 0].set(hist[:, 0]).at[:, 1].set(hist[:, 1]).reshape(m, dff)
    col = lambda j: (0, j)
    return pl.pallas_call(
        _ffn_up_short_kernel,
        grid=(n_col,),
        in_specs=[pl.BlockSpec((m, k), lambda j: (0, 0)),
                  pl.BlockSpec((k, tn), col),
                  pl.BlockSpec((k, tn), lambda j: (0, j + n_col)),
                  pl.BlockSpec((m, tn), col), pl.BlockSpec((m, tn), col),
                  pl.BlockSpec((FFN_W, tn), col), pl.BlockSpec((1, tn), col)],
        out_specs=[pl.BlockSpec((m, tn), col), pl.BlockSpec((m, tn), col)],
        out_shape=[jax.ShapeDtypeStruct((m, dff), F32), jax.ShapeDtypeStruct((m, dff), F32)],
        compiler_params=_cparams(1),
        name="ffn_up_short",
    )(x, w_up_bf16, w_up_bf16, h1, h2, conv_w, conv_b.reshape(1, dff))


def _compress_kernel(pt_ref, *refs, pages):
    page_refs = refs[:pages + 1]
    w_ref, pecol_ref, w1_ref, w2_ref, o_ref = refs[pages + 1:pages + 6]
    rows_refs = refs[pages + 6:]
    parts = len(rows_refs)
    ppp = (pages + 1) // parts
    cpp = PAGE_SIZE // CMP_STRIDE
    n = (pages + 1) * cpp
    n_p = ppp * cpp
    hidden = w1_ref.shape[1]
    gpr = LANES // NSA_DH
    pieces = NSA_KV // gpr
    for i, r in enumerate(page_refs):
        for pc in range(pieces):
            tile = r[0, pc * gpr:(pc + 1) * gpr].reshape(LANES, PAGE_SIZE)
            rows_refs[i // ppp][pc, (i % ppp) * PAGE_SIZE:(i % ppp + 1) * PAGE_SIZE, :] = tile.T
    pe_term = jnp.sum(pecol_ref[...] * w1_ref[...], axis=0, keepdims=True)
    accs = []
    for rows_ref in rows_refs:
        lhs = jnp.concatenate([rows_ref[pc].reshape(n_p, CMP_STRIDE * LANES) for pc in range(pieces)], axis=0)
        accs.append(_dot(lhs.astype(BF16), w_ref[...]))
    for pc in range(pieces):
        for gl in range(gpr):
            a = jnp.concatenate([acc[pc * n_p:(pc + 1) * n_p, gl * 2 * hidden:(gl + 1) * 2 * hidden]
                                 for acc in accs], axis=0)
            nxt = pltpu.roll(a, n - 1, 0)
            pre = pe_term + a[:, 0:hidden]
            pre = pre + nxt[:, hidden:2 * hidden]
            o_ref[0, pc * gpr + gl] = _dot(_gelu(pre[0:pages * cpp]).astype(BF16), w2_ref[...])


def compress(rows_t, page_table, pe, w1, w2, pages=COMPRESS_PAGES):
    pooled = page_table is not None
    if pooled:
        n_seq, n_pages = page_table.shape
    else:
        n_seq, n_pages = rows_t.shape[0], rows_t.shape[3] // PAGE_SIZE
        page_table = jnp.zeros((1, 1), jnp.int32)
    pages = min(pages, n_pages)
    assert n_pages % pages == 0
    parts = next(p for p in (3, 5, 1) if (pages + 1) % p == 0)
    hidden = w1.shape[1]
    cpp = PAGE_SIZE // CMP_STRIDE
    r = CMP_LEN // CMP_STRIDE
    gpr = LANES // NSA_DH
    assert r == 2 and gpr == 2
    w1p = w1.reshape(r, CMP_STRIDE, NSA_DH, hidden)
    w16 = jnp.concatenate([w1p[0], w1p[1]], axis=-1)
    zero = jnp.zeros_like(w16)
    per_tok = jnp.concatenate([jnp.concatenate([w16, zero], axis=-1),
                               jnp.concatenate([zero, w16], axis=-1)], axis=1)
    w_chunk = per_tok.reshape(CMP_STRIDE * LANES, gpr * 2 * hidden).astype(BF16)
    w2p = jnp.pad(w2, ((0, 0), (0, SLOT - NSA_DH))).astype(BF16)
    pecol = pe.reshape(CMP_LEN * NSA_DH, 1)

    def page_map(i):
        if pooled:
            return lambda s, j, pt: (pt[s, jnp.minimum(j * pages + i, n_pages - 1)], 0, 0, 0)
        return lambda s, j, pt: (s, 0, 0, jnp.minimum(j * pages + i, n_pages - 1))

    fixed2 = lambda s, j, pt: (0, 0)
    grid_spec = pltpu.PrefetchScalarGridSpec(
        num_scalar_prefetch=1,
        grid=(n_seq, n_pages // pages),
        in_specs=[pl.BlockSpec((1, NSA_KV, NSA_DH, PAGE_SIZE), page_map(i)) for i in range(pages + 1)] + [
            pl.BlockSpec((CMP_STRIDE * LANES, gpr * 2 * hidden), fixed2),
            pl.BlockSpec((CMP_LEN * NSA_DH, 1), fixed2),
            pl.BlockSpec((CMP_LEN * NSA_DH, hidden), fixed2),
            pl.BlockSpec((hidden, SLOT), fixed2)],
        out_specs=pl.BlockSpec((1, NSA_KV, pages * cpp, SLOT), lambda s, j, pt: (s, 0, j, 0)),
        scratch_shapes=[pltpu.VMEM((NSA_KV // gpr, (pages + 1) // parts * PAGE_SIZE, LANES), F32)
                        for _ in range(parts)],
    )
    return pl.pallas_call(
        functools.partial(_compress_kernel, pages=pages),
        grid_spec=grid_spec,
        out_shape=jax.ShapeDtypeStruct((n_seq, NSA_KV, n_pages * cpp, SLOT), F32),
        compiler_params=_cparams(2),
        name="compress",
    )(page_table, *([rows_t] * (pages + 1)), w_chunk, pecol, w1, w2p)


def _cmp_select_kernel(q_ref, kc_ref, vc_ref, ov_ref, o_ref, idx_ref, *, tq, n_cmp, n_slc, pos0):
    ncp = kc_ref.shape[2]
    nsp = ov_ref.shape[0]
    hrows = NSA_GROUP * tq
    q_pos = pos0 + (lax.broadcasted_iota(jnp.int32, (hrows, ncp), 0) & (tq - 1))
    blk_i = lax.broadcasted_iota(jnp.int32, (hrows, ncp), 1)
    valid = (blk_i * CMP_STRIDE + (CMP_LEN - 1) <= q_pos) & (blk_i < n_cmp)
    ov = ov_ref[...]
    groups = range(NSA_KV)
    heads = [range(g * NSA_GROUP, (g + 1) * NSA_GROUP) for g in groups]
    scores = [_nt_dot(jnp.concatenate([q_ref[:, h * SLOT:(h + 1) * SLOT] for h in heads[g]], axis=0).astype(BF16),
                      kc_ref[0, g].astype(BF16)) for g in groups]
    probs = []
    for g in groups:
        s = jnp.where(valid, scores[g], NEG_INF)
        m = jnp.max(s, axis=-1, keepdims=True)
        e = jnp.where(valid, jnp.exp(s - m), 0.0)
        den = jnp.sum(e, axis=-1, keepdims=True)
        probs.append(e * (1.0 / jnp.where(den > 0.0, den, 1.0)))
    outs = [_dot(probs[g].astype(BF16), vc_ref[0, g].astype(BF16)) for g in groups]
    imps = []
    for g in groups:
        p_sum = jnp.zeros((tq, ncp), F32)
        for j, h in enumerate(heads[g]):
            o_ref[:, h * SLOT:(h + 1) * SLOT] = outs[g][j * tq:(j + 1) * tq]
            p_sum = p_sum + probs[g][j * tq:(j + 1) * tq]
        hi = p_sum.astype(BF16)
        lo = (p_sum - hi.astype(F32)).astype(BF16)
        imps.append(_nt_dot(ov, hi) + _nt_dot(ov, lo))
    cols = NSA_KV * tq
    imp = jnp.concatenate(imps, axis=1)
    blk = lax.broadcasted_iota(jnp.int32, (nsp, cols), 0)
    cur = (pos0 + (lax.broadcasted_iota(jnp.int32, (nsp, cols), 1) & (tq - 1))) // SLC_BLOCK
    real = blk < n_slc
    causal = real & (blk <= cur)
    forced = (blk == 0) | (blk == cur) | (blk == cur - 1)
    score = jnp.where(causal, imp + jnp.where(forced, FORCE_BONUS, 0.0), NEG_INF)
    score = jnp.where(real, score, REMOVED)
    idx = jnp.zeros((SLC_TOPN, cols), jnp.int32)
    idx_row = lax.broadcasted_iota(jnp.int32, (SLC_TOPN, cols), 0)
    for it in range(min(SLC_TOPN, n_slc)):
        m = jnp.max(score, axis=0, keepdims=True)
        first = jnp.min(jnp.where(score == m, blk, nsp), axis=0, keepdims=True)
        score = jnp.where(blk == first, REMOVED, score)
        idx = jnp.where(idx_row == it, first, idx)
    idx_ref[0] = idx


def cmp_block_overlap(n_cmp_pad, n_cmp, n_slc, n_slc_pad, lane_off):
    i = np.arange(n_cmp_pad)[:, None]
    j = np.arange(n_slc_pad)[None, :] - lane_off
    start = i * CMP_STRIDE
    hit = (start < (j + 1) * SLC_BLOCK) & (start + CMP_LEN > j * SLC_BLOCK) & (i < n_cmp) & (j >= 0) & (j < n_slc)
    return jnp.asarray(hit.astype(np.float32), dtype=BF16)


def cmp_attention_select(q_slots, kcc, vcc, n_seq, n_cmp, n_slc, pos0):
    tokens = q_slots.shape[0]
    tq = tokens // n_seq
    assert tq & (tq - 1) == 0
    ncp = kcc.shape[2]
    nsp = -(-n_slc // SUBLANES) * SUBLANES
    ov_t = cmp_block_overlap(ncp, n_cmp, n_slc, nsp, 0).T
    hw = NSA_HEADS * SLOT
    o_cmp, idx_t = pl.pallas_call(
        functools.partial(_cmp_select_kernel, tq=tq, n_cmp=n_cmp, n_slc=n_slc, pos0=pos0),
        grid=(n_seq,),
        in_specs=[pl.BlockSpec((tq, hw), lambda s: (s, 0)),
                  pl.BlockSpec((1, NSA_KV, ncp, SLOT), lambda s: (s, 0, 0, 0)),
                  pl.BlockSpec((1, NSA_KV, ncp, SLOT), lambda s: (s, 0, 0, 0)),
                  pl.BlockSpec((nsp, ncp), lambda s: (0, 0))],
        out_specs=[pl.BlockSpec((tq, hw), lambda s: (s, 0)),
                   pl.BlockSpec((1, SLC_TOPN, NSA_KV * tq), lambda s: (s, 0, 0))],
        out_shape=[jax.ShapeDtypeStruct((tokens, hw), F32),
                   jax.ShapeDtypeStruct((n_seq, SLC_TOPN, NSA_KV * tq), jnp.int32)],
        compiler_params=_cparams(1),
        name="cmp_attention_select",
    )(q_slots, kcc, vcc, ov_t)
    topn = min(SLC_TOPN, n_slc)
    idx = idx_t[:, :topn].reshape(n_seq, topn, NSA_KV, tq).transpose(0, 2, 3, 1)
    return o_cmp, idx


def _cmp_select_prompt_kernel(q_ref, kc_ref, vc_ref, ovt_ref, gt_ref, o_ref, sel_ref, *, tq, n_cmp, n_slc):
    t0 = pl.program_id(2) * tq
    ncp = kc_ref.shape[2]
    nsr = ovt_ref.shape[0]
    kc = kc_ref[0, 0].astype(BF16)
    vc = vc_ref[0, 0].astype(BF16)
    q_pos = t0 + lax.broadcasted_iota(jnp.int32, (ncp, tq), 1)
    blk_i = lax.broadcasted_iota(jnp.int32, (ncp, tq), 0)
    valid = (blk_i * CMP_STRIDE + (CMP_LEN - 1) <= q_pos) & (blk_i < n_cmp)
    p_sum = jnp.zeros((ncp, tq), F32)
    for j in range(NSA_GROUP):
        s = jnp.where(valid, _nt_dot(kc, q_ref[:, j * SLOT:(j + 1) * SLOT]), NEG_INF)
        m = jnp.max(s, axis=0, keepdims=True)
        e = jnp.where(valid, jnp.exp(s - m), 0.0)
        den = jnp.sum(e, axis=0, keepdims=True)
        p = e * (1.0 / jnp.where(den > 0.0, den, 1.0))
        gate = jax.nn.sigmoid(gt_ref[0, pl.ds(pl.program_id(1) * NSA_GROUP + j, 1), :])
        o_ref[:, j * SLOT:(j + 1) * SLOT] = _tn_dot((p * gate).astype(BF16), vc).astype(o_ref.dtype)
        p_sum = p_sum + p
    hi = p_sum.astype(BF16)
    lo = (p_sum - hi.astype(F32)).astype(BF16)
    ovt = ovt_ref[...]
    imp = _dot(ovt, hi) + _dot(ovt, lo)
    blk = lax.broadcasted_iota(jnp.int32, (nsr, tq), 0)
    cur = (t0 + lax.broadcasted_iota(jnp.int32, (nsr, tq), 1)) // SLC_BLOCK
    real = blk < n_slc
    causal = real & (blk <= cur)
    forced = (blk == 0) | (blk == cur) | (blk == cur - 1)
    score = jnp.where(causal, imp + jnp.where(forced, FORCE_BONUS, 0.0), NEG_INF)
    score = jnp.where(real, score, REMOVED)
    picked = jnp.zeros((nsr, tq), jnp.bool_)
    for _ in range(min(SLC_TOPN, n_slc)):
        m = jnp.max(score, axis=0, keepdims=True)
        first = jnp.min(jnp.where(score == m, blk, nsr), axis=0, keepdims=True)
        hit = blk == first
        picked = picked | hit
        score = jnp.where(hit, REMOVED, score)
    bias_t = jnp.where(real & ~(picked & causal), NEG_INF, 0.0)
    slot_t = jnp.concatenate([jnp.zeros((NSA_DH, tq), F32), bias_t], axis=0)
    sel_ref[0, 0] = slot_t.T


def cmp_attention_select_prompt(q_slots, kcc, vcc, gates_t, n_seq, n_cmp, n_slc, tq):
    tokens = q_slots.shape[0]
    t = tokens // n_seq
    tq = _row_tile(t, tq)
    nt = t // tq
    ncp = kcc.shape[2]
    nsr = SLOT - NSA_DH
    assert n_slc <= nsr
    ovt = cmp_block_overlap(ncp, n_cmp, n_slc, nsr, 0).T
    gw = NSA_GROUP * SLOT
    return pl.pallas_call(
        functools.partial(_cmp_select_prompt_kernel, tq=tq, n_cmp=n_cmp, n_slc=n_slc),
        grid=(n_seq, NSA_KV, nt),
        in_specs=[pl.BlockSpec((tq, gw), lambda s, g, i: (s * nt + i, g)),
                  pl.BlockSpec((1, 1, ncp, SLOT), lambda s, g, i: (s, g, 0, 0)),
                  pl.BlockSpec((1, 1, ncp, SLOT), lambda s, g, i: (s, g, 0, 0)),
                  pl.BlockSpec((nsr, ncp), lambda s, g, i: (0, 0)),
                  pl.BlockSpec((1, LANES, tq), lambda s, g, i: (s, 0, i))],
        out_specs=[pl.BlockSpec((tq, gw), lambda s, g, i: (s * nt + i, g)),
                   pl.BlockSpec((1, 1, tq, SLOT), lambda s, g, i: (s, g, i, 0))],
        out_shape=[jax.ShapeDtypeStruct((tokens, NSA_HEADS * SLOT), BF16),
                   jax.ShapeDtypeStruct((n_seq, NSA_KV, t, SLOT), F32)],
        compiler_params=_cparams(3),
        name="cmp_attention_select_prompt",
    )(q_slots, kcc, vcc, ovt, gates_t)


def _prompt_slc_win_kernel(q_ref, sel_ref, ks_ref, vs_ref, kw_ref, vw_ref, oh_ref, gt_ref, o_ref,
                           m_slc, acc_slc, m_win, acc_win, o_sum, *, tq, seq):
    acc_rows = NSA_DH + 16
    qi = pl.program_id(2)
    t0 = qi * tq
    rows = NSA_GROUP * tq
    sel = sel_ref[0, 0]
    q_plain = jnp.concatenate([q_ref[:, j * SLOT:(j + 1) * SLOT] for j in range(NSA_GROUP)], axis=0)
    q_aug = jnp.concatenate([(q_ref[:, j * SLOT:(j + 1) * SLOT].astype(F32) + sel).astype(BF16)
                             for j in range(NSA_GROUP)], axis=0)
    zeros_k = jnp.zeros((SLOT - NSA_DH, tq), F32)
    ones_row = (lax.broadcasted_iota(jnp.int32, (acc_rows - NSA_DH, tq), 0) == 0).astype(BF16)
    rel = lax.broadcasted_iota(jnp.int32, (tq, LANES), 1) - lax.broadcasted_iota(jnp.int32, (tq, LANES), 0)

    def scores(q_rows, k_top, k_bottom):
        k_rows = jnp.concatenate([k_top, k_bottom], axis=0).T.astype(BF16)
        return [_nt_dot(k_rows, q_rows[c:c + 2 * LANES]) for c in range(0, rows, 2 * LANES)]

    def update(state, s_t, v_top, start, mask):
        m_ref, acc_ref = state
        v_t = jnp.concatenate([v_top.astype(BF16), ones_row], axis=0)
        for cg in range(rows // (2 * LANES)):
            p_parts, a_parts = [], []
            for h in range(2):
                c0 = (2 * cg + h) * LANES
                x = s_t[cg][:, h * LANES:(h + 1) * LANES]
                if mask is not None:
                    lo, hi = mask
                    off = t0 - start + (c0 & (tq - 1))
                    keep = rel >= lo - off
                    if hi is not None:
                        keep = keep & (rel < hi - off)
                    x = jnp.where(keep, x, NEG_INF)
                m_old = m_ref[:, c0:c0 + LANES]
                m_new = jnp.maximum(m_old, jnp.max(x, axis=0, keepdims=True))
                m_ref[:, c0:c0 + LANES] = m_new
                a_parts.append(jnp.exp(m_old - m_new))
                p_parts.append(jnp.exp(x - m_new).astype(BF16))
            c0 = 2 * cg * LANES
            pv = _dot(v_t, jnp.concatenate(p_parts, axis=1))
            acc_ref[:, c0:c0 + 2 * LANES] = jnp.concatenate(a_parts, axis=1) * acc_ref[:, c0:c0 + 2 * LANES] + pv

    slc_state = (m_slc, acc_slc)
    win_state = (m_win, acc_win)
    for m_ref, acc_ref in (slc_state, win_state):
        m_ref[...] = jnp.full(m_ref.shape, NEG_INF, F32)
        acc_ref[...] = jnp.zeros(acc_ref.shape, F32)

    slot_pad = jnp.zeros((SLOT - NSA_DH, tq), F32)

    def finish(state, branch, accumulate):
        for j in range(NSA_GROUP):
            a = state[1][:, j * tq:(j + 1) * tq]
            head = pl.program_id(1) * NSA_GROUP + j
            gate = jax.nn.sigmoid(gt_ref[0, pl.ds(branch * NSA_HEADS + head, 1), :])
            o_t = jnp.concatenate([a[0:NSA_DH] * (gate * (1.0 / a[NSA_DH:NSA_DH + 1, :])), slot_pad], axis=0)
            if accumulate:
                o_ref[:, j * SLOT:(j + 1) * SLOT] = (o_sum[:, j * SLOT:(j + 1) * SLOT] + o_t.T).astype(o_ref.dtype)
            else:
                o_sum[:, j * SLOT:(j + 1) * SLOT] = o_t.T

    def slc_scores(start):
        return scores(q_aug, ks_ref[0, :, pl.ds(start, tq)], oh_ref[:, pl.ds(start, tq)])

    def slc_update(s_t, start, mask):
        update(slc_state, s_t, vs_ref[0, :, pl.ds(start, tq)], start, mask)

    k0 = jnp.clip(t0 - WINDOW, 0, seq - WINDOW - tq)
    n_win = WINDOW // tq + 1
    win_start = [pl.multiple_of(k0 + i * tq, tq) for i in range(n_win)]
    win_mask = (0, WINDOW)

    def win_scores(i):
        return scores(q_plain, kw_ref[0, :, pl.ds(win_start[i], tq)], zeros_k)

    def slc_tiles(first, count):
        starts = [pl.multiple_of((first + i) * tq, tq) for i in range(count)]
        s_all = [slc_scores(st) for st in starts]
        for s_t, st in zip(s_all, starts):
            slc_update(s_t, st, None)

    def quad(k4, carry):
        slc_tiles(4 * k4, 4)
        return carry

    lax.fori_loop(0, qi // 4, quad, 0)

    @pl.when(qi % 4 >= 2)
    def _():
        slc_tiles(4 * (qi // 4), 2)

    @pl.when(qi % 2 == 1)
    def _():
        slc_tiles(qi - 1, 1)

    diag = pl.multiple_of(t0, tq)
    s_diag = slc_scores(diag)
    s_win = [win_scores(i) for i in range(n_win)]
    slc_update(s_diag, diag, (0, None))
    for i in range(n_win):
        update(win_state, s_win[i], vw_ref[0, :, pl.ds(win_start[i], tq)], win_start[i], win_mask)
    finish(slc_state, 1, False)
    finish(win_state, 2, True)


def prompt_slc_win_attention(q_slots, sel, ks_t, vs_t, kw_t, vw_t, gates_t, n_seq, tq):
    tokens = q_slots.shape[0]
    seq = tokens // n_seq
    tq = _row_tile(seq, tq)
    assert tq & (tq - 1) == 0 and WINDOW % tq == 0 and seq >= WINDOW + tq
    nt = seq // tq
    gw = NSA_GROUP * SLOT
    assert tq % (2 * LANES) == 0
    onehot_t = jax.nn.one_hot(jnp.arange(seq) // SLC_BLOCK, SLOT - NSA_DH, dtype=F32).T
    kv_spec = pl.BlockSpec((1, NSA_DH, seq), lambda s, g, i: (s, g, 0))
    return pl.pallas_call(
        functools.partial(_prompt_slc_win_kernel, tq=tq, seq=seq),
        grid=(n_seq, NSA_KV, nt),
        in_specs=[pl.BlockSpec((tq, gw), lambda s, g, i: (s * nt + i, g)),
                  pl.BlockSpec((1, 1, tq, SLOT), lambda s, g, i: (s, g, i, 0)),
                  kv_spec, kv_spec, kv_spec, kv_spec,
                  pl.BlockSpec((SLOT - NSA_DH, seq), lambda s, g, i: (0, 0)),
                  pl.BlockSpec((1, LANES, tq), lambda s, g, i: (s, 0, i))],
        out_specs=pl.BlockSpec((tq, gw), lambda s, g, i: (s * nt + i, g)),
        out_shape=jax.ShapeDtypeStruct((tokens, NSA_HEADS * SLOT), BF16),
        scratch_shapes=[pltpu.VMEM((1, NSA_GROUP * tq), F32),
                        pltpu.VMEM((NSA_DH + 16, NSA_GROUP * tq), F32),
                        pltpu.VMEM((1, NSA_GROUP * tq), F32),
                        pltpu.VMEM((NSA_DH + 16, NSA_GROUP * tq), F32),
                        pltpu.VMEM((tq, gw), F32)],
        compiler_params=_cparams(3),
        name="prompt_slc_win_attention",
    )(q_slots, sel, ks_t, vs_t, kw_t, vw_t, onehot_t, gates_t)


def _sample_slc_kernel(idx_ref, pg_ref, q_ref, kn_ref, vn_ref, pool_k, pool_v, o_ref, kbuf, vbuf, sems,
                       *, topn, past, t_pad, t_real):
    s_id, g_id = pl.program_id(0), pl.program_id(1)
    step = s_id * NSA_KV + g_id
    n_steps = pl.num_programs(0) * NSA_KV

    def copies(item_step, t, slot):
        base = (item_step * t_pad + t) * topn
        g = item_step % NSA_KV
        out = []
        for kk in range(topn):
            page = pg_ref[base + kk]
            out.append(pltpu.make_async_copy(pool_k.at[page, g], kbuf.at[slot, kk], sems.at[slot, 0]))
            out.append(pltpu.make_async_copy(pool_v.at[page, g], vbuf.at[slot, kk], sems.at[slot, 1]))
        return out

    @pl.when(step == 0)
    def _():
        for c in copies(step, 0, 0):
            c.start()

    for t in range(t_real):
        slot = t % 2
        if t + 1 < t_real:
            for c in copies(step, t + 1, 1 - slot):
                c.start()
        else:
            @pl.when(step + 1 < n_steps)
            def _():
                for c in copies(step + 1, 0, 1 - slot):
                    c.start()
        for c in copies(step, t, slot):
            c.wait()
        _sample_slc_token(idx_ref, q_ref, kn_ref, vn_ref, o_ref, kbuf[slot], vbuf[slot],
                          base=(step * t_pad + t) * topn, t=t, topn=topn, past=past)


def _sample_slc_token(idx_ref, q_ref, kn_ref, vn_ref, o_ref, k_tiles, v_tiles, *, base, t, topn, past):
    q_pos = past + t
    cur = q_pos // SLC_BLOCK
    first_new = past // SLC_BLOCK
    q = q_ref[0, 0, t].astype(BF16)
    kb = jnp.concatenate([k_tiles[kk] for kk in range(topn)], axis=1).astype(BF16)
    vb = jnp.concatenate([v_tiles[kk] for kk in range(topn)], axis=1).astype(BF16)
    n_keys = topn * PAGE_SIZE
    lane = lax.broadcasted_iota(jnp.int32, (1, n_keys), 1)
    slot = lane // PAGE_SIZE
    in_page = lane % PAGE_SIZE
    k_pos = in_page
    limit = jnp.zeros((1, n_keys), jnp.int32)
    n_new = jnp.int32(0)
    for kk in range(topn):
        b = idx_ref[base + kk]
        here = slot == kk
        k_pos = jnp.where(here, (b // 2) * PAGE_SIZE + in_page, k_pos)
        last = jnp.where(b <= cur, jnp.minimum(q_pos, past - 1), -1)
        limit = jnp.where(here, jnp.where(in_page // SLC_BLOCK == b % 2, last, -1), limit)
        n_new = n_new + jnp.where(b == first_new, 1, 0)
    valid = k_pos <= limit
    s_old = jnp.where(valid, _dot(q, kb), NEG_INF)
    new_lane = lax.broadcasted_iota(jnp.int32, (1, SUBLANES), 1)
    valid_new = past + new_lane <= jnp.where(n_new > 0, q_pos, past - 1)
    s_new = jnp.where(valid_new, _dot(q, kn_ref[0, 0].astype(BF16)), NEG_INF)
    m = jnp.maximum(jnp.max(s_old, axis=-1, keepdims=True), jnp.max(s_new, axis=-1, keepdims=True))
    p_old = jnp.exp(s_old - m)
    p_new = jnp.exp(s_new - m)
    l = jnp.sum(p_old, axis=-1, keepdims=True) + jnp.sum(p_new, axis=-1, keepdims=True)
    o = _nt_dot(p_old.astype(BF16), vb) + _nt_dot(p_new.astype(BF16), vn_ref[0, 0].astype(BF16))
    o_ref[0, 0, t] = o / l


def sample_slc_attention(q_rows, idx, page_table, pool_k, pool_v, k_new_t, v_new_t, t_real, past):
    n_seq, _, t_pad, _, dh = q_rows.shape
    topn = idx.shape[-1]
    assert past % SLC_BLOCK == 0 and t_real <= SUBLANES and PAGE_SIZE == 2 * SLC_BLOCK
    last_old = past // SLC_BLOCK - 1
    logical = jnp.clip(idx, 0, last_old) // 2
    n_pages = page_table.shape[1]
    hit = logical[..., None] == jnp.arange(n_pages, dtype=jnp.int32)
    phys = jnp.sum(jnp.where(hit, page_table[:, None, None, None, :], 0), axis=-1)

    assert t_real % 2 == 0
    new_spec = pl.BlockSpec((1, 1, dh, SUBLANES), lambda s, g, i_r, p_r: (s, g, 0, 0))
    hbm = pl.BlockSpec(memory_space=pl.ANY)
    grid_spec = pltpu.PrefetchScalarGridSpec(
        num_scalar_prefetch=2,
        grid=(n_seq, NSA_KV),
        in_specs=[pl.BlockSpec((1, 1, t_pad, SUBLANES, dh), lambda s, g, i_r, p_r: (s, g, 0, 0, 0)),
                  new_spec, new_spec, hbm, hbm],
        out_specs=pl.BlockSpec((1, 1, t_real, SUBLANES, dh), lambda s, g, i_r, p_r: (s, g, 0, 0, 0)),
        scratch_shapes=[pltpu.VMEM((2, topn, dh, PAGE_SIZE), F32),
                        pltpu.VMEM((2, topn, dh, PAGE_SIZE), F32),
                        pltpu.SemaphoreType.DMA((2, 2))],
    )
    return pl.pallas_call(
        functools.partial(_sample_slc_kernel, topn=topn, past=past, t_pad=t_pad, t_real=t_real),
        grid_spec=grid_spec,
        out_shape=jax.ShapeDtypeStruct((n_seq, NSA_KV, t_real, SUBLANES, dh), F32),
        compiler_params=_cparams(2),
        name="sample_slc_attention",
    )(idx.reshape(-1), phys.reshape(-1), q_rows, k_new_t, v_new_t, pool_k, pool_v)


def _sample_win_kernel(q_ref, wk_ref, wv_ref, kn_ref, vn_ref, o_ref, *, past, t_pad):
    rows = t_pad * SUBLANES
    wb = wk_ref.shape[-1]
    groups = range(NSA_KV)
    q_pos = past + lax.broadcasted_iota(jnp.int32, (rows, 1), 0) // SUBLANES
    k_pos = past - wb + lax.broadcasted_iota(jnp.int32, (1, wb), 1)
    dist = q_pos - k_pos
    valid = (dist >= 0) & (dist < WINDOW) & (k_pos >= 0)
    n_pos = past + lax.broadcasted_iota(jnp.int32, (1, SUBLANES), 1)
    dist_n = q_pos - n_pos
    valid_n = (dist_n >= 0) & (dist_n < WINDOW)
    qs = [q_ref[0, g].reshape(rows, q_ref.shape[-1]).astype(BF16) for g in groups]
    s_old = [_dot(qs[g], wk_ref[0, g].astype(BF16)) for g in groups]
    s_new = [_dot(qs[g], kn_ref[0, g].astype(BF16)) for g in groups]
    p_old, p_new, inv_l = [], [], []
    for g in groups:
        so = jnp.where(valid, s_old[g], NEG_INF)
        sn = jnp.where(valid_n, s_new[g], NEG_INF)
        m = jnp.maximum(jnp.max(so, axis=-1, keepdims=True), jnp.max(sn, axis=-1, keepdims=True))
        po = jnp.exp(so - m)
        pn = jnp.exp(sn - m)
        inv_l.append(1.0 / (jnp.sum(po, axis=-1, keepdims=True) + jnp.sum(pn, axis=-1, keepdims=True)))
        p_old.append(po.astype(BF16))
        p_new.append(pn.astype(BF16))
    for g in groups:
        o = _nt_dot(p_old[g], wv_ref[0, g].astype(BF16)) + _nt_dot(p_new[g], vn_ref[0, g].astype(BF16))
        o_ref[0, g] = (o * inv_l[g]).reshape(t_pad, SUBLANES, o.shape[-1])


def sample_win_attention(q_rows, win_k_t, win_v_t, k_new_t, v_new_t, past):
    n_seq, _, t_pad, _, dh = q_rows.shape
    wb = win_k_t.shape[-1]
    q_spec = pl.BlockSpec((1, NSA_KV, t_pad, SUBLANES, dh), lambda s: (s, 0, 0, 0, 0))
    win_spec = pl.BlockSpec((1, NSA_KV, dh, wb), lambda s: (s, 0, 0, 0))
    new_spec = pl.BlockSpec((1, NSA_KV, dh, SUBLANES), lambda s: (s, 0, 0, 0))
    return pl.pallas_call(
        functools.partial(_sample_win_kernel, past=past, t_pad=t_pad),
        grid=(n_seq,),
        in_specs=[q_spec, win_spec, win_spec, new_spec, new_spec],
        out_specs=q_spec,
        out_shape=jax.ShapeDtypeStruct(q_rows.shape, F32),
        compiler_params=_cparams(1),
        name="sample_win_attention",
    )(q_rows, win_k_t, win_v_t, k_new_t, v_new_t)


def _to_slots(a):
    lead = a.shape[:-1]
    n = a.shape[-1] // NSA_DH
    a = a.reshape(*lead, n, NSA_DH)
    a = jnp.pad(a, [(0, 0)] * (a.ndim - 1) + [(0, SLOT - NSA_DH)])
    return a.reshape(*lead, n * SLOT)


def _odd_weights(w_in, w_out):
    d = w_in.shape[0]
    hq = NSA_HEADS * NSA_DH
    kvw = NSA_KV * NSA_DH
    wq = _to_slots(w_in[:, :hq] * np.float32(NSA_DH ** -0.5))
    wg = jnp.pad(w_in[:, hq + 6 * kvw:], ((0, 0), (0, LANES - 3 * NSA_HEADS)))
    w_q = jnp.concatenate([wq, wg], axis=1).astype(BF16)
    w_kvt = jnp.concatenate([w_in[:, hq:hq + 6 * kvw], wg], axis=1).T.astype(BF16)
    wo = jnp.pad(w_out.reshape(NSA_HEADS, NSA_DH, d), ((0, 0), (0, SLOT - NSA_DH), (0, 0)))
    wo = wo.reshape(NSA_HEADS * SLOT, d).astype(BF16)
    k = NSA_HEADS * SLOT
    e = np.zeros((LANES, 3 * k), np.float32)
    for c in range(3):
        for h in range(NSA_HEADS):
            e[c * NSA_HEADS + h, c * k + h * SLOT:c * k + (h + 1) * SLOT] = 1.0
    return w_q, w_kvt, wo, jnp.asarray(e, dtype=BF16)


def _group_rows(q_slots, n_seq, t_pad):
    q = q_slots.reshape(n_seq, t_pad, NSA_KV, NSA_GROUP, SLOT)[..., :NSA_DH]
    q = q.transpose(0, 2, 1, 3, 4)
    return jnp.pad(q, ((0, 0), (0, 0), (0, 0), (0, SUBLANES - NSA_GROUP), (0, 0)))


def _ungroup_rows(o, n_seq, t_pad):
    t = o.shape[2]
    o = o[:, :, :, :NSA_GROUP].transpose(0, 2, 1, 3, 4)
    o = jnp.pad(o, ((0, 0), (0, t_pad - t), (0, 0), (0, 0), (0, SLOT - NSA_DH)))
    return o.reshape(n_seq * t_pad, NSA_HEADS * SLOT)


def _feature_major(cache):
    return cache.transpose(0, 2, 3, 1)


def _token_major(a_t):
    return a_t.transpose(0, 3, 1, 2)


def _pad_rows(a, t_pad):
    return jnp.pad(a, ((0, 0), (0, t_pad - a.shape[1])) + ((0, 0),) * (a.ndim - 2))


def kernel(x_prompt, x_sample, state_sconv, state_ret, cache_cmp_k, cache_cmp_v, cache_slc_k, cache_slc_v,
           cache_win_k, cache_win_v, state_ffn_conv, page_table,
           w_in_even, sconv_w, sconv_b, ret_gn_g, w_out_even,
           w_in_odd, cmp_pe, cmp_w1, cmp_w2, w_out_odd,
           ln_mix_g, ln_mix_b, ffn_w_up, ffn_conv_w, ffn_conv_b, ffn_w_down, ln_ffn_g, ln_ffn_b):
    b_p, s_p, d_model = x_prompt.shape
    b_s, t_s, _ = x_sample.shape
    n_pages = page_table.shape[1]
    past = n_pages * PAGE_SIZE
    t_pad = SUBLANES
    assert t_s <= t_pad and t_s >= SCONV_W - 1 and t_s < CMP_STRIDE and past % PAGE_SIZE == 0
    assert s_p % RET_CHUNK == 0 and s_p % PAGE_SIZE == 0
    d_sconv = sconv_w.shape[-1]
    d_ff = ffn_conv_w.shape[-1]
    gd = NSA_KV * NSA_DH
    depth = ln_mix_g.shape[0]

    xp = x_prompt.reshape(b_p * s_p, d_model)
    xs = _pad_rows(x_sample, t_pad).reshape(b_s * t_pad, d_model)
    outs = {k: [] for k in ("sconv_p", "sconv_s", "ret_p", "ret_s", "cmp_k_p", "cmp_v_p", "slc_k_p", "slc_v_p",
                            "cmp_k_s", "cmp_v_s", "slc_k_s", "slc_v_s", "win_k_p", "win_v_p", "win_k_s",
                            "win_v_s", "ffn_p", "ffn_s")}

    for layer in range(depth):
        if layer % 2 == 0:
            e = layer // 2
            w_in = w_in_even[e].astype(BF16)
            w_out = w_out_even[e].astype(BF16)
            n_in = w_in.shape[1]
            yp, hc, st = even_mixer(xp, w_in, jnp.zeros((b_p, SCONV_W - 1, d_sconv), F32),
                                    jnp.zeros((b_p,) + state_ret.shape[2:], F32), jnp.arange(s_p),
                                    RET_CHUNK, RET_CHUNK, sconv_w[e], sconv_b[e], ret_gn_g[e],
                                    chunks=MIXER_CHUNKS if s_p % (MIXER_CHUNKS * RET_CHUNK) == 0 else 1)
            outs["sconv_p"].append(hc)
            outs["ret_p"].append(st)
            xp = matmul_residual_ln(yp, w_out, xp, ln_mix_g[layer], ln_mix_b[layer])
            (zs,) = matmul_split(xs, w_in, [n_in], [F32])
            ys, hc, st = even_mixer(zs, None, state_sconv[e], state_ret[e], past + jnp.arange(t_pad),
                                    t_pad, t_s, sconv_w[e], sconv_b[e], ret_gn_g[e])
            outs["sconv_s"].append(hc)
            outs["ret_s"].append(st)
            xs = matmul_residual_ln(ys, w_out, xs, ln_mix_g[layer], ln_mix_b[layer])
        else:
            o = layer // 2
            w_q, w_kvt, w_out, e_gate = _odd_weights(w_in_odd[o], w_out_odd[o])
            pe, w1, w2 = cmp_pe[o], cmp_w1[o], cmp_w2[o]
            qp, _, gtp, kc, vc, ks, vs, kw, vw = nsa_projection(xp, b_p, w_q, w_kvt, BF16)
            as_cache = lambda a_t: _token_major(a_t.reshape(b_p, NSA_KV, NSA_DH, -1))
            keep = min(WINDOW, s_p)
            outs["cmp_k_p"].append(as_cache(kc))
            outs["cmp_v_p"].append(as_cache(vc))
            outs["slc_k_p"].append(as_cache(ks))
            outs["slc_v_p"].append(as_cache(vs))
            outs["win_k_p"].append(as_cache(kw[:, :, s_p - keep:]))
            outs["win_v_p"].append(as_cache(vw[:, :, s_p - keep:]))
            kcc = compress(kc.reshape(b_p, NSA_KV, NSA_DH, s_p), None, pe[0], w1[0], w2[0])
            vcc = compress(vc.reshape(b_p, NSA_KV, NSA_DH, s_p), None, pe[1], w1[1], w2[1])
            n_cmp = s_p // CMP_STRIDE - CMP_LEN // CMP_STRIDE + 1
            n_slc = s_p // SLC_BLOCK
            oc, sel = cmp_attention_select_prompt(qp, kcc, vcc, gtp, b_p, n_cmp, n_slc, SELECT_ROWS)
            osw = prompt_slc_win_attention(qp, sel, ks, vs, kw, vw, gtp, b_p, ATTN_ROWS)
            xp = sum2_matmul_residual_ln(oc, osw, w_out, xp, ln_mix_g[layer], ln_mix_b[layer])
            qs, gs, _, *kv_s = nsa_projection(xs, 1, w_q, w_kvt, F32)
            kc, vc, ks, vs, kw, vw = [a.reshape(NSA_KV, NSA_DH, b_s, t_pad).transpose(2, 0, 1, 3) for a in kv_s]
            new_rows = lambda a_t: _token_major(a_t[..., :t_s])
            outs["cmp_k_s"].append(new_rows(kc))
            outs["cmp_v_s"].append(new_rows(vc))
            outs["slc_k_s"].append(new_rows(ks))
            outs["slc_v_s"].append(new_rows(vs))
            win_k = _feature_major(cache_win_k[o])
            win_v = _feature_major(cache_win_v[o])
            wb = win_k.shape[-1]
            keep = min(WINDOW, wb + t_s)
            outs["win_k_s"].append(_token_major(jnp.concatenate([win_k, kw[..., :t_s]], axis=-1)[..., -keep:]))
            outs["win_v_s"].append(_token_major(jnp.concatenate([win_v, vw[..., :t_s]], axis=-1)[..., -keep:]))
            kcc = compress(_feature_major(cache_cmp_k[o]), page_table, pe[0], w1[0], w2[0])
            vcc = compress(_feature_major(cache_cmp_v[o]), page_table, pe[1], w1[1], w2[1])
            n_cmp = (past + t_s) // CMP_STRIDE - CMP_LEN // CMP_STRIDE + 1
            n_slc = -(-(past + t_s) // SLC_BLOCK)
            oc, idx = cmp_attention_select(qs, kcc, vcc, b_s, n_cmp, n_slc, past)
            q_rows = _group_rows(qs, b_s, t_pad)
            osl = sample_slc_attention(q_rows, idx, page_table, _feature_major(cache_slc_k[o]),
                                       _feature_major(cache_slc_v[o]), ks, vs, t_s, past)
            ow = sample_win_attention(q_rows, win_k, win_v, kw, vw, past)
            xs = nsa_merge_residual_ln(oc, _ungroup_rows(osl, b_s, t_pad), _ungroup_rows(ow, b_s, t_pad), gs,
                                       e_gate, w_out, xs, ln_mix_g[layer], ln_mix_b[layer])
        w_up = ffn_w_up[layer].astype(BF16)
        w_down = ffn_w_down[layer].astype(BF16)
        hp, hist_p = ffn_up_sequences(xp, b_p, jnp.zeros((b_p, FFN_W - 1, d_ff), F32), w_up,
                                      ffn_conv_w[layer], ffn_conv_b[layer])
        outs["ffn_p"].append(hist_p)
        xp = matmul_residual_ln(hp, w_down, xp, ln_ffn_g[layer], ln_ffn_b[layer])
        hs, a_s = ffn_up_short(xs, state_ffn_conv[layer], w_up, ffn_conv_w[layer], ffn_conv_b[layer])
        outs["ffn_s"].append(a_s.reshape(b_s, t_pad, d_ff)[:, t_s - (FFN_W - 1):t_s])
        xs = matmul_residual_ln(hs, w_down, xs, ln_ffn_g[layer], ln_ffn_b[layer])

    st = jnp.stack
    y_p = xp.reshape(b_p, s_p, d_model)
    y_s = xs.reshape(b_s, t_pad, d_model)[:, :t_s]
    order = ("sconv_p", "sconv_s", "ret_p", "ret_s", "cmp_k_p", "cmp_v_p", "slc_k_p", "slc_v_p",
             "cmp_k_s", "cmp_v_s", "slc_k_s", "slc_v_s", "win_k_p", "win_v_p", "win_k_s", "win_v_s",
             "ffn_p", "ffn_s")
    return (y_p, y_s) + tuple(st(outs[k]) for k in order)
```

```python
import functools

import numpy as np
import jax
import jax.numpy as jnp
from jax import lax
from jax.experimental import pallas as pl
from jax.experimental.pallas import tpu as pltpu

F32 = jnp.float32
BF16 = jnp.bfloat16

SUBLANES = 8
LANES = 128
VMEM_LIMIT_BYTES = 56 * 1024 * 1024
MATMUL_ROWS = 512
MERGE_ROWS = 256
ATTN_ROWS = 256
SELECT_ROWS = 1024
COMPRESS_PAGES = 64
MIXER_CHUNKS = 4

DEPTH = 2
SCONV_W = 3
RET_HEADS = 4
RET_CHUNK = 128
ROPE_BASE = 10000.0
NSA_HEADS = 16
NSA_KV = 4
NSA_GROUP = NSA_HEADS // NSA_KV
NSA_DH = 64
CMP_LEN = 32
CMP_STRIDE = 16
SLC_BLOCK = 64
SLC_TOPN = 16
WINDOW = 512
PAGE_SIZE = 128
FFN_W = 3
ALPHA = (2.0 * DEPTH) ** 0.25
LN_EPS = 1e-5
NEG_INF = -1e30
REMOVED = -3e38
FORCE_BONUS = 1e4
SLOT = 2 * NSA_DH


def _cparams(n_grid):
    return pltpu.CompilerParams(dimension_semantics=("arbitrary",) * n_grid,
                                vmem_limit_bytes=VMEM_LIMIT_BYTES)


def _row_tile(m, want):
    t = min(m, want)
    assert m % t == 0, (m, t)
    return t


def _nt_dot(a, b):
    return lax.dot_general(a, b, (((1,), (1,)), ((), ())), preferred_element_type=F32)


def _tn_dot(a, b):
    return lax.dot_general(a, b, (((0,), (0,)), ((), ())), preferred_element_type=F32)


def _dot(a, b):
    return jnp.dot(a, b, preferred_element_type=F32)


def _gelu(x):
    return 0.5 * x * (1.0 + jnp.tanh(np.float32(np.sqrt(2.0 / np.pi)) * (x + 0.044715 * (x * x * x))))


def _layer_norm_rows(r, g, b):
    mu = jnp.mean(r, axis=-1, keepdims=True)
    d = r - mu
    var = jnp.mean(d * d, axis=-1, keepdims=True)
    return d * lax.rsqrt(var + LN_EPS) * g + b


def _mm_split_kernel(x_ref, w_ref, *o_refs, cuts):
    acc = _dot(x_ref[...].astype(BF16), w_ref[...])
    for o_ref, (lo, hi) in zip(o_refs, cuts):
        o_ref[...] = acc[:, lo:hi].astype(o_ref.dtype)


def matmul_split(x, w_bf16, widths, dtypes, tm=MATMUL_ROWS):
    m, k = x.shape
    n = w_bf16.shape[1]
    assert sum(widths) == n and all(wd % LANES == 0 for wd in widths)
    tm = _row_tile(m, tm)
    cuts, lo = [], 0
    for wd in widths:
        cuts.append((lo, lo + wd))
        lo += wd
    return pl.pallas_call(
        functools.partial(_mm_split_kernel, cuts=tuple(cuts)),
        grid=(m // tm,),
        in_specs=[pl.BlockSpec((tm, k), lambda i: (i, 0)),
                  pl.BlockSpec((k, n), lambda i: (0, 0))],
        out_specs=[pl.BlockSpec((tm, wd), lambda i: (i, 0)) for wd in widths],
        out_shape=[jax.ShapeDtypeStruct((m, wd), dt) for wd, dt in zip(widths, dtypes)],
        compiler_params=_cparams(1),
        name="matmul_split",
    )(x, w_bf16)


def _nsa_proj_kernel(x_ref, wq_ref, wt_ref, q_ref, g_ref, gt_ref, *kv_refs, nq, gd):
    xb = x_ref[...].astype(BF16)
    acc = _dot(xb, wq_ref[...])
    q_ref[...] = acc[:, :nq].astype(q_ref.dtype)
    g_ref[...] = acc[:, nq:]
    acc_t = _nt_dot(wt_ref[...], xb)
    for i, r in enumerate(kv_refs):
        r[0] = acc_t[i * gd:(i + 1) * gd, :]
    gt_ref[0] = acc_t[len(kv_refs) * gd:, :]


def nsa_projection(x, n_seq, wq_bf16, wt_bf16, q_dtype, n_kv=6, tm=MATMUL_ROWS):
    m, d = x.shape
    seq = m // n_seq
    tm = _row_tile(seq, tm)
    nt = seq // tm
    nq = wq_bf16.shape[1] - LANES
    gd = (wt_bf16.shape[0] - LANES) // n_kv
    fm = lambda rows: pl.BlockSpec((1, rows, tm), lambda s, i: (s, 0, i))
    return pl.pallas_call(
        functools.partial(_nsa_proj_kernel, nq=nq, gd=gd),
        grid=(n_seq, nt),
        in_specs=[pl.BlockSpec((tm, d), lambda s, i: (s * nt + i, 0)),
                  pl.BlockSpec((d, nq + LANES), lambda s, i: (0, 0)),
                  pl.BlockSpec((n_kv * gd + LANES, d), lambda s, i: (0, 0))],
        out_specs=[pl.BlockSpec((tm, nq), lambda s, i: (s * nt + i, 0)),
                   pl.BlockSpec((tm, LANES), lambda s, i: (s * nt + i, 0)),
                   fm(LANES)] + [fm(gd) for _ in range(n_kv)],
        out_shape=[jax.ShapeDtypeStruct((m, nq), q_dtype), jax.ShapeDtypeStruct((m, LANES), F32),
                   jax.ShapeDtypeStruct((n_seq, LANES, seq), F32)] + [
            jax.ShapeDtypeStruct((n_seq, gd, seq), F32) for _ in range(n_kv)],
        compiler_params=_cparams(2),
        name="nsa_projection",
    )(x, wq_bf16, wt_bf16)


def _mm_res_ln_kernel(a_ref, w_ref, x_ref, g_ref, b_ref, o_ref):
    y = _dot(a_ref[...].astype(BF16), w_ref[...])
    o_ref[...] = _layer_norm_rows(ALPHA * x_ref[...] + y, g_ref[...], b_ref[...])


def matmul_residual_ln(a, w_bf16, x, g, b, tm=MATMUL_ROWS):
    m, k = a.shape
    d = w_bf16.shape[1]
    tm = _row_tile(m, tm)
    return pl.pallas_call(
        _mm_res_ln_kernel,
        grid=(m // tm,),
        in_specs=[pl.BlockSpec((tm, k), lambda i: (i, 0)),
                  pl.BlockSpec((k, d), lambda i: (0, 0)),
                  pl.BlockSpec((tm, d), lambda i: (i, 0)),
                  pl.BlockSpec((1, d), lambda i: (0, 0)),
                  pl.BlockSpec((1, d), lambda i: (0, 0))],
        out_specs=pl.BlockSpec((tm, d), lambda i: (i, 0)),
        out_shape=jax.ShapeDtypeStruct((m, d), F32),
        compiler_params=_cparams(1),
        name="matmul_residual_ln",
    )(a, w_bf16, x, g.reshape(1, d), b.reshape(1, d))


def _sum2_mm_res_ln_kernel(a_ref, b2_ref, w_ref, x_ref, g_ref, b_ref, o_ref):
    y = _dot((a_ref[...].astype(F32) + b2_ref[...].astype(F32)).astype(BF16), w_ref[...])
    o_ref[...] = _layer_norm_rows(ALPHA * x_ref[...] + y, g_ref[...], b_ref[...])


def sum2_matmul_residual_ln(a, b2, w_bf16, x, g, b, tm=MATMUL_ROWS):
    m, k = a.shape
    d = w_bf16.shape[1]
    tm = _row_tile(m, tm)
    row = lambda i: (i, 0)
    fixed = lambda i: (0, 0)
    return pl.pallas_call(
        _sum2_mm_res_ln_kernel,
        grid=(m // tm,),
        in_specs=[pl.BlockSpec((tm, k), row), pl.BlockSpec((tm, k), row),
                  pl.BlockSpec((k, d), fixed),
                  pl.BlockSpec((tm, d), row),
                  pl.BlockSpec((1, d), fixed), pl.BlockSpec((1, d), fixed)],
        out_specs=pl.BlockSpec((tm, d), row),
        out_shape=jax.ShapeDtypeStruct((m, d), F32),
        compiler_params=_cparams(1),
        name="sum2_matmul_residual_ln",
    )(a, b2, w_bf16, x, g.reshape(1, d), b.reshape(1, d))


def _expand_gates(gates_raw, e_ref):
    sig = jax.nn.sigmoid(gates_raw)
    hi = sig.astype(BF16)
    lo = (sig - hi.astype(F32)).astype(BF16)
    e = e_ref[...]
    return _dot(hi, e) + _dot(lo, e)


def _nsa_merge_ln_kernel(oc_ref, os_ref, ow_ref, gt_ref, e_ref, w_ref, x_ref, g_ref, b_ref, o_ref, *, k):
    gx = _expand_gates(gt_ref[...], e_ref)
    o = gx[:, 0:k] * oc_ref[...] + gx[:, k:2 * k] * os_ref[...] + gx[:, 2 * k:3 * k] * ow_ref[...]
    y = _dot(o.astype(BF16), w_ref[...])
    o_ref[...] = _layer_norm_rows(ALPHA * x_ref[...] + y, g_ref[...], b_ref[...])


def nsa_merge_residual_ln(oc, osl, ow, gates, e_bf16, w_bf16, x, g, b, tm=MERGE_ROWS):
    m, k = oc.shape
    d = w_bf16.shape[1]
    tm = _row_tile(m, tm)
    row = lambda i: (i, 0)
    fixed = lambda i: (0, 0)
    return pl.pallas_call(
        functools.partial(_nsa_merge_ln_kernel, k=k),
        grid=(m // tm,),
        in_specs=[pl.BlockSpec((tm, k), row), pl.BlockSpec((tm, k), row), pl.BlockSpec((tm, k), row),
                  pl.BlockSpec((tm, LANES), row),
                  pl.BlockSpec((LANES, 3 * k), fixed),
                  pl.BlockSpec((k, d), fixed),
                  pl.BlockSpec((tm, d), row),
                  pl.BlockSpec((1, d), fixed), pl.BlockSpec((1, d), fixed)],
        out_specs=pl.BlockSpec((tm, d), row),
        out_shape=jax.ShapeDtypeStruct((m, d), F32),
        compiler_params=_cparams(1),
        name="nsa_merge_residual_ln",
    )(oc, osl, ow, gates, e_bf16, w_bf16, x, g.reshape(1, d), b.reshape(1, d))


def _even_mixer_kernel(*refs, rows, valid, dconv, dk, chunks, project):
    if project:
        x_ref, w_ref, *refs = refs
    else:
        z_ref, *refs = refs
    (hist_ref, st_ref, cos_ref, sin_ref, decay_ref, qdec_ref, kdec_ref, sdec_ref, cw_ref, cb_ref, gn_ref,
     y_ref, hist_out_ref, st_out_ref, carry, state) = refs
    r0 = valid - 2 - (rows - SUBLANES)

    @pl.when(pl.program_id(1) == 0)
    def _():
        carry[r0:r0 + 2, :] = hist_ref[0]
        state[...] = st_ref[0]

    z_all = _dot(x_ref[...].astype(BF16), w_ref[...]) if project else z_ref[...]
    d = dconv
    scale = np.float32(dk ** -0.5)
    heads = range(RET_HEADS)
    col = lambda part, hh: slice(part * d + hh * dk, part * d + (hh + 1) * dk)
    row = lax.broadcasted_iota(jnp.int32, (rows, d), 0)
    for ci in range(chunks):
        at = slice(ci * rows, (ci + 1) * rows)
        z = z_all[at]
        ch = z[:, 2 * d:3 * d] * z[:, 0:d]
        h0 = carry[r0:r0 + 1, :]
        h1 = carry[r0 + 1:r0 + 2, :]
        m1 = jnp.where(row == 0, h1, pltpu.roll(ch, 1, 0))
        m2 = jnp.where(row == 0, h0, jnp.where(row == 1, h1, pltpu.roll(ch, 2, 0)))
        u = ((cb_ref[...] + m2 * cw_ref[0:1, :]) + m1 * cw_ref[1:2, :]) + ch * cw_ref[2:3, :]
        y_ref[at, 0:d] = z[:, d:2 * d] * u
        carry[...] = ch[rows - SUBLANES:rows, :]

        cosf = cos_ref[at, :]
        sinf = sin_ref[at, :]
        qs, ks, vbs = [], [], []
        for hh in heads:
            q = z[:, col(3, hh)]
            k = z[:, col(4, hh)]
            qs.append(((q * cosf + pltpu.roll(q, dk // 2, 1) * sinf) * scale).astype(BF16))
            ks.append(k * cosf + pltpu.roll(k, dk // 2, 1) * sinf)
            vbs.append(z[:, col(5, hh)].astype(BF16))
        s_old = [state[hh] for hh in heads]
        scores = [_nt_dot(qs[hh], ks[hh].astype(BF16)) * decay_ref[hh] for hh in heads]
        cross = [_dot(qs[hh], s_old[hh].astype(BF16)) * qdec_ref[hh] for hh in heads]
        intra = [_dot(scores[hh].astype(BF16), vbs[hh]) for hh in heads]
        for hh in heads:
            kd = (ks[hh] * kdec_ref[hh]).astype(BF16)
            state[hh] = s_old[hh] * sdec_ref[hh] + _tn_dot(kd, vbs[hh])
        for hh in heads:
            o = intra[hh] + cross[hh]
            mu = jnp.mean(o, axis=-1, keepdims=True)
            dv = o - mu
            var = jnp.mean(dv * dv, axis=-1, keepdims=True)
            on = dv * lax.rsqrt(var + LN_EPS) * gn_ref[:, hh * dk:(hh + 1) * dk]
            gsw = z[:, col(6, hh)]
            y_ref[at, d + hh * dk:d + (hh + 1) * dk] = (gsw * jax.nn.sigmoid(gsw)) * on
    hist_out_ref[0] = carry[r0:r0 + 2, :]
    st_out_ref[0] = state[...]


def _retention_tables(rows, valid, dk):
    log_gamma = jnp.log1p(-jnp.exp2(-5.0 - jnp.arange(RET_HEADS, dtype=F32)))
    n = jnp.arange(rows, dtype=F32)
    diff = n[:, None] - n[None, :]
    lg = log_gamma[:, None, None]
    decay = jnp.where(diff >= 0, jnp.exp(lg * jnp.maximum(diff, 0.0)), 0.0)
    q_dec = jnp.exp((n[None, :] + 1.0) * log_gamma[:, None])
    k_dec = jnp.where(n[None, :] < valid, jnp.exp((valid - 1.0 - n[None, :]) * log_gamma[:, None]), 0.0)
    s_dec = jnp.exp(valid * log_gamma)
    bc = lambda a: jnp.broadcast_to(a[:, :, None], (RET_HEADS, rows, dk))
    return decay, bc(q_dec), bc(k_dec), jnp.broadcast_to(s_dec[:, None, None], (RET_HEADS, 1, dk))


def _rope_tables(pos, dk):
    half = dk // 2
    inv = ROPE_BASE ** (-jnp.arange(half, dtype=F32) / half)
    ang = pos.astype(F32)[:, None] * inv
    cos, sin = jnp.cos(ang), jnp.sin(ang)
    return jnp.concatenate([cos, cos], axis=-1), jnp.concatenate([-sin, sin], axis=-1)


def even_mixer(x, w_in_bf16, hist, st, pos, rows, valid, conv_w, conv_b, gn_g, chunks=1):
    n_seq, _, dconv = hist.shape
    dk = st.shape[-1]
    project = w_in_bf16 is not None
    tile = chunks * rows
    n_steps = x.shape[0] // (n_seq * tile)
    cosf, sinf = _rope_tables(pos, dk)
    decay, q_dec, k_dec, s_dec = _retention_tables(rows, valid, dk)
    fixed3 = lambda s, c: (0, 0, 0)
    fixed2 = lambda s, c: (0, 0)
    lead_specs = [pl.BlockSpec((tile, x.shape[1]), lambda s, c: (s * n_steps + c, 0))]
    lead_args = [x]
    if project:
        lead_specs.append(pl.BlockSpec(w_in_bf16.shape, fixed2))
        lead_args.append(w_in_bf16)
    n_chunks = n_steps
    rows_out = x.shape[0]
    return pl.pallas_call(
        functools.partial(_even_mixer_kernel, rows=rows, valid=valid, dconv=dconv, dk=dk, chunks=chunks,
                          project=project),
        grid=(n_seq, n_steps),
        in_specs=lead_specs + [
                  pl.BlockSpec((1, 2, dconv), lambda s, c: (s, 0, 0)),
                  pl.BlockSpec((1, RET_HEADS, dk, dk), lambda s, c: (s, 0, 0, 0)),
                  pl.BlockSpec((tile, dk), lambda s, c: (c, 0)),
                  pl.BlockSpec((tile, dk), lambda s, c: (c, 0)),
                  pl.BlockSpec((RET_HEADS, rows, rows), fixed3),
                  pl.BlockSpec((RET_HEADS, rows, dk), fixed3),
                  pl.BlockSpec((RET_HEADS, rows, dk), fixed3),
                  pl.BlockSpec((RET_HEADS, 1, dk), fixed3),
                  pl.BlockSpec((SCONV_W, dconv), fixed2),
                  pl.BlockSpec((1, dconv), fixed2),
                  pl.BlockSpec((1, RET_HEADS * dk), fixed2)],
        out_specs=[pl.BlockSpec((tile, 2 * dconv), lambda s, c: (s * n_chunks + c, 0)),
                   pl.BlockSpec((1, 2, dconv), lambda s, c: (s, 0, 0)),
                   pl.BlockSpec((1, RET_HEADS, dk, dk), lambda s, c: (s, 0, 0, 0))],
        out_shape=[jax.ShapeDtypeStruct((rows_out, 2 * dconv), F32),
                   jax.ShapeDtypeStruct((n_seq, 2, dconv), F32),
                   jax.ShapeDtypeStruct((n_seq, RET_HEADS, dk, dk), F32)],
        scratch_shapes=[pltpu.VMEM((SUBLANES, dconv), F32), pltpu.VMEM((RET_HEADS, dk, dk), F32)],
        compiler_params=_cparams(2),
        name="even_mixer",
    )(*lead_args, hist, st, cosf, sinf, decay, q_dec, k_dec, s_dec, conv_w, conv_b.reshape(1, dconv),
      gn_g.reshape(1, RET_HEADS * dk))


def _conv_gate(a, gate, m1, m2, cw_ref, cb_ref):
    conv = ((cb_ref[...] + m2 * cw_ref[0:1, :]) + m1 * cw_ref[1:2, :]) + a * cw_ref[2:3, :]
    return _gelu(conv) * gate


def _ffn_up_seq_kernel(x_ref, wa_ref, wg_ref, h_ref, cw_ref, cb_ref, o_ref, hist_out_ref, carry, *, tm):
    @pl.when(pl.program_id(2) == 0)
    def _():
        carry[SUBLANES - 2:SUBLANES, :] = h_ref[0]

    xb = x_ref[...].astype(BF16)
    a = _dot(xb, wa_ref[...])
    gate = _dot(xb, wg_ref[...])
    row = lax.broadcasted_iota(jnp.int32, a.shape, 0)
    h0 = carry[SUBLANES - 2:SUBLANES - 1, :]
    h1 = carry[SUBLANES - 1:SUBLANES, :]
    m1 = jnp.where(row == 0, h1, pltpu.roll(a, 1, 0))
    m2 = jnp.where(row == 0, h0, jnp.where(row == 1, h1, pltpu.roll(a, 2, 0)))
    o_ref[...] = _conv_gate(a, gate, m1, m2, cw_ref, cb_ref).astype(o_ref.dtype)
    carry[...] = a[tm - SUBLANES:tm, :]
    hist_out_ref[0] = carry[SUBLANES - 2:SUBLANES, :]


def ffn_up_sequences(x, n_seq, hist, w_up_bf16, conv_w, conv_b, tm=MATMUL_ROWS, n_col=1):
    m, k = x.shape
    dff = conv_w.shape[1]
    seq = m // n_seq
    tm = _row_tile(seq, tm)
    tps = seq // tm
    tn = dff // n_col
    assert tn % LANES == 0
    return pl.pallas_call(
        functools.partial(_ffn_up_seq_kernel, tm=tm),
        grid=(n_col, n_seq, tps),
        in_specs=[pl.BlockSpec((tm, k), lambda j, s, i: (s * tps + i, 0)),
                  pl.BlockSpec((k, tn), lambda j, s, i: (0, j)),
                  pl.BlockSpec((k, tn), lambda j, s, i: (0, j + n_col)),
                  pl.BlockSpec((1, 2, tn), lambda j, s, i: (s, 0, j)),
                  pl.BlockSpec((FFN_W, tn), lambda j, s, i: (0, j)),
                  pl.BlockSpec((1, tn), lambda j, s, i: (0, j))],
        out_specs=[pl.BlockSpec((tm, tn), lambda j, s, i: (s * tps + i, j)),
                   pl.BlockSpec((1, 2, tn), lambda j, s, i: (s, 0, j))],
        out_shape=[jax.ShapeDtypeStruct((m, dff), BF16),
                   jax.ShapeDtypeStruct((n_seq, 2, dff), F32)],
        scratch_shapes=[pltpu.VMEM((SUBLANES, tn), F32)],
        compiler_params=_cparams(3),
        name="ffn_up_sequences",
    )(x, w_up_bf16, w_up_bf16, hist, conv_w, conv_b.reshape(1, dff))


def _ffn_up_short_kernel(x_ref, wa_ref, wg_ref, h1_ref, h2_ref, cw_ref, cb_ref, o_ref, a_ref):
    xb = x_ref[...].astype(BF16)
    a = _dot(xb, wa_ref[...])
    gate = _dot(xb, wg_ref[...])
    t = lax.broadcasted_iota(jnp.int32, a.shape, 0) % SUBLANES
    m1 = jnp.where(t == 0, h1_ref[...], pltpu.roll(a, 1, 0))
    m2 = jnp.where(t < 2, h2_ref[...], pltpu.roll(a, 2, 0))
    o_ref[...] = _conv_gate(a, gate, m1, m2, cw_ref, cb_ref).astype(o_ref.dtype)
    a_ref[...] = a


def ffn_up_short(x, hist, w_up_bf16, conv_w, conv_b, n_col=2):
    m, k = x.shape
    dff = conv_w.shape[1]
    n_seq = m // SUBLANES
    tn = dff // n_col
    zeros = jnp.zeros((n_seq, SUBLANES, dff), F32)
    h1 = zeros.at[:, 0].set(hist[:, 1]).reshape(m, dff)
    h2 = zeros.at[:, 0].set(hist[:, 0]).at[:, 1].set(hist[:, 1]).reshape(m, dff)
    col = lambda j: (0, j)
    return pl.pallas_call(
        _ffn_up_short_kernel,
        grid=(n_col,),
        in_specs=[pl.BlockSpec((m, k), lambda j: (0, 0)),
                  pl.BlockSpec((k, tn), col),
                  pl.BlockSpec((k, tn), lambda j: (0, j + n_col)),
                  pl.BlockSpec((m, tn), col), pl.BlockSpec((m, tn), col),
                  pl.BlockSpec((FFN_W, tn), col), pl.BlockSpec((1, tn), col)],
        out_specs=[pl.BlockSpec((m, tn), col), pl.BlockSpec((m, tn), col)],
        out_shape=[jax.ShapeDtypeStruct((m, dff), F32), jax.ShapeDtypeStruct((m, dff), F32)],
        compiler_params=_cparams(1),
        name="ffn_up_short",
    )(x, w_up_bf16, w_up_bf16, h1, h2, conv_w, conv_b.reshape(1, dff))


def _compress_kernel(pt_ref, *refs, pages):
    page_refs = refs[:pages + 1]
    w_ref, pecol_ref, w1_ref, w2_ref, o_ref = refs[pages + 1:pages + 6]
    rows_refs = refs[pages + 6:]
    parts = len(rows_refs)
    ppp = (pages + 1) // parts
    cpp = PAGE_SIZE // CMP_STRIDE
    n = (pages + 1) * cpp
    n_p = ppp * cpp
    hidden = w1_ref.shape[1]
    gpr = LANES // NSA_DH
    pieces = NSA_KV // gpr
    for i, r in enumerate(page_refs):
        for pc in range(pieces):
            tile = r[0, pc * gpr:(pc + 1) * gpr].reshape(LANES, PAGE_SIZE)
            rows_refs[i // ppp][pc, (i % ppp) * PAGE_SIZE:(i % ppp + 1) * PAGE_SIZE, :] = tile.T
    pe_term = jnp.sum(pecol_ref[...] * w1_ref[...], axis=0, keepdims=True)
    accs = []
    for rows_ref in rows_refs:
        lhs = jnp.concatenate([rows_ref[pc].reshape(n_p, CMP_STRIDE * LANES) for pc in range(pieces)], axis=0)
        accs.append(_dot(lhs.astype(BF16), w_ref[...]))
    for pc in range(pieces):
        for gl in range(gpr):
            a = jnp.concatenate([acc[pc * n_p:(pc + 1) * n_p, gl * 2 * hidden:(gl + 1) * 2 * hidden]
                                 for acc in accs], axis=0)
            nxt = pltpu.roll(a, n - 1, 0)
            pre = pe_term + a[:, 0:hidden]
            pre = pre + nxt[:, hidden:2 * hidden]
            o_ref[0, pc * gpr + gl] = _dot(_gelu(pre[0:pages * cpp]).astype(BF16), w2_ref[...])


def compress(rows_t, page_table, pe, w1, w2, pages=COMPRESS_PAGES):
    pooled = page_table is not None
    if pooled:
        n_seq, n_pages = page_table.shape
    else:
        n_seq, n_pages = rows_t.shape[0], rows_t.shape[3] // PAGE_SIZE
        page_table = jnp.zeros((1, 1), jnp.int32)
    pages = min(pages, n_pages)
    assert n_pages % pages == 0
    parts = next(p for p in (3, 5, 1) if (pages + 1) % p == 0)
    hidden = w1.shape[1]
    cpp = PAGE_SIZE // CMP_STRIDE
    r = CMP_LEN // CMP_STRIDE
    gpr = LANES // NSA_DH
    assert r == 2 and gpr == 2
    w1p = w1.reshape(r, CMP_STRIDE, NSA_DH, hidden)
    w16 = jnp.concatenate([w1p[0], w1p[1]], axis=-1)
    zero = jnp.zeros_like(w16)
    per_tok = jnp.concatenate([jnp.concatenate([w16, zero], axis=-1),
                               jnp.concatenate([zero, w16], axis=-1)], axis=1)
    w_chunk = per_tok.reshape(CMP_STRIDE * LANES, gpr * 2 * hidden).astype(BF16)
    w2p = jnp.pad(w2, ((0, 0), (0, SLOT - NSA_DH))).astype(BF16)
    pecol = pe.reshape(CMP_LEN * NSA_DH, 1)

    def page_map(i):
        if pooled:
            return lambda s, j, pt: (pt[s, jnp.minimum(j * pages + i, n_pages - 1)], 0, 0, 0)
        return lambda s, j, pt: (s, 0, 0, jnp.minimum(j * pages + i, n_pages - 1))

    fixed2 = lambda s, j, pt: (0, 0)
    grid_spec = pltpu.PrefetchScalarGridSpec(
        num_scalar_prefetch=1,
        grid=(n_seq, n_pages // pages),
        in_specs=[pl.BlockSpec((1, NSA_KV, NSA_DH, PAGE_SIZE), page_map(i)) for i in range(pages + 1)] + [
            pl.BlockSpec((CMP_STRIDE * LANES, gpr * 2 * hidden), fixed2),
            pl.BlockSpec((CMP_LEN * NSA_DH, 1), fixed2),
            pl.BlockSpec((CMP_LEN * NSA_DH, hidden), fixed2),
            pl.BlockSpec((hidden, SLOT), fixed2)],
        out_specs=pl.BlockSpec((1, NSA_KV, pages * cpp, SLOT), lambda s, j, pt: (s, 0, j, 0)),
        scratch_shapes=[pltpu.VMEM((NSA_KV // gpr, (pages + 1) // parts * PAGE_SIZE, LANES), F32)
                        for _ in range(parts)],
    )
    return pl.pallas_call(
        functools.partial(_compress_kernel, pages=pages),
        grid_spec=grid_spec,
        out_shape=jax.ShapeDtypeStruct((n_seq, NSA_KV, n_pages * cpp, SLOT), F32),
        compiler_params=_cparams(2),
        name="compress",
    )(page_table, *([rows_t] * (pages + 1)), w_chunk, pecol, w1, w2p)


def _cmp_select_kernel(q_ref, kc_ref, vc_ref, ov_ref, o_ref, idx_ref, *, tq, n_cmp, n_slc, pos0):
    ncp = kc_ref.shape[2]
    nsp = ov_ref.shape[0]
    hrows = NSA_GROUP * tq
    q_pos = pos0 + (lax.broadcasted_iota(jnp.int32, (hrows, ncp), 0) & (tq - 1))
    blk_i = lax.broadcasted_iota(jnp.int32, (hrows, ncp), 1)
    valid = (blk_i * CMP_STRIDE + (CMP_LEN - 1) <= q_pos) & (blk_i < n_cmp)
    ov = ov_ref[...]
    groups = range(NSA_KV)
    heads = [range(g * NSA_GROUP, (g + 1) * NSA_GROUP) for g in groups]
    scores = [_nt_dot(jnp.concatenate([q_ref[:, h * SLOT:(h + 1) * SLOT] for h in heads[g]], axis=0).astype(BF16),
                      kc_ref[0, g].astype(BF16)) for g in groups]
    probs = []
    for g in groups:
        s = jnp.where(valid, scores[g], NEG_INF)
        m = jnp.max(s, axis=-1, keepdims=True)
        e = jnp.where(valid, jnp.exp(s - m), 0.0)
        den = jnp.sum(e, axis=-1, keepdims=True)
        probs.append(e * (1.0 / jnp.where(den > 0.0, den, 1.0)))
    outs = [_dot(probs[g].astype(BF16), vc_ref[0, g].astype(BF16)) for g in groups]
    imps = []
    for g in groups:
        p_sum = jnp.zeros((tq, ncp), F32)
        for j, h in enumerate(heads[g]):
            o_ref[:, h * SLOT:(h + 1) * SLOT] = outs[g][j * tq:(j + 1) * tq]
            p_sum = p_sum + probs[g][j * tq:(j + 1) * tq]
        hi = p_sum.astype(BF16)
        lo = (p_sum - hi.astype(F32)).astype(BF16)
        imps.append(_nt_dot(ov, hi) + _nt_dot(ov, lo))
    cols = NSA_KV * tq
    imp = jnp.concatenate(imps, axis=1)
    blk = lax.broadcasted_iota(jnp.int32, (nsp, cols), 0)
    cur = (pos0 + (lax.broadcasted_iota(jnp.int32, (nsp, cols), 1) & (tq - 1))) // SLC_BLOCK
    real = blk < n_slc
    causal = real & (blk <= cur)
    forced = (blk == 0) | (blk == cur) | (blk == cur - 1)
    score = jnp.where(causal, imp + jnp.where(forced, FORCE_BONUS, 0.0), NEG_INF)
    score = jnp.where(real, score, REMOVED)
    idx = jnp.zeros((SLC_TOPN, cols), jnp.int32)
    idx_row = lax.broadcasted_iota(jnp.int32, (SLC_TOPN, cols), 0)
    for it in range(min(SLC_TOPN, n_slc)):
        m = jnp.max(score, axis=0, keepdims=True)
        first = jnp.min(jnp.where(score == m, blk, nsp), axis=0, keepdims=True)
        score = jnp.where(blk == first, REMOVED, score)
        idx = jnp.where(idx_row == it, first, idx)
    idx_ref[0] = idx


def cmp_block_overlap(n_cmp_pad, n_cmp, n_slc, n_slc_pad, lane_off):
    i = np.arange(n_cmp_pad)[:, None]
    j = np.arange(n_slc_pad)[None, :] - lane_off
    start = i * CMP_STRIDE
    hit = (start < (j + 1) * SLC_BLOCK) & (start + CMP_LEN > j * SLC_BLOCK) & (i < n_cmp) & (j >= 0) & (j < n_slc)
    return jnp.asarray(hit.astype(np.float32), dtype=BF16)


def cmp_attention_select(q_slots, kcc, vcc, n_seq, n_cmp, n_slc, pos0):
    tokens = q_slots.shape[0]
    tq = tokens // n_seq
    assert tq & (tq - 1) == 0
    ncp = kcc.shape[2]
    nsp = -(-n_slc // SUBLANES) * SUBLANES
    ov_t = cmp_block_overlap(ncp, n_cmp, n_slc, nsp, 0).T
    hw = NSA_HEADS * SLOT
    o_cmp, idx_t = pl.pallas_call(
        functools.partial(_cmp_select_kernel, tq=tq, n_cmp=n_cmp, n_slc=n_slc, pos0=pos0),
        grid=(n_seq,),
        in_specs=[pl.BlockSpec((tq, hw), lambda s: (s, 0)),
                  pl.BlockSpec((1, NSA_KV, ncp, SLOT), lambda s: (s, 0, 0, 0)),
                  pl.BlockSpec((1, NSA_KV, ncp, SLOT), lambda s: (s, 0, 0, 0)),
                  pl.BlockSpec((nsp, ncp), lambda s: (0, 0))],
        out_specs=[pl.BlockSpec((tq, hw), lambda s: (s, 0)),
                   pl.BlockSpec((1, SLC_TOPN, NSA_KV * tq), lambda s: (s, 0, 0))],
        out_shape=[jax.ShapeDtypeStruct((tokens, hw), F32),
                   jax.ShapeDtypeStruct((n_seq, SLC_TOPN, NSA_KV * tq), jnp.int32)],
        compiler_params=_cparams(1),
        name="cmp_attention_select",
    )(q_slots, kcc, vcc, ov_t)
    topn = min(SLC_TOPN, n_slc)
    idx = idx_t[:, :topn].reshape(n_seq, topn, NSA_KV, tq).transpose(0, 2, 3, 1)
    return o_cmp, idx


def _cmp_select_prompt_kernel(q_ref, kc_ref, vc_ref, ovt_ref, gt_ref, o_ref, sel_ref, *, tq, n_cmp, n_slc):
    t0 = pl.program_id(2) * tq
    ncp = kc_ref.shape[2]
    nsr = ovt_ref.shape[0]
    kc = kc_ref[0, 0].astype(BF16)
    vc = vc_ref[0, 0].astype(BF16)
    q_pos = t0 + lax.broadcasted_iota(jnp.int32, (ncp, tq), 1)
    blk_i = lax.broadcasted_iota(jnp.int32, (ncp, tq), 0)
    valid = (blk_i * CMP_STRIDE + (CMP_LEN - 1) <= q_pos) & (blk_i < n_cmp)
    p_sum = jnp.zeros((ncp, tq), F32)
    for j in range(NSA_GROUP):
        s = jnp.where(valid, _nt_dot(kc, q_ref[:, j * SLOT:(j + 1) * SLOT]), NEG_INF)
        m = jnp.max(s, axis=0, keepdims=True)
        e = jnp.where(valid, jnp.exp(s - m), 0.0)
        den = jnp.sum(e, axis=0, keepdims=True)
        p = e * (1.0 / jnp.where(den > 0.0, den, 1.0))
        gate = jax.nn.sigmoid(gt_ref[0, pl.ds(pl.program_id(1) * NSA_GROUP + j, 1), :])
        o_ref[:, j * SLOT:(j + 1) * SLOT] = _tn_dot((p * gate).astype(BF16), vc).astype(o_ref.dtype)
        p_sum = p_sum + p
    hi = p_sum.astype(BF16)
    lo = (p_sum - hi.astype(F32)).astype(BF16)
    ovt = ovt_ref[...]
    imp = _dot(ovt, hi) + _dot(ovt, lo)
    blk = lax.broadcasted_iota(jnp.int32, (nsr, tq), 0)
    cur = (t0 + lax.broadcasted_iota(jnp.int32, (nsr, tq), 1)) // SLC_BLOCK
    real = blk < n_slc
    causal = real & (blk <= cur)
    forced = (blk == 0) | (blk == cur) | (blk == cur - 1)
    score = jnp.where(causal, imp + jnp.where(forced, FORCE_BONUS, 0.0), NEG_INF)
    score = jnp.where(real, score, REMOVED)
    picked = jnp.zeros((nsr, tq), jnp.bool_)
    for _ in range(min(SLC_TOPN, n_slc)):
        m = jnp.max(score, axis=0, keepdims=True)
        first = jnp.min(jnp.where(score == m, blk, nsr), axis=0, keepdims=True)
        hit = blk == first
        picked = picked | hit
        score = jnp.where(hit, REMOVED, score)
    bias_t = jnp.where(real & ~(picked & causal), NEG_INF, 0.0)
    slot_t = jnp.concatenate([jnp.zeros((NSA_DH, tq), F32), bias_t], axis=0)
    sel_ref[0, 0] = slot_t.T


def cmp_attention_select_prompt(q_slots, kcc, vcc, gates_t, n_seq, n_cmp, n_slc, tq):
    tokens = q_slots.shape[0]
    t = tokens // n_seq
    tq = _row_tile(t, tq)
    nt = t // tq
    ncp = kcc.shape[2]
    nsr = SLOT - NSA_DH
    assert n_slc <= nsr
    ovt = cmp_block_overlap(ncp, n_cmp, n_slc, nsr, 0).T
    gw = NSA_GROUP * SLOT
    return pl.pallas_call(
        functools.partial(_cmp_select_prompt_kernel, tq=tq, n_cmp=n_cmp, n_slc=n_slc),
        grid=(n_seq, NSA_KV, nt),
        in_specs=[pl.BlockSpec((tq, gw), lambda s, g, i: (s * nt + i, g)),
                  pl.BlockSpec((1, 1, ncp, SLOT), lambda s, g, i: (s, g, 0, 0)),
                  pl.BlockSpec((1, 1, ncp, SLOT), lambda s, g, i: (s, g, 0, 0)),
                  pl.BlockSpec((nsr, ncp), lambda s, g, i: (0, 0)),
                  pl.BlockSpec((1, LANES, tq), lambda s, g, i: (s, 0, i))],
        out_specs=[pl.BlockSpec((tq, gw), lambda s, g, i: (s * nt + i, g)),
                   pl.BlockSpec((1, 1, tq, SLOT), lambda s, g, i: (s, g, i, 0))],
        out_shape=[jax.ShapeDtypeStruct((tokens, NSA_HEADS * SLOT), BF16),
                   jax.ShapeDtypeStruct((n_seq, NSA_KV, t, SLOT), F32)],
        compiler_params=_cparams(3),
        name="cmp_attention_select_prompt",
    )(q_slots, kcc, vcc, ovt, gates_t)


def _prompt_slc_win_kernel(q_ref, sel_ref, ks_ref, vs_ref, kw_ref, vw_ref, oh_ref, gt_ref, o_ref,
                           m_slc, acc_slc, m_win, acc_win, o_sum, *, tq, seq):
    acc_rows = NSA_DH + 16
    qi = pl.program_id(2)
    t0 = qi * tq
    rows = NSA_GROUP * tq
    sel = sel_ref[0, 0]
    q_plain = jnp.concatenate([q_ref[:, j * SLOT:(j + 1) * SLOT] for j in range(NSA_GROUP)], axis=0)
    q_aug = jnp.concatenate([(q_ref[:, j * SLOT:(j + 1) * SLOT].astype(F32) + sel).astype(BF16)
                             for j in range(NSA_GROUP)], axis=0)
    zeros_k = jnp.zeros((SLOT - NSA_DH, tq), F32)
    ones_row = (lax.broadcasted_iota(jnp.int32, (acc_rows - NSA_DH, tq), 0) == 0).astype(BF16)
    rel = lax.broadcasted_iota(jnp.int32, (tq, LANES), 1) - lax.broadcasted_iota(jnp.int32, (tq, LANES), 0)

    def scores(q_rows, k_top, k_bottom):
        k_rows = jnp.concatenate([k_top, k_bottom], axis=0).T.astype(BF16)
        return [_nt_dot(k_rows, q_rows[c:c + 2 * LANES]) for c in range(0, rows, 2 * LANES)]

    def update(state, s_t, v_top, start, mask):
        m_ref, acc_ref = state
        v_t = jnp.concatenate([v_top.astype(BF16), ones_row], axis=0)
        for cg in range(rows // (2 * LANES)):
            p_parts, a_parts = [], []
            for h in range(2):
                c0 = (2 * cg + h) * LANES
                x = s_t[cg][:, h * LANES:(h + 1) * LANES]
                if mask is not None:
                    lo, hi = mask
                    off = t0 - start + (c0 & (tq - 1))
                    keep = rel >= lo - off
                    if hi is not None:
                        keep = keep & (rel < hi - off)
                    x = jnp.where(keep, x, NEG_INF)
                m_old = m_ref[:, c0:c0 + LANES]
                m_new = jnp.maximum(m_old, jnp.max(x, axis=0, keepdims=True))
                m_ref[:, c0:c0 + LANES] = m_new
                a_parts.append(jnp.exp(m_old - m_new))
                p_parts.append(jnp.exp(x - m_new).astype(BF16))
            c0 = 2 * cg * LANES
            pv = _dot(v_t, jnp.concatenate(p_parts, axis=1))
            acc_ref[:, c0:c0 + 2 * LANES] = jnp.concatenate(a_parts, axis=1) * acc_ref[:, c0:c0 + 2 * LANES] + pv

    slc_state = (m_slc, acc_slc)
    win_state = (m_win, acc_win)
    for m_ref, acc_ref in (slc_state, win_state):
        m_ref[...] = jnp.full(m_ref.shape, NEG_INF, F32)
        acc_ref[...] = jnp.zeros(acc_ref.shape, F32)

    slot_pad = jnp.zeros((SLOT - NSA_DH, tq), F32)

    def finish(state, branch, accumulate):
        for j in range(NSA_GROUP):
            a = state[1][:, j * tq:(j + 1) * tq]
            head = pl.program_id(1) * NSA_GROUP + j
            gate = jax.nn.sigmoid(gt_ref[0, pl.ds(branch * NSA_HEADS + head, 1), :])
            o_t = jnp.concatenate([a[0:NSA_DH] * (gate * (1.0 / a[NSA_DH:NSA_DH + 1, :])), slot_pad], axis=0)
            if accumulate:
                o_ref[:, j * SLOT:(j + 1) * SLOT] = (o_sum[:, j * SLOT:(j + 1) * SLOT] + o_t.T).astype(o_ref.dtype)
            else:
                o_sum[:, j * SLOT:(j + 1) * SLOT] = o_t.T

    def slc_scores(start):
        return scores(q_aug, ks_ref[0, :, pl.ds(start, tq)], oh_ref[:, pl.ds(start, tq)])

    def slc_update(s_t, start, mask):
        update(slc_state, s_t, vs_ref[0, :, pl.ds(start, tq)], start, mask)

    k0 = jnp.clip(t0 - WINDOW, 0, seq - WINDOW - tq)
    n_win = WINDOW // tq + 1
    win_start = [pl.multiple_of(k0 + i * tq, tq) for i in range(n_win)]
    win_mask = (0, WINDOW)

    def win_scores(i):
        return scores(q_plain, kw_ref[0, :, pl.ds(win_start[i], tq)], zeros_k)

    def slc_tiles(first, count):
        starts = [pl.multiple_of((first + i) * tq, tq) for i in range(count)]
        s_all = [slc_scores(st) for st in starts]
        for s_t, st in zip(s_all, starts):
            slc_update(s_t, st, None)

    def quad(k4, carry):
        slc_tiles(4 * k4, 4)
        return carry

    lax.fori_loop(0, qi // 4, quad, 0)

    @pl.when(qi % 4 >= 2)
    def _():
        slc_tiles(4 * (qi // 4), 2)

    @pl.when(qi % 2 == 1)
    def _():
        slc_tiles(qi - 1, 1)

    diag = pl.multiple_of(t0, tq)
    s_diag = slc_scores(diag)
    s_win = [win_scores(i) for i in range(n_win)]
    slc_update(s_diag, diag, (0, None))
    for i in range(n_win):
        update(win_state, s_win[i], vw_ref[0, :, pl.ds(win_start[i], tq)], win_start[i], win_mask)
    finish(slc_state, 1, False)
    finish(win_state, 2, True)


def prompt_slc_win_attention(q_slots, sel, ks_t, vs_t, kw_t, vw_t, gates_t, n_seq, tq):
    tokens = q_slots.shape[0]
    seq = tokens // n_seq
    tq = _row_tile(seq, tq)
    assert tq & (tq - 1) == 0 and WINDOW % tq == 0 and seq >= WINDOW + tq
    nt = seq // tq
    gw = NSA_GROUP * SLOT
    assert tq % (2 * LANES) == 0
    onehot_t = jax.nn.one_hot(jnp.arange(seq) // SLC_BLOCK, SLOT - NSA_DH, dtype=F32).T
    kv_spec = pl.BlockSpec((1, NSA_DH, seq), lambda s, g, i: (s, g, 0))
    return pl.pallas_call(
        functools.partial(_prompt_slc_win_kernel, tq=tq, seq=seq),
        grid=(n_seq, NSA_KV, nt),
        in_specs=[pl.BlockSpec((tq, gw), lambda s, g, i: (s * nt + i, g)),
                  pl.BlockSpec((1, 1, tq, SLOT), lambda s, g, i: (s, g, i, 0)),
                  kv_spec, kv_spec, kv_spec, kv_spec,
                  pl.BlockSpec((SLOT - NSA_DH, seq), lambda s, g, i: (0, 0)),
                  pl.BlockSpec((1, LANES, tq), lambda s, g, i: (s, 0, i))],
        out_specs=pl.BlockSpec((tq, gw), lambda s, g, i: (s * nt + i, g)),
        out_shape=jax.ShapeDtypeStruct((tokens, NSA_HEADS * SLOT), BF16),
        scratch_shapes=[pltpu.VMEM((1, NSA_GROUP * tq), F32),
                        pltpu.VMEM((NSA_DH + 16, NSA_GROUP * tq), F32),
                        pltpu.VMEM((1, NSA_GROUP * tq), F32),
                        pltpu.VMEM((NSA_DH + 16, NSA_GROUP * tq), F32),
                        pltpu.VMEM((tq, gw), F32)],
        compiler_params=_cparams(3),
        name="prompt_slc_win_attention",
    )(q_slots, sel, ks_t, vs_t, kw_t, vw_t, onehot_t, gates_t)


def _sample_slc_kernel(idx_ref, pg_ref, q_ref, kn_ref, vn_ref, pool_k, pool_v, o_ref, kbuf, vbuf, sems,
                       *, topn, past, t_pad, t_real):
    s_id, g_id = pl.program_id(0), pl.program_id(1)
    step = s_id * NSA_KV + g_id
    n_steps = pl.num_programs(0) * NSA_KV

    def copies(item_step, t, slot):
        base = (item_step * t_pad + t) * topn
        g = item_step % NSA_KV
        out = []
        for kk in range(topn):
            page = pg_ref[base + kk]
            out.append(pltpu.make_async_copy(pool_k.at[page, g], kbuf.at[slot, kk], sems.at[slot, 0]))
            out.append(pltpu.make_async_copy(pool_v.at[page, g], vbuf.at[slot, kk], sems.at[slot, 1]))
        return out

    def start_all(item_step, t, slot):
        for i, c in enumerate(copies(item_step, t, slot)):
            c.start(priority=i % 2)

    @pl.when(step == 0)
    def _():
        start_all(step, 0, 0)

    for t in range(t_real):
        slot = t % 2
        if t + 1 < t_real:
            start_all(step, t + 1, 1 - slot)
        else:
            @pl.when(step + 1 < n_steps)
            def _():
                start_all(step + 1, 0, 1 - slot)
        for c in copies(step, t, slot):
            c.wait()
        _sample_slc_token(idx_ref, q_ref, kn_ref, vn_ref, o_ref, kbuf[slot], vbuf[slot],
                          base=(step * t_pad + t) * topn, t=t, topn=topn, past=past)


def _sample_slc_token(idx_ref, q_ref, kn_ref, vn_ref, o_ref, k_tiles, v_tiles, *, base, t, topn, past):
    q_pos = past + t
    cur = q_pos // SLC_BLOCK
    first_new = past // SLC_BLOCK
    q = q_ref[0, 0, t].astype(BF16)
    kb = jnp.concatenate([k_tiles[kk] for kk in range(topn)], axis=1).astype(BF16)
    vb = jnp.concatenate([v_tiles[kk] for kk in range(topn)], axis=1).astype(BF16)
    n_keys = topn * PAGE_SIZE
    lane = lax.broadcasted_iota(jnp.int32, (1, n_keys), 1)
    slot = lane // PAGE_SIZE
    in_page = lane % PAGE_SIZE
    k_pos = in_page
    limit = jnp.zeros((1, n_keys), jnp.int32)
    n_new = jnp.int32(0)
    for kk in range(topn):
        b = idx_ref[base + kk]
        here = slot == kk
        k_pos = jnp.where(here, (b // 2) * PAGE_SIZE + in_page, k_pos)
        last = jnp.where(b <= cur, jnp.minimum(q_pos, past - 1), -1)
        limit = jnp.where(here, jnp.where(in_page // SLC_BLOCK == b % 2, last, -1), limit)
        n_new = n_new + jnp.where(b == first_new, 1, 0)
    valid = k_pos <= limit
    s_old = jnp.where(valid, _dot(q, kb), NEG_INF)
    new_lane = lax.broadcasted_iota(jnp.int32, (1, SUBLANES), 1)
    valid_new = past + new_lane <= jnp.where(n_new > 0, q_pos, past - 1)
    s_new = jnp.where(valid_new, _dot(q, kn_ref[0, 0].astype(BF16)), NEG_INF)
    m = jnp.maximum(jnp.max(s_old, axis=-1, keepdims=True), jnp.max(s_new, axis=-1, keepdims=True))
    p_old = jnp.exp(s_old - m)
    p_new = jnp.exp(s_new - m)
    l = jnp.sum(p_old, axis=-1, keepdims=True) + jnp.sum(p_new, axis=-1, keepdims=True)
    o = _nt_dot(p_old.astype(BF16), vb) + _nt_dot(p_new.astype(BF16), vn_ref[0, 0].astype(BF16))
    o_ref[0, 0, t] = o / l


def sample_slc_attention(q_rows, idx, page_table, pool_k, pool_v, k_new_t, v_new_t, t_real, past):
    n_seq, _, t_pad, _, dh = q_rows.shape
    topn = idx.shape[-1]
    assert past % SLC_BLOCK == 0 and t_real <= SUBLANES and PAGE_SIZE == 2 * SLC_BLOCK
    last_old = past // SLC_BLOCK - 1
    logical = jnp.clip(idx, 0, last_old) // 2
    n_pages = page_table.shape[1]
    hit = logical[..., None] == jnp.arange(n_pages, dtype=jnp.int32)
    phys = jnp.sum(jnp.where(hit, page_table[:, None, None, None, :], 0), axis=-1)

    assert t_real % 2 == 0
    new_spec = pl.BlockSpec((1, 1, dh, SUBLANES), lambda s, g, i_r, p_r: (s, g, 0, 0))
    hbm = pl.BlockSpec(memory_space=pl.ANY)
    grid_spec = pltpu.PrefetchScalarGridSpec(
        num_scalar_prefetch=2,
        grid=(n_seq, NSA_KV),
        in_specs=[pl.BlockSpec((1, 1, t_pad, SUBLANES, dh), lambda s, g, i_r, p_r: (s, g, 0, 0, 0)),
                  new_spec, new_spec, hbm, hbm],
        out_specs=pl.BlockSpec((1, 1, t_real, SUBLANES, dh), lambda s, g, i_r, p_r: (s, g, 0, 0, 0)),
        scratch_shapes=[pltpu.VMEM((2, topn, dh, PAGE_SIZE), F32),
                        pltpu.VMEM((2, topn, dh, PAGE_SIZE), F32),
                        pltpu.SemaphoreType.DMA((2, 2))],
    )
    return pl.pallas_call(
        functools.partial(_sample_slc_kernel, topn=topn, past=past, t_pad=t_pad, t_real=t_real),
        grid_spec=grid_spec,
        out_shape=jax.ShapeDtypeStruct((n_seq, NSA_KV, t_real, SUBLANES, dh), F32),
        compiler_params=_cparams(2),
        name="sample_slc_attention",
    )(idx.reshape(-1), phys.reshape(-1), q_rows, k_new_t, v_new_t, pool_k, pool_v)


def _sample_win_kernel(q_ref, wk_ref, wv_ref, kn_ref, vn_ref, o_ref, *, past, t_pad):
    rows = t_pad * SUBLANES
    wb = wk_ref.shape[-1]
    groups = range(NSA_KV)
    q_pos = past + lax.broadcasted_iota(jnp.int32, (rows, 1), 0) // SUBLANES
    k_pos = past - wb + lax.broadcasted_iota(jnp.int32, (1, wb), 1)
    dist = q_pos - k_pos
    valid = (dist >= 0) & (dist < WINDOW) & (k_pos >= 0)
    n_pos = past + lax.broadcasted_iota(jnp.int32, (1, SUBLANES), 1)
    dist_n = q_pos - n_pos
    valid_n = (dist_n >= 0) & (dist_n < WINDOW)
    qs = [q_ref[0, g].reshape(rows, q_ref.shape[-1]).astype(BF16) for g in groups]
    s_old = [_dot(qs[g], wk_ref[0, g].astype(BF16)) for g in groups]
    s_new = [_dot(qs[g], kn_ref[0, g].astype(BF16)) for g in groups]
    p_old, p_new, inv_l = [], [], []
    for g in groups:
        so = jnp.where(valid, s_old[g], NEG_INF)
        sn = jnp.where(valid_n, s_new[g], NEG_INF)
        m = jnp.maximum(jnp.max(so, axis=-1, keepdims=True), jnp.max(sn, axis=-1, keepdims=True))
        po = jnp.exp(so - m)
        pn = jnp.exp(sn - m)
        inv_l.append(1.0 / (jnp.sum(po, axis=-1, keepdims=True) + jnp.sum(pn, axis=-1, keepdims=True)))
        p_old.append(po.astype(BF16))
        p_new.append(pn.astype(BF16))
    for g in groups:
        o = _nt_dot(p_old[g], wv_ref[0, g].astype(BF16)) + _nt_dot(p_new[g], vn_ref[0, g].astype(BF16))
        o_ref[0, g] = (o * inv_l[g]).reshape(t_pad, SUBLANES, o.shape[-1])


def sample_win_attention(q_rows, win_k_t, win_v_t, k_new_t, v_new_t, past):
    n_seq, _, t_pad, _, dh = q_rows.shape
    wb = win_k_t.shape[-1]
    q_spec = pl.BlockSpec((1, NSA_KV, t_pad, SUBLANES, dh), lambda s: (s, 0, 0, 0, 0))
    win_spec = pl.BlockSpec((1, NSA_KV, dh, wb), lambda s: (s, 0, 0, 0))
    new_spec = pl.BlockSpec((1, NSA_KV, dh, SUBLANES), lambda s: (s, 0, 0, 0))
    return pl.pallas_call(
        functools.partial(_sample_win_kernel, past=past, t_pad=t_pad),
        grid=(n_seq,),
        in_specs=[q_spec, win_spec, win_spec, new_spec, new_spec],
        out_specs=q_spec,
        out_shape=jax.ShapeDtypeStruct(q_rows.shape, F32),
        compiler_params=_cparams(1),
        name="sample_win_attention",
    )(q_rows, win_k_t, win_v_t, k_new_t, v_new_t)


def _to_slots(a):
    lead = a.shape[:-1]
    n = a.shape[-1] // NSA_DH
    a = a.reshape(*lead, n, NSA_DH)
    a = jnp.pad(a, [(0, 0)] * (a.ndim - 1) + [(0, SLOT - NSA_DH)])
    return a.reshape(*lead, n * SLOT)


def _odd_weights(w_in, w_out):
    d = w_in.shape[0]
    hq = NSA_HEADS * NSA_DH
    kvw = NSA_KV * NSA_DH
    wq = _to_slots(w_in[:, :hq] * np.float32(NSA_DH ** -0.5))
    wg = jnp.pad(w_in[:, hq + 6 * kvw:], ((0, 0), (0, LANES - 3 * NSA_HEADS)))
    w_q = jnp.concatenate([wq, wg], axis=1).astype(BF16)
    w_kvt = jnp.concatenate([w_in[:, hq:hq + 6 * kvw], wg], axis=1).T.astype(BF16)
    wo = jnp.pad(w_out.reshape(NSA_HEADS, NSA_DH, d), ((0, 0), (0, SLOT - NSA_DH), (0, 0)))
    wo = wo.reshape(NSA_HEADS * SLOT, d).astype(BF16)
    k = NSA_HEADS * SLOT
    e = np.zeros((LANES, 3 * k), np.float32)
    for c in range(3):
        for h in range(NSA_HEADS):
            e[c * NSA_HEADS + h, c * k + h * SLOT:c * k + (h + 1) * SLOT] = 1.0
    return w_q, w_kvt, wo, jnp.asarray(e, dtype=BF16)


def _group_rows(q_slots, n_seq, t_pad):
    q = q_slots.reshape(n_seq, t_pad, NSA_KV, NSA_GROUP, SLOT)[..., :NSA_DH]
    q = q.transpose(0, 2, 1, 3, 4)
    return jnp.pad(q, ((0, 0), (0, 0), (0, 0), (0, SUBLANES - NSA_GROUP), (0, 0)))


def _ungroup_rows(o, n_seq, t_pad):
    t = o.shape[2]
    o = o[:, :, :, :NSA_GROUP].transpose(0, 2, 1, 3, 4)
    o = jnp.pad(o, ((0, 0), (0, t_pad - t), (0, 0), (0, 0), (0, SLOT - NSA_DH)))
    return o.reshape(n_seq * t_pad, NSA_HEADS * SLOT)


def _feature_major(cache):
    return cache.transpose(0, 2, 3, 1)


def _token_major(a_t):
    return a_t.transpose(0, 3, 1, 2)


def _pad_rows(a, t_pad):
    return jnp.pad(a, ((0, 0), (0, t_pad - a.shape[1])) + ((0, 0),) * (a.ndim - 2))


def kernel(x_prompt, x_sample, state_sconv, state_ret, cache_cmp_k, cache_cmp_v, cache_slc_k, cache_slc_v,
           cache_win_k, cache_win_v, state_ffn_conv, page_table,
           w_in_even, sconv_w, sconv_b, ret_gn_g, w_out_even,
           w_in_odd, cmp_pe, cmp_w1, cmp_w2, w_out_odd,
           ln_mix_g, ln_mix_b, ffn_w_up, ffn_conv_w, ffn_conv_b, ffn_w_down, ln_ffn_g, ln_ffn_b):
    b_p, s_p, d_model = x_prompt.shape
    b_s, t_s, _ = x_sample.shape
    n_pages = page_table.shape[1]
    past = n_pages * PAGE_SIZE
    t_pad = SUBLANES
    assert t_s <= t_pad and t_s >= SCONV_W - 1 and t_s < CMP_STRIDE and past % PAGE_SIZE == 0
    assert s_p % RET_CHUNK == 0 and s_p % PAGE_SIZE == 0
    d_sconv = sconv_w.shape[-1]
    d_ff = ffn_conv_w.shape[-1]
    gd = NSA_KV * NSA_DH
    depth = ln_mix_g.shape[0]

    xp = x_prompt.reshape(b_p * s_p, d_model)
    xs = _pad_rows(x_sample, t_pad).reshape(b_s * t_pad, d_model)
    outs = {k: [] for k in ("sconv_p", "sconv_s", "ret_p", "ret_s", "cmp_k_p", "cmp_v_p", "slc_k_p", "slc_v_p",
                            "cmp_k_s", "cmp_v_s", "slc_k_s", "slc_v_s", "win_k_p", "win_v_p", "win_k_s",
                            "win_v_s", "ffn_p", "ffn_s")}

    for layer in range(depth):
        if layer % 2 == 0:
            e = layer // 2
            w_in = w_in_even[e].astype(BF16)
            w_out = w_out_even[e].astype(BF16)
            n_in = w_in.shape[1]
            yp, hc, st = even_mixer(xp, w_in, jnp.zeros((b_p, SCONV_W - 1, d_sconv), F32),
                                    jnp.zeros((b_p,) + state_ret.shape[2:], F32), jnp.arange(s_p),
                                    RET_CHUNK, RET_CHUNK, sconv_w[e], sconv_b[e], ret_gn_g[e],
                                    chunks=MIXER_CHUNKS if s_p % (MIXER_CHUNKS * RET_CHUNK) == 0 else 1)
            outs["sconv_p"].append(hc)
            outs["ret_p"].append(st)
            xp = matmul_residual_ln(yp, w_out, xp, ln_mix_g[layer], ln_mix_b[layer])
            (zs,) = matmul_split(xs, w_in, [n_in], [F32])
            ys, hc, st = even_mixer(zs, None, state_sconv[e], state_ret[e], past + jnp.arange(t_pad),
                                    t_pad, t_s, sconv_w[e], sconv_b[e], ret_gn_g[e])
            outs["sconv_s"].append(hc)
            outs["ret_s"].append(st)
            xs = matmul_residual_ln(ys, w_out, xs, ln_mix_g[layer], ln_mix_b[layer])
        else:
            o = layer // 2
            w_q, w_kvt, w_out, e_gate = _odd_weights(w_in_odd[o], w_out_odd[o])
            pe, w1, w2 = cmp_pe[o], cmp_w1[o], cmp_w2[o]
            qp, _, gtp, kc, vc, ks, vs, kw, vw = nsa_projection(xp, b_p, w_q, w_kvt, BF16)
            as_cache = lambda a_t: _token_major(a_t.reshape(b_p, NSA_KV, NSA_DH, -1))
            keep = min(WINDOW, s_p)
            outs["cmp_k_p"].append(as_cache(kc))
            outs["cmp_v_p"].append(as_cache(vc))
            outs["slc_k_p"].append(as_cache(ks))
            outs["slc_v_p"].append(as_cache(vs))
            outs["win_k_p"].append(as_cache(kw[:, :, s_p - keep:]))
            outs["win_v_p"].append(as_cache(vw[:, :, s_p - keep:]))
            kcc = compress(kc.reshape(b_p, NSA_KV, NSA_DH, s_p), None, pe[0], w1[0], w2[0])
            vcc = compress(vc.reshape(b_p, NSA_KV, NSA_DH, s_p), None, pe[1], w1[1], w2[1])
            n_cmp = s_p // CMP_STRIDE - CMP_LEN // CMP_STRIDE + 1
            n_slc = s_p // SLC_BLOCK
            oc, sel = cmp_attention_select_prompt(qp, kcc, vcc, gtp, b_p, n_cmp, n_slc, SELECT_ROWS)
            osw = prompt_slc_win_attention(qp, sel, ks, vs, kw, vw, gtp, b_p, ATTN_ROWS)
            xp = sum2_matmul_residual_ln(oc, osw, w_out, xp, ln_mix_g[layer], ln_mix_b[layer])
            qs, gs, _, *kv_s = nsa_projection(xs, 1, w_q, w_kvt, F32)
            kc, vc, ks, vs, kw, vw = [a.reshape(NSA_KV, NSA_DH, b_s, t_pad).transpose(2, 0, 1, 3) for a in kv_s]
            new_rows = lambda a_t: _token_major(a_t[..., :t_s])
            outs["cmp_k_s"].append(new_rows(kc))
            outs["cmp_v_s"].append(new_rows(vc))
            outs["slc_k_s"].append(new_rows(ks))
            outs["slc_v_s"].append(new_rows(vs))
            win_k = _feature_major(cache_win_k[o])
            win_v = _feature_major(cache_win_v[o])
            wb = win_k.shape[-1]
            keep = min(WINDOW, wb + t_s)
            outs["win_k_s"].append(_token_major(jnp.concatenate([win_k, kw[..., :t_s]], axis=-1)[..., -keep:]))
            outs["win_v_s"].append(_token_major(jnp.concatenate([win_v, vw[..., :t_s]], axis=-1)[..., -keep:]))
            kcc = compress(_feature_major(cache_cmp_k[o]), page_table, pe[0], w1[0], w2[0])
            vcc = compress(_feature_major(cache_cmp_v[o]), page_table, pe[1], w1[1], w2[1])
            n_cmp = (past + t_s) // CMP_STRIDE - CMP_LEN // CMP_STRIDE + 1
            n_slc = -(-(past + t_s) // SLC_BLOCK)
            oc, idx = cmp_attention_select(qs, kcc, vcc, b_s, n_cmp, n_slc, past)
            q_rows = _group_rows(qs, b_s, t_pad)
            osl = sample_slc_attention(q_rows, idx, page_table, _feature_major(cache_slc_k[o]),
                                       _feature_major(cache_slc_v[o]), ks, vs, t_s, past)
            ow = sample_win_attention(q_rows, win_k, win_v, kw, vw, past)
            xs = nsa_merge_residual_ln(oc, _ungroup_rows(osl, b_s, t_pad), _ungroup_rows(ow, b_s, t_pad), gs,
                                       e_gate, w_out, xs, ln_mix_g[layer], ln_mix_b[layer])
        w_up = ffn_w_up[layer].astype(BF16)
        w_down = ffn_w_down[layer].astype(BF16)
        hp, hist_p = ffn_up_sequences(xp, b_p, jnp.zeros((b_p, FFN_W - 1, d_ff), F32), w_up,
                                      ffn_conv_w[layer], ffn_conv_b[layer])
        outs["ffn_p"].append(hist_p)
        xp = matmul_residual_ln(hp, w_down, xp, ln_ffn_g[layer], ln_ffn_b[layer])
        hs, a_s = ffn_up_short(xs, state_ffn_conv[layer], w_up, ffn_conv_w[layer], ffn_conv_b[layer])
        outs["ffn_s"].append(a_s.reshape(b_s, t_pad, d_ff)[:, t_s - (FFN_W - 1):t_s])
        xs = matmul_residual_ln(hs, w_down, xs, ln_ffn_g[layer], ln_ffn_b[layer])

    st = jnp.stack
    y_p = xp.reshape(b_p, s_p, d_model)
    y_s = xs.reshape(b_s, t_pad, d_model)[:, :t_s]
    order = ("sconv_p", "sconv_s", "ret_p", "ret_s", "cmp_k_p", "cmp_v_p", "slc_k_p", "slc_v_p",
             "cmp_k_s", "cmp_v_s", "slc_k_s", "slc_v_s", "win_k_p", "win_v_p", "win_k_s", "win_v_s",
             "ffn_p", "ffn_s")
    return (y_p, y_s) + tuple(st(outs[k]) for k in order)
```
